```python
import math
import jax
import jax.numpy as jnp
from jax import lax
import numpy as np

D_MODEL = 1024
BATCH = 8
SEQ = 4096
DEPTH = 2

N_MIXERS = 2
POOL_WINDOWS = (2, 4, 8, 16)
N_POOL_GROUPS = len(POOL_WINDOWS)
POOL_GROUP_DIM = D_MODEL // N_POOL_GROUPS
HEAD_DIM = 64
N_HEADS = D_MODEL // HEAD_DIM
DIL_CONFIGS = ((128, 1), (512, 4), (2048, 16))
N_DIL_GROUPS = len(DIL_CONFIGS)
ATTN_WIDTH = N_HEADS * HEAD_DIM
QKV_WIDTH = N_DIL_GROUPS * 3 * ATTN_WIDTH
D_FF = 2816
MACARON_WEIGHT = 0.5
ALPHA = (2.0 * DEPTH) ** 0.25
BETA = (8.0 * DEPTH) ** -0.25
LN_EPS = 1e-5
MASK_VALUE = -1e30

kernel_name = "hybrid_pool_dilated_attn_macaron_deepnorm"


def _alibi_slopes():
    n = N_DIL_GROUPS * N_HEADS
    s = 2.0 ** (-8.0 * np.arange(1, n + 1) / n)
    return s.reshape(N_DIL_GROUPS, N_HEADS).astype(np.float32)


def _layer_norm(x, g, b):
    x32 = x.astype(jnp.float32)
    mu = jnp.mean(x32, axis=-1, keepdims=True)
    var = jnp.mean(jnp.square(x32 - mu), axis=-1, keepdims=True)
    y = (x32 - mu) * lax.rsqrt(var + LN_EPS)
    return (y * g.astype(jnp.float32) + b.astype(jnp.float32)).astype(x.dtype)


def _swiglu(x, w_gate, w_up, w_down):
    return (jax.nn.silu(x @ w_gate) * (x @ w_up)) @ w_down


def _pool_mixer(x, w_in, w_group, scale, w_out):
    B, S, _ = x.shape
    u = (x @ w_in).reshape(B, S, N_POOL_GROUPS, POOL_GROUP_DIM).astype(jnp.float32)
    csum = jnp.concatenate([jnp.zeros_like(u[:, :1]), jnp.cumsum(u, axis=1)], axis=1)
    half = jnp.asarray([w // 2 for w in POOL_WINDOWS], dtype=jnp.int32)
    t = jnp.arange(S, dtype=jnp.int32)[:, None]
    lo = jnp.clip(t - half[None, :], 0, S)
    hi = jnp.clip(t + half[None, :], 0, S)
    gidx = jnp.arange(N_POOL_GROUPS)[None, :]
    win_sum = csum[:, hi, gidx] - csum[:, lo, gidx]
    mean = win_sum / (hi - lo).astype(jnp.float32)[None, :, :, None]
    mixed = (mean - u).astype(x.dtype)
    y = jnp.einsum('bsgc,gce->bsge', mixed, w_group).reshape(B, S, D_MODEL) * scale
    return y @ w_out


def _dilated_group(q, k, v, window, dilation, slopes):
    B, S, H, E = q.shape
    d = dilation
    L = S // d
    R = window // (2 * d)
    W = R
    nb = -(-L // W)
    Lp = nb * W

    def to_sub(a):
        return a.reshape(B, L, d, H, E).transpose(0, 2, 3, 1, 4)

    qs = jnp.pad(to_sub(q), ((0, 0),) * 3 + ((0, Lp - L), (0, 0))).reshape(B, d, H, nb, W, E)

    def windows(a):
        ap = jnp.pad(to_sub(a), ((0, 0),) * 3 + ((W, W + Lp - L), (0, 0)))
        ap = ap.reshape(B, d, H, nb + 2, W, E)
        return jnp.concatenate([ap[:, :, :, :-2], ap[:, :, :, 1:-1], ap[:, :, :, 2:]], axis=4)

    kw = windows(k)
    vw = windows(v)
    a_idx = jnp.arange(W)
    c_idx = jnp.arange(3 * W)
    n_idx = jnp.arange(nb)
    rel = c_idx[None, :] - W - a_idx[:, None]
    j = (n_idx[:, None, None] - 1) * W + c_idx[None, None, :]
    valid = (jnp.abs(rel)[None] <= R) & (j >= 0) & (j < L)
    dist = (d * jnp.abs(rel)).astype(jnp.float32)
    bias = -slopes[:, None, None] * dist[None]
    scores = jnp.einsum('bdhnqe,bdhnke->bdhnqk', qs.astype(jnp.float32),
                        kw.astype(jnp.float32)) * (E ** -0.5) + bias[:, None]
    scores = jnp.where(valid, scores, MASK_VALUE)
    lse = jax.nn.logsumexp(scores, axis=-1)
    p = jnp.exp(scores - lse[..., None])
    o = jnp.einsum('bdhnqk,bdhnke->bdhnqe', p, vw.astype(jnp.float32))
    o = o.reshape(B, d, H, Lp, E)[:, :, :, :L].transpose(0, 3, 1, 2, 4).reshape(B, S, H, E)
    lse = lse.reshape(B, d, H, Lp)[..., :L].transpose(0, 3, 1, 2).reshape(B, S, H)
    return o, lse


def _dilated_attention_mixer(x, w_qkv, w_out):
    B, S, _ = x.shape
    qkv = (x @ w_qkv).reshape(B, S, N_DIL_GROUPS, 3, N_HEADS, HEAD_DIM)
    slopes = jnp.asarray(_alibi_slopes())
    outs, lses = [], []
    for g, (window, dil) in enumerate(DIL_CONFIGS):
        o, l = _dilated_group(qkv[:, :, g, 0], qkv[:, :, g, 1], qkv[:, :, g, 2],
                              window, dil, slopes[g])
        outs.append(o)
        lses.append(l)
    wts = jax.nn.softmax(jnp.stack(lses), axis=0)
    o = jnp.sum(wts[..., None] * jnp.stack(outs), axis=0)
    return o.reshape(B, S, ATTN_WIDTH).astype(x.dtype) @ w_out


def _fwd_setup_inputs(seed: int = 0) -> dict:
    key = jax.random.key(seed)
    ks = jax.random.split(key, 16)
    n_pool = (DEPTH + 1) // 2
    n_attn = DEPTH // 2
    f32 = jnp.float32

    def nrm(k, shape, scale):
        return jax.random.normal(k, shape, f32) * scale

    return {
        "x": jax.random.normal(ks[0], (BATCH, SEQ, D_MODEL), f32),
        "ffn1_w_gate": nrm(ks[1], (DEPTH, D_MODEL, D_FF), D_MODEL ** -0.5),
        "ffn1_w_up": nrm(ks[2], (DEPTH, D_MODEL, D_FF), D_MODEL ** -0.5),
        "ffn1_w_down": nrm(ks[3], (DEPTH, D_FF, D_MODEL), BETA * D_FF ** -0.5),
        "ffn2_w_gate": nrm(ks[4], (DEPTH, D_MODEL, D_FF), D_MODEL ** -0.5),
        "ffn2_w_up": nrm(ks[5], (DEPTH, D_MODEL, D_FF), D_MODEL ** -0.5),
        "ffn2_w_down": nrm(ks[6], (DEPTH, D_FF, D_MODEL), BETA * D_FF ** -0.5),
        "ln_gain": 1.0 + nrm(ks[7], (DEPTH, 3, D_MODEL), 0.02),
        "ln_bias": nrm(ks[8], (DEPTH, 3, D_MODEL), 0.02),
        "pool_w_in": nrm(ks[9], (n_pool, D_MODEL, D_MODEL), D_MODEL ** -0.5),
        "pool_w_group": nrm(ks[10], (n_pool, N_POOL_GROUPS, POOL_GROUP_DIM, POOL_GROUP_DIM),
                            POOL_GROUP_DIM ** -0.5),
        "pool_scale": 1.0 + nrm(ks[11], (n_pool, D_MODEL), 0.1),
        "pool_w_out": nrm(ks[12], (n_pool, D_MODEL, D_MODEL), BETA * D_MODEL ** -0.5),
        "attn_w_qkv": nrm(ks[13], (n_attn, D_MODEL, QKV_WIDTH), D_MODEL ** -0.5),
        "attn_w_out": nrm(ks[14], (n_attn, ATTN_WIDTH, D_MODEL), BETA * ATTN_WIDTH ** -0.5),
    }


def _fwd_reference(x, ffn1_w_gate, ffn1_w_up, ffn1_w_down, ffn2_w_gate, ffn2_w_up, ffn2_w_down,
              ln_gain, ln_bias, pool_w_in, pool_w_group, pool_scale, pool_w_out,
              attn_w_qkv, attn_w_out):
    for i in range(DEPTH):
        h = _swiglu(x, ffn1_w_gate[i], ffn1_w_up[i], ffn1_w_down[i])
        x = _layer_norm(ALPHA * x + MACARON_WEIGHT * h, ln_gain[i, 0], ln_bias[i, 0])
        if i % N_MIXERS == 0:
            p = i // N_MIXERS
            m = _pool_mixer(x, pool_w_in[p], pool_w_group[p], pool_scale[p], pool_w_out[p])
        else:
            a = i // N_MIXERS
            m = _dilated_attention_mixer(x, attn_w_qkv[a], attn_w_out[a])
        x = _layer_norm(ALPHA * x + m, ln_gain[i, 1], ln_bias[i, 1])
        h = _swiglu(x, ffn2_w_gate[i], ffn2_w_up[i], ffn2_w_down[i])
        x = _layer_norm(ALPHA * x + MACARON_WEIGHT * h, ln_gain[i, 2], ln_bias[i, 2])
    return x


import jax as _jax
import jax.numpy as _jnp

TWIN_FORMAT = 'train_step'
FWD_PARAMS = ['x', 'ffn1_w_gate', 'ffn1_w_up', 'ffn1_w_down', 'ffn2_w_gate', 'ffn2_w_up', 'ffn2_w_down', 'ln_gain', 'ln_bias', 'pool_w_in', 'pool_w_group', 'pool_scale', 'pool_w_out', 'attn_w_qkv', 'attn_w_out']
TWIN_WEIGHTS = ['ffn1_w_gate', 'ffn1_w_up', 'ffn1_w_down', 'ffn2_w_gate', 'ffn2_w_up', 'ffn2_w_down', 'ln_gain', 'ln_bias', 'pool_w_in', 'pool_w_group', 'pool_scale', 'pool_w_out', 'attn_w_qkv', 'attn_w_out']
TWIN_DIFF_INPUT = 'x'
TWIN_INPUTS = ['x', 'ffn1_w_gate', 'ffn1_w_up', 'ffn1_w_down', 'ffn2_w_gate', 'ffn2_w_up', 'ffn2_w_down', 'ln_gain', 'ln_bias', 'pool_w_in', 'pool_w_group', 'pool_scale', 'pool_w_out', 'attn_w_qkv', 'attn_w_out', 'loss_target', 'm_ffn1_w_gate', 'm_ffn1_w_up', 'm_ffn1_w_down', 'm_ffn2_w_gate', 'm_ffn2_w_up', 'm_ffn2_w_down', 'm_ln_gain', 'm_ln_bias', 'm_pool_w_in', 'm_pool_w_group', 'm_pool_scale', 'm_pool_w_out', 'm_attn_w_qkv', 'm_attn_w_out', 'v_ffn1_w_gate', 'v_ffn1_w_up', 'v_ffn1_w_down', 'v_ffn2_w_gate', 'v_ffn2_w_up', 'v_ffn2_w_down', 'v_ln_gain', 'v_ln_bias', 'v_pool_w_in', 'v_pool_w_group', 'v_pool_scale', 'v_pool_w_out', 'v_attn_w_qkv', 'v_attn_w_out']
TWIN_OUTPUTS = ['loss', 'grad_x', 'grad_ffn1_w_gate', 'grad_ffn1_w_up', 'grad_ffn1_w_down', 'grad_ffn2_w_gate', 'grad_ffn2_w_up', 'grad_ffn2_w_down', 'grad_ln_gain', 'grad_ln_bias', 'grad_pool_w_in', 'grad_pool_w_group', 'grad_pool_scale', 'grad_pool_w_out', 'grad_attn_w_qkv', 'grad_attn_w_out', 'delta_ffn1_w_gate', 'delta_ffn1_w_up', 'delta_ffn1_w_down', 'delta_ffn2_w_gate', 'delta_ffn2_w_up', 'delta_ffn2_w_down', 'delta_ln_gain', 'delta_ln_bias', 'delta_pool_w_in', 'delta_pool_w_group', 'delta_pool_scale', 'delta_pool_w_out', 'delta_attn_w_qkv', 'delta_attn_w_out', 'new_m_ffn1_w_gate', 'new_m_ffn1_w_up', 'new_m_ffn1_w_down', 'new_m_ffn2_w_gate', 'new_m_ffn2_w_up', 'new_m_ffn2_w_down', 'new_m_ln_gain', 'new_m_ln_bias', 'new_m_pool_w_in', 'new_m_pool_w_group', 'new_m_pool_scale', 'new_m_pool_w_out', 'new_m_attn_w_qkv', 'new_m_attn_w_out', 'new_v_ffn1_w_gate', 'new_v_ffn1_w_up', 'new_v_ffn1_w_down', 'new_v_ffn2_w_gate', 'new_v_ffn2_w_up', 'new_v_ffn2_w_down', 'new_v_ln_gain', 'new_v_ln_bias', 'new_v_pool_w_in', 'new_v_pool_w_group', 'new_v_pool_scale', 'new_v_pool_w_out', 'new_v_attn_w_qkv', 'new_v_attn_w_out']
TWIN_LEAF_KINDS = {'loss': 'loss', 'grad_x': 'grad_x', 'grad_ffn1_w_gate': 'grad_w', 'grad_ffn1_w_up': 'grad_w', 'grad_ffn1_w_down': 'grad_w', 'grad_ffn2_w_gate': 'grad_w', 'grad_ffn2_w_up': 'grad_w', 'grad_ffn2_w_down': 'grad_w', 'grad_ln_gain': 'grad_w', 'grad_ln_bias': 'grad_w', 'grad_pool_w_in': 'grad_w', 'grad_pool_w_group': 'grad_w', 'grad_pool_scale': 'grad_w', 'grad_pool_w_out': 'grad_w', 'grad_attn_w_qkv': 'grad_w', 'grad_attn_w_out': 'grad_w', 'delta_ffn1_w_gate': 'delta_w', 'delta_ffn1_w_up': 'delta_w', 'delta_ffn1_w_down': 'delta_w', 'delta_ffn2_w_gate': 'delta_w', 'delta_ffn2_w_up': 'delta_w', 'delta_ffn2_w_down': 'delta_w', 'delta_ln_gain': 'delta_w', 'delta_ln_bias': 'delta_w', 'delta_pool_w_in': 'delta_w', 'delta_pool_w_group': 'delta_w', 'delta_pool_scale': 'delta_w', 'delta_pool_w_out': 'delta_w', 'delta_attn_w_qkv': 'delta_w', 'delta_attn_w_out': 'delta_w', 'new_m_ffn1_w_gate': 'new_m', 'new_m_ffn1_w_up': 'new_m', 'new_m_ffn1_w_down': 'new_m', 'new_m_ffn2_w_gate': 'new_m', 'new_m_ffn2_w_up': 'new_m', 'new_m_ffn2_w_down': 'new_m', 'new_m_ln_gain': 'new_m', 'new_m_ln_bias': 'new_m', 'new_m_pool_w_in': 'new_m', 'new_m_pool_w_group': 'new_m', 'new_m_pool_scale': 'new_m', 'new_m_pool_w_out': 'new_m', 'new_m_attn_w_qkv': 'new_m', 'new_m_attn_w_out': 'new_m', 'new_v_ffn1_w_gate': 'new_v', 'new_v_ffn1_w_up': 'new_v', 'new_v_ffn1_w_down': 'new_v', 'new_v_ffn2_w_gate': 'new_v', 'new_v_ffn2_w_up': 'new_v', 'new_v_ffn2_w_down': 'new_v', 'new_v_ln_gain': 'new_v', 'new_v_ln_bias': 'new_v', 'new_v_pool_w_in': 'new_v', 'new_v_pool_w_group': 'new_v', 'new_v_pool_scale': 'new_v', 'new_v_pool_w_out': 'new_v', 'new_v_attn_w_qkv': 'new_v', 'new_v_attn_w_out': 'new_v'}


def _forward(args):
    return _fwd_reference(*[args[k] for k in FWD_PARAMS])


def _output_shape():
    def fwd():
        inp = _fwd_setup_inputs(0)
        return _fwd_reference(*[inp[k] for k in FWD_PARAMS])
    out = _jax.eval_shape(fwd)
    return out.shape, out.dtype

N_MICROBATCH = 1
ADAM_LR = 0.001
ADAM_B1 = 0.9
ADAM_B2 = 0.999
ADAM_EPS = 1e-08
ADAM_WD = 0.01
ADAM_STEP = 10
PER_EXAMPLE_BATCH_AXIS = {'x': 0, 'loss_target': 0}
SHARED_INPUTS = []
_WEIGHT_DTYPES = {'ffn1_w_gate': _jnp.float32, 'ffn1_w_up': _jnp.float32, 'ffn1_w_down': _jnp.float32, 'ffn2_w_gate': _jnp.float32, 'ffn2_w_up': _jnp.float32, 'ffn2_w_down': _jnp.float32, 'ln_gain': _jnp.float32, 'ln_bias': _jnp.float32, 'pool_w_in': _jnp.float32, 'pool_w_group': _jnp.float32, 'pool_scale': _jnp.float32, 'pool_w_out': _jnp.float32, 'attn_w_qkv': _jnp.float32, 'attn_w_out': _jnp.float32}
MOMENT_SCALE = {'ffn1_w_gate': 1.177804e-02, 'ffn1_w_up': 1.140523e-02, 'ffn1_w_down': 3.782114e-02, 'ffn2_w_gate': 1.167342e-02, 'ffn2_w_up': 1.131826e-02, 'ffn2_w_down': 3.751595e-02, 'ln_gain': 1.315254e+01, 'ln_bias': 7.811373e-01, 'pool_w_in': 5.404979e-02, 'pool_w_group': 5.436804e-02, 'pool_scale': 5.468567e-02, 'pool_w_out': 1.090586e-01, 'attn_w_qkv': 8.606495e-03, 'attn_w_out': 3.223584e-02}


def _to_microbatches(a, axis):
    t = _jnp.moveaxis(a, axis, 0)
    t = t.reshape((N_MICROBATCH, t.shape[0] // N_MICROBATCH) + t.shape[1:])
    return _jnp.moveaxis(t, 1, axis + 1)


def setup_inputs(seed: int = 0) -> dict:
    inp = _fwd_setup_inputs(seed)
    key = _jax.random.fold_in(_jax.random.key(seed), 7919)
    shape, _ = _output_shape()
    out = dict(inp)
    out["loss_target"] = _jax.random.normal(_jax.random.fold_in(key, 0), shape, _jnp.float32)
    for i, name in enumerate(TWIN_WEIGHTS):
        w = inp[name].astype(_jnp.float32)
        if MOMENT_SCALE is None:
            s = _jnp.sqrt(_jnp.mean(_jnp.square(w)) + 1e-30)
        else:
            s = MOMENT_SCALE[name]
        km, kv = _jax.random.split(_jax.random.fold_in(key, i + 1))
        out[name] = w
        out["m_" + name] = s * _jax.random.normal(km, w.shape, _jnp.float32)
        out["v_" + name] = (s * s) * _jax.random.uniform(kv, w.shape, _jnp.float32, 0.5, 1.5)
    if N_MICROBATCH > 1:
        for name, axis in PER_EXAMPLE_BATCH_AXIS.items():
            out[name] = _to_microbatches(out[name], axis)
    return {'x': out['x'], 'ffn1_w_gate': out['ffn1_w_gate'], 'ffn1_w_up': out['ffn1_w_up'], 'ffn1_w_down': out['ffn1_w_down'], 'ffn2_w_gate': out['ffn2_w_gate'], 'ffn2_w_up': out['ffn2_w_up'], 'ffn2_w_down': out['ffn2_w_down'], 'ln_gain': out['ln_gain'], 'ln_bias': out['ln_bias'], 'pool_w_in': out['pool_w_in'], 'pool_w_group': out['pool_w_group'], 'pool_scale': out['pool_scale'], 'pool_w_out': out['pool_w_out'], 'attn_w_qkv': out['attn_w_qkv'], 'attn_w_out': out['attn_w_out'], 'loss_target': out['loss_target'], 'm_ffn1_w_gate': out['m_ffn1_w_gate'], 'm_ffn1_w_up': out['m_ffn1_w_up'], 'm_ffn1_w_down': out['m_ffn1_w_down'], 'm_ffn2_w_gate': out['m_ffn2_w_gate'], 'm_ffn2_w_up': out['m_ffn2_w_up'], 'm_ffn2_w_down': out['m_ffn2_w_down'], 'm_ln_gain': out['m_ln_gain'], 'm_ln_bias': out['m_ln_bias'], 'm_pool_w_in': out['m_pool_w_in'], 'm_pool_w_group': out['m_pool_w_group'], 'm_pool_scale': out['m_pool_scale'], 'm_pool_w_out': out['m_pool_w_out'], 'm_attn_w_qkv': out['m_attn_w_qkv'], 'm_attn_w_out': out['m_attn_w_out'], 'v_ffn1_w_gate': out['v_ffn1_w_gate'], 'v_ffn1_w_up': out['v_ffn1_w_up'], 'v_ffn1_w_down': out['v_ffn1_w_down'], 'v_ffn2_w_gate': out['v_ffn2_w_gate'], 'v_ffn2_w_up': out['v_ffn2_w_up'], 'v_ffn2_w_down': out['v_ffn2_w_down'], 'v_ln_gain': out['v_ln_gain'], 'v_ln_bias': out['v_ln_bias'], 'v_pool_w_in': out['v_pool_w_in'], 'v_pool_w_group': out['v_pool_w_group'], 'v_pool_scale': out['v_pool_scale'], 'v_pool_w_out': out['v_pool_w_out'], 'v_attn_w_qkv': out['v_attn_w_qkv'], 'v_attn_w_out': out['v_attn_w_out']}


def _loss(weights, diff, rest, loss_target):
    with _jax.named_scope("forward"):
        args = {**rest, TWIN_DIFF_INPUT: diff, **{k: w.astype(_WEIGHT_DTYPES[k]) for k, w in weights.items()}}
        y = _forward(args)
    with _jax.named_scope("loss_head"):
        err = _jnp.square(y.astype(_jnp.float32) - loss_target)
        return 0.5 * _jnp.sum(_jnp.mean(err, axis=-1)) if err.ndim else 0.5 * err


def _adamw(w, g, m, v):
    m = ADAM_B1 * m + (1.0 - ADAM_B1) * g
    v = ADAM_B2 * v + (1.0 - ADAM_B2) * _jnp.square(g)
    m_hat = m / (1.0 - ADAM_B1 ** ADAM_STEP)
    v_hat = v / (1.0 - ADAM_B2 ** ADAM_STEP)
    delta = -ADAM_LR * (m_hat / (_jnp.sqrt(v_hat) + ADAM_EPS) + ADAM_WD * w)
    return delta, m, v


def reference(x, ffn1_w_gate, ffn1_w_up, ffn1_w_down, ffn2_w_gate, ffn2_w_up, ffn2_w_down, ln_gain, ln_bias, pool_w_in, pool_w_group, pool_scale, pool_w_out, attn_w_qkv, attn_w_out, loss_target, m_ffn1_w_gate, m_ffn1_w_up, m_ffn1_w_down, m_ffn2_w_gate, m_ffn2_w_up, m_ffn2_w_down, m_ln_gain, m_ln_bias, m_pool_w_in, m_pool_w_group, m_pool_scale, m_pool_w_out, m_attn_w_qkv, m_attn_w_out, v_ffn1_w_gate, v_ffn1_w_up, v_ffn1_w_down, v_ffn2_w_gate, v_ffn2_w_up, v_ffn2_w_down, v_ln_gain, v_ln_bias, v_pool_w_in, v_pool_w_group, v_pool_scale, v_pool_w_out, v_attn_w_qkv, v_attn_w_out):
    given = dict(x=x, ffn1_w_gate=ffn1_w_gate, ffn1_w_up=ffn1_w_up, ffn1_w_down=ffn1_w_down, ffn2_w_gate=ffn2_w_gate, ffn2_w_up=ffn2_w_up, ffn2_w_down=ffn2_w_down, ln_gain=ln_gain, ln_bias=ln_bias, pool_w_in=pool_w_in, pool_w_group=pool_w_group, pool_scale=pool_scale, pool_w_out=pool_w_out, attn_w_qkv=attn_w_qkv, attn_w_out=attn_w_out, loss_target=loss_target, m_ffn1_w_gate=m_ffn1_w_gate, m_ffn1_w_up=m_ffn1_w_up, m_ffn1_w_down=m_ffn1_w_down, m_ffn2_w_gate=m_ffn2_w_gate, m_ffn2_w_up=m_ffn2_w_up, m_ffn2_w_down=m_ffn2_w_down, m_ln_gain=m_ln_gain, m_ln_bias=m_ln_bias, m_pool_w_in=m_pool_w_in, m_pool_w_group=m_pool_w_group, m_pool_scale=m_pool_scale, m_pool_w_out=m_pool_w_out, m_attn_w_qkv=m_attn_w_qkv, m_attn_w_out=m_attn_w_out, v_ffn1_w_gate=v_ffn1_w_gate, v_ffn1_w_up=v_ffn1_w_up, v_ffn1_w_down=v_ffn1_w_down, v_ffn2_w_gate=v_ffn2_w_gate, v_ffn2_w_up=v_ffn2_w_up, v_ffn2_w_down=v_ffn2_w_down, v_ln_gain=v_ln_gain, v_ln_bias=v_ln_bias, v_pool_w_in=v_pool_w_in, v_pool_w_group=v_pool_w_group, v_pool_scale=v_pool_scale, v_pool_w_out=v_pool_w_out, v_attn_w_qkv=v_attn_w_qkv, v_attn_w_out=v_attn_w_out)
    weights = {n: given[n] for n in TWIN_WEIGHTS}
    shared = {n: given[n] for n in SHARED_INPUTS}
    per_example = {n: given[n] for n in ['x']}
    grad_fn = _jax.value_and_grad(_loss, argnums=(0, 1))

    def one_microbatch(ex, loss_target):
        ex = dict(ex)
        diff = ex.pop(TWIN_DIFF_INPUT)
        return grad_fn(weights, diff, {**shared, **ex}, loss_target)

    if N_MICROBATCH == 1:
        loss, (grad_w, grad_x) = one_microbatch(per_example, given["loss_target"])
    else:
        def body(carry, xs):
            loss_sum, grad_sum = carry
            l_k, (gw_k, gx_k) = one_microbatch(xs[0], xs[1])
            with _jax.named_scope("update"):
                return (loss_sum + l_k, _jax.tree.map(_jnp.add, grad_sum, gw_k)), gx_k

        init = (_jnp.zeros((), _jnp.float32), _jax.tree.map(_jnp.zeros_like, weights))
        (loss, grad_w), grad_x = _jax.lax.scan(body, init, (per_example, given["loss_target"]))
    with _jax.named_scope("update"):
        delta_w, new_m, new_v = {}, {}, {}
        for n in TWIN_WEIGHTS:
            delta_w[n], new_m[n], new_v[n] = _adamw(weights[n], grad_w[n], given["m_" + n], given["v_" + n])
    return (loss, grad_x, *[grad_w[n] for n in TWIN_WEIGHTS], *[delta_w[n] for n in TWIN_WEIGHTS],
            *[new_m[n] for n in TWIN_WEIGHTS], *[new_v[n] for n in TWIN_WEIGHTS])
```

```python
import functools
import math

import numpy as np
import jax
import jax.numpy as jnp
from jax import lax
from jax.experimental import pallas as pl
from jax.experimental.pallas import tpu as pltpu

F32 = jnp.float32
BF16 = jnp.bfloat16

DEPTH = 2
ALPHA = (2.0 * DEPTH) ** 0.25
MACARON_WEIGHT = 0.5
LN_EPS = 1e-5
MASK_VALUE = -1e30
POOL_WINDOWS = (2, 4, 8, 16)
POOL_PAD = 16
HEAD_DIM = 64
N_HEADS = 16
DIL_CONFIGS = ((128, 1), (512, 4), (2048, 16))
ATTN_R = 64
ATTN_BQ = 128
ATTN_W = ATTN_BQ + 2 * ATTN_R
LANES = 128
ADAM_LR = 0.001
ADAM_B1 = 0.9
ADAM_B2 = 0.999
ADAM_EPS = 1e-08
ADAM_WD = 0.01
ADAM_STEP = 10
N_CHIPS = 4
VMEM_LIMIT = 56 * 1024 * 1024
MESH = pl.DeviceIdType.MESH
ANY = pl.BlockSpec(memory_space=pl.ANY)


def _params(sem=None, vmem=VMEM_LIMIT):
    return pltpu.CompilerParams(dimension_semantics=sem, vmem_limit_bytes=vmem)


def _alibi_slopes():
    n = len(DIL_CONFIGS) * N_HEADS
    s = 2.0 ** (-8.0 * np.arange(1, n + 1) / n)
    return s.reshape(len(DIL_CONFIGS), N_HEADS).astype(np.float32)


def _ln_fwd(z, g, b):
    mu = jnp.mean(z, axis=-1, keepdims=True)
    zc = z - mu
    var = jnp.mean(zc * zc, axis=-1, keepdims=True)
    rstd = lax.rsqrt(var + LN_EPS)
    xhat = zc * rstd
    return xhat * g + b, xhat, rstd


def _dot(a, b):
    return jnp.dot(a, b, preferred_element_type=F32)


def _dot_nt(a, b):
    return lax.dot_general(a, b, (((1,), (1,)), ((), ())), preferred_element_type=F32)


def _dot_tn(a, b):
    return lax.dot_general(a, b, (((0,), (0,)), ((), ())), preferred_element_type=F32)


def mm_nn(a, b, out_dtype, tm=512):
    S, K = a.shape
    nb, Nb = b.shape[1] // 1024, 1024

    def body(a_ref, b_ref, o_ref):
        o_ref[...] = _dot(a_ref[...].astype(BF16), b_ref[...]).astype(out_dtype)

    return pl.pallas_call(
        body, name="mm_nn",
        grid=(S // tm, nb),
        in_specs=[pl.BlockSpec((tm, K), lambda i, j: (i, 0)), pl.BlockSpec((K, Nb), lambda i, j: (0, j))],
        out_specs=pl.BlockSpec((None, tm, Nb), lambda i, j: (j, i, 0)),
        out_shape=jax.ShapeDtypeStruct((nb, S, Nb), out_dtype),
        compiler_params=_params(("parallel", "arbitrary")),
    )(a, b)


def proj_ln(a, w, resid, gain, bias, tm=512):
    S, K = a.shape
    D = w.shape[1]

    def body(a_ref, w_ref, r_ref, g_ref, b_ref, y_ref, xh_ref, rs_ref):
        z = ALPHA * r_ref[...] + _dot(a_ref[...].astype(BF16), w_ref[...])
        y, xh, rs = _ln_fwd(z, g_ref[...], b_ref[...])
        y_ref[...] = y
        xh_ref[...] = xh
        rs_ref[...] = rs

    row = pl.BlockSpec((tm, D), lambda i: (i, 0))
    vec = pl.BlockSpec((1, D), lambda i: (0, 0))
    return pl.pallas_call(
        body, name="proj_ln",
        grid=(S // tm,),
        in_specs=[pl.BlockSpec((tm, K), lambda i: (i, 0)), pl.BlockSpec((K, D), lambda i: (0, 0)), row, vec, vec],
        out_specs=[row, row, pl.BlockSpec((tm, 1), lambda i: (i, 0))],
        out_shape=[jax.ShapeDtypeStruct((S, D), F32), jax.ShapeDtypeStruct((S, D), F32), jax.ShapeDtypeStruct((S, 1), F32)],
        compiler_params=_params(("parallel",)),
    )(a, w, resid, gain, bias)


def mm_nt(a, w, resid, a_blocked, out_dtype=F32, tm=512):
    if a_blocked:
        nk, S, Kb = a.shape
        a_spec = pl.BlockSpec((None, tm, Kb), lambda i, n: (n, i, 0))
    else:
        S, Kb = a.shape
        nk = 1
        a_spec = pl.BlockSpec((tm, Kb), lambda i, n: (i, 0))
    M = w.shape[0]
    has_resid = resid is not None

    def body(*refs):
        if has_resid:
            a_ref, w_ref, r_ref, o_ref, acc = refs
        else:
            a_ref, w_ref, o_ref, acc = refs
        n = pl.program_id(1)
        part = _dot_nt(a_ref[...].astype(BF16), w_ref[...])

        @pl.when(n == 0)
        def _():
            acc[...] = part

        @pl.when(n > 0)
        def _():
            acc[...] += part

        @pl.when(n == nk - 1)
        def _():
            out = acc[...]
            if has_resid:
                out = out + ALPHA * r_ref[...]
            o_ref[...] = out.astype(out_dtype)

    row = pl.BlockSpec((tm, M), lambda i, n: (i, 0))
    in_specs = [a_spec, pl.BlockSpec((M, Kb), lambda i, n: (0, n))] + ([row] if has_resid else [])
    args = (a, w) + ((resid,) if has_resid else ())
    return pl.pallas_call(
        body, name="mm_nt",
        grid=(S // tm, nk),
        in_specs=in_specs,
        out_specs=row,
        out_shape=jax.ShapeDtypeStruct((S, M), out_dtype),
        scratch_shapes=[pltpu.VMEM((tm, M), F32)],
        compiler_params=_params(("parallel", "arbitrary")),
    )(*args)


def mm_tn(a, bs, *, nblk, a_blocked, b_blocked, out_shape, out_block, out_index, scale=1.0, alias=None, tk=512, name="mm_tn"):
    S = a.shape[-2]
    M, N = a.shape[-1], bs[0].shape[-1]
    nb_out = len(bs)

    def spec(blocked, width):
        if blocked:
            return pl.BlockSpec((None, tk, width), lambda j, k: (j, k, 0))
        return pl.BlockSpec((tk, width), lambda j, k: (k, 0))

    def body(*refs):
        refs = refs[len(alias) if alias is not None else 0:]
        a_ref, b_refs = refs[0], refs[1:1 + nb_out]
        o_refs, accs = refs[1 + nb_out:1 + 2 * nb_out], refs[1 + 2 * nb_out:]
        k = pl.program_id(1)
        at = a_ref[...].astype(BF16)
        for b_ref, o_ref, acc in zip(b_refs, o_refs, accs):
            part = _dot_tn(at, b_ref[...].astype(BF16))

            @pl.when(k == 0)
            def _():
                acc[...] = part

            @pl.when(k > 0)
            def _():
                acc[...] += part

            @pl.when(k == S // tk - 1)
            def _():
                o_ref[...] = (scale * acc[...]).astype(BF16)

    out_spec = pl.BlockSpec(out_block, lambda j, k: out_index(j))
    n_alias = len(alias) if alias is not None else 0
    outs = pl.pallas_call(
        body, name=name,
        grid=(nblk, S // tk),
        in_specs=[ANY] * n_alias + [spec(a_blocked, M)] + [spec(b_blocked, N)] * nb_out,
        out_specs=[out_spec] * nb_out,
        out_shape=[jax.ShapeDtypeStruct(out_shape, BF16)] * nb_out,
        scratch_shapes=[pltpu.VMEM((M, N), F32)] * nb_out,
        input_output_aliases={i: i for i in range(n_alias)},
        compiler_params=_params(("parallel", "arbitrary")),
    )(*(tuple(alias) if alias is not None else ()), a, *bs)
    return list(outs)


def ffn_fwd(x, wg, wu, wd, layer, gain, bias, tm=512):
    S, D = x.shape
    nb, FB = wg.shape[0], wg.shape[3]

    def body(x_ref, wg_ref, wu_ref, wd_ref, g_ref, b_ref, y_ref, xh_ref, rs_ref, a_ref, u_ref, acc):
        j = pl.program_id(1)
        xb = x_ref[...].astype(BF16)
        a = _dot(xb, wg_ref[...])
        u = _dot(xb, wu_ref[...])
        a_ref[...] = a.astype(BF16)
        u_ref[...] = u.astype(BF16)
        h = a * jax.nn.sigmoid(a) * u
        part = _dot(h.astype(BF16), wd_ref[...])

        @pl.when(j == 0)
        def _():
            acc[...] = part

        @pl.when(j > 0)
        def _():
            acc[...] += part

        @pl.when(j == nb - 1)
        def _():
            z = ALPHA * x_ref[...] + MACARON_WEIGHT * acc[...]
            y, xh, rs = _ln_fwd(z, g_ref[...], b_ref[...])
            y_ref[...] = y
            xh_ref[...] = xh
            rs_ref[...] = rs

    row = pl.BlockSpec((tm, D), lambda i, j: (i, 0))
    vec = pl.BlockSpec((1, D), lambda i, j: (0, 0))
    w_in = pl.BlockSpec((None, None, D, FB), lambda i, j: (j, layer, 0, 0))
    w_out = pl.BlockSpec((None, None, FB, D), lambda i, j: (j, layer, 0, 0))
    act = pl.BlockSpec((None, tm, FB), lambda i, j: (j, i, 0))
    return pl.pallas_call(
        body, name="ffn_fwd",
        grid=(S // tm, nb),
        in_specs=[row, w_in, w_in, w_out, vec, vec],
        out_specs=[row, row, pl.BlockSpec((tm, 1), lambda i, j: (i, 0)), act, act],
        out_shape=[jax.ShapeDtypeStruct((S, D), F32), jax.ShapeDtypeStruct((S, D), F32), jax.ShapeDtypeStruct((S, 1), F32),
                   jax.ShapeDtypeStruct((nb, S, FB), BF16), jax.ShapeDtypeStruct((nb, S, FB), BF16)],
        scratch_shapes=[pltpu.VMEM((tm, D), F32)],
        compiler_params=_params(("parallel", "arbitrary")),
    )(x, wg, wu, wd, gain, bias)


def ffn_bwd(dz, a, u, wg, wu, wd, layer, tm=512):
    S, D = dz.shape
    nb, FB = wg.shape[0], wg.shape[3]

    def body(dz_ref, a_ref, u_ref, wg_ref, wu_ref, wd_ref, dx_ref, h_ref, da_ref, du_ref, acc):
        j = pl.program_id(1)
        dzb = (MACARON_WEIGHT * dz_ref[...]).astype(BF16)
        dh = _dot_nt(dzb, wd_ref[...])
        av = a_ref[...].astype(F32)
        uv = u_ref[...].astype(F32)
        s = jax.nn.sigmoid(av)
        silu = av * s
        h_ref[...] = (silu * uv).astype(BF16)
        da = (dh * uv * (s * (1.0 + av * (1.0 - s)))).astype(BF16)
        du = (dh * silu).astype(BF16)
        da_ref[...] = da
        du_ref[...] = du
        part = _dot_nt(da, wg_ref[...]) + _dot_nt(du, wu_ref[...])

        @pl.when(j == 0)
        def _():
            acc[...] = part

        @pl.when(j > 0)
        def _():
            acc[...] += part

        @pl.when(j == nb - 1)
        def _():
            dx_ref[...] = ALPHA * dz_ref[...] + acc[...]

    row = pl.BlockSpec((tm, D), lambda i, j: (i, 0))
    w_in = pl.BlockSpec((None, None, D, FB), lambda i, j: (j, layer, 0, 0))
    w_out = pl.BlockSpec((None, None, FB, D), lambda i, j: (j, layer, 0, 0))
    act = pl.BlockSpec((None, tm, FB), lambda i, j: (j, i, 0))
    act_shape = jax.ShapeDtypeStruct((nb, S, FB), BF16)
    return pl.pallas_call(
        body, name="ffn_bwd",
        grid=(S // tm, nb),
        in_specs=[row, act, act, w_in, w_in, w_out],
        out_specs=[row, act, act, act],
        out_shape=[jax.ShapeDtypeStruct((S, D), F32), act_shape, act_shape, act_shape],
        scratch_shapes=[pltpu.VMEM((tm, D), F32)],
        compiler_params=_params(("parallel", "arbitrary")),
    )(dz, a, u, wg, wu, wd)


def ln_bwd(dy, xhat, rstd, gain, tm=512):
    S, D = dy.shape

    def body(dy_ref, xh_ref, rs_ref, g_ref, dz_ref, dg_ref, db_ref):
        i = pl.program_id(0)
        dy = dy_ref[...]
        xh = xh_ref[...]
        dxh = dy * g_ref[...]
        m1 = jnp.mean(dxh, axis=-1, keepdims=True)
        m2 = jnp.mean(dxh * xh, axis=-1, keepdims=True)
        dz_ref[...] = rs_ref[...] * (dxh - m1 - xh * m2)
        dg = jnp.sum(dy * xh, axis=0, keepdims=True)
        db = jnp.sum(dy, axis=0, keepdims=True)

        @pl.when(i == 0)
        def _():
            dg_ref[...] = dg
            db_ref[...] = db

        @pl.when(i > 0)
        def _():
            dg_ref[...] += dg
            db_ref[...] += db

    row = pl.BlockSpec((tm, D), lambda i: (i, 0))
    vec = pl.BlockSpec((1, D), lambda i: (0, 0))
    return pl.pallas_call(
        body, name="ln_bwd",
        grid=(S // tm,),
        in_specs=[row, row, pl.BlockSpec((tm, 1), lambda i: (i, 0)), vec],
        out_specs=[row, vec, vec],
        out_shape=[jax.ShapeDtypeStruct((S, D), F32), jax.ShapeDtypeStruct((1, D), F32), jax.ShapeDtypeStruct((1, D), F32)],
        compiler_params=_params(("arbitrary",)),
    )(dy, xhat, rstd, gain)


def loss_head(y, target, tm=512):
    S, D = y.shape

    def body(y_ref, t_ref, dy_ref, l_ref):
        i = pl.program_id(0)
        e = y_ref[...] - t_ref[...]
        dy_ref[...] = e / D
        part = 0.5 * jnp.sum(jnp.mean(e * e, axis=-1, keepdims=True), axis=0, keepdims=True)

        @pl.when(i == 0)
        def _():
            l_ref[...] = part

        @pl.when(i > 0)
        def _():
            l_ref[...] += part

    row = pl.BlockSpec((tm, D), lambda i: (i, 0))
    return pl.pallas_call(
        body, name="loss_head",
        grid=(S // tm,),
        in_specs=[row, row],
        out_specs=[row, pl.BlockSpec((1, 1), lambda i: (0, 0))],
        out_shape=[jax.ShapeDtypeStruct((S, D), F32), jax.ShapeDtypeStruct((1, 1), F32)],
        compiler_params=_params(("arbitrary",)),
    )(y, target)


def _pool_window(xp, g):
    n = xp.shape[0]
    w = xp + pltpu.roll(xp, 1, 0)
    out = w
    for level, shift in enumerate((1, 2, 4), start=1):
        w = pltpu.roll(w, shift, 0) + pltpu.roll(w, n - shift, 0)
        out = jnp.where(g >= level, w, out)
    return out


def _pool_count(S, C, g):
    half = lax.shift_left(jnp.int32(1), g)
    t = lax.broadcasted_iota(jnp.int32, (S, C), 0)
    return (jnp.minimum(t + half, S) - jnp.maximum(t - half, 0)).astype(F32)


def pool_mix(u, wgrp, scale):
    S, D = u.shape
    G, C = wgrp.shape[0], wgrp.shape[1]

    def body(u_ref, w_ref, s_ref, mix_ref, v_ref, pad):
        g = pl.program_id(0)
        zeros = jnp.zeros((POOL_PAD, C), F32)
        pad[pl.ds(0, POOL_PAD), :] = zeros
        pad[pl.ds(POOL_PAD + S, POOL_PAD), :] = zeros
        pad[pl.ds(POOL_PAD, S), :] = u_ref[...]
        win = _pool_window(pad[...], g)[POOL_PAD:POOL_PAD + S]
        mixed = (win / _pool_count(S, C, g) - u_ref[...]).astype(BF16)
        mix_ref[...] = mixed
        v_ref[...] = _dot(mixed, w_ref[...]) * s_ref[...]

    col = pl.BlockSpec((S, C), lambda g: (0, g))
    return pl.pallas_call(
        body, name="pool_mix",
        grid=(G,),
        in_specs=[col, pl.BlockSpec((None, C, C), lambda g: (g, 0, 0)), pl.BlockSpec((1, C), lambda g: (0, g))],
        out_specs=[col, col],
        out_shape=[jax.ShapeDtypeStruct((S, D), BF16), jax.ShapeDtypeStruct((S, D), F32)],
        scratch_shapes=[pltpu.VMEM((S + 2 * POOL_PAD, C), F32)],
        compiler_params=_params(("arbitrary",)),
    )(u, wgrp, scale)


def pool_mix_bwd(dv, mixed, wgrp, scale):
    S, D = dv.shape
    G, C = wgrp.shape[0], wgrp.shape[1]

    def body(dv_ref, mix_ref, w_ref, s_ref, du_ref, dw_ref, ds_ref, pad):
        g = pl.program_id(0)
        mixed = mix_ref[...]
        dv = dv_ref[...]
        yg = _dot(mixed, w_ref[...])
        ds_ref[...] = jnp.sum(dv * yg, axis=0, keepdims=True)
        dyg = (dv * s_ref[...]).astype(BF16)
        dw_ref[...] = _dot_tn(mixed, dyg).astype(BF16)
        dmix = _dot_nt(dyg, w_ref[...])
        zeros = jnp.zeros((POOL_PAD, C), F32)
        pad[pl.ds(0, POOL_PAD), :] = zeros
        pad[pl.ds(POOL_PAD + S, POOL_PAD), :] = zeros
        pad[pl.ds(POOL_PAD, S), :] = dmix / _pool_count(S, C, g)
        win = _pool_window(pad[...], g)
        win = pltpu.roll(win, win.shape[0] - 1, 0)[POOL_PAD:POOL_PAD + S]
        du_ref[...] = win - dmix

    col = pl.BlockSpec((S, C), lambda g: (0, g))
    return pl.pallas_call(
        body, name="pool_mix_bwd",
        grid=(G,),
        in_specs=[col, col, pl.BlockSpec((None, C, C), lambda g: (g, 0, 0)), pl.BlockSpec((1, C), lambda g: (0, g))],
        out_specs=[col, pl.BlockSpec((None, C, C), lambda g: (g, 0, 0)), pl.BlockSpec((1, C), lambda g: (0, g))],
        out_shape=[jax.ShapeDtypeStruct((S, D), F32), jax.ShapeDtypeStruct((G, C, C), BF16), jax.ShapeDtypeStruct((1, D), F32)],
        scratch_shapes=[pltpu.VMEM((S + 2 * POOL_PAD, C), F32)],
        compiler_params=_params(("arbitrary",)),
    )(dv, mixed, wgrp, scale)


def _slope_table(group, dilation):
    s = _alibi_slopes()[group].reshape(N_HEADS // 2, 2, 1, 1) * float(dilation)
    return jnp.asarray(np.broadcast_to(s, (N_HEADS // 2, 2, 1, ATTN_W)).copy())


def _attn_geometry(L):
    a = lax.broadcasted_iota(jnp.int32, (ATTN_BQ, ATTN_W), 0)
    c = lax.broadcasted_iota(jnp.int32, (ATTN_BQ, ATTN_W), 1)
    rel = jnp.abs(c - ATTN_R - a)
    return c, rel <= ATTN_R, rel.astype(F32)


def _attn_scores(q, kw, head_mask, slope_row, absrel, band, c, i, L):
    qh = jnp.where(head_mask, q, jnp.zeros_like(q))
    s = _dot_nt(qh, kw) * (HEAD_DIM ** -0.5) - slope_row * absrel
    lo = ATTN_R - i * ATTN_BQ
    valid = band & (c >= lo) & (c < L + lo)
    return qh, jnp.where(valid, s, MASK_VALUE)


def _fill_padded(dst, src, L):
    zeros = jnp.zeros((ATTN_R, LANES), dst.dtype)
    dst[pl.ds(0, ATTN_R), :] = zeros
    dst[pl.ds(ATTN_R + L, ATTN_R), :] = zeros
    dst[pl.ds(ATTN_R, L), :] = src[...]


def attn_fwd(qkv, group, dilation):
    _, S, D = qkv.shape
    d = dilation
    L = S // d
    nq = L // ATTN_BQ
    ncol = D // LANES
    view = qkv.reshape(9, L, d * D)
    slopes = _slope_table(group, d)

    def body(q_ref, k_ref, v_ref, sl_ref, o_ref, lse_ref, kpad, vpad):
        _fill_padded(kpad, k_ref, L)
        _fill_padded(vpad, v_ref, L)
        c, band, absrel = _attn_geometry(L)
        lane = lax.broadcasted_iota(jnp.int32, (ATTN_BQ, LANES), 1)
        masks = (lane < HEAD_DIM, lane >= HEAD_DIM)

        def step(i, carry):
            r0 = pl.multiple_of(i * ATTN_BQ, ATTN_BQ)
            q = q_ref[pl.ds(r0, ATTN_BQ), :]
            kw = kpad[pl.ds(r0, ATTN_W), :]
            vw = vpad[pl.ds(r0, ATTN_W), :]
            outs, lses = [], []
            for h in range(2):
                _, s = _attn_scores(q, kw, masks[h], sl_ref[h], absrel, band, c, i, L)
                m = jnp.max(s, axis=-1, keepdims=True)
                e = jnp.exp(s - m)
                l = jnp.sum(e, axis=-1, keepdims=True)
                p = e * (1.0 / l)
                outs.append(_dot(p.astype(BF16), vw))
                lses.append(jnp.broadcast_to(m + jnp.log(l), (ATTN_BQ, LANES)))
            o_ref[pl.ds(r0, ATTN_BQ), :] = jnp.where(masks[0], outs[0], outs[1])
            lse_ref[pl.ds(r0, ATTN_BQ), :] = jnp.where(masks[0], lses[0], lses[1])
            return carry

        lax.fori_loop(0, nq, step, 0)

    def col(which):
        return pl.BlockSpec((None, L, LANES), lambda r, hp: (3 * group + which, 0, r * ncol + hp))

    out = pl.BlockSpec((L, LANES), lambda r, hp: (0, r * ncol + hp))
    o, lse = pl.pallas_call(
        body, name=f"attn_fwd_g{group}",
        grid=(d, ncol),
        in_specs=[col(0), col(1), col(2), pl.BlockSpec((None, 2, 1, ATTN_W), lambda r, hp: (hp, 0, 0, 0))],
        out_specs=[out, out],
        out_shape=[jax.ShapeDtypeStruct((L, d * D), F32), jax.ShapeDtypeStruct((L, d * D), F32)],
        scratch_shapes=[pltpu.VMEM((L + 2 * ATTN_R, LANES), BF16), pltpu.VMEM((L + 2 * ATTN_R, LANES), BF16)],
        compiler_params=_params(("parallel", "parallel")),
    )(view, view, view, slopes)
    return o.reshape(S, D), lse.reshape(S, D)


def attn_combine(os, lses, tm=512):
    S, D = os[0].shape
    n = len(os)

    def body(*refs):
        o_refs, l_refs, out_ref, lse_ref = refs[:n], refs[n:2 * n], refs[2 * n], refs[2 * n + 1]
        ls = [r[...] for r in l_refs]
        m = functools.reduce(jnp.maximum, ls)
        es = [jnp.exp(l - m) for l in ls]
        tot = functools.reduce(lambda x, y: x + y, es)
        inv = 1.0 / tot
        out_ref[...] = functools.reduce(lambda x, y: x + y, [(e * inv) * r[...] for e, r in zip(es, o_refs)])
        lse_ref[...] = m + jnp.log(tot)

    row = pl.BlockSpec((tm, D), lambda i: (i, 0))
    return pl.pallas_call(
        body, name="attn_combine",
        grid=(S // tm,),
        in_specs=[row] * (2 * n),
        out_specs=[row, row],
        out_shape=[jax.ShapeDtypeStruct((S, D), F32), jax.ShapeDtypeStruct((S, D), F32)],
        compiler_params=_params(("parallel",)),
    )(*os, *lses)


def attn_bwd(qkv, do, o, lse, group, dilation, dqkv):
    _, S, D = qkv.shape
    d = dilation
    L = S // d
    nq = L // ATTN_BQ
    ncol = D // LANES
    view = qkv.reshape(9, L, d * D)
    slopes = _slope_table(group, d)
    has_alias = dqkv is not None

    def body(*refs):
        if has_alias:
            refs = refs[1:]
        q_ref, k_ref, v_ref, do_ref, o_ref, lse_ref, sl_ref, dx_ref, kpad, vpad, dkacc, dvacc = refs
        _fill_padded(kpad, k_ref, L)
        _fill_padded(vpad, v_ref, L)
        dkacc[...] = jnp.zeros_like(dkacc)
        dvacc[...] = jnp.zeros_like(dvacc)
        c, band, absrel = _attn_geometry(L)
        lane = lax.broadcasted_iota(jnp.int32, (ATTN_BQ, LANES), 1)
        masks = (lane < HEAD_DIM, lane >= HEAD_DIM)

        def step(i, carry):
            r0 = pl.multiple_of(i * ATTN_BQ, ATTN_BQ)
            rows = pl.ds(r0, ATTN_BQ)
            win = pl.ds(r0, ATTN_W)
            q = q_ref[rows, :]
            kw = kpad[win, :]
            vw = vpad[win, :]
            dov = do_ref[rows, :]
            prod = dov * o_ref[rows, :]
            dob = dov.astype(BF16)
            lse_v = lse_ref[rows, :]
            dq = jnp.zeros((ATTN_BQ, LANES), F32)
            for h in range(2):
                qh, s = _attn_scores(q, kw, masks[h], sl_ref[h], absrel, band, c, i, L)
                lse_h = lse_v[:, h * HEAD_DIM:h * HEAD_DIM + 1]
                p = jnp.exp(s - lse_h)
                dterm = jnp.sum(jnp.where(masks[h], prod, 0.0), axis=-1, keepdims=True)
                doh = jnp.where(masks[h], dob, jnp.zeros_like(dob))
                dp = _dot_nt(doh, vw)
                ds = (p * (dp - dterm) * (HEAD_DIM ** -0.5)).astype(BF16)
                dvacc[win, :] += _dot_tn(p.astype(BF16), doh)
                dkacc[win, :] += _dot_tn(ds, qh)
                dq = dq + jnp.where(masks[h], _dot(ds, kw), 0.0)
            dx_ref[0, rows, :] = dq.astype(BF16)
            return carry

        lax.fori_loop(0, nq, step, 0)
        dx_ref[1] = dkacc[pl.ds(ATTN_R, L), :].astype(BF16)
        dx_ref[2] = dvacc[pl.ds(ATTN_R, L), :].astype(BF16)

    def col(which):
        return pl.BlockSpec((None, L, LANES), lambda r, hp: (3 * group + which, 0, r * ncol + hp))

    act = pl.BlockSpec((L, LANES), lambda r, hp: (0, r * ncol + hp))
    n_groups = len(DIL_CONFIGS)
    args = (view, view, view, do.reshape(L, d * D), o.reshape(L, d * D), lse.reshape(L, d * D), slopes)
    if has_alias:
        args = (dqkv.reshape(n_groups, 3, L, d * D),) + args
    out = pl.pallas_call(
        body, name=f"attn_bwd_g{group}",
        grid=(d, ncol),
        in_specs=([ANY] if has_alias else []) + [col(0), col(1), col(2), act, act, act,
                                                 pl.BlockSpec((None, 2, 1, ATTN_W), lambda r, hp: (hp, 0, 0, 0))],
        out_specs=pl.BlockSpec((None, 3, L, LANES), lambda r, hp: (group, 0, 0, r * ncol + hp)),
        out_shape=jax.ShapeDtypeStruct((n_groups, 3, L, d * D), BF16),
        scratch_shapes=[pltpu.VMEM((L + 2 * ATTN_R, LANES), BF16), pltpu.VMEM((L + 2 * ATTN_R, LANES), BF16),
                        pltpu.VMEM((L + 2 * ATTN_R, LANES), F32), pltpu.VMEM((L + 2 * ATTN_R, LANES), F32)],
        input_output_aliases={0: 0} if has_alias else {},
        compiler_params=_params(("parallel", "parallel")),
    )(*args)
    return out.reshape(3 * n_groups, S, D)


TILE_ELEMS = 256 * 1024


def _row_tile(R, C):
    if R * C <= TILE_ELEMS or R % 16:
        return R
    return max(t for t in range(16, R + 1, 16) if R % t == 0 and (t * C <= TILE_ELEMS or t == 16))


def pair_sum(core, g, recv):
    _, _, R, C = g.shape
    tr = _row_tile(R, C)

    def body(c_ref, g_ref, r_ref, o_ref):
        o_ref[...] = (g_ref[...].astype(F32) + r_ref[...].astype(F32)).astype(BF16)

    blk = pl.BlockSpec((None, tr, C), lambda d, i, c_ref: (d, i, 0))
    return pl.pallas_call(
        body, name="pair_sum",
        grid_spec=pltpu.PrefetchScalarGridSpec(
            num_scalar_prefetch=1, grid=(N_CHIPS, R // tr),
            in_specs=[pl.BlockSpec((None, None, tr, C), lambda d, i, c_ref: (d, c_ref[0], i, 0)), blk],
            out_specs=blk),
        out_shape=jax.ShapeDtypeStruct((N_CHIPS, R, C), BF16),
        compiler_params=_params(("parallel", "parallel")),
    )(core, g, recv)


def chip_sum(parts):
    _, R, C = parts.shape
    tr = _row_tile(R, C)

    def body(p_ref, o_ref):
        acc = p_ref[0].astype(F32)
        for s in range(1, N_CHIPS):
            acc = acc + p_ref[s].astype(F32)
        o_ref[...] = acc

    return pl.pallas_call(
        body, name="chip_sum",
        grid=(R // tr,),
        in_specs=[pl.BlockSpec((N_CHIPS, tr, C), lambda i: (0, i, 0))],
        out_specs=pl.BlockSpec((tr, C), lambda i: (i, 0)),
        out_shape=jax.ShapeDtypeStruct((R, C), F32),
        compiler_params=_params(("parallel",)),
    )(parts)


def adamw(w, g, m, v):
    H, R, C = w.shape
    tr = _row_tile(R, C)

    def body(w_ref, g_ref, m_ref, v_ref, d_ref, nm_ref, nv_ref):
        g = g_ref[...]
        m = ADAM_B1 * m_ref[...] + (1.0 - ADAM_B1) * g
        v = ADAM_B2 * v_ref[...] + (1.0 - ADAM_B2) * (g * g)
        m_hat = m / (1.0 - ADAM_B1 ** ADAM_STEP)
        v_hat = v / (1.0 - ADAM_B2 ** ADAM_STEP)
        d_ref[...] = -ADAM_LR * (m_hat / (jnp.sqrt(v_hat) + ADAM_EPS) + ADAM_WD * w_ref[...])
        nm_ref[...] = m
        nv_ref[...] = v

    blk = pl.BlockSpec((None, tr, C), lambda h, i: (h, i, 0))
    shape = jax.ShapeDtypeStruct((H, R, C), F32)
    return pl.pallas_call(
        body, name="adamw",
        grid=(H, R // tr),
        in_specs=[blk] * 4,
        out_specs=[blk] * 3,
        out_shape=[shape] * 3,
        compiler_params=_params(("parallel", "parallel")),
    )(w, g, m, v)


def _place():
    return lax.axis_index("x"), lax.axis_index("y"), lax.axis_index("c")


def _other_chips(x, y):
    return [(2 * (1 - x) + y, (1 - x, y)), (2 * x + (1 - y), (x, 1 - y)), (2 * (1 - x) + (1 - y), (1 - x, 1 - y))]


def all_gather_shards(shards):
    n = len(shards)

    def body(*refs):
        ins, outs = refs[:n], refs[n:2 * n]
        send_sems, recv_sems, local_sems = refs[2 * n:]
        x, y, c = _place()
        me = 2 * x + y
        sibling = (x, y, 1 - c)
        chips = _other_chips(x, y)

        def copy(a, k, src, dst, to):
            return pltpu.make_async_remote_copy(src_ref=src, dst_ref=dst, send_sem=send_sems.at[a, k], recv_sem=recv_sems.at[a, k],
                                                device_id=to, device_id_type=MESH)

        started = []
        for a in range(n):
            mine = pltpu.make_async_copy(ins[a], outs[a].at[me], local_sems.at[a])
            mine.start()
            started.append(mine)
        sends = []
        for a in range(n):
            for k, (_, (px, py)) in enumerate(chips):
                cp = copy(a, k, ins[a].at[c], outs[a].at[me, c], (px, py, c))
                cp.start()
                sends.append(cp)
        for a in range(n):
            for k, (chip, _) in enumerate(chips):
                landed = outs[a].at[chip, c]
                copy(a, k, landed, landed, sibling).wait_recv()
                cp = copy(a, 3 + k, landed, landed, sibling)
                cp.start()
                sends.append(cp)
        for a in range(n):
            for k, (chip, _) in enumerate(chips):
                other = outs[a].at[chip, 1 - c]
                copy(a, 3 + k, other, other, sibling).wait_recv()
        for cp in sends:
            cp.wait_send()
        for cp in started:
            cp.wait()

    return pl.pallas_call(
        body, name="all_gather_shards",
        in_specs=[ANY] * n,
        out_specs=[ANY] * n,
        out_shape=[jax.ShapeDtypeStruct((N_CHIPS,) + s.shape, s.dtype) for s in shards],
        scratch_shapes=[pltpu.SemaphoreType.DMA((n, 6)), pltpu.SemaphoreType.DMA((n, 6)), pltpu.SemaphoreType.DMA((n,))],
        compiler_params=pltpu.CompilerParams(has_side_effects=True),
    )(*shards)


def sibling_exchange_halves(grads):
    n = len(grads)

    def body(*refs):
        ins, outs = refs[:n], refs[n:2 * n]
        send_sems, recv_sems = refs[2 * n:]
        x, y, c = _place()
        sibling = (x, y, 1 - c)
        copies = [pltpu.make_async_remote_copy(src_ref=ins[a].at[:, 1 - c], dst_ref=outs[a], send_sem=send_sems.at[a],
                                               recv_sem=recv_sems.at[a], device_id=sibling, device_id_type=MESH) for a in range(n)]
        for cp in copies:
            cp.start()
        for cp in copies:
            cp.wait()

    return pl.pallas_call(
        body, name="sibling_exchange_halves",
        in_specs=[ANY] * n,
        out_specs=[ANY] * n,
        out_shape=[jax.ShapeDtypeStruct((N_CHIPS,) + g.shape[2:], g.dtype) for g in grads],
        scratch_shapes=[pltpu.SemaphoreType.DMA((n,)), pltpu.SemaphoreType.DMA((n,))],
        compiler_params=pltpu.CompilerParams(has_side_effects=True),
    )(*grads)


def chip_exchange(sums):
    n = len(sums)

    def body(*refs):
        ins, outs = refs[:n], refs[n:2 * n]
        send_sems, recv_sems, local_sems = refs[2 * n:]
        x, y, c = _place()
        me = 2 * x + y
        chips = _other_chips(x, y)
        copies = []
        for a in range(n):
            mine = pltpu.make_async_copy(ins[a].at[me], outs[a].at[me], local_sems.at[a])
            mine.start()
            copies.append(mine)
            for k, (chip, (px, py)) in enumerate(chips):
                cp = pltpu.make_async_remote_copy(src_ref=ins[a].at[chip], dst_ref=outs[a].at[me], send_sem=send_sems.at[a, k],
                                                  recv_sem=recv_sems.at[a, k], device_id=(px, py, c), device_id_type=MESH)
                cp.start()
                copies.append(cp)
        for cp in copies:
            cp.wait()

    return pl.pallas_call(
        body, name="chip_exchange",
        in_specs=[ANY] * n,
        out_specs=[ANY] * n,
        out_shape=[jax.ShapeDtypeStruct(s.shape, s.dtype) for s in sums],
        scratch_shapes=[pltpu.SemaphoreType.DMA((n, 3)), pltpu.SemaphoreType.DMA((n, 3)), pltpu.SemaphoreType.DMA((n,))],
        compiler_params=pltpu.CompilerParams(has_side_effects=True),
    )(*sums)


def sibling_share(halves):
    n = len(halves)

    def body(*refs):
        ins, outs = refs[:n], refs[n:2 * n]
        send_sems, recv_sems, local_sems = refs[2 * n:]
        x, y, c = _place()
        sibling = (x, y, 1 - c)
        copies = []
        for a in range(n):
            mine = pltpu.make_async_copy(ins[a], outs[a].at[c], local_sems.at[a])
            mine.start()
            copies.append(mine)
            cp = pltpu.make_async_remote_copy(src_ref=ins[a], dst_ref=outs[a].at[c], send_sem=send_sems.at[a],
                                              recv_sem=recv_sems.at[a], device_id=sibling, device_id_type=MESH)
            cp.start()
            copies.append(cp)
        for cp in copies:
            cp.wait()

    return pl.pallas_call(
        body, name="sibling_share",
        in_specs=[ANY] * n,
        out_specs=[ANY] * n,
        out_shape=[jax.ShapeDtypeStruct((2,) + h.shape, h.dtype) for h in halves],
        scratch_shapes=[pltpu.SemaphoreType.DMA((n,)), pltpu.SemaphoreType.DMA((n,)), pltpu.SemaphoreType.DMA((n,))],
        compiler_params=pltpu.CompilerParams(has_side_effects=True),
    )(*halves)


def all_reduce_small(v):
    R, C = v.shape
    n_dev = 8

    def body(v_ref, o_ref, buf, send_sems, recv_sems):
        x, y, c = _place()
        me = 4 * x + 2 * y + c
        buf[me] = v_ref[...]
        copies = []
        for rel in range(1, n_dev):
            fx, fy, fc = rel >> 2, (rel >> 1) & 1, rel & 1
            peer = (x ^ fx, y ^ fy, c ^ fc)
            cp = pltpu.make_async_remote_copy(src_ref=v_ref, dst_ref=buf.at[me], send_sem=send_sems.at[rel - 1],
                                              recv_sem=recv_sems.at[rel - 1], device_id=peer, device_id_type=MESH)
            cp.start()
            copies.append(cp)
        for cp in copies:
            cp.wait()
        acc = buf[0]
        for k in range(1, n_dev):
            acc = acc + buf[k]
        o_ref[...] = acc

    return pl.pallas_call(
        body, name="all_reduce_small",
        in_specs=[pl.BlockSpec(memory_space=pltpu.VMEM)],
        out_specs=pl.BlockSpec(memory_space=pltpu.VMEM),
        out_shape=jax.ShapeDtypeStruct((R, C), F32),
        scratch_shapes=[pltpu.VMEM((n_dev, R, C), F32), pltpu.SemaphoreType.DMA((n_dev - 1,)), pltpu.SemaphoreType.DMA((n_dev - 1,))],
        compiler_params=pltpu.CompilerParams(has_side_effects=True),
    )(v)


WEIGHT_NAMES = ("ffn1_w_gate", "ffn1_w_up", "ffn1_w_down", "ffn2_w_gate", "ffn2_w_up", "ffn2_w_down", "ln_gain", "ln_bias",
                "pool_w_in", "pool_w_group", "pool_scale", "pool_w_out", "attn_w_qkv", "attn_w_out")
MATRIX_NAMES = ("ffn1_w_gate", "ffn1_w_up", "ffn1_w_down", "ffn2_w_gate", "ffn2_w_up", "ffn2_w_down",
                "pool_w_in", "pool_w_group", "pool_w_out", "attn_w_qkv", "attn_w_out")


def _halves(w):
    return w.reshape(2, -1, w.shape[-1])


def kernel(x, ffn1_w_gate, ffn1_w_up, ffn1_w_down, ffn2_w_gate, ffn2_w_up, ffn2_w_down, ln_gain, ln_bias, pool_w_in, pool_w_group, pool_scale, pool_w_out, attn_w_qkv, attn_w_out, loss_target, m_ffn1_w_gate, m_ffn1_w_up, m_ffn1_w_down, m_ffn2_w_gate, m_ffn2_w_up, m_ffn2_w_down, m_ln_gain, m_ln_bias, m_pool_w_in, m_pool_w_group, m_pool_scale, m_pool_w_out, m_attn_w_qkv, m_attn_w_out, v_ffn1_w_gate, v_ffn1_w_up, v_ffn1_w_down, v_ffn2_w_gate, v_ffn2_w_up, v_ffn2_w_down, v_ln_gain, v_ln_bias, v_pool_w_in, v_pool_w_group, v_pool_scale, v_pool_w_out, v_attn_w_qkv, v_attn_w_out):
    weights = dict(zip(WEIGHT_NAMES, (ffn1_w_gate, ffn1_w_up, ffn1_w_down, ffn2_w_gate, ffn2_w_up, ffn2_w_down, ln_gain, ln_bias,
                                      pool_w_in, pool_w_group, pool_scale, pool_w_out, attn_w_qkv, attn_w_out)))
    moms = dict(zip(WEIGHT_NAMES, (m_ffn1_w_gate, m_ffn1_w_up, m_ffn1_w_down, m_ffn2_w_gate, m_ffn2_w_up, m_ffn2_w_down, m_ln_gain,
                                   m_ln_bias, m_pool_w_in, m_pool_w_group, m_pool_scale, m_pool_w_out, m_attn_w_qkv, m_attn_w_out)))
    vels = dict(zip(WEIGHT_NAMES, (v_ffn1_w_gate, v_ffn1_w_up, v_ffn1_w_down, v_ffn2_w_gate, v_ffn2_w_up, v_ffn2_w_down, v_ln_gain,
                                   v_ln_bias, v_pool_w_in, v_pool_w_group, v_pool_scale, v_pool_w_out, v_attn_w_qkv, v_attn_w_out)))
    S, D = x.shape[1], x.shape[2]
    FB = ffn1_w_gate.shape[2]
    QKV = attn_w_qkv.shape[2] * N_CHIPS
    G, CB = pool_w_group.shape[1], pool_w_group.shape[2]
    C = pool_w_group.shape[3]
    cx, cy, cc = _place()
    chip = 2 * cx + cy
    xs = x.reshape(S, D)
    target = loss_target.reshape(S, D)

    ln_rows = jnp.concatenate([ln_gain, ln_bias, jnp.zeros((DEPTH, 2, ln_gain.shape[2]), F32)], axis=1)
    gathered = all_gather_shards([_halves(weights[n].astype(BF16)) for n in MATRIX_NAMES] + [ln_rows])
    full = dict(zip(MATRIX_NAMES, gathered[:-1]))
    wg = {f: full[f + "_w_gate"].reshape(N_CHIPS, DEPTH, D, FB) for f in ("ffn1", "ffn2")}
    wu = {f: full[f + "_w_up"].reshape(N_CHIPS, DEPTH, D, FB) for f in ("ffn1", "ffn2")}
    wd = {f: full[f + "_w_down"].reshape(N_CHIPS, DEPTH, FB, D) for f in ("ffn1", "ffn2")}
    w_pool_in = full["pool_w_in"].reshape(D, D)
    w_pool_out = full["pool_w_out"].reshape(D, D)
    w_attn_out = full["attn_w_out"].reshape(D, D)
    w_group = full["pool_w_group"].reshape(N_CHIPS, G, CB, C).transpose(1, 0, 2, 3).reshape(G, N_CHIPS * CB, C)
    w_qkv = full["attn_w_qkv"].reshape(N_CHIPS, D, QKV // N_CHIPS).transpose(1, 0, 2).reshape(D, QKV)
    ln_full = gathered[-1].transpose(1, 2, 0, 3).reshape(DEPTH, 8, D)
    gain = lambda i, k: ln_full[i, k].reshape(1, D)
    bias = lambda i, k: ln_full[i, 3 + k].reshape(1, D)

    saved = []
    y = xs
    for i in range(DEPTH):
        y_in = y
        y, xh, rs, a, u = ffn_fwd(y_in, wg["ffn1"], wu["ffn1"], wd["ffn1"], i, gain(i, 0), bias(i, 0))
        f1 = (y_in, xh, rs, a, u)
        y_mid = y
        if i % 2 == 0:
            pu = mm_nn(y_mid, w_pool_in, F32)[0]
            mixed, pv = pool_mix(pu, w_group, pool_scale)
            y, xh, rs = proj_ln(pv, w_pool_out, y_mid, gain(i, 1), bias(i, 1))
            mix = (y_mid, xh, rs, mixed, pv)
        else:
            qkv = mm_nn(y_mid, w_qkv, BF16)
            parts = [attn_fwd(qkv, g, dil) for g, (_, dil) in enumerate(DIL_CONFIGS)]
            ao, lse = attn_combine([p[0] for p in parts], [p[1] for p in parts])
            y, xh, rs = proj_ln(ao, w_attn_out, y_mid, gain(i, 1), bias(i, 1))
            mix = (y_mid, xh, rs, qkv, ao, lse)
        y_in2 = y
        y, xh, rs, a, u = ffn_fwd(y_in2, wg["ffn2"], wu["ffn2"], wd["ffn2"], i, gain(i, 2), bias(i, 2))
        f2 = (y_in2, xh, rs, a, u)
        saved.append((f1, mix, f2))

    dy, loss_part = loss_head(y, target)
    loss = lax.psum(loss_part[0, 0], ("x", "y", "c"))

    grads = {}
    dgain = [[None] * 3 for _ in range(DEPTH)]
    dbias = [[None] * 3 for _ in range(DEPTH)]
    dscale = None

    def ffn_backward(name, i, dy, state):
        y_in, xh, rs, a, u = state
        k = 0 if name == "ffn1" else 2
        dz, dgain[i][k], dbias[i][k] = ln_bwd(dy, xh, rs, gain(i, k))
        dx, h, da, du = ffn_bwd(dz, a, u, wg[name], wu[name], wd[name], i)
        prev = [grads.get(name + s) for s in ("_w_gate", "_w_up", "_w_down")]
        in_shape, in_block = (N_CHIPS, DEPTH, D, FB), (None, None, D, FB)
        out_shape, out_block = (N_CHIPS, DEPTH, FB, D), (None, None, FB, D)
        index = lambda j: (j, i, 0, 0)
        g_gate, g_up = mm_tn(y_in, [da, du], nblk=N_CHIPS, a_blocked=False, b_blocked=True, out_shape=in_shape, out_block=in_block,
                             out_index=index, alias=prev[:2] if prev[0] is not None else None, name="ffn_wgrad_in")
        (g_down,) = mm_tn(h, [dz], nblk=N_CHIPS, a_blocked=True, b_blocked=False, out_shape=out_shape, out_block=out_block,
                          out_index=index, scale=MACARON_WEIGHT, alias=prev[2:] if prev[2] is not None else None, name="ffn_wgrad_out")
        grads[name + "_w_gate"], grads[name + "_w_up"], grads[name + "_w_down"] = g_gate, g_up, g_down
        return dx

    def square_grad(a, b):
        return mm_tn(a, [b], nblk=1, a_blocked=False, b_blocked=False, out_shape=(D, D), out_block=(D, D),
                     out_index=lambda j: (0, 0), name="square_wgrad")[0]

    for i in reversed(range(DEPTH)):
        f1, mix, f2 = saved[i]
        dy = ffn_backward("ffn2", i, dy, f2)
        dz, dgain[i][1], dbias[i][1] = ln_bwd(dy, mix[1], mix[2], gain(i, 1))
        if i % 2 == 0:
            y_mid, _, _, mixed, pv = mix
            grads["pool_w_out"] = square_grad(pv, dz)
            dv = mm_nt(dz, w_pool_out, None, a_blocked=False)
            du, dwg, dscale = pool_mix_bwd(dv, mixed, w_group, pool_scale)
            grads["pool_w_group"] = dwg.reshape(G, N_CHIPS, CB, C).transpose(1, 0, 2, 3)
            grads["pool_w_in"] = square_grad(y_mid, du)
            dy = mm_nt(du, w_pool_in, dz, a_blocked=False)
        else:
            y_mid, _, _, qkv, ao, lse = mix
            grads["attn_w_out"] = square_grad(ao, dz)
            dao = mm_nt(dz, w_attn_out, None, a_blocked=False)
            dqkv = None
            for g, (_, dil) in enumerate(DIL_CONFIGS):
                dqkv = attn_bwd(qkv, dao, ao, lse, g, dil, dqkv)
            n_blocks = QKV // D
            g_qkv = mm_tn(y_mid, [dqkv], nblk=n_blocks, a_blocked=False, b_blocked=True, out_shape=(D, QKV), out_block=(D, D),
                          out_index=lambda j: (0, j), name="qkv_wgrad")[0]
            grads["attn_w_qkv"] = g_qkv.reshape(D, N_CHIPS, QKV // N_CHIPS).transpose(1, 0, 2)
            dy = mm_nt(dqkv, w_qkv, dz, a_blocked=True)
        dy = ffn_backward("ffn1", i, dy, f1)
    grad_x = dy.reshape(x.shape)

    def as_blocks(g):
        return g.reshape(N_CHIPS, 2, -1, g.shape[-1])

    blocks = [as_blocks(grads[n]) for n in MATRIX_NAMES]
    from_sibling = sibling_exchange_halves(blocks)
    core = cc.reshape(1).astype(jnp.int32)
    pair_sums = [pair_sum(core, b, r) for b, r in zip(blocks, from_sibling)]
    from_chips = chip_exchange(pair_sums)
    reduced = sibling_share([chip_sum(p) for p in from_chips])
    grad_w = {n: r.reshape(weights[n].shape) for n, r in zip(MATRIX_NAMES, reduced)}

    small = jnp.concatenate([jnp.concatenate(dgain[i] + dbias[i], axis=0) for i in range(DEPTH)] + [dscale, jnp.zeros((3, D), F32)], axis=0)
    small = all_reduce_small(small)
    per_layer = small[:6 * DEPTH].reshape(DEPTH, 6, D)
    cols = D // N_CHIPS
    grad_w["ln_gain"] = lax.dynamic_slice_in_dim(per_layer[:, 0:3], chip * cols, cols, axis=2)
    grad_w["ln_bias"] = lax.dynamic_slice_in_dim(per_layer[:, 3:6], chip * cols, cols, axis=2)
    grad_w["pool_scale"] = small[6 * DEPTH:6 * DEPTH + 1]

    delta, new_m, new_v = {}, {}, {}
    for n in WEIGHT_NAMES:
        shape = weights[n].shape
        as3 = (lambda t: _halves(t)) if n in MATRIX_NAMES else (lambda t: t.reshape(1, -1, t.shape[-1]))
        d_, m_, v_ = adamw(as3(weights[n]), as3(grad_w[n]), as3(moms[n]), as3(vels[n]))
        delta[n], new_m[n], new_v[n] = d_.reshape(shape), m_.reshape(shape), v_.reshape(shape)

    return (loss, grad_x, *[grad_w[n] for n in WEIGHT_NAMES], *[delta[n] for n in WEIGHT_NAMES],
            *[new_m[n] for n in WEIGHT_NAMES], *[new_v[n] for n in WEIGHT_NAMES])
```

```python
import functools
import math

import numpy as np
import jax
import jax.numpy as jnp
from jax import lax
from jax.experimental import pallas as pl
from jax.experimental.pallas import tpu as pltpu

F32 = jnp.float32
BF16 = jnp.bfloat16

DEPTH = 2
ALPHA = (2.0 * DEPTH) ** 0.25
MACARON_WEIGHT = 0.5
LN_EPS = 1e-5
MASK_VALUE = -1e30
POOL_WINDOWS = (2, 4, 8, 16)
POOL_PAD = 16
HEAD_DIM = 64
N_HEADS = 16
DIL_CONFIGS = ((128, 1), (512, 4), (2048, 16))
ATTN_R = 64
ATTN_BQ = 128
ATTN_W = ATTN_BQ + 2 * ATTN_R
LANES = 128
ADAM_LR = 0.001
ADAM_B1 = 0.9
ADAM_B2 = 0.999
ADAM_EPS = 1e-08
ADAM_WD = 0.01
ADAM_STEP = 10
N_CHIPS = 4
VMEM_LIMIT = 56 * 1024 * 1024
MESH = pl.DeviceIdType.MESH
ANY = pl.BlockSpec(memory_space=pl.ANY)


def _params(sem=None, vmem=VMEM_LIMIT):
    return pltpu.CompilerParams(dimension_semantics=sem, vmem_limit_bytes=vmem)


def _alibi_slopes():
    n = len(DIL_CONFIGS) * N_HEADS
    s = 2.0 ** (-8.0 * np.arange(1, n + 1) / n)
    return s.reshape(len(DIL_CONFIGS), N_HEADS).astype(np.float32)


def _ln_fwd(z, g, b):
    mu = jnp.mean(z, axis=-1, keepdims=True)
    zc = z - mu
    var = jnp.mean(zc * zc, axis=-1, keepdims=True)
    rstd = lax.rsqrt(var + LN_EPS)
    xhat = zc * rstd
    return xhat * g + b, xhat, rstd


def _dot(a, b):
    return jnp.dot(a, b, preferred_element_type=F32)


def _dot_nt(a, b):
    return lax.dot_general(a, b, (((1,), (1,)), ((), ())), preferred_element_type=F32)


def _dot_tn(a, b):
    return lax.dot_general(a, b, (((0,), (0,)), ((), ())), preferred_element_type=F32)


def mm_nn(a, b, out_dtype, tm=512):
    S, K = a.shape
    nb, Nb = b.shape[1] // 1024, 1024

    def body(a_ref, b_ref, o_ref):
        o_ref[...] = _dot(a_ref[...].astype(BF16), b_ref[...]).astype(out_dtype)

    return pl.pallas_call(
        body, name="mm_nn",
        grid=(S // tm, nb),
        in_specs=[pl.BlockSpec((tm, K), lambda i, j: (i, 0)), pl.BlockSpec((K, Nb), lambda i, j: (0, j))],
        out_specs=pl.BlockSpec((None, tm, Nb), lambda i, j: (j, i, 0)),
        out_shape=jax.ShapeDtypeStruct((nb, S, Nb), out_dtype),
        compiler_params=_params(("parallel", "arbitrary")),
    )(a, b)


def proj_ln(a, w, resid, gain, bias, tm=512):
    S, K = a.shape
    D = w.shape[1]

    def body(a_ref, w_ref, r_ref, g_ref, b_ref, y_ref, xh_ref, rs_ref):
        z = ALPHA * r_ref[...] + _dot(a_ref[...].astype(BF16), w_ref[...])
        y, xh, rs = _ln_fwd(z, g_ref[...], b_ref[...])
        y_ref[...] = y
        xh_ref[...] = xh
        rs_ref[...] = rs

    row = pl.BlockSpec((tm, D), lambda i: (i, 0))
    vec = pl.BlockSpec((1, D), lambda i: (0, 0))
    return pl.pallas_call(
        body, name="proj_ln",
        grid=(S // tm,),
        in_specs=[pl.BlockSpec((tm, K), lambda i: (i, 0)), pl.BlockSpec((K, D), lambda i: (0, 0)), row, vec, vec],
        out_specs=[row, row, pl.BlockSpec((tm, 1), lambda i: (i, 0))],
        out_shape=[jax.ShapeDtypeStruct((S, D), F32), jax.ShapeDtypeStruct((S, D), F32), jax.ShapeDtypeStruct((S, 1), F32)],
        compiler_params=_params(("parallel",)),
    )(a, w, resid, gain, bias)


def mm_nt(a, w, resid, a_blocked, out_dtype=F32, tm=512):
    if a_blocked:
        nk, S, Kb = a.shape
        a_spec = pl.BlockSpec((None, tm, Kb), lambda i, n: (n, i, 0))
    else:
        S, Kb = a.shape
        nk = 1
        a_spec = pl.BlockSpec((tm, Kb), lambda i, n: (i, 0))
    M = w.shape[0]
    has_resid = resid is not None

    def body(*refs):
        if has_resid:
            a_ref, w_ref, r_ref, o_ref, acc = refs
        else:
            a_ref, w_ref, o_ref, acc = refs
        n = pl.program_id(1)
        part = _dot_nt(a_ref[...].astype(BF16), w_ref[...])

        @pl.when(n == 0)
        def _():
            acc[...] = part

        @pl.when(n > 0)
        def _():
            acc[...] += part

        @pl.when(n == nk - 1)
        def _():
            out = acc[...]
            if has_resid:
                out = out + ALPHA * r_ref[...]
            o_ref[...] = out.astype(out_dtype)

    row = pl.BlockSpec((tm, M), lambda i, n: (i, 0))
    in_specs = [a_spec, pl.BlockSpec((M, Kb), lambda i, n: (0, n))] + ([row] if has_resid else [])
    args = (a, w) + ((resid,) if has_resid else ())
    return pl.pallas_call(
        body, name="mm_nt",
        grid=(S // tm, nk),
        in_specs=in_specs,
        out_specs=row,
        out_shape=jax.ShapeDtypeStruct((S, M), out_dtype),
        scratch_shapes=[pltpu.VMEM((tm, M), F32)],
        compiler_params=_params(("parallel", "arbitrary")),
    )(*args)


def mm_tn(pairs, *, nblk, out_shape, out_block, out_index, alias=None, tk=512, name="mm_tn"):
    operands = []
    for a, b, _ in pairs:
        for t in (a, b):
            if not any(t is o for o in operands):
                operands.append(t)
    where = lambda t: next(i for i, o in enumerate(operands) if o is t)
    S = pairs[0][0].shape[-2]
    n_out, n_in = len(pairs), len(operands)
    n_alias = len(alias) if alias is not None else 0

    def spec(t):
        if t.ndim == 3:
            return pl.BlockSpec((None, tk, t.shape[-1]), lambda j, k: (j, k, 0))
        return pl.BlockSpec((tk, t.shape[-1]), lambda j, k: (k, 0))

    def body(*refs):
        refs = refs[n_alias:]
        in_refs, o_refs, accs = refs[:n_in], refs[n_in:n_in + n_out], refs[n_in + n_out:]
        k = pl.program_id(1)
        for (a, b, scale), o_ref, acc in zip(pairs, o_refs, accs):
            part = _dot_tn(in_refs[where(a)][...].astype(BF16), in_refs[where(b)][...].astype(BF16))

            @pl.when(k == 0)
            def _():
                acc[...] = part

            @pl.when(k > 0)
            def _():
                acc[...] += part

            @pl.when(k == S // tk - 1)
            def _():
                o_ref[...] = (scale * acc[...]).astype(BF16)

    out_spec = pl.BlockSpec(out_block, lambda j, k: out_index(j))
    outs = pl.pallas_call(
        body, name=name,
        grid=(nblk, S // tk),
        in_specs=[ANY] * n_alias + [spec(t) for t in operands],
        out_specs=[out_spec] * n_out,
        out_shape=[jax.ShapeDtypeStruct(out_shape, BF16)] * n_out,
        scratch_shapes=[pltpu.VMEM((a.shape[-1], b.shape[-1]), F32) for a, b, _ in pairs],
        input_output_aliases={i: i for i in range(n_alias)},
        compiler_params=_params(("parallel", "arbitrary")),
    )(*(tuple(alias) if alias is not None else ()), *operands)
    return list(outs)


def ffn_fwd(x, wg, wu, wd, layer, gain, bias, tm=512):
    S, D = x.shape
    nb, FB = wg.shape[0], wg.shape[2]

    def body(x_ref, wg_ref, wu_ref, wd_ref, g_ref, b_ref, y_ref, xh_ref, rs_ref, a_ref, u_ref, acc):
        j = pl.program_id(1)
        xb = x_ref[...].astype(BF16)
        a = _dot_nt(xb, wg_ref[...])
        u = _dot_nt(xb, wu_ref[...])
        a_ref[...] = a.astype(BF16)
        u_ref[...] = u.astype(BF16)
        h = a * jax.nn.sigmoid(a) * u
        part = _dot(h.astype(BF16), wd_ref[...])

        @pl.when(j == 0)
        def _():
            acc[...] = part

        @pl.when(j > 0)
        def _():
            acc[...] += part

        @pl.when(j == nb - 1)
        def _():
            z = ALPHA * x_ref[...] + MACARON_WEIGHT * acc[...]
            y, xh, rs = _ln_fwd(z, g_ref[...], b_ref[...])
            y_ref[...] = y
            xh_ref[...] = xh
            rs_ref[...] = rs

    row = pl.BlockSpec((tm, D), lambda i, j: (i, 0))
    vec = pl.BlockSpec((1, D), lambda i, j: (0, 0))
    w_out = pl.BlockSpec((None, None, FB, D), lambda i, j: (j, layer, 0, 0))
    act = pl.BlockSpec((None, tm, FB), lambda i, j: (j, i, 0))
    return pl.pallas_call(
        body, name="ffn_fwd",
        grid=(S // tm, nb),
        in_specs=[row, w_out, w_out, w_out, vec, vec],
        out_specs=[row, row, pl.BlockSpec((tm, 1), lambda i, j: (i, 0)), act, act],
        out_shape=[jax.ShapeDtypeStruct((S, D), F32), jax.ShapeDtypeStruct((S, D), F32), jax.ShapeDtypeStruct((S, 1), F32),
                   jax.ShapeDtypeStruct((nb, S, FB), BF16), jax.ShapeDtypeStruct((nb, S, FB), BF16)],
        scratch_shapes=[pltpu.VMEM((tm, D), F32)],
        compiler_params=_params(("parallel", "arbitrary")),
    )(x, wg, wu, wd, gain, bias)


def ffn_bwd(dz, a, u, wg, wu, wd, layer, tm=512):
    S, D = dz.shape
    nb, FB = wg.shape[0], wg.shape[2]

    def body(dz_ref, a_ref, u_ref, wg_ref, wu_ref, wd_ref, dx_ref, h_ref, da_ref, du_ref, acc):
        j = pl.program_id(1)
        dzb = (MACARON_WEIGHT * dz_ref[...]).astype(BF16)
        dh = _dot_nt(dzb, wd_ref[...])
        av = a_ref[...].astype(F32)
        uv = u_ref[...].astype(F32)
        s = jax.nn.sigmoid(av)
        silu = av * s
        h_ref[...] = (silu * uv).astype(BF16)
        da = (dh * uv * (s * (1.0 + av * (1.0 - s)))).astype(BF16)
        du = (dh * silu).astype(BF16)
        da_ref[...] = da
        du_ref[...] = du
        part = _dot(da, wg_ref[...]) + _dot(du, wu_ref[...])

        @pl.when(j == 0)
        def _():
            acc[...] = part

        @pl.when(j > 0)
        def _():
            acc[...] += part

        @pl.when(j == nb - 1)
        def _():
            dx_ref[...] = ALPHA * dz_ref[...] + acc[...]

    row = pl.BlockSpec((tm, D), lambda i, j: (i, 0))
    w_out = pl.BlockSpec((None, None, FB, D), lambda i, j: (j, layer, 0, 0))
    act = pl.BlockSpec((None, tm, FB), lambda i, j: (j, i, 0))
    act_shape = jax.ShapeDtypeStruct((nb, S, FB), BF16)
    return pl.pallas_call(
        body, name="ffn_bwd",
        grid=(S // tm, nb),
        in_specs=[row, act, act, w_out, w_out, w_out],
        out_specs=[row, act, act, act],
        out_shape=[jax.ShapeDtypeStruct((S, D), F32), act_shape, act_shape, act_shape],
        scratch_shapes=[pltpu.VMEM((tm, D), F32)],
        compiler_params=_params(("parallel", "arbitrary")),
    )(dz, a, u, wg, wu, wd)


def ln_bwd(dy, xhat, rstd, gain, tm=512):
    S, D = dy.shape

    def body(dy_ref, xh_ref, rs_ref, g_ref, dz_ref, dg_ref, db_ref):
        i = pl.program_id(0)
        dy = dy_ref[...]
        xh = xh_ref[...]
        dxh = dy * g_ref[...]
        m1 = jnp.mean(dxh, axis=-1, keepdims=True)
        m2 = jnp.mean(dxh * xh, axis=-1, keepdims=True)
        dz_ref[...] = rs_ref[...] * (dxh - m1 - xh * m2)
        dg = jnp.sum(dy * xh, axis=0, keepdims=True)
        db = jnp.sum(dy, axis=0, keepdims=True)

        @pl.when(i == 0)
        def _():
            dg_ref[...] = dg
            db_ref[...] = db

        @pl.when(i > 0)
        def _():
            dg_ref[...] += dg
            db_ref[...] += db

    row = pl.BlockSpec((tm, D), lambda i: (i, 0))
    vec = pl.BlockSpec((1, D), lambda i: (0, 0))
    return pl.pallas_call(
        body, name="ln_bwd",
        grid=(S // tm,),
        in_specs=[row, row, pl.BlockSpec((tm, 1), lambda i: (i, 0)), vec],
        out_specs=[row, vec, vec],
        out_shape=[jax.ShapeDtypeStruct((S, D), F32), jax.ShapeDtypeStruct((1, D), F32), jax.ShapeDtypeStruct((1, D), F32)],
        compiler_params=_params(("arbitrary",)),
    )(dy, xhat, rstd, gain)


def loss_head(y, target, tm=512):
    S, D = y.shape

    def body(y_ref, t_ref, dy_ref, l_ref):
        i = pl.program_id(0)
        e = y_ref[...] - t_ref[...]
        dy_ref[...] = e / D
        part = 0.5 * jnp.sum(jnp.mean(e * e, axis=-1, keepdims=True), axis=0, keepdims=True)

        @pl.when(i == 0)
        def _():
            l_ref[...] = part

        @pl.when(i > 0)
        def _():
            l_ref[...] += part

    row = pl.BlockSpec((tm, D), lambda i: (i, 0))
    return pl.pallas_call(
        body, name="loss_head",
        grid=(S // tm,),
        in_specs=[row, row],
        out_specs=[row, pl.BlockSpec((1, 1), lambda i: (0, 0))],
        out_shape=[jax.ShapeDtypeStruct((S, D), F32), jax.ShapeDtypeStruct((1, 1), F32)],
        compiler_params=_params(("arbitrary",)),
    )(y, target)


def _pool_window(xp, g):
    n = xp.shape[0]
    w = xp + pltpu.roll(xp, 1, 0)
    out = w
    for level, shift in enumerate((1, 2, 4), start=1):
        w = pltpu.roll(w, shift, 0) + pltpu.roll(w, n - shift, 0)
        out = jnp.where(g >= level, w, out)
    return out


def _pool_count(S, C, g):
    half = lax.shift_left(jnp.int32(1), g)
    t = lax.broadcasted_iota(jnp.int32, (S, C), 0)
    return (jnp.minimum(t + half, S) - jnp.maximum(t - half, 0)).astype(F32)


def pool_mix(u, wgrp, scale):
    S, D = u.shape
    G, C = wgrp.shape[0], wgrp.shape[1]

    def body(u_ref, w_ref, s_ref, mix_ref, v_ref, pad):
        g = pl.program_id(0)
        zeros = jnp.zeros((POOL_PAD, C), F32)
        pad[pl.ds(0, POOL_PAD), :] = zeros
        pad[pl.ds(POOL_PAD + S, POOL_PAD), :] = zeros
        pad[pl.ds(POOL_PAD, S), :] = u_ref[...]
        win = _pool_window(pad[...], g)[POOL_PAD:POOL_PAD + S]
        mixed = (win / _pool_count(S, C, g) - u_ref[...]).astype(BF16)
        mix_ref[...] = mixed
        v_ref[...] = _dot(mixed, w_ref[...]) * s_ref[...]

    col = pl.BlockSpec((S, C), lambda g: (0, g))
    return pl.pallas_call(
        body, name="pool_mix",
        grid=(G,),
        in_specs=[col, pl.BlockSpec((None, C, C), lambda g: (g, 0, 0)), pl.BlockSpec((1, C), lambda g: (0, g))],
        out_specs=[col, col],
        out_shape=[jax.ShapeDtypeStruct((S, D), BF16), jax.ShapeDtypeStruct((S, D), F32)],
        scratch_shapes=[pltpu.VMEM((S + 2 * POOL_PAD, C), F32)],
        compiler_params=_params(("arbitrary",)),
    )(u, wgrp, scale)


def pool_mix_bwd(dv, mixed, wgrp, scale):
    S, D = dv.shape
    G, C = wgrp.shape[0], wgrp.shape[1]

    def body(dv_ref, mix_ref, w_ref, s_ref, du_ref, dw_ref, ds_ref, pad):
        g = pl.program_id(0)
        mixed = mix_ref[...]
        dv = dv_ref[...]
        yg = _dot(mixed, w_ref[...])
        ds_ref[...] = jnp.sum(dv * yg, axis=0, keepdims=True)
        dyg = (dv * s_ref[...]).astype(BF16)
        dw_ref[...] = _dot_tn(mixed, dyg).astype(BF16)
        dmix = _dot_nt(dyg, w_ref[...])
        zeros = jnp.zeros((POOL_PAD, C), F32)
        pad[pl.ds(0, POOL_PAD), :] = zeros
        pad[pl.ds(POOL_PAD + S, POOL_PAD), :] = zeros
        pad[pl.ds(POOL_PAD, S), :] = dmix / _pool_count(S, C, g)
        win = _pool_window(pad[...], g)
        win = pltpu.roll(win, win.shape[0] - 1, 0)[POOL_PAD:POOL_PAD + S]
        du_ref[...] = win - dmix

    col = pl.BlockSpec((S, C), lambda g: (0, g))
    return pl.pallas_call(
        body, name="pool_mix_bwd",
        grid=(G,),
        in_specs=[col, col, pl.BlockSpec((None, C, C), lambda g: (g, 0, 0)), pl.BlockSpec((1, C), lambda g: (0, g))],
        out_specs=[col, pl.BlockSpec((None, C, C), lambda g: (g, 0, 0)), pl.BlockSpec((1, C), lambda g: (0, g))],
        out_shape=[jax.ShapeDtypeStruct((S, D), F32), jax.ShapeDtypeStruct((G, C, C), BF16), jax.ShapeDtypeStruct((1, D), F32)],
        scratch_shapes=[pltpu.VMEM((S + 2 * POOL_PAD, C), F32)],
        compiler_params=_params(("arbitrary",)),
    )(dv, mixed, wgrp, scale)


def _slope_table(group, dilation):
    s = _alibi_slopes()[group].reshape(N_HEADS // 2, 2, 1, 1) * float(dilation)
    return jnp.asarray(np.broadcast_to(s, (N_HEADS // 2, 2, 1, ATTN_W)).copy())


def _attn_geometry(L):
    a = lax.broadcasted_iota(jnp.int32, (ATTN_BQ, ATTN_W), 0)
    c = lax.broadcasted_iota(jnp.int32, (ATTN_BQ, ATTN_W), 1)
    rel = jnp.abs(c - ATTN_R - a)
    return c, rel <= ATTN_R, rel.astype(F32)


def _attn_scores(q, kw, head_mask, slope_row, absrel, band, c, i, L):
    qh = jnp.where(head_mask, q, jnp.zeros_like(q))
    s = _dot_nt(qh, kw) * (HEAD_DIM ** -0.5) - slope_row * absrel
    lo = ATTN_R - i * ATTN_BQ
    valid = band & (c >= lo) & (c < L + lo)
    return qh, jnp.where(valid, s, MASK_VALUE)


def _fill_padded(dst, src, L):
    zeros = jnp.zeros((ATTN_R, LANES), dst.dtype)
    dst[pl.ds(0, ATTN_R), :] = zeros
    dst[pl.ds(ATTN_R + L, ATTN_R), :] = zeros
    dst[pl.ds(ATTN_R, L), :] = src[...]


def attn_fwd(qkv, group, dilation):
    _, S, D = qkv.shape
    d = dilation
    L = S // d
    nq = L // ATTN_BQ
    ncol = D // LANES
    view = qkv.reshape(9, L, d * D)
    slopes = _slope_table(group, d)

    def body(q_ref, k_ref, v_ref, sl_ref, o_ref, lse_ref, kpad, vpad):
        _fill_padded(kpad, k_ref, L)
        _fill_padded(vpad, v_ref, L)
        c, band, absrel = _attn_geometry(L)
        lane = lax.broadcasted_iota(jnp.int32, (ATTN_BQ, LANES), 1)
        masks = (lane < HEAD_DIM, lane >= HEAD_DIM)

        def step(i, carry):
            r0 = pl.multiple_of(i * ATTN_BQ, ATTN_BQ)
            q = q_ref[pl.ds(r0, ATTN_BQ), :]
            kw = kpad[pl.ds(r0, ATTN_W), :]
            vw = vpad[pl.ds(r0, ATTN_W), :]
            outs, lses = [], []
            for h in range(2):
                _, s = _attn_scores(q, kw, masks[h], sl_ref[h], absrel, band, c, i, L)
                m = jnp.max(s, axis=-1, keepdims=True)
                e = jnp.exp(s - m)
                l = jnp.sum(e, axis=-1, keepdims=True)
                p = e * (1.0 / l)
                outs.append(_dot(p.astype(BF16), vw))
                lses.append(jnp.broadcast_to(m + jnp.log(l), (ATTN_BQ, LANES)))
            o_ref[pl.ds(r0, ATTN_BQ), :] = jnp.where(masks[0], outs[0], outs[1])
            lse_ref[pl.ds(r0, ATTN_BQ), :] = jnp.where(masks[0], lses[0], lses[1])
            return carry

        lax.fori_loop(0, nq, step, 0)

    def col(which):
        return pl.BlockSpec((None, L, LANES), lambda r, hp: (3 * group + which, 0, r * ncol + hp))

    out = pl.BlockSpec((L, LANES), lambda r, hp: (0, r * ncol + hp))
    o, lse = pl.pallas_call(
        body, name=f"attn_fwd_g{group}",
        grid=(d, ncol),
        in_specs=[col(0), col(1), col(2), pl.BlockSpec((None, 2, 1, ATTN_W), lambda r, hp: (hp, 0, 0, 0))],
        out_specs=[out, out],
        out_shape=[jax.ShapeDtypeStruct((L, d * D), F32), jax.ShapeDtypeStruct((L, d * D), F32)],
        scratch_shapes=[pltpu.VMEM((L + 2 * ATTN_R, LANES), BF16), pltpu.VMEM((L + 2 * ATTN_R, LANES), BF16)],
        compiler_params=_params(("parallel", "parallel")),
    )(view, view, view, slopes)
    return o.reshape(S, D), lse.reshape(S, D)


def attn_combine(os, lses, tm=512):
    S, D = os[0].shape
    n = len(os)

    def body(*refs):
        o_refs, l_refs, out_ref, lse_ref = refs[:n], refs[n:2 * n], refs[2 * n], refs[2 * n + 1]
        ls = [r[...] for r in l_refs]
        m = functools.reduce(jnp.maximum, ls)
        es = [jnp.exp(l - m) for l in ls]
        tot = functools.reduce(lambda x, y: x + y, es)
        inv = 1.0 / tot
        out_ref[...] = functools.reduce(lambda x, y: x + y, [(e * inv) * r[...] for e, r in zip(es, o_refs)])
        lse_ref[...] = m + jnp.log(tot)

    row = pl.BlockSpec((tm, D), lambda i: (i, 0))
    return pl.pallas_call(
        body, name="attn_combine",
        grid=(S // tm,),
        in_specs=[row] * (2 * n),
        out_specs=[row, row],
        out_shape=[jax.ShapeDtypeStruct((S, D), F32), jax.ShapeDtypeStruct((S, D), F32)],
        compiler_params=_params(("parallel",)),
    )(*os, *lses)


def attn_bwd(qkv, do, o, lse, group, dilation, dqkv):
    _, S, D = qkv.shape
    d = dilation
    L = S // d
    nq = L // ATTN_BQ
    ncol = D // LANES
    view = qkv.reshape(9, L, d * D)
    slopes = _slope_table(group, d)
    has_alias = dqkv is not None

    def body(*refs):
        if has_alias:
            refs = refs[1:]
        q_ref, k_ref, v_ref, do_ref, o_ref, lse_ref, sl_ref, dx_ref, kpad, vpad, dkacc, dvacc = refs
        _fill_padded(kpad, k_ref, L)
        _fill_padded(vpad, v_ref, L)
        dkacc[...] = jnp.zeros_like(dkacc)
        dvacc[...] = jnp.zeros_like(dvacc)
        c, band, absrel = _attn_geometry(L)
        lane = lax.broadcasted_iota(jnp.int32, (ATTN_BQ, LANES), 1)
        masks = (lane < HEAD_DIM, lane >= HEAD_DIM)

        def step(i, carry):
            r0 = pl.multiple_of(i * ATTN_BQ, ATTN_BQ)
            rows = pl.ds(r0, ATTN_BQ)
            win = pl.ds(r0, ATTN_W)
            q = q_ref[rows, :]
            kw = kpad[win, :]
            vw = vpad[win, :]
            dov = do_ref[rows, :]
            prod = dov * o_ref[rows, :]
            dob = dov.astype(BF16)
            lse_v = lse_ref[rows, :]
            dq = jnp.zeros((ATTN_BQ, LANES), F32)
            for h in range(2):
                qh, s = _attn_scores(q, kw, masks[h], sl_ref[h], absrel, band, c, i, L)
                lse_h = lse_v[:, h * HEAD_DIM:h * HEAD_DIM + 1]
                p = jnp.exp(s - lse_h)
                dterm = jnp.sum(jnp.where(masks[h], prod, 0.0), axis=-1, keepdims=True)
                doh = jnp.where(masks[h], dob, jnp.zeros_like(dob))
                dp = _dot_nt(doh, vw)
                ds = (p * (dp - dterm) * (HEAD_DIM ** -0.5)).astype(BF16)
                dvacc[win, :] += _dot_tn(p.astype(BF16), doh)
                dkacc[win, :] += _dot_tn(ds, qh)
                dq = dq + jnp.where(masks[h], _dot(ds, kw), 0.0)
            dx_ref[0, rows, :] = dq.astype(BF16)
            return carry

        lax.fori_loop(0, nq, step, 0)
        dx_ref[1] = dkacc[pl.ds(ATTN_R, L), :].astype(BF16)
        dx_ref[2] = dvacc[pl.ds(ATTN_R, L), :].astype(BF16)

    def col(which):
        return pl.BlockSpec((None, L, LANES), lambda r, hp: (3 * group + which, 0, r * ncol + hp))

    act = pl.BlockSpec((L, LANES), lambda r, hp: (0, r * ncol + hp))
    n_groups = len(DIL_CONFIGS)
    args = (view, view, view, do.reshape(L, d * D), o.reshape(L, d * D), lse.reshape(L, d * D), slopes)
    if has_alias:
        args = (dqkv.reshape(n_groups, 3, L, d * D),) + args
    out = pl.pallas_call(
        body, name=f"attn_bwd_g{group}",
        grid=(d, ncol),
        in_specs=([ANY] if has_alias else []) + [col(0), col(1), col(2), act, act, act,
                                                 pl.BlockSpec((None, 2, 1, ATTN_W), lambda r, hp: (hp, 0, 0, 0))],
        out_specs=pl.BlockSpec((None, 3, L, LANES), lambda r, hp: (group, 0, 0, r * ncol + hp)),
        out_shape=jax.ShapeDtypeStruct((n_groups, 3, L, d * D), BF16),
        scratch_shapes=[pltpu.VMEM((L + 2 * ATTN_R, LANES), BF16), pltpu.VMEM((L + 2 * ATTN_R, LANES), BF16),
                        pltpu.VMEM((L + 2 * ATTN_R, LANES), F32), pltpu.VMEM((L + 2 * ATTN_R, LANES), F32)],
        input_output_aliases={0: 0} if has_alias else {},
        compiler_params=_params(("parallel", "parallel")),
    )(*args)
    return out.reshape(3 * n_groups, S, D)


TILE_ELEMS = 256 * 1024


def _row_tile(R, C):
    if R * C <= TILE_ELEMS or R % 16:
        return R
    return max(t for t in range(16, R + 1, 16) if R % t == 0 and (t * C <= TILE_ELEMS or t == 16))


def pair_sum(core, g, recv):
    _, _, R, C = g.shape
    tr = _row_tile(R, C)

    def body(c_ref, g_ref, r_ref, o_ref):
        o_ref[...] = (g_ref[...].astype(F32) + r_ref[...].astype(F32)).astype(BF16)

    blk = pl.BlockSpec((None, tr, C), lambda d, i, c_ref: (d, i, 0))
    return pl.pallas_call(
        body, name="pair_sum",
        grid_spec=pltpu.PrefetchScalarGridSpec(
            num_scalar_prefetch=1, grid=(N_CHIPS, R // tr),
            in_specs=[pl.BlockSpec((None, None, tr, C), lambda d, i, c_ref: (d, c_ref[0], i, 0)), blk],
            out_specs=blk),
        out_shape=jax.ShapeDtypeStruct((N_CHIPS, R, C), BF16),
        compiler_params=_params(("parallel", "parallel")),
    )(core, g, recv)


def chip_sum(chip, own, recv):
    _, R, C = own.shape
    tr = _row_tile(R, C)
    slot_of_relation = {2: 0, 1: 1, 3: 2}

    def body(chip_ref, own_ref, r_ref, o_ref):
        me = chip_ref[0]
        mine = own_ref[...].astype(F32)
        theirs = {rel: r_ref[k].astype(F32) for rel, k in slot_of_relation.items()}
        acc = None
        for s in range(N_CHIPS):
            rel = jnp.bitwise_xor(me, s)
            part = jnp.where(rel == 0, mine, jnp.where(rel == 2, theirs[2], jnp.where(rel == 1, theirs[1], theirs[3])))
            acc = part if acc is None else acc + part
        o_ref[...] = acc

    return pl.pallas_call(
        body, name="chip_sum",
        grid_spec=pltpu.PrefetchScalarGridSpec(
            num_scalar_prefetch=1, grid=(R // tr,),
            in_specs=[pl.BlockSpec((None, tr, C), lambda i, chip_ref: (chip_ref[0], i, 0)),
                      pl.BlockSpec((N_CHIPS - 1, tr, C), lambda i, chip_ref: (0, i, 0))],
            out_specs=pl.BlockSpec((tr, C), lambda i, chip_ref: (i, 0))),
        out_shape=jax.ShapeDtypeStruct((R, C), F32),
        compiler_params=_params(("parallel",)),
    )(chip, own, recv)


def adamw(core, w, g_own, g_recv, m, v):
    H, R, C = w.shape
    tr = _row_tile(R, C)

    def body(c_ref, w_ref, go_ref, gr_ref, m_ref, v_ref, g_ref, d_ref, nm_ref, nv_ref):
        g = jnp.where(pl.program_id(0) == c_ref[0], go_ref[...], gr_ref[...])
        m = ADAM_B1 * m_ref[...] + (1.0 - ADAM_B1) * g
        v = ADAM_B2 * v_ref[...] + (1.0 - ADAM_B2) * (g * g)
        m_hat = m / (1.0 - ADAM_B1 ** ADAM_STEP)
        v_hat = v / (1.0 - ADAM_B2 ** ADAM_STEP)
        g_ref[...] = g
        d_ref[...] = -ADAM_LR * (m_hat / (jnp.sqrt(v_hat) + ADAM_EPS) + ADAM_WD * w_ref[...])
        nm_ref[...] = m
        nv_ref[...] = v

    blk = pl.BlockSpec((None, tr, C), lambda h, i, c_ref: (h, i, 0))
    half = pl.BlockSpec((tr, C), lambda h, i, c_ref: (i, 0))
    shape = jax.ShapeDtypeStruct((H, R, C), F32)
    return pl.pallas_call(
        body, name="adamw",
        grid_spec=pltpu.PrefetchScalarGridSpec(
            num_scalar_prefetch=1, grid=(H, R // tr),
            in_specs=[blk, half, half, blk, blk],
            out_specs=[blk] * 4),
        out_shape=[shape] * 4,
        compiler_params=_params(("parallel", "parallel")),
    )(core, w, g_own, g_recv, m, v)


def _place():
    return lax.axis_index("x"), lax.axis_index("y"), lax.axis_index("c")


def _other_chips(x, y):
    return [(2 * (1 - x) + y, (1 - x, y)), (2 * x + (1 - y), (x, 1 - y)), (2 * (1 - x) + (1 - y), (1 - x, 1 - y))]


def all_gather_shards(shards, placed):
    n = len(shards)

    def body(*refs):
        ins, outs = refs[:n], refs[2 * n:3 * n]
        send_sems, recv_sems = refs[3 * n:]
        x, y, c = _place()
        me = 2 * x + y
        sibling = (x, y, 1 - c)
        chips = _other_chips(x, y)

        def copy(a, k, src, dst, to):
            return pltpu.make_async_remote_copy(src_ref=src, dst_ref=dst, send_sem=send_sems.at[a, k], recv_sem=recv_sems.at[a, k],
                                                device_id=to, device_id_type=MESH)

        sends = []
        for a in range(n):
            for k, (_, (px, py)) in enumerate(chips):
                cp = copy(a, k, ins[a].at[c], outs[a].at[me, c], (px, py, c))
                cp.start()
                sends.append(cp)
        for a in range(n):
            for k, (chip, _) in enumerate(chips):
                landed = outs[a].at[chip, c]
                copy(a, k, landed, landed, sibling).wait_recv()
                cp = copy(a, 3 + k, landed, landed, sibling)
                cp.start()
                sends.append(cp)
        for a in range(n):
            for k, (chip, _) in enumerate(chips):
                other = outs[a].at[chip, 1 - c]
                copy(a, 3 + k, other, other, sibling).wait_recv()
        for cp in sends:
            cp.wait_send()

    return pl.pallas_call(
        body, name="all_gather_shards",
        in_specs=[ANY] * (2 * n),
        out_specs=[ANY] * n,
        out_shape=[jax.ShapeDtypeStruct(p.shape, p.dtype) for p in placed],
        scratch_shapes=[pltpu.SemaphoreType.DMA((n, 6)), pltpu.SemaphoreType.DMA((n, 6))],
        input_output_aliases={n + a: a for a in range(n)},
        compiler_params=pltpu.CompilerParams(has_side_effects=True),
    )(*shards, *placed)


def sibling_exchange_halves(grads):
    n = len(grads)

    def body(*refs):
        ins, outs = refs[:n], refs[n:2 * n]
        send_sems, recv_sems = refs[2 * n:]
        x, y, c = _place()
        sibling = (x, y, 1 - c)
        copies = [pltpu.make_async_remote_copy(src_ref=ins[a].at[:, 1 - c], dst_ref=outs[a], send_sem=send_sems.at[a],
                                               recv_sem=recv_sems.at[a], device_id=sibling, device_id_type=MESH) for a in range(n)]
        for cp in copies:
            cp.start()
        for cp in copies:
            cp.wait()

    return pl.pallas_call(
        body, name="sibling_exchange_halves",
        in_specs=[ANY] * n,
        out_specs=[ANY] * n,
        out_shape=[jax.ShapeDtypeStruct((N_CHIPS,) + g.shape[2:], g.dtype) for g in grads],
        scratch_shapes=[pltpu.SemaphoreType.DMA((n,)), pltpu.SemaphoreType.DMA((n,))],
        compiler_params=pltpu.CompilerParams(has_side_effects=True),
    )(*grads)


def chip_exchange(sums):
    n = len(sums)

    def body(*refs):
        ins, outs = refs[:n], refs[n:2 * n]
        send_sems, recv_sems = refs[2 * n:]
        x, y, c = _place()
        copies = []
        for a in range(n):
            for k, (chip, (px, py)) in enumerate(_other_chips(x, y)):
                cp = pltpu.make_async_remote_copy(src_ref=ins[a].at[chip], dst_ref=outs[a].at[k], send_sem=send_sems.at[a, k],
                                                  recv_sem=recv_sems.at[a, k], device_id=(px, py, c), device_id_type=MESH)
                cp.start()
                copies.append(cp)
        for cp in copies:
            cp.wait()

    return pl.pallas_call(
        body, name="chip_exchange",
        in_specs=[ANY] * n,
        out_specs=[ANY] * n,
        out_shape=[jax.ShapeDtypeStruct((N_CHIPS - 1,) + s.shape[1:], s.dtype) for s in sums],
        scratch_shapes=[pltpu.SemaphoreType.DMA((n, 3)), pltpu.SemaphoreType.DMA((n, 3))],
        compiler_params=pltpu.CompilerParams(has_side_effects=True),
    )(*sums)


def sibling_share(halves):
    n = len(halves)

    def body(*refs):
        ins, outs = refs[:n], refs[n:2 * n]
        send_sems, recv_sems = refs[2 * n:]
        x, y, c = _place()
        copies = [pltpu.make_async_remote_copy(src_ref=ins[a], dst_ref=outs[a], send_sem=send_sems.at[a], recv_sem=recv_sems.at[a],
                                               device_id=(x, y, 1 - c), device_id_type=MESH) for a in range(n)]
        for cp in copies:
            cp.start()
        for cp in copies:
            cp.wait()

    return pl.pallas_call(
        body, name="sibling_share",
        in_specs=[ANY] * n,
        out_specs=[ANY] * n,
        out_shape=[jax.ShapeDtypeStruct(h.shape, h.dtype) for h in halves],
        scratch_shapes=[pltpu.SemaphoreType.DMA((n,)), pltpu.SemaphoreType.DMA((n,))],
        compiler_params=pltpu.CompilerParams(has_side_effects=True),
    )(*halves)


def all_reduce_small(v):
    R, C = v.shape
    n_dev = 8

    def body(v_ref, o_ref, buf, send_sems, recv_sems):
        x, y, c = _place()
        me = 4 * x + 2 * y + c
        buf[me] = v_ref[...]
        copies = []
        for rel in range(1, n_dev):
            fx, fy, fc = rel >> 2, (rel >> 1) & 1, rel & 1
            peer = (x ^ fx, y ^ fy, c ^ fc)
            cp = pltpu.make_async_remote_copy(src_ref=v_ref, dst_ref=buf.at[me], send_sem=send_sems.at[rel - 1],
                                              recv_sem=recv_sems.at[rel - 1], device_id=peer, device_id_type=MESH)
            cp.start()
            copies.append(cp)
        for cp in copies:
            cp.wait()
        acc = buf[0]
        for k in range(1, n_dev):
            acc = acc + buf[k]
        o_ref[...] = acc

    return pl.pallas_call(
        body, name="all_reduce_small",
        in_specs=[pl.BlockSpec(memory_space=pltpu.VMEM)],
        out_specs=pl.BlockSpec(memory_space=pltpu.VMEM),
        out_shape=jax.ShapeDtypeStruct((R, C), F32),
        scratch_shapes=[pltpu.VMEM((n_dev, R, C), F32), pltpu.SemaphoreType.DMA((n_dev - 1,)), pltpu.SemaphoreType.DMA((n_dev - 1,))],
        compiler_params=pltpu.CompilerParams(has_side_effects=True),
    )(v)


WEIGHT_NAMES = ("ffn1_w_gate", "ffn1_w_up", "ffn1_w_down", "ffn2_w_gate", "ffn2_w_up", "ffn2_w_down", "ln_gain", "ln_bias",
                "pool_w_in", "pool_w_group", "pool_scale", "pool_w_out", "attn_w_qkv", "attn_w_out")
MATRIX_NAMES = ("ffn1_w_gate", "ffn1_w_up", "ffn1_w_down", "ffn2_w_gate", "ffn2_w_up", "ffn2_w_down",
                "pool_w_in", "pool_w_group", "pool_w_out", "attn_w_qkv", "attn_w_out")


TRANSPOSED_NAMES = ("ffn1_w_gate", "ffn1_w_up", "ffn2_w_gate", "ffn2_w_up")


def _halves(name, w):
    if name in TRANSPOSED_NAMES:
        w = jnp.swapaxes(w, 1, 2)
    return w.reshape(2, -1, w.shape[-1])


def _unhalves(name, t, shape):
    if name in TRANSPOSED_NAMES:
        return jnp.swapaxes(t.reshape(shape[0], shape[2], shape[1]), 1, 2)
    return t.reshape(shape)


def kernel(x, ffn1_w_gate, ffn1_w_up, ffn1_w_down, ffn2_w_gate, ffn2_w_up, ffn2_w_down, ln_gain, ln_bias, pool_w_in, pool_w_group, pool_scale, pool_w_out, attn_w_qkv, attn_w_out, loss_target, m_ffn1_w_gate, m_ffn1_w_up, m_ffn1_w_down, m_ffn2_w_gate, m_ffn2_w_up, m_ffn2_w_down, m_ln_gain, m_ln_bias, m_pool_w_in, m_pool_w_group, m_pool_scale, m_pool_w_out, m_attn_w_qkv, m_attn_w_out, v_ffn1_w_gate, v_ffn1_w_up, v_ffn1_w_down, v_ffn2_w_gate, v_ffn2_w_up, v_ffn2_w_down, v_ln_gain, v_ln_bias, v_pool_w_in, v_pool_w_group, v_pool_scale, v_pool_w_out, v_attn_w_qkv, v_attn_w_out):
    weights = dict(zip(WEIGHT_NAMES, (ffn1_w_gate, ffn1_w_up, ffn1_w_down, ffn2_w_gate, ffn2_w_up, ffn2_w_down, ln_gain, ln_bias,
                                      pool_w_in, pool_w_group, pool_scale, pool_w_out, attn_w_qkv, attn_w_out)))
    moms = dict(zip(WEIGHT_NAMES, (m_ffn1_w_gate, m_ffn1_w_up, m_ffn1_w_down, m_ffn2_w_gate, m_ffn2_w_up, m_ffn2_w_down, m_ln_gain,
                                   m_ln_bias, m_pool_w_in, m_pool_w_group, m_pool_scale, m_pool_w_out, m_attn_w_qkv, m_attn_w_out)))
    vels = dict(zip(WEIGHT_NAMES, (v_ffn1_w_gate, v_ffn1_w_up, v_ffn1_w_down, v_ffn2_w_gate, v_ffn2_w_up, v_ffn2_w_down, v_ln_gain,
                                   v_ln_bias, v_pool_w_in, v_pool_w_group, v_pool_scale, v_pool_w_out, v_attn_w_qkv, v_attn_w_out)))
    S, D = x.shape[1], x.shape[2]
    FB = ffn1_w_gate.shape[2]
    QKV = attn_w_qkv.shape[2] * N_CHIPS
    G, CB = pool_w_group.shape[1], pool_w_group.shape[2]
    C = pool_w_group.shape[3]
    cx, cy, cc = _place()
    chip = 2 * cx + cy
    xs = x.reshape(S, D)
    target = loss_target.reshape(S, D)

    ln_rows = jnp.concatenate([ln_gain, ln_bias, jnp.zeros((DEPTH, 2, ln_gain.shape[2]), F32)], axis=1)
    shards = [_halves(n, weights[n]).astype(BF16) for n in MATRIX_NAMES] + [ln_rows]
    placed = [lax.dynamic_update_slice(lax.empty((N_CHIPS,) + s.shape, s.dtype), s[None], (chip, 0, 0, 0)) for s in shards]
    gathered = all_gather_shards(shards, placed)
    full = dict(zip(MATRIX_NAMES, gathered[:-1]))
    wg = {f: full[f + "_w_gate"] for f in ("ffn1", "ffn2")}
    wu = {f: full[f + "_w_up"] for f in ("ffn1", "ffn2")}
    wd = {f: full[f + "_w_down"] for f in ("ffn1", "ffn2")}
    w_pool_in = full["pool_w_in"].reshape(D, D)
    w_pool_out = full["pool_w_out"].reshape(D, D)
    w_attn_out = full["attn_w_out"].reshape(D, D)
    w_group = full["pool_w_group"].reshape(N_CHIPS, G, CB, C).transpose(1, 0, 2, 3).reshape(G, N_CHIPS * CB, C)
    w_qkv = full["attn_w_qkv"].reshape(N_CHIPS, D, QKV // N_CHIPS).transpose(1, 0, 2).reshape(D, QKV)
    ln_full = gathered[-1].transpose(1, 2, 0, 3).reshape(DEPTH, 8, D)
    gain = lambda i, k: ln_full[i, k].reshape(1, D)
    bias = lambda i, k: ln_full[i, 3 + k].reshape(1, D)

    saved = []
    y = xs
    for i in range(DEPTH):
        y_in = y
        y, xh, rs, a, u = ffn_fwd(y_in, wg["ffn1"], wu["ffn1"], wd["ffn1"], i, gain(i, 0), bias(i, 0))
        f1 = (y_in, xh, rs, a, u)
        y_mid = y
        if i % 2 == 0:
            pu = mm_nn(y_mid, w_pool_in, F32)[0]
            mixed, pv = pool_mix(pu, w_group, pool_scale)
            y, xh, rs = proj_ln(pv, w_pool_out, y_mid, gain(i, 1), bias(i, 1))
            mix = (y_mid, xh, rs, mixed, pv)
        else:
            qkv = mm_nn(y_mid, w_qkv, BF16)
            parts = [attn_fwd(qkv, g, dil) for g, (_, dil) in enumerate(DIL_CONFIGS)]
            ao, lse = attn_combine([p[0] for p in parts], [p[1] for p in parts])
            y, xh, rs = proj_ln(ao, w_attn_out, y_mid, gain(i, 1), bias(i, 1))
            mix = (y_mid, xh, rs, qkv, ao, lse)
        y_in2 = y
        y, xh, rs, a, u = ffn_fwd(y_in2, wg["ffn2"], wu["ffn2"], wd["ffn2"], i, gain(i, 2), bias(i, 2))
        f2 = (y_in2, xh, rs, a, u)
        saved.append((f1, mix, f2))

    dy, loss_part = loss_head(y, target)
    loss = lax.psum(loss_part[0, 0], ("x", "y", "c"))

    grads = {}
    dgain = [[None] * 3 for _ in range(DEPTH)]
    dbias = [[None] * 3 for _ in range(DEPTH)]
    dscale = None

    def ffn_backward(name, i, dy, state):
        y_in, xh, rs, a, u = state
        k = 0 if name == "ffn1" else 2
        dz, dgain[i][k], dbias[i][k] = ln_bwd(dy, xh, rs, gain(i, k))
        dx, h, da, du = ffn_bwd(dz, a, u, wg[name], wu[name], wd[name], i)
        names = [name + s for s in ("_w_gate", "_w_up", "_w_down")]
        prev = [grads[n] for n in names] if names[0] in grads else None
        outs = mm_tn([(da, y_in, 1.0), (du, y_in, 1.0), (h, dz, MACARON_WEIGHT)], nblk=N_CHIPS, out_shape=(N_CHIPS, DEPTH, FB, D),
                     out_block=(None, None, FB, D), out_index=lambda j: (j, i, 0, 0), alias=prev, name="ffn_wgrad")
        grads.update(zip(names, outs))
        return dx

    def square_grad(a, b):
        return mm_tn([(a, b, 1.0)], nblk=1, out_shape=(D, D), out_block=(D, D), out_index=lambda j: (0, 0), name="square_wgrad")[0]

    for i in reversed(range(DEPTH)):
        f1, mix, f2 = saved[i]
        dy = ffn_backward("ffn2", i, dy, f2)
        dz, dgain[i][1], dbias[i][1] = ln_bwd(dy, mix[1], mix[2], gain(i, 1))
        if i % 2 == 0:
            y_mid, _, _, mixed, pv = mix
            grads["pool_w_out"] = square_grad(pv, dz)
            dv = mm_nt(dz, w_pool_out, None, a_blocked=False)
            du, dwg, dscale = pool_mix_bwd(dv, mixed, w_group, pool_scale)
            grads["pool_w_group"] = dwg.reshape(G, N_CHIPS, CB, C).transpose(1, 0, 2, 3)
            grads["pool_w_in"] = square_grad(y_mid, du)
            dy = mm_nt(du, w_pool_in, dz, a_blocked=False)
        else:
            y_mid, _, _, qkv, ao, lse = mix
            grads["attn_w_out"] = square_grad(ao, dz)
            dao = mm_nt(dz, w_attn_out, None, a_blocked=False)
            dqkv = None
            for g, (_, dil) in enumerate(DIL_CONFIGS):
                dqkv = attn_bwd(qkv, dao, ao, lse, g, dil, dqkv)
            n_blocks = QKV // D
            g_qkv = mm_tn([(y_mid, dqkv, 1.0)], nblk=n_blocks, out_shape=(D, QKV), out_block=(D, D),
                          out_index=lambda j: (0, j), name="qkv_wgrad")[0]
            grads["attn_w_qkv"] = g_qkv.reshape(D, N_CHIPS, QKV // N_CHIPS).transpose(1, 0, 2)
            dy = mm_nt(dqkv, w_qkv, dz, a_blocked=True)
        dy = ffn_backward("ffn1", i, dy, f1)
    grad_x = dy.reshape(x.shape)

    def as_blocks(g):
        return g.reshape(N_CHIPS, 2, -1, g.shape[-1])

    blocks = [as_blocks(grads[n]) for n in MATRIX_NAMES]
    from_sibling = sibling_exchange_halves(blocks)
    core = cc.reshape(1).astype(jnp.int32)
    chip_id = chip.reshape(1).astype(jnp.int32)
    pair_sums = [pair_sum(core, b, r) for b, r in zip(blocks, from_sibling)]
    from_chips = chip_exchange(pair_sums)
    own_half = dict(zip(MATRIX_NAMES, [chip_sum(chip_id, p, r) for p, r in zip(pair_sums, from_chips)]))
    other_half = dict(zip(MATRIX_NAMES, sibling_share([own_half[n] for n in MATRIX_NAMES])))

    small = jnp.concatenate([jnp.concatenate(dgain[i] + dbias[i], axis=0) for i in range(DEPTH)] + [dscale, jnp.zeros((3, D), F32)], axis=0)
    small = all_reduce_small(small)
    per_layer = small[:6 * DEPTH].reshape(DEPTH, 6, D)
    cols = D // N_CHIPS
    small_grads = {"ln_gain": lax.dynamic_slice_in_dim(per_layer[:, 0:3], chip * cols, cols, axis=2),
                   "ln_bias": lax.dynamic_slice_in_dim(per_layer[:, 3:6], chip * cols, cols, axis=2),
                   "pool_scale": small[6 * DEPTH:6 * DEPTH + 1]}

    grad_w, delta, new_m, new_v = {}, {}, {}, {}
    for n in WEIGHT_NAMES:
        shape = weights[n].shape
        if n in MATRIX_NAMES:
            outs = adamw(core, _halves(n, weights[n]), own_half[n], other_half[n], _halves(n, moms[n]), _halves(n, vels[n]))
            grad_w[n], delta[n], new_m[n], new_v[n] = [_unhalves(n, t, shape) for t in outs]
        else:
            as3 = lambda t: t.reshape(1, -1, t.shape[-1])
            g2 = small_grads[n].reshape(-1, shape[-1])
            outs = adamw(core, as3(weights[n]), g2, g2, as3(moms[n]), as3(vels[n]))
            grad_w[n], delta[n], new_m[n], new_v[n] = [t.reshape(shape) for t in outs]

    return (loss, grad_x, *[grad_w[n] for n in WEIGHT_NAMES], *[delta[n] for n in WEIGHT_NAMES],
            *[new_m[n] for n in WEIGHT_NAMES], *[new_v[n] for n in WEIGHT_NAMES])
```

```python
import functools
import math

import numpy as np
import jax
import jax.numpy as jnp
from jax import lax
from jax.experimental import pallas as pl
from jax.experimental.pallas import tpu as pltpu

F32 = jnp.float32
BF16 = jnp.bfloat16

DEPTH = 2
ALPHA = (2.0 * DEPTH) ** 0.25
MACARON_WEIGHT = 0.5
LN_EPS = 1e-5
MASK_VALUE = -1e30
POOL_WINDOWS = (2, 4, 8, 16)
POOL_PAD = 16
HEAD_DIM = 64
N_HEADS = 16
DIL_CONFIGS = ((128, 1), (512, 4), (2048, 16))
ATTN_R = 64
ATTN_BQ = 128
ATTN_W = ATTN_BQ + 2 * ATTN_R
ATTN_UNROLL = 4
LANES = 128
ADAM_LR = 0.001
ADAM_B1 = 0.9
ADAM_B2 = 0.999
ADAM_EPS = 1e-08
ADAM_WD = 0.01
ADAM_STEP = 10
N_CHIPS = 4
VMEM_LIMIT = 56 * 1024 * 1024
MESH = pl.DeviceIdType.MESH
ANY = pl.BlockSpec(memory_space=pl.ANY)


def _params(sem=None, vmem=VMEM_LIMIT):
    return pltpu.CompilerParams(dimension_semantics=sem, vmem_limit_bytes=vmem)


def _alibi_slopes():
    n = len(DIL_CONFIGS) * N_HEADS
    s = 2.0 ** (-8.0 * np.arange(1, n + 1) / n)
    return s.reshape(len(DIL_CONFIGS), N_HEADS).astype(np.float32)


def _ln_fwd(z, g, b):
    mu = jnp.mean(z, axis=-1, keepdims=True)
    zc = z - mu
    var = jnp.mean(zc * zc, axis=-1, keepdims=True)
    rstd = lax.rsqrt(var + LN_EPS)
    xhat = zc * rstd
    return xhat * g + b, xhat, rstd


def _dot(a, b):
    return jnp.dot(a, b, preferred_element_type=F32)


def _dot_nt(a, b):
    return lax.dot_general(a, b, (((1,), (1,)), ((), ())), preferred_element_type=F32)


def _dot_tn(a, b):
    return lax.dot_general(a, b, (((0,), (0,)), ((), ())), preferred_element_type=F32)


def mm_nn(a, b, out_dtype, first_block=0, nb=None, tm=512):
    S, K = a.shape
    Nb = K
    nb = b.shape[1] // Nb if nb is None else nb

    def body(a_ref, b_ref, o_ref):
        o_ref[...] = _dot(a_ref[...].astype(BF16), b_ref[...]).astype(out_dtype)

    return pl.pallas_call(
        body, name="mm_nn",
        grid=(S // tm, nb),
        in_specs=[pl.BlockSpec((tm, K), lambda i, j: (i, 0)), pl.BlockSpec((K, Nb), lambda i, j: (0, first_block + j))],
        out_specs=pl.BlockSpec((None, tm, Nb), lambda i, j: (j, i, 0)),
        out_shape=jax.ShapeDtypeStruct((nb, S, Nb), out_dtype),
        compiler_params=_params(("parallel", "arbitrary")),
    )(a, b)


def proj_ln(a, w, resid, gain, bias, tm=512):
    S, K = a.shape
    D = w.shape[1]

    def body(a_ref, w_ref, r_ref, g_ref, b_ref, y_ref, xh_ref, rs_ref):
        z = ALPHA * r_ref[...] + _dot(a_ref[...].astype(BF16), w_ref[...])
        y, xh, rs = _ln_fwd(z, g_ref[...], b_ref[...])
        y_ref[...] = y
        xh_ref[...] = xh
        rs_ref[...] = rs

    row = pl.BlockSpec((tm, D), lambda i: (i, 0))
    vec = pl.BlockSpec((1, D), lambda i: (0, 0))
    return pl.pallas_call(
        body, name="proj_ln",
        grid=(S // tm,),
        in_specs=[pl.BlockSpec((tm, K), lambda i: (i, 0)), pl.BlockSpec((K, D), lambda i: (0, 0)), row, vec, vec],
        out_specs=[row, row, pl.BlockSpec((tm, 1), lambda i: (i, 0))],
        out_shape=[jax.ShapeDtypeStruct((S, D), F32), jax.ShapeDtypeStruct((S, D), F32), jax.ShapeDtypeStruct((S, 1), F32)],
        compiler_params=_params(("parallel",)),
    )(a, w, resid, gain, bias)


def mm_nt(a, w, resid, a_blocked, out_dtype=F32, tm=512):
    if a_blocked:
        nk, S, Kb = a.shape
        a_spec = pl.BlockSpec((None, tm, Kb), lambda i, n: (n, i, 0))
    else:
        S, Kb = a.shape
        nk = 1
        a_spec = pl.BlockSpec((tm, Kb), lambda i, n: (i, 0))
    M = w.shape[0]
    has_resid = resid is not None

    def body(*refs):
        if has_resid:
            a_ref, w_ref, r_ref, o_ref, acc = refs
        else:
            a_ref, w_ref, o_ref, acc = refs
        n = pl.program_id(1)
        part = _dot_nt(a_ref[...].astype(BF16), w_ref[...])

        @pl.when(n == 0)
        def _():
            acc[...] = part

        @pl.when(n > 0)
        def _():
            acc[...] += part

        @pl.when(n == nk - 1)
        def _():
            out = acc[...]
            if has_resid:
                out = out + ALPHA * r_ref[...]
            o_ref[...] = out.astype(out_dtype)

    row = pl.BlockSpec((tm, M), lambda i, n: (i, 0))
    in_specs = [a_spec, pl.BlockSpec((M, Kb), lambda i, n: (0, n))] + ([row] if has_resid else [])
    args = (a, w) + ((resid,) if has_resid else ())
    return pl.pallas_call(
        body, name="mm_nt",
        grid=(S // tm, nk),
        in_specs=in_specs,
        out_specs=row,
        out_shape=jax.ShapeDtypeStruct((S, M), out_dtype),
        scratch_shapes=[pltpu.VMEM((tm, M), F32)],
        compiler_params=_params(("parallel", "arbitrary")),
    )(*args)


def mm_nt_dilated(parts, dils, w, resid, tm=512):
    n_groups = len(parts)
    _, S, K = parts[0].shape
    M = w.shape[0]
    nk = 3 * n_groups

    def body(*refs):
        a_refs = refs[:n_groups]
        w_ref, r_ref, o_ref, group_acc, total = refs[n_groups:]
        n = pl.program_id(1)
        for g in range(n_groups):
            for k in range(3):
                @pl.when(n == 3 * g + k)
                def _():
                    part = _dot_nt(a_refs[g][...], w_ref[...])
                    for c in range(M // LANES):
                        lanes = slice(c * LANES, (c + 1) * LANES)
                        if k == 0:
                            group_acc[c] = part[:, lanes]
                        else:
                            group_acc[c] += part[:, lanes]
                        if k == 2:
                            _rows_from_dilated(group_acc.at[c], total.at[c], tm, dils[g], accumulate=g > 0)

        @pl.when(n == nk - 1)
        def _():
            for c in range(M // LANES):
                lanes = slice(c * LANES, (c + 1) * LANES)
                o_ref[:, lanes] = total[c] + ALPHA * r_ref[:, lanes]

    def a_spec(g):
        return pl.BlockSpec((None, tm, K), lambda i, n: (jnp.clip(n - 3 * g, 0, 2), i, 0))

    row = pl.BlockSpec((tm, M), lambda i, n: (i, 0))
    return pl.pallas_call(
        body, name="mm_nt_dilated",
        grid=(S // tm, nk),
        in_specs=[a_spec(g) for g in range(n_groups)] + [pl.BlockSpec((M, K), lambda i, n: (0, n)), row],
        out_specs=row,
        out_shape=jax.ShapeDtypeStruct((S, M), F32),
        scratch_shapes=[pltpu.VMEM((M // LANES, tm, LANES), F32), pltpu.VMEM((M // LANES, tm, LANES), F32)],
        compiler_params=_params(("parallel", "arbitrary")),
    )(*parts, w, resid)


def mm_tn(pairs, *, nblk, out_shape, out_block, out_index, alias=None, tk=512, name="mm_tn"):
    operands = []
    for a, b, _ in pairs:
        for t in (a, b):
            if not any(t is o for o in operands):
                operands.append(t)
    where = lambda t: next(i for i, o in enumerate(operands) if o is t)
    S = pairs[0][0].shape[-2]
    n_out, n_in = len(pairs), len(operands)
    n_alias = len(alias) if alias is not None else 0

    def spec(t):
        if t.ndim == 3:
            return pl.BlockSpec((None, tk, t.shape[-1]), lambda j, k: (j, k, 0))
        return pl.BlockSpec((tk, t.shape[-1]), lambda j, k: (k, 0))

    def body(*refs):
        refs = refs[n_alias:]
        in_refs, o_refs, accs = refs[:n_in], refs[n_in:n_in + n_out], refs[n_in + n_out:]
        k = pl.program_id(1)
        for (a, b, scale), o_ref, acc in zip(pairs, o_refs, accs):
            part = _dot_tn(in_refs[where(a)][...].astype(BF16), in_refs[where(b)][...].astype(BF16))

            @pl.when(k == 0)
            def _():
                acc[...] = part

            @pl.when(k > 0)
            def _():
                acc[...] += part

            @pl.when(k == S // tk - 1)
            def _():
                o_ref[...] = (scale * acc[...]).astype(BF16)

    out_spec = pl.BlockSpec(out_block, lambda j, k: out_index(j))
    outs = pl.pallas_call(
        body, name=name,
        grid=(nblk, S // tk),
        in_specs=[ANY] * n_alias + [spec(t) for t in operands],
        out_specs=[out_spec] * n_out,
        out_shape=[jax.ShapeDtypeStruct(out_shape, BF16)] * n_out,
        scratch_shapes=[pltpu.VMEM((a.shape[-1], b.shape[-1]), F32) for a, b, _ in pairs],
        input_output_aliases={i: i for i in range(n_alias)},
        compiler_params=_params(("parallel", "arbitrary")),
    )(*(tuple(alias) if alias is not None else ()), *operands)
    return list(outs)


def ffn_fwd(x, wg, wu, wd, layer, gain, bias, tm=512):
    S, D = x.shape
    nb, FB = wg.shape[0], wg.shape[2]

    def body(x_ref, wg_ref, wu_ref, wd_ref, g_ref, b_ref, y_ref, xh_ref, rs_ref, a_ref, u_ref, acc):
        j = pl.program_id(1)
        xb = x_ref[...].astype(BF16)
        a = _dot_nt(xb, wg_ref[...])
        u = _dot_nt(xb, wu_ref[...])
        a_ref[...] = a.astype(BF16)
        u_ref[...] = u.astype(BF16)
        h = a * jax.nn.sigmoid(a) * u
        part = _dot(h.astype(BF16), wd_ref[...])

        @pl.when(j == 0)
        def _():
            acc[...] = part

        @pl.when(j > 0)
        def _():
            acc[...] += part

        @pl.when(j == nb - 1)
        def _():
            z = ALPHA * x_ref[...] + MACARON_WEIGHT * acc[...]
            y, xh, rs = _ln_fwd(z, g_ref[...], b_ref[...])
            y_ref[...] = y
            xh_ref[...] = xh
            rs_ref[...] = rs

    row = pl.BlockSpec((tm, D), lambda i, j: (i, 0))
    vec = pl.BlockSpec((1, D), lambda i, j: (0, 0))
    w_out = pl.BlockSpec((None, None, FB, D), lambda i, j: (j, layer, 0, 0))
    act = pl.BlockSpec((None, tm, FB), lambda i, j: (j, i, 0))
    return pl.pallas_call(
        body, name="ffn_fwd",
        grid=(S // tm, nb),
        in_specs=[row, w_out, w_out, w_out, vec, vec],
        out_specs=[row, row, pl.BlockSpec((tm, 1), lambda i, j: (i, 0)), act, act],
        out_shape=[jax.ShapeDtypeStruct((S, D), F32), jax.ShapeDtypeStruct((S, D), F32), jax.ShapeDtypeStruct((S, 1), F32),
                   jax.ShapeDtypeStruct((nb, S, FB), BF16), jax.ShapeDtypeStruct((nb, S, FB), BF16)],
        scratch_shapes=[pltpu.VMEM((tm, D), F32)],
        compiler_params=_params(("parallel", "arbitrary")),
    )(x, wg, wu, wd, gain, bias)


def ffn_bwd(dz, a, u, wg, wu, wd, layer, tm=512):
    S, D = dz.shape
    nb, FB = wg.shape[0], wg.shape[2]

    def body(dz_ref, a_ref, u_ref, wg_ref, wu_ref, wd_ref, dx_ref, h_ref, da_ref, du_ref, acc):
        j = pl.program_id(1)
        dzb = (MACARON_WEIGHT * dz_ref[...]).astype(BF16)
        dh = _dot_nt(dzb, wd_ref[...])
        av = a_ref[...].astype(F32)
        uv = u_ref[...].astype(F32)
        s = jax.nn.sigmoid(av)
        silu = av * s
        h_ref[...] = (silu * uv).astype(BF16)
        da = (dh * uv * (s * (1.0 + av * (1.0 - s)))).astype(BF16)
        du = (dh * silu).astype(BF16)
        da_ref[...] = da
        du_ref[...] = du
        part = _dot(da, wg_ref[...]) + _dot(du, wu_ref[...])

        @pl.when(j == 0)
        def _():
            acc[...] = part

        @pl.when(j > 0)
        def _():
            acc[...] += part

        @pl.when(j == nb - 1)
        def _():
            dx_ref[...] = ALPHA * dz_ref[...] + acc[...]

    row = pl.BlockSpec((tm, D), lambda i, j: (i, 0))
    w_out = pl.BlockSpec((None, None, FB, D), lambda i, j: (j, layer, 0, 0))
    act = pl.BlockSpec((None, tm, FB), lambda i, j: (j, i, 0))
    act_shape = jax.ShapeDtypeStruct((nb, S, FB), BF16)
    return pl.pallas_call(
        body, name="ffn_bwd",
        grid=(S // tm, nb),
        in_specs=[row, act, act, w_out, w_out, w_out],
        out_specs=[row, act, act, act],
        out_shape=[jax.ShapeDtypeStruct((S, D), F32), act_shape, act_shape, act_shape],
        scratch_shapes=[pltpu.VMEM((tm, D), F32)],
        compiler_params=_params(("parallel", "arbitrary")),
    )(dz, a, u, wg, wu, wd)


def ln_bwd(dy, xhat, rstd, gain, tm=512):
    S, D = dy.shape

    def body(dy_ref, xh_ref, rs_ref, g_ref, dz_ref, dg_ref, db_ref):
        i = pl.program_id(0)
        dy = dy_ref[...]
        xh = xh_ref[...]
        dxh = dy * g_ref[...]
        m1 = jnp.mean(dxh, axis=-1, keepdims=True)
        m2 = jnp.mean(dxh * xh, axis=-1, keepdims=True)
        dz_ref[...] = rs_ref[...] * (dxh - m1 - xh * m2)
        dg = jnp.sum(dy * xh, axis=0, keepdims=True)
        db = jnp.sum(dy, axis=0, keepdims=True)

        @pl.when(i == 0)
        def _():
            dg_ref[...] = dg
            db_ref[...] = db

        @pl.when(i > 0)
        def _():
            dg_ref[...] += dg
            db_ref[...] += db

    row = pl.BlockSpec((tm, D), lambda i: (i, 0))
    vec = pl.BlockSpec((1, D), lambda i: (0, 0))
    return pl.pallas_call(
        body, name="ln_bwd",
        grid=(S // tm,),
        in_specs=[row, row, pl.BlockSpec((tm, 1), lambda i: (i, 0)), vec],
        out_specs=[row, vec, vec],
        out_shape=[jax.ShapeDtypeStruct((S, D), F32), jax.ShapeDtypeStruct((1, D), F32), jax.ShapeDtypeStruct((1, D), F32)],
        compiler_params=_params(("arbitrary",)),
    )(dy, xhat, rstd, gain)


def loss_head(y, target, tm=512):
    S, D = y.shape

    def body(y_ref, t_ref, dy_ref, l_ref):
        i = pl.program_id(0)
        e = y_ref[...] - t_ref[...]
        dy_ref[...] = e / D
        part = 0.5 * jnp.sum(jnp.mean(e * e, axis=-1, keepdims=True), axis=0, keepdims=True)

        @pl.when(i == 0)
        def _():
            l_ref[...] = part

        @pl.when(i > 0)
        def _():
            l_ref[...] += part

    row = pl.BlockSpec((tm, D), lambda i: (i, 0))
    return pl.pallas_call(
        body, name="loss_head",
        grid=(S // tm,),
        in_specs=[row, row],
        out_specs=[row, pl.BlockSpec((1, 1), lambda i: (0, 0))],
        out_shape=[jax.ShapeDtypeStruct((S, D), F32), jax.ShapeDtypeStruct((1, 1), F32)],
        compiler_params=_params(("arbitrary",)),
    )(y, target)


def _pool_window(xp, g):
    n = xp.shape[0]
    w = xp + pltpu.roll(xp, 1, 0)
    out = w
    for level, shift in enumerate((1, 2, 4), start=1):
        w = pltpu.roll(w, shift, 0) + pltpu.roll(w, n - shift, 0)
        out = jnp.where(g >= level, w, out)
    return out


def _pool_count(S, C, g):
    half = lax.shift_left(jnp.int32(1), g)
    t = lax.broadcasted_iota(jnp.int32, (S, C), 0)
    return (jnp.minimum(t + half, S) - jnp.maximum(t - half, 0)).astype(F32)


def pool_mix(u, wgrp, scale):
    S, D = u.shape
    G, C = wgrp.shape[0], wgrp.shape[1]

    def body(u_ref, w_ref, s_ref, mix_ref, v_ref, pad):
        g = pl.program_id(0)
        zeros = jnp.zeros((POOL_PAD, C), F32)
        pad[pl.ds(0, POOL_PAD), :] = zeros
        pad[pl.ds(POOL_PAD + S, POOL_PAD), :] = zeros
        pad[pl.ds(POOL_PAD, S), :] = u_ref[...]
        win = _pool_window(pad[...], g)[POOL_PAD:POOL_PAD + S]
        mixed = (win / _pool_count(S, C, g) - u_ref[...]).astype(BF16)
        mix_ref[...] = mixed
        v_ref[...] = _dot(mixed, w_ref[...]) * s_ref[...]

    col = pl.BlockSpec((S, C), lambda g: (0, g))
    return pl.pallas_call(
        body, name="pool_mix",
        grid=(G,),
        in_specs=[col, pl.BlockSpec((None, C, C), lambda g: (g, 0, 0)), pl.BlockSpec((1, C), lambda g: (0, g))],
        out_specs=[col, col],
        out_shape=[jax.ShapeDtypeStruct((S, D), BF16), jax.ShapeDtypeStruct((S, D), F32)],
        scratch_shapes=[pltpu.VMEM((S + 2 * POOL_PAD, C), F32)],
        compiler_params=_params(("arbitrary",)),
    )(u, wgrp, scale)


def pool_mix_bwd(dv, mixed, wgrp, scale):
    S, D = dv.shape
    G, C = wgrp.shape[0], wgrp.shape[1]

    def body(dv_ref, mix_ref, w_ref, s_ref, du_ref, dw_ref, ds_ref, pad):
        g = pl.program_id(0)
        mixed = mix_ref[...]
        dv = dv_ref[...]
        yg = _dot(mixed, w_ref[...])
        ds_ref[...] = jnp.sum(dv * yg, axis=0, keepdims=True)
        dyg = (dv * s_ref[...]).astype(BF16)
        dw_ref[...] = _dot_tn(mixed, dyg).astype(BF16)
        dmix = _dot_nt(dyg, w_ref[...])
        zeros = jnp.zeros((POOL_PAD, C), F32)
        pad[pl.ds(0, POOL_PAD), :] = zeros
        pad[pl.ds(POOL_PAD + S, POOL_PAD), :] = zeros
        pad[pl.ds(POOL_PAD, S), :] = dmix / _pool_count(S, C, g)
        win = _pool_window(pad[...], g)
        win = pltpu.roll(win, win.shape[0] - 1, 0)[POOL_PAD:POOL_PAD + S]
        du_ref[...] = win - dmix

    col = pl.BlockSpec((S, C), lambda g: (0, g))
    return pl.pallas_call(
        body, name="pool_mix_bwd",
        grid=(G,),
        in_specs=[col, col, pl.BlockSpec((None, C, C), lambda g: (g, 0, 0)), pl.BlockSpec((1, C), lambda g: (0, g))],
        out_specs=[col, pl.BlockSpec((None, C, C), lambda g: (g, 0, 0)), pl.BlockSpec((1, C), lambda g: (0, g))],
        out_shape=[jax.ShapeDtypeStruct((S, D), F32), jax.ShapeDtypeStruct((G, C, C), BF16), jax.ShapeDtypeStruct((1, D), F32)],
        scratch_shapes=[pltpu.VMEM((S + 2 * POOL_PAD, C), F32)],
        compiler_params=_params(("arbitrary",)),
    )(dv, mixed, wgrp, scale)


PERM_BLOCK = 256


def _dilated_runs(rows, d):
    n = PERM_BLOCK // d
    return [(c * PERM_BLOCK, r, n) for c in range(rows // PERM_BLOCK) for r in range(d)]


def _rows_to_dilated(src_ref, dst_ref, rows, d):
    for base, r, n in _dilated_runs(rows, d):
        dst_ref[pl.ds(base + r * n, n), :] = src_ref[pl.ds(base + r, n, stride=d), :].astype(dst_ref.dtype)


def _rows_from_dilated(src_ref, dst_ref, rows, d, accumulate=False):
    for base, r, n in _dilated_runs(rows, d):
        at = pl.ds(base + r, n, stride=d)
        v = src_ref[pl.ds(base + r * n, n), :]
        dst_ref[at, :] = dst_ref[at, :] + v if accumulate else v


def dilate_rows(x, dils, out_dtype, tm=1024):
    S, D = x.shape

    def body(x_ref, *o_refs):
        for d, o_ref in zip(dils, o_refs):
            _rows_to_dilated(x_ref, o_ref, tm, d)

    tile = pl.BlockSpec((tm, LANES), lambda i, j: (i, j))
    return pl.pallas_call(
        body, name="dilate_rows",
        grid=(S // tm, D // LANES),
        in_specs=[tile],
        out_specs=[tile] * len(dils),
        out_shape=[jax.ShapeDtypeStruct((S, D), out_dtype)] * len(dils),
        compiler_params=_params(("parallel", "parallel")),
    )(x)


def _slope_table(group, dilation):
    s = _alibi_slopes()[group].reshape(N_HEADS // 2, 2, 1, 1) * float(dilation)
    return jnp.asarray(np.broadcast_to(s, (N_HEADS // 2, 2, 1, ATTN_W)).copy())


def _residue_shape(S, D, d):
    return (S, D) if d == 1 else (S // PERM_BLOCK, d, PERM_BLOCK // d, D)


def _residue_view(x, d):
    return x.reshape(x.shape[:-2] + _residue_shape(x.shape[-2], x.shape[-1], d))


def _residue_spec(lead_block, lead_index, S, d):
    if d == 1:
        return pl.BlockSpec(lead_block + (S, LANES), lambda hp, r: lead_index + (0, hp))
    return pl.BlockSpec(lead_block + (S // PERM_BLOCK, None, PERM_BLOCK // d, LANES), lambda hp, r: lead_index + (0, r, 0, hp))


def _whole(ref, lead, L):
    return ref[lead + (slice(None),) * (len(ref.shape) - len(lead))].reshape(L, LANES)


def _query_rows(lead, i, d):
    if d == 1:
        return lead + (pl.ds(pl.multiple_of(i * ATTN_BQ, ATTN_BQ), ATTN_BQ), slice(None)), (ATTN_BQ, LANES)
    n = PERM_BLOCK // d
    return lead + (pl.ds(i * (ATTN_BQ // n), ATTN_BQ // n), slice(None), slice(None)), (ATTN_BQ // n, n, LANES)


def _load_query_rows(ref, lead, i, d):
    at, _ = _query_rows(lead, i, d)
    return ref[at].reshape(ATTN_BQ, LANES)


def _store_query_rows(ref, lead, i, d, value):
    at, shape = _query_rows(lead, i, d)
    ref[at] = value.reshape(shape)


def _stage_keys(dst, src_ref, L):
    rows = _whole(src_ref, (), L)
    lane = lax.broadcasted_iota(jnp.int32, (L, LANES), 1)
    zeros = jnp.zeros((ATTN_R, LANES), dst.dtype)
    for h in range(2):
        mine = (lane < HEAD_DIM) if h == 0 else (lane >= HEAD_DIM)
        dst[h, pl.ds(0, ATTN_R), :] = zeros
        dst[h, pl.ds(ATTN_R + L, ATTN_R), :] = zeros
        dst[h, pl.ds(ATTN_R, L), :] = jnp.where(mine, rows, jnp.zeros_like(rows))


def _fill_bias(bias, sl_ref):
    a = lax.broadcasted_iota(jnp.int32, (ATTN_BQ, ATTN_W), 0)
    c = lax.broadcasted_iota(jnp.int32, (ATTN_BQ, ATTN_W), 1)
    rel = jnp.abs(c - ATTN_R - a)
    band = rel <= ATTN_R
    after_start = c >= ATTN_R
    before_end = c < ATTN_BQ + ATTN_R
    for h in range(2):
        base = -(sl_ref[h] * rel.astype(F32))
        for variant in range(4):
            ok = band
            if variant & 1:
                ok = ok & after_start
            if variant & 2:
                ok = ok & before_end
            bias[variant, h] = jnp.where(ok, base, MASK_VALUE)


def _bias_variant(i, nq):
    return jnp.where(i == 0, 1, 0) + jnp.where(i == nq - 1, 2, 0)


def attn_fwd(qkv, group, dilation):
    _, S, D = qkv.shape
    d = dilation
    L = S // d
    nq = L // ATTN_BQ
    ncol = D // LANES
    view = _residue_view(qkv, d)
    slopes = _slope_table(group, d)
    scale = HEAD_DIM ** -0.5

    def body(q_ref, k_ref, v_ref, sl_ref, o_ref, lse_ref, k2, v2, bias):
        @pl.when(pl.program_id(1) == 0)
        def _():
            _fill_bias(bias, sl_ref)

        _stage_keys(k2, k_ref, L)
        _stage_keys(v2, v_ref, L)
        head0 = lax.broadcasted_iota(jnp.int32, (ATTN_BQ, LANES), 1) < HEAD_DIM

        def step(i, carry):
            variant = _bias_variant(i, nq)
            win = pl.ds(pl.multiple_of(i * ATTN_BQ, ATTN_BQ), ATTN_W)
            qs = _load_query_rows(q_ref, (), i, d) * jnp.asarray(scale, BF16)
            acc, ms, ls = None, [], []
            for h in range(2):
                s = _dot_nt(qs, k2[h, win, :]) + bias[variant, h]
                m = jnp.max(s, axis=-1, keepdims=True)
                e = jnp.exp(s - m)
                ls.append(jnp.sum(e, axis=-1, keepdims=True))
                ms.append(m)
                part = _dot(e.astype(BF16), v2[h, win, :])
                acc = part if acc is None else acc + part
            out = acc * jnp.where(head0, 1.0 / ls[0], 1.0 / ls[1])
            lse = jnp.where(head0, ms[0] + jnp.log(ls[0]), ms[1] + jnp.log(ls[1]))
            _store_query_rows(o_ref, (), i, d, out)
            _store_query_rows(lse_ref, (), i, d, lse)
            return carry

        lax.fori_loop(0, nq, step, 0, unroll=min(ATTN_UNROLL, nq))

    def col(which):
        return _residue_spec((None,), (which,), S, d)

    out = _residue_spec((), (), S, d)
    o, lse = pl.pallas_call(
        body, name=f"attn_fwd_g{group}",
        grid=(ncol, d),
        in_specs=[col(0), col(1), col(2), pl.BlockSpec((None, 2, 1, ATTN_W), lambda hp, r: (hp, 0, 0, 0))],
        out_specs=[out, out],
        out_shape=[jax.ShapeDtypeStruct(_residue_shape(S, D, d), F32)] * 2,
        scratch_shapes=[pltpu.VMEM((2, L + 2 * ATTN_R, LANES), BF16), pltpu.VMEM((2, L + 2 * ATTN_R, LANES), BF16),
                        pltpu.VMEM((4, 2, ATTN_BQ, ATTN_W), F32)],
        compiler_params=_params(("arbitrary", "arbitrary")),
    )(view, view, view, slopes)
    return o.reshape(S, D), lse.reshape(S, D)


def attn_combine(os, lses, dils, tm=1024):
    S, D = os[0].shape
    n = len(os)
    n_moved = sum(d > 1 for d in dils)

    def body(*refs):
        o_refs, l_refs, out_ref, lse_ref = list(refs[:n]), list(refs[n:2 * n]), refs[2 * n], refs[2 * n + 1]
        spare = list(refs[2 * n + 2:])
        for g, d in enumerate(dils):
            if d > 1:
                for which in (o_refs, l_refs):
                    token_order = spare.pop()
                    _rows_from_dilated(which[g], token_order, tm, d)
                    which[g] = token_order
        ls = [r[...] for r in l_refs]
        m = functools.reduce(jnp.maximum, ls)
        es = [jnp.exp(l - m) for l in ls]
        tot = functools.reduce(lambda x, y: x + y, es)
        inv = 1.0 / tot
        out_ref[...] = functools.reduce(lambda x, y: x + y, [(e * inv) * r[...] for e, r in zip(es, o_refs)])
        lse_ref[...] = m + jnp.log(tot)

    tile = pl.BlockSpec((tm, LANES), lambda i, j: (i, j))
    return pl.pallas_call(
        body, name="attn_combine",
        grid=(S // tm, D // LANES),
        in_specs=[tile] * (2 * n),
        out_specs=[tile, tile],
        out_shape=[jax.ShapeDtypeStruct((S, D), F32), jax.ShapeDtypeStruct((S, D), F32)],
        scratch_shapes=[pltpu.VMEM((tm, LANES), F32)] * (2 * n_moved),
        compiler_params=_params(("parallel", "parallel")),
    )(*os, *lses)


def attn_bwd(qkv, do, o, lse, group, dilation):
    _, S, D = qkv.shape
    d = dilation
    L = S // d
    nq = L // ATTN_BQ
    ncol = D // LANES
    view = _residue_view(qkv, d)
    slopes = _slope_table(group, d)
    scale = HEAD_DIM ** -0.5

    def body(q_ref, k_ref, v_ref, do_ref, o_ref, lse_ref, sl_ref, dx_ref, k2, v2, dkacc, dvacc, bias):
        @pl.when(pl.program_id(1) == 0)
        def _():
            _fill_bias(bias, sl_ref)

        _stage_keys(k2, k_ref, L)
        _stage_keys(v2, v_ref, L)
        dkacc[...] = jnp.zeros_like(dkacc)
        dvacc[...] = jnp.zeros_like(dvacc)
        lane = lax.broadcasted_iota(jnp.int32, (ATTN_BQ, LANES), 1)
        heads = (lane < HEAD_DIM, lane >= HEAD_DIM)
        key_head0 = lax.broadcasted_iota(jnp.int32, (ATTN_W, LANES), 1) < HEAD_DIM

        def step(i, carry):
            variant = _bias_variant(i, nq)
            win = pl.ds(pl.multiple_of(i * ATTN_BQ, ATTN_BQ), ATTN_W)
            q = _load_query_rows(q_ref, (), i, d)
            qs = q * jnp.asarray(scale, BF16)
            dov = _load_query_rows(do_ref, (), i, d)
            prod = dov * _load_query_rows(o_ref, (), i, d)
            lse_v = _load_query_rows(lse_ref, (), i, d)
            dob = dov.astype(BF16)
            dq, dks, dvs = None, [], []
            for h in range(2):
                s = _dot_nt(qs, k2[h, win, :]) + bias[variant, h]
                lse_h = jnp.max(jnp.where(heads[h], lse_v, -jnp.inf), axis=-1, keepdims=True)
                dterm = jnp.sum(jnp.where(heads[h], prod, 0.0), axis=-1, keepdims=True)
                p = jnp.exp(s - lse_h)
                dp = _dot_nt(dob, v2[h, win, :])
                ds = (p * (dp - dterm) * scale).astype(BF16)
                dvs.append(_dot_tn(p.astype(BF16), dob))
                dks.append(_dot_tn(ds, q))
                part = _dot(ds, k2[h, win, :])
                dq = part if dq is None else dq + part
            dvacc[win, :] += jnp.where(key_head0, dvs[0], dvs[1])
            dkacc[win, :] += jnp.where(key_head0, dks[0], dks[1])
            _store_query_rows(dx_ref, (0,), i, d, dq.astype(BF16))
            return carry

        lax.fori_loop(0, nq, step, 0, unroll=min(ATTN_UNROLL, nq))
        block_shape = dx_ref.shape[1:]
        dx_ref[1] = dkacc[pl.ds(ATTN_R, L), :].astype(BF16).reshape(block_shape)
        dx_ref[2] = dvacc[pl.ds(ATTN_R, L), :].astype(BF16).reshape(block_shape)

    def col(which):
        return _residue_spec((None,), (which,), S, d)

    act = _residue_spec((), (), S, d)
    out = pl.pallas_call(
        body, name=f"attn_bwd_g{group}",
        grid=(ncol, d),
        in_specs=[col(0), col(1), col(2), act, act, act, pl.BlockSpec((None, 2, 1, ATTN_W), lambda hp, r: (hp, 0, 0, 0))],
        out_specs=_residue_spec((3,), (0,), S, d),
        out_shape=jax.ShapeDtypeStruct((3,) + _residue_shape(S, D, d), BF16),
        scratch_shapes=[pltpu.VMEM((2, L + 2 * ATTN_R, LANES), BF16), pltpu.VMEM((2, L + 2 * ATTN_R, LANES), BF16),
                        pltpu.VMEM((L + 2 * ATTN_R, LANES), F32), pltpu.VMEM((L + 2 * ATTN_R, LANES), F32),
                        pltpu.VMEM((4, 2, ATTN_BQ, ATTN_W), F32)],
        compiler_params=_params(("arbitrary", "arbitrary")),
    )(view, view, view, _residue_view(do, d), _residue_view(o, d), _residue_view(lse, d), slopes)
    return out.reshape(3, S, D)


TILE_ELEMS = 256 * 1024


def _row_tile(R, C):
    if R * C <= TILE_ELEMS or R % 16:
        return R
    return max(t for t in range(16, R + 1, 16) if R % t == 0 and (t * C <= TILE_ELEMS or t == 16))


def pair_sum(core, g, recv):
    _, _, R, C = g.shape
    tr = _row_tile(R, C)

    def body(c_ref, g_ref, r_ref, o_ref):
        o_ref[...] = (g_ref[...].astype(F32) + r_ref[...].astype(F32)).astype(BF16)

    blk = pl.BlockSpec((None, tr, C), lambda d, i, c_ref: (d, i, 0))
    return pl.pallas_call(
        body, name="pair_sum",
        grid_spec=pltpu.PrefetchScalarGridSpec(
            num_scalar_prefetch=1, grid=(N_CHIPS, R // tr),
            in_specs=[pl.BlockSpec((None, None, tr, C), lambda d, i, c_ref: (d, c_ref[0], i, 0)), blk],
            out_specs=blk),
        out_shape=jax.ShapeDtypeStruct((N_CHIPS, R, C), BF16),
        compiler_params=_params(("parallel", "parallel")),
    )(core, g, recv)


def chip_sum(chip, own, recv):
    _, R, C = own.shape
    tr = _row_tile(R, C)
    slot_of_relation = {2: 0, 1: 1, 3: 2}

    def body(chip_ref, own_ref, r_ref, o_ref):
        me = chip_ref[0]
        mine = own_ref[...].astype(F32)
        theirs = {rel: r_ref[k].astype(F32) for rel, k in slot_of_relation.items()}
        acc = None
        for s in range(N_CHIPS):
            rel = jnp.bitwise_xor(me, s)
            part = jnp.where(rel == 0, mine, jnp.where(rel == 2, theirs[2], jnp.where(rel == 1, theirs[1], theirs[3])))
            acc = part if acc is None else acc + part
        o_ref[...] = acc

    return pl.pallas_call(
        body, name="chip_sum",
        grid_spec=pltpu.PrefetchScalarGridSpec(
            num_scalar_prefetch=1, grid=(R // tr,),
            in_specs=[pl.BlockSpec((None, tr, C), lambda i, chip_ref: (chip_ref[0], i, 0)),
                      pl.BlockSpec((N_CHIPS - 1, tr, C), lambda i, chip_ref: (0, i, 0))],
            out_specs=pl.BlockSpec((tr, C), lambda i, chip_ref: (i, 0))),
        out_shape=jax.ShapeDtypeStruct((R, C), F32),
        compiler_params=_params(("parallel",)),
    )(chip, own, recv)


def adamw(core, w, g_own, g_recv, m, v):
    H, R, C = w.shape
    tr = _row_tile(R, C)

    def body(c_ref, w_ref, go_ref, gr_ref, m_ref, v_ref, g_ref, d_ref, nm_ref, nv_ref):
        g = jnp.where(pl.program_id(0) == c_ref[0], go_ref[...], gr_ref[...])
        m = ADAM_B1 * m_ref[...] + (1.0 - ADAM_B1) * g
        v = ADAM_B2 * v_ref[...] + (1.0 - ADAM_B2) * (g * g)
        m_hat = m / (1.0 - ADAM_B1 ** ADAM_STEP)
        v_hat = v / (1.0 - ADAM_B2 ** ADAM_STEP)
        g_ref[...] = g
        d_ref[...] = -ADAM_LR * (m_hat / (jnp.sqrt(v_hat) + ADAM_EPS) + ADAM_WD * w_ref[...])
        nm_ref[...] = m
        nv_ref[...] = v

    blk = pl.BlockSpec((None, tr, C), lambda h, i, c_ref: (h, i, 0))
    half = pl.BlockSpec((tr, C), lambda h, i, c_ref: (i, 0))
    shape = jax.ShapeDtypeStruct((H, R, C), F32)
    return pl.pallas_call(
        body, name="adamw",
        grid_spec=pltpu.PrefetchScalarGridSpec(
            num_scalar_prefetch=1, grid=(H, R // tr),
            in_specs=[blk, half, half, blk, blk],
            out_specs=[blk] * 4),
        out_shape=[shape] * 4,
        compiler_params=_params(("parallel", "parallel")),
    )(core, w, g_own, g_recv, m, v)


def _place():
    return lax.axis_index("x"), lax.axis_index("y"), lax.axis_index("c")


def _other_chips(x, y):
    return [(2 * (1 - x) + y, (1 - x, y)), (2 * x + (1 - y), (x, 1 - y)), (2 * (1 - x) + (1 - y), (1 - x, 1 - y))]


def all_gather_shards(shards, placed):
    n = len(shards)

    def body(*refs):
        ins, outs = refs[:n], refs[2 * n:3 * n]
        send_sems, recv_sems = refs[3 * n:]
        x, y, c = _place()
        me = 2 * x + y
        sibling = (x, y, 1 - c)
        chips = _other_chips(x, y)

        def copy(a, k, src, dst, to):
            return pltpu.make_async_remote_copy(src_ref=src, dst_ref=dst, send_sem=send_sems.at[a, k], recv_sem=recv_sems.at[a, k],
                                                device_id=to, device_id_type=MESH)

        sends = []
        for a in range(n):
            for k, (_, (px, py)) in enumerate(chips):
                cp = copy(a, k, ins[a].at[c], outs[a].at[me, c], (px, py, c))
                cp.start()
                sends.append(cp)
        for a in range(n):
            for k, (chip, _) in enumerate(chips):
                landed = outs[a].at[chip, c]
                copy(a, k, landed, landed, sibling).wait_recv()
                cp = copy(a, 3 + k, landed, landed, sibling)
                cp.start()
                sends.append(cp)
        for a in range(n):
            for k, (chip, _) in enumerate(chips):
                other = outs[a].at[chip, 1 - c]
                copy(a, 3 + k, other, other, sibling).wait_recv()
        for cp in sends:
            cp.wait_send()

    return pl.pallas_call(
        body, name="all_gather_shards",
        in_specs=[ANY] * (2 * n),
        out_specs=[ANY] * n,
        out_shape=[jax.ShapeDtypeStruct(p.shape, p.dtype) for p in placed],
        scratch_shapes=[pltpu.SemaphoreType.DMA((n, 6)), pltpu.SemaphoreType.DMA((n, 6))],
        input_output_aliases={n + a: a for a in range(n)},
        compiler_params=pltpu.CompilerParams(has_side_effects=True),
    )(*shards, *placed)


def sibling_exchange_halves(grads):
    n = len(grads)

    def body(*refs):
        ins, outs = refs[:n], refs[n:2 * n]
        send_sems, recv_sems = refs[2 * n:]
        x, y, c = _place()
        sibling = (x, y, 1 - c)
        copies = [pltpu.make_async_remote_copy(src_ref=ins[a].at[:, 1 - c], dst_ref=outs[a], send_sem=send_sems.at[a],
                                               recv_sem=recv_sems.at[a], device_id=sibling, device_id_type=MESH) for a in range(n)]
        for cp in copies:
            cp.start()
        for cp in copies:
            cp.wait()

    return pl.pallas_call(
        body, name="sibling_exchange_halves",
        in_specs=[ANY] * n,
        out_specs=[ANY] * n,
        out_shape=[jax.ShapeDtypeStruct((N_CHIPS,) + g.shape[2:], g.dtype) for g in grads],
        scratch_shapes=[pltpu.SemaphoreType.DMA((n,)), pltpu.SemaphoreType.DMA((n,))],
        compiler_params=pltpu.CompilerParams(has_side_effects=True),
    )(*grads)


def chip_exchange(sums):
    n = len(sums)

    def body(*refs):
        ins, outs = refs[:n], refs[n:2 * n]
        send_sems, recv_sems = refs[2 * n:]
        x, y, c = _place()
        copies = []
        for a in range(n):
            for k, (chip, (px, py)) in enumerate(_other_chips(x, y)):
                cp = pltpu.make_async_remote_copy(src_ref=ins[a].at[chip], dst_ref=outs[a].at[k], send_sem=send_sems.at[a, k],
                                                  recv_sem=recv_sems.at[a, k], device_id=(px, py, c), device_id_type=MESH)
                cp.start()
                copies.append(cp)
        for cp in copies:
            cp.wait()

    return pl.pallas_call(
        body, name="chip_exchange",
        in_specs=[ANY] * n,
        out_specs=[ANY] * n,
        out_shape=[jax.ShapeDtypeStruct((N_CHIPS - 1,) + s.shape[1:], s.dtype) for s in sums],
        scratch_shapes=[pltpu.SemaphoreType.DMA((n, 3)), pltpu.SemaphoreType.DMA((n, 3))],
        compiler_params=pltpu.CompilerParams(has_side_effects=True),
    )(*sums)


def sibling_share(halves):
    n = len(halves)

    def body(*refs):
        ins, outs = refs[:n], refs[n:2 * n]
        send_sems, recv_sems = refs[2 * n:]
        x, y, c = _place()
        copies = [pltpu.make_async_remote_copy(src_ref=ins[a], dst_ref=outs[a], send_sem=send_sems.at[a], recv_sem=recv_sems.at[a],
                                               device_id=(x, y, 1 - c), device_id_type=MESH) for a in range(n)]
        for cp in copies:
            cp.start()
        for cp in copies:
            cp.wait()

    return pl.pallas_call(
        body, name="sibling_share",
        in_specs=[ANY] * n,
        out_specs=[ANY] * n,
        out_shape=[jax.ShapeDtypeStruct(h.shape, h.dtype) for h in halves],
        scratch_shapes=[pltpu.SemaphoreType.DMA((n,)), pltpu.SemaphoreType.DMA((n,))],
        compiler_params=pltpu.CompilerParams(has_side_effects=True),
    )(*halves)


def all_reduce_small(v):
    R, C = v.shape
    n_dev = 8

    def body(v_ref, o_ref, buf, send_sems, recv_sems):
        x, y, c = _place()
        me = 4 * x + 2 * y + c
        buf[me] = v_ref[...]
        copies = []
        for rel in range(1, n_dev):
            fx, fy, fc = rel >> 2, (rel >> 1) & 1, rel & 1
            peer = (x ^ fx, y ^ fy, c ^ fc)
            cp = pltpu.make_async_remote_copy(src_ref=v_ref, dst_ref=buf.at[me], send_sem=send_sems.at[rel - 1],
                                              recv_sem=recv_sems.at[rel - 1], device_id=peer, device_id_type=MESH)
            cp.start()
            copies.append(cp)
        for cp in copies:
            cp.wait()
        acc = buf[0]
        for k in range(1, n_dev):
            acc = acc + buf[k]
        o_ref[...] = acc

    return pl.pallas_call(
        body, name="all_reduce_small",
        in_specs=[pl.BlockSpec(memory_space=pltpu.VMEM)],
        out_specs=pl.BlockSpec(memory_space=pltpu.VMEM),
        out_shape=jax.ShapeDtypeStruct((R, C), F32),
        scratch_shapes=[pltpu.VMEM((n_dev, R, C), F32), pltpu.SemaphoreType.DMA((n_dev - 1,)), pltpu.SemaphoreType.DMA((n_dev - 1,))],
        compiler_params=pltpu.CompilerParams(has_side_effects=True),
    )(v)


WEIGHT_NAMES = ("ffn1_w_gate", "ffn1_w_up", "ffn1_w_down", "ffn2_w_gate", "ffn2_w_up", "ffn2_w_down", "ln_gain", "ln_bias",
                "pool_w_in", "pool_w_group", "pool_scale", "pool_w_out", "attn_w_qkv", "attn_w_out")
MATRIX_NAMES = ("ffn1_w_gate", "ffn1_w_up", "ffn1_w_down", "ffn2_w_gate", "ffn2_w_up", "ffn2_w_down",
                "pool_w_in", "pool_w_group", "pool_w_out", "attn_w_qkv", "attn_w_out")


TRANSPOSED_NAMES = ("ffn1_w_gate", "ffn1_w_up", "ffn2_w_gate", "ffn2_w_up")


def _halves(name, w):
    if name in TRANSPOSED_NAMES:
        w = jnp.swapaxes(w, 1, 2)
    return w.reshape(2, -1, w.shape[-1])


def _unhalves(name, t, shape):
    if name in TRANSPOSED_NAMES:
        return jnp.swapaxes(t.reshape(shape[0], shape[2], shape[1]), 1, 2)
    return t.reshape(shape)


def kernel(x, ffn1_w_gate, ffn1_w_up, ffn1_w_down, ffn2_w_gate, ffn2_w_up, ffn2_w_down, ln_gain, ln_bias, pool_w_in, pool_w_group, pool_scale, pool_w_out, attn_w_qkv, attn_w_out, loss_target, m_ffn1_w_gate, m_ffn1_w_up, m_ffn1_w_down, m_ffn2_w_gate, m_ffn2_w_up, m_ffn2_w_down, m_ln_gain, m_ln_bias, m_pool_w_in, m_pool_w_group, m_pool_scale, m_pool_w_out, m_attn_w_qkv, m_attn_w_out, v_ffn1_w_gate, v_ffn1_w_up, v_ffn1_w_down, v_ffn2_w_gate, v_ffn2_w_up, v_ffn2_w_down, v_ln_gain, v_ln_bias, v_pool_w_in, v_pool_w_group, v_pool_scale, v_pool_w_out, v_attn_w_qkv, v_attn_w_out):
    weights = dict(zip(WEIGHT_NAMES, (ffn1_w_gate, ffn1_w_up, ffn1_w_down, ffn2_w_gate, ffn2_w_up, ffn2_w_down, ln_gain, ln_bias,
                                      pool_w_in, pool_w_group, pool_scale, pool_w_out, attn_w_qkv, attn_w_out)))
    moms = dict(zip(WEIGHT_NAMES, (m_ffn1_w_gate, m_ffn1_w_up, m_ffn1_w_down, m_ffn2_w_gate, m_ffn2_w_up, m_ffn2_w_down, m_ln_gain,
                                   m_ln_bias, m_pool_w_in, m_pool_w_group, m_pool_scale, m_pool_w_out, m_attn_w_qkv, m_attn_w_out)))
    vels = dict(zip(WEIGHT_NAMES, (v_ffn1_w_gate, v_ffn1_w_up, v_ffn1_w_down, v_ffn2_w_gate, v_ffn2_w_up, v_ffn2_w_down, v_ln_gain,
                                   v_ln_bias, v_pool_w_in, v_pool_w_group, v_pool_scale, v_pool_w_out, v_attn_w_qkv, v_attn_w_out)))
    S, D = x.shape[1], x.shape[2]
    FB = ffn1_w_gate.shape[2]
    QKV = attn_w_qkv.shape[2] * N_CHIPS
    G, CB = pool_w_group.shape[1], pool_w_group.shape[2]
    C = pool_w_group.shape[3]
    cx, cy, cc = _place()
    chip = 2 * cx + cy
    xs = x.reshape(S, D)
    target = loss_target.reshape(S, D)

    ln_rows = jnp.concatenate([ln_gain, ln_bias, jnp.zeros((DEPTH, 2, ln_gain.shape[2]), F32)], axis=1)
    shards = [_halves(n, weights[n]).astype(BF16) for n in MATRIX_NAMES] + [ln_rows]
    placed = [lax.dynamic_update_slice(lax.empty((N_CHIPS,) + s.shape, s.dtype), s[None], (chip, 0, 0, 0)) for s in shards]
    gathered = all_gather_shards(shards, placed)
    full = dict(zip(MATRIX_NAMES, gathered[:-1]))
    wg = {f: full[f + "_w_gate"] for f in ("ffn1", "ffn2")}
    wu = {f: full[f + "_w_up"] for f in ("ffn1", "ffn2")}
    wd = {f: full[f + "_w_down"] for f in ("ffn1", "ffn2")}
    w_pool_in = full["pool_w_in"].reshape(D, D)
    w_pool_out = full["pool_w_out"].reshape(D, D)
    w_attn_out = full["attn_w_out"].reshape(D, D)
    w_group = full["pool_w_group"].reshape(N_CHIPS, G, CB, C).transpose(1, 0, 2, 3).reshape(G, N_CHIPS * CB, C)
    w_qkv = full["attn_w_qkv"].reshape(N_CHIPS, D, QKV // N_CHIPS).transpose(1, 0, 2).reshape(D, QKV)
    ln_full = gathered[-1].transpose(1, 2, 0, 3).reshape(DEPTH, 8, D)
    gain = lambda i, k: ln_full[i, k].reshape(1, D)
    bias = lambda i, k: ln_full[i, 3 + k].reshape(1, D)

    dils = [dil for _, dil in DIL_CONFIGS]
    moved_dils = [dil for dil in dils if dil > 1]

    saved = []
    y = xs
    for i in range(DEPTH):
        y_in = y
        y, xh, rs, a, u = ffn_fwd(y_in, wg["ffn1"], wu["ffn1"], wd["ffn1"], i, gain(i, 0), bias(i, 0))
        f1 = (y_in, xh, rs, a, u)
        y_mid = y
        if i % 2 == 0:
            pu = mm_nn(y_mid, w_pool_in, F32)[0]
            mixed, pv = pool_mix(pu, w_group, pool_scale)
            y, xh, rs = proj_ln(pv, w_pool_out, y_mid, gain(i, 1), bias(i, 1))
            mix = (y_mid, xh, rs, mixed, pv)
        else:
            moved = dict(zip(moved_dils, dilate_rows(y_mid, moved_dils, BF16)))
            srcs = [moved.get(dil, y_mid) for dil in dils]
            qkvs = [mm_nn(src, w_qkv, BF16, first_block=3 * g, nb=3) for g, src in enumerate(srcs)]
            parts = [attn_fwd(qkv, g, dil) for g, (qkv, dil) in enumerate(zip(qkvs, dils))]
            ao, lse = attn_combine([p[0] for p in parts], [p[1] for p in parts], dils)
            y, xh, rs = proj_ln(ao, w_attn_out, y_mid, gain(i, 1), bias(i, 1))
            mix = (y_mid, xh, rs, srcs, qkvs, ao, lse)
        y_in2 = y
        y, xh, rs, a, u = ffn_fwd(y_in2, wg["ffn2"], wu["ffn2"], wd["ffn2"], i, gain(i, 2), bias(i, 2))
        f2 = (y_in2, xh, rs, a, u)
        saved.append((f1, mix, f2))

    dy, loss_part = loss_head(y, target)
    loss = lax.psum(loss_part[0, 0], ("x", "y", "c"))

    grads = {}
    dgain = [[None] * 3 for _ in range(DEPTH)]
    dbias = [[None] * 3 for _ in range(DEPTH)]
    dscale = None

    def ffn_backward(name, i, dy, state):
        y_in, xh, rs, a, u = state
        k = 0 if name == "ffn1" else 2
        dz, dgain[i][k], dbias[i][k] = ln_bwd(dy, xh, rs, gain(i, k))
        dx, h, da, du = ffn_bwd(dz, a, u, wg[name], wu[name], wd[name], i)
        names = [name + s for s in ("_w_gate", "_w_up", "_w_down")]
        prev = [grads[n] for n in names] if names[0] in grads else None
        outs = mm_tn([(da, y_in, 1.0), (du, y_in, 1.0), (h, dz, MACARON_WEIGHT)], nblk=N_CHIPS, out_shape=(N_CHIPS, DEPTH, FB, D),
                     out_block=(None, None, FB, D), out_index=lambda j: (j, i, 0, 0), alias=prev, name="ffn_wgrad")
        grads.update(zip(names, outs))
        return dx

    def square_grad(a, b):
        return mm_tn([(a, b, 1.0)], nblk=1, out_shape=(D, D), out_block=(D, D), out_index=lambda j: (0, 0), name="square_wgrad")[0]

    for i in reversed(range(DEPTH)):
        f1, mix, f2 = saved[i]
        dy = ffn_backward("ffn2", i, dy, f2)
        dz, dgain[i][1], dbias[i][1] = ln_bwd(dy, mix[1], mix[2], gain(i, 1))
        if i % 2 == 0:
            y_mid, _, _, mixed, pv = mix
            grads["pool_w_out"] = square_grad(pv, dz)
            dv = mm_nt(dz, w_pool_out, None, a_blocked=False)
            du, dwg, dscale = pool_mix_bwd(dv, mixed, w_group, pool_scale)
            grads["pool_w_group"] = dwg.reshape(G, N_CHIPS, CB, C).transpose(1, 0, 2, 3)
            grads["pool_w_in"] = square_grad(y_mid, du)
            dy = mm_nt(du, w_pool_in, dz, a_blocked=False)
        else:
            y_mid, _, _, srcs, qkvs, ao, lse = mix
            grads["attn_w_out"] = square_grad(ao, dz)
            dao = mm_nt(dz, w_attn_out, None, a_blocked=False)
            in_order = [dict(zip(moved_dils, dilate_rows(t, moved_dils, F32))) for t in (dao, ao, lse)]
            dqkvs = [attn_bwd(qkvs[g], *[m.get(dil, t) for m, t in zip(in_order, (dao, ao, lse))], g, dil)
                     for g, dil in enumerate(dils)]
            g_qkv = [mm_tn([(src, dqkv, 1.0)], nblk=3, out_shape=(D, 3 * D), out_block=(D, D), out_index=lambda j: (0, j),
                           name="qkv_wgrad")[0] for src, dqkv in zip(srcs, dqkvs)]
            grads["attn_w_qkv"] = jnp.concatenate(g_qkv, axis=1).reshape(D, N_CHIPS, QKV // N_CHIPS).transpose(1, 0, 2)
            dy = mm_nt_dilated(dqkvs, dils, w_qkv, dz)
        dy = ffn_backward("ffn1", i, dy, f1)
    grad_x = dy.reshape(x.shape)

    def as_blocks(g):
        return g.reshape(N_CHIPS, 2, -1, g.shape[-1])

    blocks = [as_blocks(grads[n]) for n in MATRIX_NAMES]
    from_sibling = sibling_exchange_halves(blocks)
    core = cc.reshape(1).astype(jnp.int32)
    chip_id = chip.reshape(1).astype(jnp.int32)
    pair_sums = [pair_sum(core, b, r) for b, r in zip(blocks, from_sibling)]
    from_chips = chip_exchange(pair_sums)
    own_half = dict(zip(MATRIX_NAMES, [chip_sum(chip_id, p, r) for p, r in zip(pair_sums, from_chips)]))
    other_half = dict(zip(MATRIX_NAMES, sibling_share([own_half[n] for n in MATRIX_NAMES])))

    small = jnp.concatenate([jnp.concatenate(dgain[i] + dbias[i], axis=0) for i in range(DEPTH)] + [dscale, jnp.zeros((3, D), F32)], axis=0)
    small = all_reduce_small(small)
    per_layer = small[:6 * DEPTH].reshape(DEPTH, 6, D)
    cols = D // N_CHIPS
    small_grads = {"ln_gain": lax.dynamic_slice_in_dim(per_layer[:, 0:3], chip * cols, cols, axis=2),
                   "ln_bias": lax.dynamic_slice_in_dim(per_layer[:, 3:6], chip * cols, cols, axis=2),
                   "pool_scale": small[6 * DEPTH:6 * DEPTH + 1]}

    grad_w, delta, new_m, new_v = {}, {}, {}, {}
    for n in WEIGHT_NAMES:
        shape = weights[n].shape
        if n in MATRIX_NAMES:
            outs = adamw(core, _halves(n, weights[n]), own_half[n], other_half[n], _halves(n, moms[n]), _halves(n, vels[n]))
            grad_w[n], delta[n], new_m[n], new_v[n] = [_unhalves(n, t, shape) for t in outs]
        else:
            as3 = lambda t: t.reshape(1, -1, t.shape[-1])
            g2 = small_grads[n].reshape(-1, shape[-1])
            outs = adamw(core, as3(weights[n]), g2, g2, as3(moms[n]), as3(vels[n]))
            grad_w[n], delta[n], new_m[n], new_v[n] = [t.reshape(shape) for t in outs]

    return (loss, grad_x, *[grad_w[n] for n in WEIGHT_NAMES], *[delta[n] for n in WEIGHT_NAMES],
            *[new_m[n] for n in WEIGHT_NAMES], *[new_v[n] for n in WEIGHT_NAMES])
```

```python
import functools
import math

import numpy as np
import jax
import jax.numpy as jnp
from jax import lax
from jax.experimental import pallas as pl
from jax.experimental.pallas import tpu as pltpu
from jax.experimental.pallas import tpu_sc as plsc

F32 = jnp.float32
BF16 = jnp.bfloat16

DEPTH = 2
ALPHA = (2.0 * DEPTH) ** 0.25
MACARON_WEIGHT = 0.5
LN_EPS = 1e-5
MASK_VALUE = -1e30
POOL_WINDOWS = (2, 4, 8, 16)
POOL_PAD = 16
HEAD_DIM = 64
N_HEADS = 16
DIL_CONFIGS = ((128, 1), (512, 4), (2048, 16))
ATTN_R = 64
ATTN_BQ = 128
ATTN_W = ATTN_BQ + 2 * ATTN_R
ATTN_UNROLL = 4
LANES = 128
ADAM_LR = 0.001
ADAM_B1 = 0.9
ADAM_B2 = 0.999
ADAM_EPS = 1e-08
ADAM_WD = 0.01
ADAM_STEP = 10
N_CHIPS = 4
GATHER_COLLECTIVE_ID = 1
VMEM_LIMIT = 56 * 1024 * 1024
MESH = pl.DeviceIdType.MESH
ANY = pl.BlockSpec(memory_space=pl.ANY)


def _params(sem=None, vmem=VMEM_LIMIT):
    return pltpu.CompilerParams(dimension_semantics=sem, vmem_limit_bytes=vmem)


def _alibi_slopes():
    n = len(DIL_CONFIGS) * N_HEADS
    s = 2.0 ** (-8.0 * np.arange(1, n + 1) / n)
    return s.reshape(len(DIL_CONFIGS), N_HEADS).astype(np.float32)


def _ln_fwd(z, g, b):
    mu = jnp.mean(z, axis=-1, keepdims=True)
    zc = z - mu
    var = jnp.mean(zc * zc, axis=-1, keepdims=True)
    rstd = lax.rsqrt(var + LN_EPS)
    xhat = zc * rstd
    return xhat * g + b, xhat, rstd


def _dot(a, b):
    return jnp.dot(a, b, preferred_element_type=F32)


def _dot_nt(a, b):
    return lax.dot_general(a, b, (((1,), (1,)), ((), ())), preferred_element_type=F32)


def _dot_tn(a, b):
    return lax.dot_general(a, b, (((0,), (0,)), ((), ())), preferred_element_type=F32)


def mm_nn(a, b, out_dtype, first_block=0, nb=None, tm=512):
    S, K = a.shape
    Nb = K
    nb = b.shape[1] // Nb if nb is None else nb

    def body(a_ref, b_ref, o_ref):
        o_ref[...] = _dot(a_ref[...].astype(BF16), b_ref[...]).astype(out_dtype)

    return pl.pallas_call(
        body, name="mm_nn",
        grid=(S // tm, nb),
        in_specs=[pl.BlockSpec((tm, K), lambda i, j: (i, 0)), pl.BlockSpec((K, Nb), lambda i, j: (0, first_block + j))],
        out_specs=pl.BlockSpec((None, tm, Nb), lambda i, j: (j, i, 0)),
        out_shape=jax.ShapeDtypeStruct((nb, S, Nb), out_dtype),
        compiler_params=_params(("parallel", "arbitrary")),
    )(a, b)


def chip_blocks_to_columns(w):
    nb, K, Nb = w.shape

    def body(w_ref, o_ref):
        o_ref[...] = w_ref[...]

    return pl.pallas_call(
        body, name="chip_blocks_to_columns",
        grid=(nb,),
        in_specs=[pl.BlockSpec((None, K, Nb), lambda b: (b, 0, 0))],
        out_specs=pl.BlockSpec((K, Nb), lambda b: (0, b)),
        out_shape=jax.ShapeDtypeStruct((K, nb * Nb), w.dtype),
        compiler_params=_params(("parallel",)),
    )(w)


def proj_ln(a, w, resid, gain, bias, tm=512):
    S, K = a.shape
    D = w.shape[1]

    def body(a_ref, w_ref, r_ref, g_ref, b_ref, y_ref, xh_ref, rs_ref):
        z = ALPHA * r_ref[...] + _dot(a_ref[...].astype(BF16), w_ref[...])
        y, xh, rs = _ln_fwd(z, g_ref[...], b_ref[...])
        y_ref[...] = y
        xh_ref[...] = xh
        rs_ref[...] = rs

    row = pl.BlockSpec((tm, D), lambda i: (i, 0))
    vec = pl.BlockSpec((1, D), lambda i: (0, 0))
    return pl.pallas_call(
        body, name="proj_ln",
        grid=(S // tm,),
        in_specs=[pl.BlockSpec((tm, K), lambda i: (i, 0)), pl.BlockSpec((K, D), lambda i: (0, 0)), row, vec, vec],
        out_specs=[row, row, pl.BlockSpec((tm, 1), lambda i: (i, 0))],
        out_shape=[jax.ShapeDtypeStruct((S, D), F32), jax.ShapeDtypeStruct((S, D), F32), jax.ShapeDtypeStruct((S, 1), F32)],
        compiler_params=_params(("parallel",)),
    )(a, w, resid, gain, bias)


def mm_nt(a, w, resid, a_blocked, out_dtype=F32, tm=512):
    if a_blocked:
        nk, S, Kb = a.shape
        a_spec = pl.BlockSpec((None, tm, Kb), lambda i, n: (n, i, 0))
    else:
        S, Kb = a.shape
        nk = 1
        a_spec = pl.BlockSpec((tm, Kb), lambda i, n: (i, 0))
    M = w.shape[0]
    has_resid = resid is not None

    def body(*refs):
        if has_resid:
            a_ref, w_ref, r_ref, o_ref, acc = refs
        else:
            a_ref, w_ref, o_ref, acc = refs
        n = pl.program_id(1)
        part = _dot_nt(a_ref[...].astype(BF16), w_ref[...])

        @pl.when(n == 0)
        def _():
            acc[...] = part

        @pl.when(n > 0)
        def _():
            acc[...] += part

        @pl.when(n == nk - 1)
        def _():
            out = acc[...]
            if has_resid:
                out = out + ALPHA * r_ref[...]
            o_ref[...] = out.astype(out_dtype)

    row = pl.BlockSpec((tm, M), lambda i, n: (i, 0))
    in_specs = [a_spec, pl.BlockSpec((M, Kb), lambda i, n: (0, n))] + ([row] if has_resid else [])
    args = (a, w) + ((resid,) if has_resid else ())
    return pl.pallas_call(
        body, name="mm_nt",
        grid=(S // tm, nk),
        in_specs=in_specs,
        out_specs=row,
        out_shape=jax.ShapeDtypeStruct((S, M), out_dtype),
        scratch_shapes=[pltpu.VMEM((tm, M), F32)],
        compiler_params=_params(("parallel", "arbitrary")),
    )(*args)


def mm_nt_dilated(parts, dils, w, resid, tm=512):
    n_groups = len(parts)
    _, S, K = parts[0].shape
    M = w.shape[0]
    nk = 3 * n_groups

    def body(*refs):
        a_refs = refs[:n_groups]
        w_ref, r_ref, o_ref, group_acc, total = refs[n_groups:]
        n = pl.program_id(1)
        for g in range(n_groups):
            for k in range(3):
                @pl.when(n == 3 * g + k)
                def _():
                    part = _dot_nt(a_refs[g][...], w_ref[...])
                    for c in range(M // LANES):
                        lanes = slice(c * LANES, (c + 1) * LANES)
                        if k == 0:
                            group_acc[c] = part[:, lanes]
                        else:
                            group_acc[c] += part[:, lanes]
                        if k == 2:
                            _rows_from_dilated(group_acc.at[c], total.at[c], tm, dils[g], accumulate=g > 0)

        @pl.when(n == nk - 1)
        def _():
            for c in range(M // LANES):
                lanes = slice(c * LANES, (c + 1) * LANES)
                o_ref[:, lanes] = total[c] + ALPHA * r_ref[:, lanes]

    def a_spec(g):
        return pl.BlockSpec((None, tm, K), lambda i, n: (jnp.clip(n - 3 * g, 0, 2), i, 0))

    row = pl.BlockSpec((tm, M), lambda i, n: (i, 0))
    return pl.pallas_call(
        body, name="mm_nt_dilated",
        grid=(S // tm, nk),
        in_specs=[a_spec(g) for g in range(n_groups)] + [pl.BlockSpec((M, K), lambda i, n: (0, n)), row],
        out_specs=row,
        out_shape=jax.ShapeDtypeStruct((S, M), F32),
        scratch_shapes=[pltpu.VMEM((M // LANES, tm, LANES), F32), pltpu.VMEM((M // LANES, tm, LANES), F32)],
        compiler_params=_params(("parallel", "arbitrary")),
    )(*parts, w, resid)


def mm_tn(pairs, *, nblk, out_shape, out_block, out_index, alias=None, tk=512, name="mm_tn"):
    operands = []
    for a, b, _ in pairs:
        for t in (a, b):
            if not any(t is o for o in operands):
                operands.append(t)
    where = lambda t: next(i for i, o in enumerate(operands) if o is t)
    S = pairs[0][0].shape[-2]
    n_out, n_in = len(pairs), len(operands)
    n_alias = len(alias) if alias is not None else 0

    def spec(t):
        if t.ndim == 3:
            return pl.BlockSpec((None, tk, t.shape[-1]), lambda j, k: (j, k, 0))
        return pl.BlockSpec((tk, t.shape[-1]), lambda j, k: (k, 0))

    def body(*refs):
        refs = refs[n_alias:]
        in_refs, o_refs, accs = refs[:n_in], refs[n_in:n_in + n_out], refs[n_in + n_out:]
        k = pl.program_id(1)
        for (a, b, scale), o_ref, acc in zip(pairs, o_refs, accs):
            part = _dot_tn(in_refs[where(a)][...].astype(BF16), in_refs[where(b)][...].astype(BF16))

            @pl.when(k == 0)
            def _():
                acc[...] = part

            @pl.when(k > 0)
            def _():
                acc[...] += part

            @pl.when(k == S // tk - 1)
            def _():
                o_ref[...] = (scale * acc[...]).astype(BF16)

    out_spec = pl.BlockSpec(out_block, lambda j, k: out_index(j))
    outs = pl.pallas_call(
        body, name=name,
        grid=(nblk, S // tk),
        in_specs=[ANY] * n_alias + [spec(t) for t in operands],
        out_specs=[out_spec] * n_out,
        out_shape=[jax.ShapeDtypeStruct(out_shape, BF16)] * n_out,
        scratch_shapes=[pltpu.VMEM((a.shape[-1], b.shape[-1]), F32) for a, b, _ in pairs],
        input_output_aliases={i: i for i in range(n_alias)},
        compiler_params=_params(("parallel", "arbitrary")),
    )(*(tuple(alias) if alias is not None else ()), *operands)
    return list(outs)


def ffn_fwd(x, wg, wu, wd, gain, bias, tm=512):
    S, D = x.shape
    nb, FB = wg.shape[0], wg.shape[1]

    def body(x_ref, wg_ref, wu_ref, wd_ref, g_ref, b_ref, y_ref, xh_ref, rs_ref, a_ref, u_ref, acc):
        j = pl.program_id(1)
        xb = x_ref[...].astype(BF16)
        a = _dot_nt(xb, wg_ref[...])
        u = _dot_nt(xb, wu_ref[...])
        a_ref[...] = a.astype(BF16)
        u_ref[...] = u.astype(BF16)
        h = a * jax.nn.sigmoid(a) * u
        part = _dot(h.astype(BF16), wd_ref[...])

        @pl.when(j == 0)
        def _():
            acc[...] = part

        @pl.when(j > 0)
        def _():
            acc[...] += part

        @pl.when(j == nb - 1)
        def _():
            z = ALPHA * x_ref[...] + MACARON_WEIGHT * acc[...]
            y, xh, rs = _ln_fwd(z, g_ref[...], b_ref[...])
            y_ref[...] = y
            xh_ref[...] = xh
            rs_ref[...] = rs

    row = pl.BlockSpec((tm, D), lambda i, j: (i, 0))
    vec = pl.BlockSpec((1, D), lambda i, j: (0, 0))
    w_out = pl.BlockSpec((None, FB, D), lambda i, j: (j, 0, 0))
    act = pl.BlockSpec((None, tm, FB), lambda i, j: (j, i, 0))
    return pl.pallas_call(
        body, name="ffn_fwd",
        grid=(S // tm, nb),
        in_specs=[row, w_out, w_out, w_out, vec, vec],
        out_specs=[row, row, pl.BlockSpec((tm, 1), lambda i, j: (i, 0)), act, act],
        out_shape=[jax.ShapeDtypeStruct((S, D), F32), jax.ShapeDtypeStruct((S, D), F32), jax.ShapeDtypeStruct((S, 1), F32),
                   jax.ShapeDtypeStruct((nb, S, FB), BF16), jax.ShapeDtypeStruct((nb, S, FB), BF16)],
        scratch_shapes=[pltpu.VMEM((tm, D), F32)],
        compiler_params=_params(("parallel", "arbitrary")),
    )(x, wg, wu, wd, gain, bias)


def ffn_bwd(dz, a, u, wg, wu, wd, tm=512):
    S, D = dz.shape
    nb, FB = wg.shape[0], wg.shape[1]

    def body(dz_ref, a_ref, u_ref, wg_ref, wu_ref, wd_ref, dx_ref, h_ref, da_ref, du_ref, acc):
        j = pl.program_id(1)
        dzb = (MACARON_WEIGHT * dz_ref[...]).astype(BF16)
        dh = _dot_nt(dzb, wd_ref[...])
        av = a_ref[...].astype(F32)
        uv = u_ref[...].astype(F32)
        s = jax.nn.sigmoid(av)
        silu = av * s
        h_ref[...] = (silu * uv).astype(BF16)
        da = (dh * uv * (s * (1.0 + av * (1.0 - s)))).astype(BF16)
        du = (dh * silu).astype(BF16)
        da_ref[...] = da
        du_ref[...] = du
        part = _dot(da, wg_ref[...]) + _dot(du, wu_ref[...])

        @pl.when(j == 0)
        def _():
            acc[...] = part

        @pl.when(j > 0)
        def _():
            acc[...] += part

        @pl.when(j == nb - 1)
        def _():
            dx_ref[...] = ALPHA * dz_ref[...] + acc[...]

    row = pl.BlockSpec((tm, D), lambda i, j: (i, 0))
    w_out = pl.BlockSpec((None, FB, D), lambda i, j: (j, 0, 0))
    act = pl.BlockSpec((None, tm, FB), lambda i, j: (j, i, 0))
    act_shape = jax.ShapeDtypeStruct((nb, S, FB), BF16)
    return pl.pallas_call(
        body, name="ffn_bwd",
        grid=(S // tm, nb),
        in_specs=[row, act, act, w_out, w_out, w_out],
        out_specs=[row, act, act, act],
        out_shape=[jax.ShapeDtypeStruct((S, D), F32), act_shape, act_shape, act_shape],
        scratch_shapes=[pltpu.VMEM((tm, D), F32)],
        compiler_params=_params(("parallel", "arbitrary")),
    )(dz, a, u, wg, wu, wd)


def ln_bwd(dy, xhat, rstd, gain, tm=512):
    S, D = dy.shape

    def body(dy_ref, xh_ref, rs_ref, g_ref, dz_ref, dg_ref, db_ref):
        i = pl.program_id(0)
        dy = dy_ref[...]
        xh = xh_ref[...]
        dxh = dy * g_ref[...]
        m1 = jnp.mean(dxh, axis=-1, keepdims=True)
        m2 = jnp.mean(dxh * xh, axis=-1, keepdims=True)
        dz_ref[...] = rs_ref[...] * (dxh - m1 - xh * m2)
        dg = jnp.sum(dy * xh, axis=0, keepdims=True)
        db = jnp.sum(dy, axis=0, keepdims=True)

        @pl.when(i == 0)
        def _():
            dg_ref[...] = dg
            db_ref[...] = db

        @pl.when(i > 0)
        def _():
            dg_ref[...] += dg
            db_ref[...] += db

    row = pl.BlockSpec((tm, D), lambda i: (i, 0))
    vec = pl.BlockSpec((1, D), lambda i: (0, 0))
    return pl.pallas_call(
        body, name="ln_bwd",
        grid=(S // tm,),
        in_specs=[row, row, pl.BlockSpec((tm, 1), lambda i: (i, 0)), vec],
        out_specs=[row, vec, vec],
        out_shape=[jax.ShapeDtypeStruct((S, D), F32), jax.ShapeDtypeStruct((1, D), F32), jax.ShapeDtypeStruct((1, D), F32)],
        compiler_params=_params(("arbitrary",)),
    )(dy, xhat, rstd, gain)


def loss_head(y, target, tm=512):
    S, D = y.shape

    def body(y_ref, t_ref, dy_ref, l_ref):
        i = pl.program_id(0)
        e = y_ref[...] - t_ref[...]
        dy_ref[...] = e / D
        part = 0.5 * jnp.sum(jnp.mean(e * e, axis=-1, keepdims=True), axis=0, keepdims=True)

        @pl.when(i == 0)
        def _():
            l_ref[...] = part

        @pl.when(i > 0)
        def _():
            l_ref[...] += part

    row = pl.BlockSpec((tm, D), lambda i: (i, 0))
    return pl.pallas_call(
        body, name="loss_head",
        grid=(S // tm,),
        in_specs=[row, row],
        out_specs=[row, pl.BlockSpec((1, 1), lambda i: (0, 0))],
        out_shape=[jax.ShapeDtypeStruct((S, D), F32), jax.ShapeDtypeStruct((1, 1), F32)],
        compiler_params=_params(("arbitrary",)),
    )(y, target)


def _pool_window(xp, g):
    n = xp.shape[0]
    w = xp + pltpu.roll(xp, 1, 0)
    out = w
    for level, shift in enumerate((1, 2, 4), start=1):
        w = pltpu.roll(w, shift, 0) + pltpu.roll(w, n - shift, 0)
        out = jnp.where(g >= level, w, out)
    return out


def _pool_count(S, C, g):
    half = lax.shift_left(jnp.int32(1), g)
    t = lax.broadcasted_iota(jnp.int32, (S, C), 0)
    return (jnp.minimum(t + half, S) - jnp.maximum(t - half, 0)).astype(F32)


def pool_mix(u, wgrp, scale):
    S, D = u.shape
    G, C = wgrp.shape[0], wgrp.shape[1]

    def body(u_ref, w_ref, s_ref, mix_ref, v_ref, pad):
        g = pl.program_id(0)
        zeros = jnp.zeros((POOL_PAD, C), F32)
        pad[pl.ds(0, POOL_PAD), :] = zeros
        pad[pl.ds(POOL_PAD + S, POOL_PAD), :] = zeros
        pad[pl.ds(POOL_PAD, S), :] = u_ref[...]
        win = _pool_window(pad[...], g)[POOL_PAD:POOL_PAD + S]
        mixed = (win / _pool_count(S, C, g) - u_ref[...]).astype(BF16)
        mix_ref[...] = mixed
        v_ref[...] = _dot(mixed, w_ref[...]) * s_ref[...]

    col = pl.BlockSpec((S, C), lambda g: (0, g))
    return pl.pallas_call(
        body, name="pool_mix",
        grid=(G,),
        in_specs=[col, pl.BlockSpec((None, C, C), lambda g: (g, 0, 0)), pl.BlockSpec((1, C), lambda g: (0, g))],
        out_specs=[col, col],
        out_shape=[jax.ShapeDtypeStruct((S, D), BF16), jax.ShapeDtypeStruct((S, D), F32)],
        scratch_shapes=[pltpu.VMEM((S + 2 * POOL_PAD, C), F32)],
        compiler_params=_params(("arbitrary",)),
    )(u, wgrp, scale)


def pool_mix_bwd(dv, mixed, wgrp, scale):
    S, D = dv.shape
    G, C = wgrp.shape[0], wgrp.shape[1]

    def body(dv_ref, mix_ref, w_ref, s_ref, du_ref, dw_ref, ds_ref, pad):
        g = pl.program_id(0)
        mixed = mix_ref[...]
        dv = dv_ref[...]
        yg = _dot(mixed, w_ref[...])
        ds_ref[...] = jnp.sum(dv * yg, axis=0, keepdims=True)
        dyg = (dv * s_ref[...]).astype(BF16)
        dw_ref[...] = _dot_tn(mixed, dyg).astype(BF16)
        dmix = _dot_nt(dyg, w_ref[...])
        zeros = jnp.zeros((POOL_PAD, C), F32)
        pad[pl.ds(0, POOL_PAD), :] = zeros
        pad[pl.ds(POOL_PAD + S, POOL_PAD), :] = zeros
        pad[pl.ds(POOL_PAD, S), :] = dmix / _pool_count(S, C, g)
        win = _pool_window(pad[...], g)
        win = pltpu.roll(win, win.shape[0] - 1, 0)[POOL_PAD:POOL_PAD + S]
        du_ref[...] = win - dmix

    col = pl.BlockSpec((S, C), lambda g: (0, g))
    return pl.pallas_call(
        body, name="pool_mix_bwd",
        grid=(G,),
        in_specs=[col, col, pl.BlockSpec((None, C, C), lambda g: (g, 0, 0)), pl.BlockSpec((1, C), lambda g: (0, g))],
        out_specs=[col, pl.BlockSpec((None, C, C), lambda g: (g, 0, 0)), pl.BlockSpec((1, C), lambda g: (0, g))],
        out_shape=[jax.ShapeDtypeStruct((S, D), F32), jax.ShapeDtypeStruct((G, C, C), BF16), jax.ShapeDtypeStruct((1, D), F32)],
        scratch_shapes=[pltpu.VMEM((S + 2 * POOL_PAD, C), F32)],
        compiler_params=_params(("arbitrary",)),
    )(dv, mixed, wgrp, scale)


PERM_BLOCK = 256


def _dilated_runs(rows, d):
    n = PERM_BLOCK // d
    return [(c * PERM_BLOCK, r, n) for c in range(rows // PERM_BLOCK) for r in range(d)]


def _rows_to_dilated(src_ref, dst_ref, rows, d):
    for base, r, n in _dilated_runs(rows, d):
        dst_ref[pl.ds(base + r * n, n), :] = src_ref[pl.ds(base + r, n, stride=d), :].astype(dst_ref.dtype)


def _rows_from_dilated(src_ref, dst_ref, rows, d, accumulate=False):
    for base, r, n in _dilated_runs(rows, d):
        at = pl.ds(base + r, n, stride=d)
        v = src_ref[pl.ds(base + r * n, n), :]
        dst_ref[at, :] = dst_ref[at, :] + v if accumulate else v


def dilate_rows(x, dils, out_dtype, tm=1024):
    S, D = x.shape

    def body(x_ref, *o_refs):
        for d, o_ref in zip(dils, o_refs):
            _rows_to_dilated(x_ref, o_ref, tm, d)

    tile = pl.BlockSpec((tm, LANES), lambda i, j: (i, j))
    return pl.pallas_call(
        body, name="dilate_rows",
        grid=(S // tm, D // LANES),
        in_specs=[tile],
        out_specs=[tile] * len(dils),
        out_shape=[jax.ShapeDtypeStruct((S, D), out_dtype)] * len(dils),
        compiler_params=_params(("parallel", "parallel")),
    )(x)


def _slope_table(group, dilation):
    s = _alibi_slopes()[group].reshape(N_HEADS // 2, 2, 1, 1) * float(dilation)
    return jnp.asarray(np.broadcast_to(s, (N_HEADS // 2, 2, 1, ATTN_W)).copy())


def _residue_shape(S, D, d):
    return (S, D) if d == 1 else (S // PERM_BLOCK, d, PERM_BLOCK // d, D)


def _residue_view(x, d):
    return x.reshape(x.shape[:-2] + _residue_shape(x.shape[-2], x.shape[-1], d))


def _residue_spec(lead_block, lead_index, S, d):
    if d == 1:
        return pl.BlockSpec(lead_block + (S, LANES), lambda hp, r: lead_index + (0, hp))
    return pl.BlockSpec(lead_block + (S // PERM_BLOCK, None, PERM_BLOCK // d, LANES), lambda hp, r: lead_index + (0, r, 0, hp))


def _whole(ref, lead, L):
    return ref[lead + (slice(None),) * (len(ref.shape) - len(lead))].reshape(L, LANES)


def _query_rows(lead, i, d):
    if d == 1:
        return lead + (pl.ds(pl.multiple_of(i * ATTN_BQ, ATTN_BQ), ATTN_BQ), slice(None)), (ATTN_BQ, LANES)
    n = PERM_BLOCK // d
    return lead + (pl.ds(i * (ATTN_BQ // n), ATTN_BQ // n), slice(None), slice(None)), (ATTN_BQ // n, n, LANES)


def _load_query_rows(ref, lead, i, d):
    at, _ = _query_rows(lead, i, d)
    return ref[at].reshape(ATTN_BQ, LANES)


def _store_query_rows(ref, lead, i, d, value):
    at, shape = _query_rows(lead, i, d)
    ref[at] = value.reshape(shape)


def _stage_keys(dst, src_ref, L):
    rows = _whole(src_ref, (), L)
    lane = lax.broadcasted_iota(jnp.int32, (L, LANES), 1)
    zeros = jnp.zeros((ATTN_R, LANES), dst.dtype)
    for h in range(2):
        mine = (lane < HEAD_DIM) if h == 0 else (lane >= HEAD_DIM)
        dst[h, pl.ds(0, ATTN_R), :] = zeros
        dst[h, pl.ds(ATTN_R + L, ATTN_R), :] = zeros
        dst[h, pl.ds(ATTN_R, L), :] = jnp.where(mine, rows, jnp.zeros_like(rows))


def _fill_bias(bias, sl_ref):
    a = lax.broadcasted_iota(jnp.int32, (ATTN_BQ, ATTN_W), 0)
    c = lax.broadcasted_iota(jnp.int32, (ATTN_BQ, ATTN_W), 1)
    rel = jnp.abs(c - ATTN_R - a)
    band = rel <= ATTN_R
    after_start = c >= ATTN_R
    before_end = c < ATTN_BQ + ATTN_R
    for h in range(2):
        base = -(sl_ref[h] * rel.astype(F32))
        for variant in range(4):
            ok = band
            if variant & 1:
                ok = ok & after_start
            if variant & 2:
                ok = ok & before_end
            bias[variant, h] = jnp.where(ok, base, MASK_VALUE)


def _bias_variant(i, nq):
    return jnp.where(i == 0, 1, 0) + jnp.where(i == nq - 1, 2, 0)


def attn_fwd(qkv, group, dilation):
    _, S, D = qkv.shape
    d = dilation
    L = S // d
    nq = L // ATTN_BQ
    ncol = D // LANES
    view = _residue_view(qkv, d)
    slopes = _slope_table(group, d)
    scale = HEAD_DIM ** -0.5

    def body(q_ref, k_ref, v_ref, sl_ref, o_ref, lse_ref, k2, v2, bias):
        @pl.when(pl.program_id(1) == 0)
        def _():
            _fill_bias(bias, sl_ref)

        _stage_keys(k2, k_ref, L)
        _stage_keys(v2, v_ref, L)
        head0 = lax.broadcasted_iota(jnp.int32, (ATTN_BQ, LANES), 1) < HEAD_DIM

        def step(i, carry):
            variant = _bias_variant(i, nq)
            win = pl.ds(pl.multiple_of(i * ATTN_BQ, ATTN_BQ), ATTN_W)
            qs = _load_query_rows(q_ref, (), i, d) * jnp.asarray(scale, BF16)
            acc, ms, ls = None, [], []
            for h in range(2):
                s = _dot_nt(qs, k2[h, win, :]) + bias[variant, h]
                m = jnp.max(s, axis=-1, keepdims=True)
                e = jnp.exp(s - m)
                ls.append(jnp.sum(e, axis=-1, keepdims=True))
                ms.append(m)
                part = _dot(e.astype(BF16), v2[h, win, :])
                acc = part if acc is None else acc + part
            out = acc * jnp.where(head0, 1.0 / ls[0], 1.0 / ls[1])
            lse = jnp.where(head0, ms[0] + jnp.log(ls[0]), ms[1] + jnp.log(ls[1]))
            _store_query_rows(o_ref, (), i, d, out)
            _store_query_rows(lse_ref, (), i, d, lse)
            return carry

        lax.fori_loop(0, nq, step, 0, unroll=min(ATTN_UNROLL, nq))

    def col(which):
        return _residue_spec((None,), (which,), S, d)

    out = _residue_spec((), (), S, d)
    o, lse = pl.pallas_call(
        body, name=f"attn_fwd_g{group}",
        grid=(ncol, d),
        in_specs=[col(0), col(1), col(2), pl.BlockSpec((None, 2, 1, ATTN_W), lambda hp, r: (hp, 0, 0, 0))],
        out_specs=[out, out],
        out_shape=[jax.ShapeDtypeStruct(_residue_shape(S, D, d), F32)] * 2,
        scratch_shapes=[pltpu.VMEM((2, L + 2 * ATTN_R, LANES), BF16), pltpu.VMEM((2, L + 2 * ATTN_R, LANES), BF16),
                        pltpu.VMEM((4, 2, ATTN_BQ, ATTN_W), F32)],
        compiler_params=_params(("arbitrary", "arbitrary")),
    )(view, view, view, slopes)
    return o.reshape(S, D), lse.reshape(S, D)


def attn_combine(os, lses, dils, tm=1024):
    S, D = os[0].shape
    n = len(os)
    n_moved = sum(d > 1 for d in dils)

    def body(*refs):
        o_refs, l_refs, out_ref, lse_ref = list(refs[:n]), list(refs[n:2 * n]), refs[2 * n], refs[2 * n + 1]
        spare = list(refs[2 * n + 2:])
        for g, d in enumerate(dils):
            if d > 1:
                for which in (o_refs, l_refs):
                    token_order = spare.pop()
                    _rows_from_dilated(which[g], token_order, tm, d)
                    which[g] = token_order
        ls = [r[...] for r in l_refs]
        m = functools.reduce(jnp.maximum, ls)
        es = [jnp.exp(l - m) for l in ls]
        tot = functools.reduce(lambda x, y: x + y, es)
        inv = 1.0 / tot
        out_ref[...] = functools.reduce(lambda x, y: x + y, [(e * inv) * r[...] for e, r in zip(es, o_refs)])
        lse_ref[...] = m + jnp.log(tot)

    tile = pl.BlockSpec((tm, LANES), lambda i, j: (i, j))
    return pl.pallas_call(
        body, name="attn_combine",
        grid=(S // tm, D // LANES),
        in_specs=[tile] * (2 * n),
        out_specs=[tile, tile],
        out_shape=[jax.ShapeDtypeStruct((S, D), F32), jax.ShapeDtypeStruct((S, D), F32)],
        scratch_shapes=[pltpu.VMEM((tm, LANES), F32)] * (2 * n_moved),
        compiler_params=_params(("parallel", "parallel")),
    )(*os, *lses)


def attn_bwd(qkv, do, o, lse, group, dilation):
    _, S, D = qkv.shape
    d = dilation
    L = S // d
    nq = L // ATTN_BQ
    ncol = D // LANES
    view = _residue_view(qkv, d)
    slopes = _slope_table(group, d)
    scale = HEAD_DIM ** -0.5

    def body(q_ref, k_ref, v_ref, do_ref, o_ref, lse_ref, sl_ref, dx_ref, k2, v2, dkacc, dvacc, bias):
        @pl.when(pl.program_id(1) == 0)
        def _():
            _fill_bias(bias, sl_ref)

        _stage_keys(k2, k_ref, L)
        _stage_keys(v2, v_ref, L)
        dkacc[...] = jnp.zeros_like(dkacc)
        dvacc[...] = jnp.zeros_like(dvacc)
        lane = lax.broadcasted_iota(jnp.int32, (ATTN_BQ, LANES), 1)
        heads = (lane < HEAD_DIM, lane >= HEAD_DIM)
        key_head0 = lax.broadcasted_iota(jnp.int32, (ATTN_W, LANES), 1) < HEAD_DIM

        def step(i, carry):
            variant = _bias_variant(i, nq)
            win = pl.ds(pl.multiple_of(i * ATTN_BQ, ATTN_BQ), ATTN_W)
            q = _load_query_rows(q_ref, (), i, d)
            qs = q * jnp.asarray(scale, BF16)
            dov = _load_query_rows(do_ref, (), i, d)
            prod = dov * _load_query_rows(o_ref, (), i, d)
            lse_v = _load_query_rows(lse_ref, (), i, d)
            dob = dov.astype(BF16)
            dq, dks, dvs = None, [], []
            for h in range(2):
                s = _dot_nt(qs, k2[h, win, :]) + bias[variant, h]
                lse_h = jnp.max(jnp.where(heads[h], lse_v, -jnp.inf), axis=-1, keepdims=True)
                dterm = jnp.sum(jnp.where(heads[h], prod, 0.0), axis=-1, keepdims=True)
                p = jnp.exp(s - lse_h)
                dp = _dot_nt(dob, v2[h, win, :])
                ds = (p * (dp - dterm) * scale).astype(BF16)
                dvs.append(_dot_tn(p.astype(BF16), dob))
                dks.append(_dot_tn(ds, q))
                part = _dot(ds, k2[h, win, :])
                dq = part if dq is None else dq + part
            dvacc[win, :] += jnp.where(key_head0, dvs[0], dvs[1])
            dkacc[win, :] += jnp.where(key_head0, dks[0], dks[1])
            _store_query_rows(dx_ref, (0,), i, d, dq.astype(BF16))
            return carry

        lax.fori_loop(0, nq, step, 0, unroll=min(ATTN_UNROLL, nq))
        block_shape = dx_ref.shape[1:]
        dx_ref[1] = dkacc[pl.ds(ATTN_R, L), :].astype(BF16).reshape(block_shape)
        dx_ref[2] = dvacc[pl.ds(ATTN_R, L), :].astype(BF16).reshape(block_shape)

    def col(which):
        return _residue_spec((None,), (which,), S, d)

    act = _residue_spec((), (), S, d)
    out = pl.pallas_call(
        body, name=f"attn_bwd_g{group}",
        grid=(ncol, d),
        in_specs=[col(0), col(1), col(2), act, act, act, pl.BlockSpec((None, 2, 1, ATTN_W), lambda hp, r: (hp, 0, 0, 0))],
        out_specs=_residue_spec((3,), (0,), S, d),
        out_shape=jax.ShapeDtypeStruct((3,) + _residue_shape(S, D, d), BF16),
        scratch_shapes=[pltpu.VMEM((2, L + 2 * ATTN_R, LANES), BF16), pltpu.VMEM((2, L + 2 * ATTN_R, LANES), BF16),
                        pltpu.VMEM((L + 2 * ATTN_R, LANES), F32), pltpu.VMEM((L + 2 * ATTN_R, LANES), F32),
                        pltpu.VMEM((4, 2, ATTN_BQ, ATTN_W), F32)],
        compiler_params=_params(("arbitrary", "arbitrary")),
    )(view, view, view, _residue_view(do, d), _residue_view(o, d), _residue_view(lse, d), slopes)
    return out.reshape(3, S, D)


TILE_ELEMS = 256 * 1024


def _row_tile(R, C):
    if R * C <= TILE_ELEMS or R % 16:
        return R
    return max(t for t in range(16, R + 1, 16) if R % t == 0 and (t * C <= TILE_ELEMS or t == 16))


def pair_sum(core, g, recv):
    _, _, R, C = g.shape
    tr = _row_tile(R, C)

    def body(c_ref, g_ref, r_ref, o_ref):
        o_ref[...] = (g_ref[...].astype(F32) + r_ref[...].astype(F32)).astype(BF16)

    blk = pl.BlockSpec((None, tr, C), lambda d, i, c_ref: (d, i, 0))
    return pl.pallas_call(
        body, name="pair_sum",
        grid_spec=pltpu.PrefetchScalarGridSpec(
            num_scalar_prefetch=1, grid=(N_CHIPS, R // tr),
            in_specs=[pl.BlockSpec((None, None, tr, C), lambda d, i, c_ref: (d, c_ref[0], i, 0)), blk],
            out_specs=blk),
        out_shape=jax.ShapeDtypeStruct((N_CHIPS, R, C), BF16),
        compiler_params=_params(("parallel", "parallel")),
    )(core, g, recv)


def chip_sum(chip, own, recv):
    _, R, C = own.shape
    tr = _row_tile(R, C)
    slot_of_relation = {2: 0, 1: 1, 3: 2}

    def body(chip_ref, own_ref, r_ref, o_ref):
        me = chip_ref[0]
        mine = own_ref[...].astype(F32)
        theirs = {rel: r_ref[k].astype(F32) for rel, k in slot_of_relation.items()}
        acc = None
        for s in range(N_CHIPS):
            rel = jnp.bitwise_xor(me, s)
            part = jnp.where(rel == 0, mine, jnp.where(rel == 2, theirs[2], jnp.where(rel == 1, theirs[1], theirs[3])))
            acc = part if acc is None else acc + part
        o_ref[...] = acc

    return pl.pallas_call(
        body, name="chip_sum",
        grid_spec=pltpu.PrefetchScalarGridSpec(
            num_scalar_prefetch=1, grid=(R // tr,),
            in_specs=[pl.BlockSpec((None, tr, C), lambda i, chip_ref: (chip_ref[0], i, 0)),
                      pl.BlockSpec((N_CHIPS - 1, tr, C), lambda i, chip_ref: (0, i, 0))],
            out_specs=pl.BlockSpec((tr, C), lambda i, chip_ref: (i, 0))),
        out_shape=jax.ShapeDtypeStruct((R, C), F32),
        compiler_params=_params(("parallel",)),
    )(chip, own, recv)


def adamw(core, w, g_own, g_recv, m, v):
    H, R, C = w.shape
    tr = _row_tile(R, C)

    def body(c_ref, w_ref, go_ref, gr_ref, m_ref, v_ref, g_ref, d_ref, nm_ref, nv_ref):
        g = jnp.where(pl.program_id(0) == c_ref[0], go_ref[...], gr_ref[...])
        m = ADAM_B1 * m_ref[...] + (1.0 - ADAM_B1) * g
        v = ADAM_B2 * v_ref[...] + (1.0 - ADAM_B2) * (g * g)
        m_hat = m / (1.0 - ADAM_B1 ** ADAM_STEP)
        v_hat = v / (1.0 - ADAM_B2 ** ADAM_STEP)
        g_ref[...] = g
        d_ref[...] = -ADAM_LR * (m_hat / (jnp.sqrt(v_hat) + ADAM_EPS) + ADAM_WD * w_ref[...])
        nm_ref[...] = m
        nv_ref[...] = v

    blk = pl.BlockSpec((None, tr, C), lambda h, i, c_ref: (h, i, 0))
    half = pl.BlockSpec((tr, C), lambda h, i, c_ref: (i, 0))
    shape = jax.ShapeDtypeStruct((H, R, C), F32)
    return pl.pallas_call(
        body, name="adamw",
        grid_spec=pltpu.PrefetchScalarGridSpec(
            num_scalar_prefetch=1, grid=(H, R // tr),
            in_specs=[blk, half, half, blk, blk],
            out_specs=[blk] * 4),
        out_shape=[shape] * 4,
        compiler_params=_params(("parallel", "parallel")),
    )(core, w, g_own, g_recv, m, v)


def _place():
    return lax.axis_index("x"), lax.axis_index("y"), lax.axis_index("c")


def _other_chips(x, y):
    return [(2 * (1 - x) + y, (1 - x, y)), (2 * x + (1 - y), (x, 1 - y)), (2 * (1 - x) + (1 - y), (1 - x, 1 - y))]


def all_gather_shards(shards, placed):
    n = len(shards)

    def body(*refs):
        ins, outs = refs[:n], refs[2 * n:3 * n]
        send_sems, recv_sems = refs[3 * n:]
        x, y, c = _place()
        me = 2 * x + y
        sibling = (x, y, 1 - c)
        chips = _other_chips(x, y)

        def copy(a, k, src, dst, to):
            return pltpu.make_async_remote_copy(src_ref=src, dst_ref=dst, send_sem=send_sems.at[a, k], recv_sem=recv_sems.at[a, k],
                                                device_id=to, device_id_type=MESH)

        sends = []
        for a in range(n):
            for k, (_, (px, py)) in enumerate(chips):
                cp = copy(a, k, ins[a].at[c], outs[a].at[me, c], (px, py, c))
                cp.start()
                sends.append(cp)
        for a in range(n):
            for k, (chip, _) in enumerate(chips):
                landed = outs[a].at[chip, c]
                copy(a, k, landed, landed, sibling).wait_recv()
                cp = copy(a, 3 + k, landed, landed, sibling)
                cp.start()
                sends.append(cp)
        for a in range(n):
            for k, (chip, _) in enumerate(chips):
                other = outs[a].at[chip, 1 - c]
                copy(a, 3 + k, other, other, sibling).wait_recv()
        for cp in sends:
            cp.wait_send()

    return pl.pallas_call(
        body, name="all_gather_shards",
        in_specs=[ANY] * (2 * n),
        out_specs=[ANY] * n,
        out_shape=[jax.ShapeDtypeStruct(p.shape, p.dtype) for p in placed],
        scratch_shapes=[pltpu.SemaphoreType.DMA((n, 6)), pltpu.SemaphoreType.DMA((n, 6))],
        input_output_aliases={n + a: a for a in range(n)},
        compiler_params=pltpu.CompilerParams(has_side_effects=True),
    )(*shards, *placed)


def all_gather_shards_async(shards, placed, collective_id, name):
    n = len(shards)
    srcs = [jax.new_ref(s, memory_space=pltpu.MemorySpace.HBM) for s in shards]
    dsts = [jax.new_ref(p, memory_space=pltpu.MemorySpace.HBM) for p in placed]

    @pl.kernel(mesh=plsc.ScalarSubcoreMesh(axis_name="sequencer", num_cores=1), name=name,
               scratch_types=(pltpu.SemaphoreType.DMA((n, 6)), pltpu.SemaphoreType.DMA((n, 6))),
               compiler_params=pltpu.CompilerParams(collective_id=collective_id))
    def launch(send_sems, recv_sems):
        x, y, c = _place()
        me = 2 * x + y
        sibling = (x, y, 1 - c)
        chips = _other_chips(x, y)
        barrier = pltpu.get_barrier_semaphore()
        peers = [sibling] + [(px, py, c) for _, (px, py) in chips]
        for peer in peers:
            pl.semaphore_signal(barrier, inc=1, device_id=peer, device_id_type=MESH)
        pl.semaphore_wait(barrier, len(peers))

        def copy(a, k, src, dst, to):
            return pltpu.make_async_remote_copy(src_ref=src, dst_ref=dst, send_sem=send_sems.at[a, k], recv_sem=recv_sems.at[a, k],
                                                device_id=to, device_id_type=MESH)

        sends = []
        for a in range(n):
            for k, (_, (px, py)) in enumerate(chips):
                cp = copy(a, k, srcs[a].at[c], dsts[a].at[me, c], (px, py, c))
                cp.start()
                sends.append(cp)
        for a in range(n):
            for k, (chip, _) in enumerate(chips):
                landed = dsts[a].at[chip, c]
                copy(a, k, landed, landed, sibling).wait_recv()
                cp = copy(a, 3 + k, landed, landed, sibling)
                cp.start()
                sends.append(cp)
        for a in range(n):
            for k, (chip, _) in enumerate(chips):
                other = dsts[a].at[chip, 1 - c]
                copy(a, 3 + k, other, other, sibling).wait_recv()
        for cp in sends:
            cp.wait_send()

    launch()
    return [d[...] for d in dsts]


def sibling_exchange_halves(grads):
    n = len(grads)

    def body(*refs):
        ins, outs = refs[:n], refs[n:2 * n]
        send_sems, recv_sems = refs[2 * n:]
        x, y, c = _place()
        sibling = (x, y, 1 - c)
        copies = [pltpu.make_async_remote_copy(src_ref=ins[a].at[:, 1 - c], dst_ref=outs[a], send_sem=send_sems.at[a],
                                               recv_sem=recv_sems.at[a], device_id=sibling, device_id_type=MESH) for a in range(n)]
        for cp in copies:
            cp.start()
        for cp in copies:
            cp.wait()

    return pl.pallas_call(
        body, name="sibling_exchange_halves",
        in_specs=[ANY] * n,
        out_specs=[ANY] * n,
        out_shape=[jax.ShapeDtypeStruct((N_CHIPS,) + g.shape[2:], g.dtype) for g in grads],
        scratch_shapes=[pltpu.SemaphoreType.DMA((n,)), pltpu.SemaphoreType.DMA((n,))],
        compiler_params=pltpu.CompilerParams(has_side_effects=True),
    )(*grads)


def chip_exchange(sums):
    n = len(sums)

    def body(*refs):
        ins, outs = refs[:n], refs[n:2 * n]
        send_sems, recv_sems = refs[2 * n:]
        x, y, c = _place()
        copies = []
        for a in range(n):
            for k, (chip, (px, py)) in enumerate(_other_chips(x, y)):
                cp = pltpu.make_async_remote_copy(src_ref=ins[a].at[chip], dst_ref=outs[a].at[k], send_sem=send_sems.at[a, k],
                                                  recv_sem=recv_sems.at[a, k], device_id=(px, py, c), device_id_type=MESH)
                cp.start()
                copies.append(cp)
        for cp in copies:
            cp.wait()

    return pl.pallas_call(
        body, name="chip_exchange",
        in_specs=[ANY] * n,
        out_specs=[ANY] * n,
        out_shape=[jax.ShapeDtypeStruct((N_CHIPS - 1,) + s.shape[1:], s.dtype) for s in sums],
        scratch_shapes=[pltpu.SemaphoreType.DMA((n, 3)), pltpu.SemaphoreType.DMA((n, 3))],
        compiler_params=pltpu.CompilerParams(has_side_effects=True),
    )(*sums)


def sibling_share(halves):
    n = len(halves)

    def body(*refs):
        ins, outs = refs[:n], refs[n:2 * n]
        send_sems, recv_sems = refs[2 * n:]
        x, y, c = _place()
        copies = [pltpu.make_async_remote_copy(src_ref=ins[a], dst_ref=outs[a], send_sem=send_sems.at[a], recv_sem=recv_sems.at[a],
                                               device_id=(x, y, 1 - c), device_id_type=MESH) for a in range(n)]
        for cp in copies:
            cp.start()
        for cp in copies:
            cp.wait()

    return pl.pallas_call(
        body, name="sibling_share",
        in_specs=[ANY] * n,
        out_specs=[ANY] * n,
        out_shape=[jax.ShapeDtypeStruct(h.shape, h.dtype) for h in halves],
        scratch_shapes=[pltpu.SemaphoreType.DMA((n,)), pltpu.SemaphoreType.DMA((n,))],
        compiler_params=pltpu.CompilerParams(has_side_effects=True),
    )(*halves)


def all_reduce_small(v):
    R, C = v.shape
    n_dev = 8

    def body(v_ref, o_ref, buf, send_sems, recv_sems):
        x, y, c = _place()
        me = 4 * x + 2 * y + c
        buf[me] = v_ref[...]
        copies = []
        for rel in range(1, n_dev):
            fx, fy, fc = rel >> 2, (rel >> 1) & 1, rel & 1
            peer = (x ^ fx, y ^ fy, c ^ fc)
            cp = pltpu.make_async_remote_copy(src_ref=v_ref, dst_ref=buf.at[me], send_sem=send_sems.at[rel - 1],
                                              recv_sem=recv_sems.at[rel - 1], device_id=peer, device_id_type=MESH)
            cp.start()
            copies.append(cp)
        for cp in copies:
            cp.wait()
        acc = buf[0]
        for k in range(1, n_dev):
            acc = acc + buf[k]
        o_ref[...] = acc

    return pl.pallas_call(
        body, name="all_reduce_small",
        in_specs=[pl.BlockSpec(memory_space=pltpu.VMEM)],
        out_specs=pl.BlockSpec(memory_space=pltpu.VMEM),
        out_shape=jax.ShapeDtypeStruct((R, C), F32),
        scratch_shapes=[pltpu.VMEM((n_dev, R, C), F32), pltpu.SemaphoreType.DMA((n_dev - 1,)), pltpu.SemaphoreType.DMA((n_dev - 1,))],
        compiler_params=pltpu.CompilerParams(has_side_effects=True),
    )(v)


WEIGHT_NAMES = ("ffn1_w_gate", "ffn1_w_up", "ffn1_w_down", "ffn2_w_gate", "ffn2_w_up", "ffn2_w_down", "ln_gain", "ln_bias",
                "pool_w_in", "pool_w_group", "pool_scale", "pool_w_out", "attn_w_qkv", "attn_w_out")
MATRIX_NAMES = ("ffn1_w_gate", "ffn1_w_up", "ffn1_w_down", "ffn2_w_gate", "ffn2_w_up", "ffn2_w_down",
                "pool_w_in", "pool_w_group", "pool_w_out", "attn_w_qkv", "attn_w_out")


TRANSPOSED_NAMES = ("ffn1_w_gate", "ffn1_w_up", "ffn2_w_gate", "ffn2_w_up")


def _halves(name, w):
    if name in TRANSPOSED_NAMES:
        w = jnp.swapaxes(w, 1, 2)
    return w.reshape(2, -1, w.shape[-1])


def _unhalves(name, t, shape):
    if name in TRANSPOSED_NAMES:
        return jnp.swapaxes(t.reshape(shape[0], shape[2], shape[1]), 1, 2)
    return t.reshape(shape)


def kernel(x, ffn1_w_gate, ffn1_w_up, ffn1_w_down, ffn2_w_gate, ffn2_w_up, ffn2_w_down, ln_gain, ln_bias, pool_w_in, pool_w_group, pool_scale, pool_w_out, attn_w_qkv, attn_w_out, loss_target, m_ffn1_w_gate, m_ffn1_w_up, m_ffn1_w_down, m_ffn2_w_gate, m_ffn2_w_up, m_ffn2_w_down, m_ln_gain, m_ln_bias, m_pool_w_in, m_pool_w_group, m_pool_scale, m_pool_w_out, m_attn_w_qkv, m_attn_w_out, v_ffn1_w_gate, v_ffn1_w_up, v_ffn1_w_down, v_ffn2_w_gate, v_ffn2_w_up, v_ffn2_w_down, v_ln_gain, v_ln_bias, v_pool_w_in, v_pool_w_group, v_pool_scale, v_pool_w_out, v_attn_w_qkv, v_attn_w_out):
    weights = dict(zip(WEIGHT_NAMES, (ffn1_w_gate, ffn1_w_up, ffn1_w_down, ffn2_w_gate, ffn2_w_up, ffn2_w_down, ln_gain, ln_bias,
                                      pool_w_in, pool_w_group, pool_scale, pool_w_out, attn_w_qkv, attn_w_out)))
    moms = dict(zip(WEIGHT_NAMES, (m_ffn1_w_gate, m_ffn1_w_up, m_ffn1_w_down, m_ffn2_w_gate, m_ffn2_w_up, m_ffn2_w_down, m_ln_gain,
                                   m_ln_bias, m_pool_w_in, m_pool_w_group, m_pool_scale, m_pool_w_out, m_attn_w_qkv, m_attn_w_out)))
    vels = dict(zip(WEIGHT_NAMES, (v_ffn1_w_gate, v_ffn1_w_up, v_ffn1_w_down, v_ffn2_w_gate, v_ffn2_w_up, v_ffn2_w_down, v_ln_gain,
                                   v_ln_bias, v_pool_w_in, v_pool_w_group, v_pool_scale, v_pool_w_out, v_attn_w_qkv, v_attn_w_out)))
    S, D = x.shape[1], x.shape[2]
    FB = ffn1_w_gate.shape[2]
    QKV = attn_w_qkv.shape[2] * N_CHIPS
    G, CB = pool_w_group.shape[1], pool_w_group.shape[2]
    C = pool_w_group.shape[3]
    cx, cy, cc = _place()
    chip = 2 * cx + cy
    xs = x.reshape(S, D)
    target = loss_target.reshape(S, D)

    ln_rows = jnp.concatenate([ln_gain, ln_bias, jnp.zeros((DEPTH, 2, ln_gain.shape[2]), F32)], axis=1)
    shard = {n: _halves(n, weights[n]).astype(BF16) for n in MATRIX_NAMES}
    ffn_layer = lambda f, i: {f"{f}{s}@{i}": shard[f + s][i].reshape(2, FB // 2, D) for s in ("_w_gate", "_w_up", "_w_down")}
    pool = {n: shard[n] for n in ("pool_w_in", "pool_w_group", "pool_w_out")}
    attn = {n: shard[n] for n in ("attn_w_qkv", "attn_w_out")}
    groups = [dict(ffn_layer("ffn1", 0), ln=ln_rows), pool, ffn_layer("ffn2", 0), ffn_layer("ffn1", 1), attn, ffn_layer("ffn2", 1)]
    full = {}

    def launch(k, after=None):
        parts = list(groups[k].values())
        if after is not None:
            *parts, after = lax.optimization_barrier((*parts, after))
        placed = [lax.dynamic_update_slice(lax.empty((N_CHIPS,) + s.shape, s.dtype), s[None], (chip, 0, 0, 0)) for s in parts]
        full.update(zip(groups[k], all_gather_shards_async(parts, placed, GATHER_COLLECTIVE_ID, f"gather_weights_{k}")))
        return after

    for k in range(4):
        launch(k)
    ffn_w = lambda f, i: [full[f"{f}{s}@{i}"].reshape(N_CHIPS, FB, D) for s in ("_w_gate", "_w_up", "_w_down")]
    w_pool_in = full["pool_w_in"].reshape(D, D)
    w_pool_out = full["pool_w_out"].reshape(D, D)
    w_group = full["pool_w_group"].reshape(N_CHIPS, G, CB, C).transpose(1, 0, 2, 3).reshape(G, N_CHIPS * CB, C)
    ln_full = full["ln"].transpose(1, 2, 0, 3).reshape(DEPTH, 8, D)
    gain = lambda i, k: ln_full[i, k].reshape(1, D)
    bias = lambda i, k: ln_full[i, 3 + k].reshape(1, D)

    dils = [dil for _, dil in DIL_CONFIGS]
    moved_dils = [dil for dil in dils if dil > 1]

    saved = []
    y = xs
    for i in range(DEPTH):
        y_in = y
        y, xh, rs, a, u = ffn_fwd(y_in, *ffn_w("ffn1", i), gain(i, 0), bias(i, 0))
        f1 = (y_in, xh, rs, a, u)
        if i == 0:
            y = launch(4, after=y)
        y_mid = y
        if i % 2 == 0:
            pu = mm_nn(y_mid, w_pool_in, F32)[0]
            mixed, pv = pool_mix(pu, w_group, pool_scale)
            y, xh, rs = proj_ln(pv, w_pool_out, y_mid, gain(i, 1), bias(i, 1))
            mix = (y_mid, xh, rs, mixed, pv)
            if i == 0:
                y = launch(5, after=y)
        else:
            qkv_blocks, y_mid = lax.optimization_barrier((full["attn_w_qkv"], y_mid))
            w_qkv = chip_blocks_to_columns(qkv_blocks.reshape(N_CHIPS, D, QKV // N_CHIPS))
            w_attn_out = full["attn_w_out"].reshape(D, D)
            moved = dict(zip(moved_dils, dilate_rows(y_mid, moved_dils, BF16)))
            srcs = [moved.get(dil, y_mid) for dil in dils]
            qkvs = [mm_nn(src, w_qkv, BF16, first_block=3 * g, nb=3) for g, src in enumerate(srcs)]
            parts = [attn_fwd(qkv, g, dil) for g, (qkv, dil) in enumerate(zip(qkvs, dils))]
            ao, lse = attn_combine([p[0] for p in parts], [p[1] for p in parts], dils)
            y, xh, rs = proj_ln(ao, w_attn_out, y_mid, gain(i, 1), bias(i, 1))
            mix = (y_mid, xh, rs, srcs, qkvs, ao, lse, w_qkv, w_attn_out)
        y_in2 = y
        y, xh, rs, a, u = ffn_fwd(y_in2, *ffn_w("ffn2", i), gain(i, 2), bias(i, 2))
        f2 = (y_in2, xh, rs, a, u)
        saved.append((f1, mix, f2))

    dy, loss_part = loss_head(y, target)
    loss = lax.psum(loss_part[0, 0], ("x", "y", "c"))

    grads = {}
    layer_grads = {f + s: [None] * DEPTH for f in ("ffn1", "ffn2") for s in ("_w_gate", "_w_up", "_w_down")}
    dgain = [[None] * 3 for _ in range(DEPTH)]
    dbias = [[None] * 3 for _ in range(DEPTH)]
    dscale = None

    def ffn_backward(name, i, dy, state):
        y_in, xh, rs, a, u = state
        k = 0 if name == "ffn1" else 2
        dz, dgain[i][k], dbias[i][k] = ln_bwd(dy, xh, rs, gain(i, k))
        dx, h, da, du = ffn_bwd(dz, a, u, *ffn_w(name, i))
        outs = mm_tn([(da, y_in, 1.0), (du, y_in, 1.0), (h, dz, MACARON_WEIGHT)], nblk=N_CHIPS, out_shape=(N_CHIPS, FB, D),
                     out_block=(None, FB, D), out_index=lambda j: (j, 0, 0), name="ffn_wgrad")
        for s, out in zip(("_w_gate", "_w_up", "_w_down"), outs):
            layer_grads[name + s][i] = out
        return dx

    def square_grad(a, b):
        return mm_tn([(a, b, 1.0)], nblk=1, out_shape=(D, D), out_block=(D, D), out_index=lambda j: (0, 0), name="square_wgrad")[0]

    for i in reversed(range(DEPTH)):
        f1, mix, f2 = saved[i]
        dy = ffn_backward("ffn2", i, dy, f2)
        dz, dgain[i][1], dbias[i][1] = ln_bwd(dy, mix[1], mix[2], gain(i, 1))
        if i % 2 == 0:
            y_mid, _, _, mixed, pv = mix
            grads["pool_w_out"] = square_grad(pv, dz)
            dv = mm_nt(dz, w_pool_out, None, a_blocked=False)
            du, dwg, dscale = pool_mix_bwd(dv, mixed, w_group, pool_scale)
            grads["pool_w_group"] = dwg.reshape(G, N_CHIPS, CB, C).transpose(1, 0, 2, 3)
            grads["pool_w_in"] = square_grad(y_mid, du)
            dy = mm_nt(du, w_pool_in, dz, a_blocked=False)
        else:
            y_mid, _, _, srcs, qkvs, ao, lse, w_qkv, w_attn_out = mix
            grads["attn_w_out"] = square_grad(ao, dz)
            dao = mm_nt(dz, w_attn_out, None, a_blocked=False)
            in_order = [dict(zip(moved_dils, dilate_rows(t, moved_dils, F32))) for t in (dao, ao, lse)]
            dqkvs = [attn_bwd(qkvs[g], *[m.get(dil, t) for m, t in zip(in_order, (dao, ao, lse))], g, dil)
                     for g, dil in enumerate(dils)]
            g_qkv = [mm_tn([(src, dqkv, 1.0)], nblk=3, out_shape=(D, 3 * D), out_block=(D, D), out_index=lambda j: (0, j),
                           name="qkv_wgrad")[0] for src, dqkv in zip(srcs, dqkvs)]
            grads["attn_w_qkv"] = jnp.concatenate(g_qkv, axis=1).reshape(D, N_CHIPS, QKV // N_CHIPS).transpose(1, 0, 2)
            dy = mm_nt_dilated(dqkvs, dils, w_qkv, dz)
        dy = ffn_backward("ffn1", i, dy, f1)
    grad_x = dy.reshape(x.shape)

    def as_blocks(g):
        return g.reshape(N_CHIPS, 2, -1, g.shape[-1])

    grads.update({n: jnp.stack(per_layer, axis=1) for n, per_layer in layer_grads.items()})
    blocks = [as_blocks(grads[n]) for n in MATRIX_NAMES]
    from_sibling = sibling_exchange_halves(blocks)
    core = cc.reshape(1).astype(jnp.int32)
    chip_id = chip.reshape(1).astype(jnp.int32)
    pair_sums = [pair_sum(core, b, r) for b, r in zip(blocks, from_sibling)]
    from_chips = chip_exchange(pair_sums)
    own_half = dict(zip(MATRIX_NAMES, [chip_sum(chip_id, p, r) for p, r in zip(pair_sums, from_chips)]))
    other_half = dict(zip(MATRIX_NAMES, sibling_share([own_half[n] for n in MATRIX_NAMES])))

    small = jnp.concatenate([jnp.concatenate(dgain[i] + dbias[i], axis=0) for i in range(DEPTH)] + [dscale, jnp.zeros((3, D), F32)], axis=0)
    small = all_reduce_small(small)
    per_layer = small[:6 * DEPTH].reshape(DEPTH, 6, D)
    cols = D // N_CHIPS
    small_grads = {"ln_gain": lax.dynamic_slice_in_dim(per_layer[:, 0:3], chip * cols, cols, axis=2),
                   "ln_bias": lax.dynamic_slice_in_dim(per_layer[:, 3:6], chip * cols, cols, axis=2),
                   "pool_scale": small[6 * DEPTH:6 * DEPTH + 1]}

    grad_w, delta, new_m, new_v = {}, {}, {}, {}
    for n in WEIGHT_NAMES:
        shape = weights[n].shape
        if n in MATRIX_NAMES:
            outs = adamw(core, _halves(n, weights[n]), own_half[n], other_half[n], _halves(n, moms[n]), _halves(n, vels[n]))
            grad_w[n], delta[n], new_m[n], new_v[n] = [_unhalves(n, t, shape) for t in outs]
        else:
            as3 = lambda t: t.reshape(1, -1, t.shape[-1])
            g2 = small_grads[n].reshape(-1, shape[-1])
            outs = adamw(core, as3(weights[n]), g2, g2, as3(moms[n]), as3(vels[n]))
            grad_w[n], delta[n], new_m[n], new_v[n] = [t.reshape(shape) for t in outs]

    return (loss, grad_x, *[grad_w[n] for n in WEIGHT_NAMES], *[delta[n] for n in WEIGHT_NAMES],
            *[new_m[n] for n in WEIGHT_NAMES], *[new_v[n] for n in WEIGHT_NAMES])
```

```python
import functools
import math

import numpy as np
import jax
import jax.numpy as jnp
from jax import lax
from jax.experimental import pallas as pl
from jax.experimental.pallas import tpu as pltpu
from jax.experimental.pallas import tpu_sc as plsc

F32 = jnp.float32
BF16 = jnp.bfloat16

DEPTH = 2
ALPHA = (2.0 * DEPTH) ** 0.25
MACARON_WEIGHT = 0.5
LN_EPS = 1e-5
MASK_VALUE = -1e30
POOL_WINDOWS = (2, 4, 8, 16)
POOL_PAD = 16
HEAD_DIM = 64
N_HEADS = 16
DIL_CONFIGS = ((128, 1), (512, 4), (2048, 16))
ATTN_R = 64
ATTN_BQ = 128
ATTN_W = ATTN_BQ + 2 * ATTN_R
ATTN_UNROLL = 4
LANES = 128
ADAM_LR = 0.001
ADAM_B1 = 0.9
ADAM_B2 = 0.999
ADAM_EPS = 1e-08
ADAM_WD = 0.01
ADAM_STEP = 10
N_CHIPS = 4
GATHER_COLLECTIVE_ID = 1
SIBLING_COLLECTIVE_ID = 2
CHIPS_COLLECTIVE_ID = 3
VMEM_LIMIT = 56 * 1024 * 1024
MESH = pl.DeviceIdType.MESH
ANY = pl.BlockSpec(memory_space=pl.ANY)


def _params(sem=None, vmem=VMEM_LIMIT):
    return pltpu.CompilerParams(dimension_semantics=sem, vmem_limit_bytes=vmem)


def _alibi_slopes():
    n = len(DIL_CONFIGS) * N_HEADS
    s = 2.0 ** (-8.0 * np.arange(1, n + 1) / n)
    return s.reshape(len(DIL_CONFIGS), N_HEADS).astype(np.float32)


def _ln_fwd(z, g, b):
    mu = jnp.mean(z, axis=-1, keepdims=True)
    zc = z - mu
    var = jnp.mean(zc * zc, axis=-1, keepdims=True)
    rstd = lax.rsqrt(var + LN_EPS)
    xhat = zc * rstd
    return xhat * g + b, xhat, rstd


def _dot(a, b):
    return jnp.dot(a, b, preferred_element_type=F32)


def _dot_nt(a, b):
    return lax.dot_general(a, b, (((1,), (1,)), ((), ())), preferred_element_type=F32)


def _dot_tn(a, b):
    return lax.dot_general(a, b, (((0,), (0,)), ((), ())), preferred_element_type=F32)


def mm_nn(a, b, out_dtype, first_block=0, nb=None, tm=512):
    S, K = a.shape
    Nb = K
    nb = b.shape[1] // Nb if nb is None else nb

    def body(a_ref, b_ref, o_ref):
        o_ref[...] = _dot(a_ref[...].astype(BF16), b_ref[...]).astype(out_dtype)

    return pl.pallas_call(
        body, name="mm_nn",
        grid=(S // tm, nb),
        in_specs=[pl.BlockSpec((tm, K), lambda i, j: (i, 0)), pl.BlockSpec((K, Nb), lambda i, j: (0, first_block + j))],
        out_specs=pl.BlockSpec((None, tm, Nb), lambda i, j: (j, i, 0)),
        out_shape=jax.ShapeDtypeStruct((nb, S, Nb), out_dtype),
        compiler_params=_params(("parallel", "arbitrary")),
    )(a, b)


def chip_blocks_to_columns(w):
    nb, K, Nb = w.shape

    def body(w_ref, o_ref):
        o_ref[...] = w_ref[...]

    return pl.pallas_call(
        body, name="chip_blocks_to_columns",
        grid=(nb,),
        in_specs=[pl.BlockSpec((None, K, Nb), lambda b: (b, 0, 0))],
        out_specs=pl.BlockSpec((K, Nb), lambda b: (0, b)),
        out_shape=jax.ShapeDtypeStruct((K, nb * Nb), w.dtype),
        compiler_params=_params(("parallel",)),
    )(w)


def proj_ln(a, w, resid, gain, bias, tm=512):
    S, K = a.shape
    D = w.shape[1]

    def body(a_ref, w_ref, r_ref, g_ref, b_ref, y_ref, xh_ref, rs_ref):
        z = ALPHA * r_ref[...] + _dot(a_ref[...].astype(BF16), w_ref[...])
        y, xh, rs = _ln_fwd(z, g_ref[...], b_ref[...])
        y_ref[...] = y
        xh_ref[...] = xh
        rs_ref[...] = rs

    row = pl.BlockSpec((tm, D), lambda i: (i, 0))
    vec = pl.BlockSpec((1, D), lambda i: (0, 0))
    return pl.pallas_call(
        body, name="proj_ln",
        grid=(S // tm,),
        in_specs=[pl.BlockSpec((tm, K), lambda i: (i, 0)), pl.BlockSpec((K, D), lambda i: (0, 0)), row, vec, vec],
        out_specs=[row, row, pl.BlockSpec((tm, 1), lambda i: (i, 0))],
        out_shape=[jax.ShapeDtypeStruct((S, D), F32), jax.ShapeDtypeStruct((S, D), F32), jax.ShapeDtypeStruct((S, 1), F32)],
        compiler_params=_params(("parallel",)),
    )(a, w, resid, gain, bias)


def mm_nt(a, w, resid, a_blocked, out_dtype=F32, tm=512):
    if a_blocked:
        nk, S, Kb = a.shape
        a_spec = pl.BlockSpec((None, tm, Kb), lambda i, n: (n, i, 0))
    else:
        S, Kb = a.shape
        nk = 1
        a_spec = pl.BlockSpec((tm, Kb), lambda i, n: (i, 0))
    M = w.shape[0]
    has_resid = resid is not None

    def body(*refs):
        if has_resid:
            a_ref, w_ref, r_ref, o_ref, acc = refs
        else:
            a_ref, w_ref, o_ref, acc = refs
        n = pl.program_id(1)
        part = _dot_nt(a_ref[...].astype(BF16), w_ref[...])

        @pl.when(n == 0)
        def _():
            acc[...] = part

        @pl.when(n > 0)
        def _():
            acc[...] += part

        @pl.when(n == nk - 1)
        def _():
            out = acc[...]
            if has_resid:
                out = out + ALPHA * r_ref[...]
            o_ref[...] = out.astype(out_dtype)

    row = pl.BlockSpec((tm, M), lambda i, n: (i, 0))
    in_specs = [a_spec, pl.BlockSpec((M, Kb), lambda i, n: (0, n))] + ([row] if has_resid else [])
    args = (a, w) + ((resid,) if has_resid else ())
    return pl.pallas_call(
        body, name="mm_nt",
        grid=(S // tm, nk),
        in_specs=in_specs,
        out_specs=row,
        out_shape=jax.ShapeDtypeStruct((S, M), out_dtype),
        scratch_shapes=[pltpu.VMEM((tm, M), F32)],
        compiler_params=_params(("parallel", "arbitrary")),
    )(*args)


def mm_nt_dilated(parts, dils, w, resid, tm=512):
    n_groups = len(parts)
    _, S, K = parts[0].shape
    M = w.shape[0]
    nk = 3 * n_groups

    def body(*refs):
        a_refs = refs[:n_groups]
        w_ref, r_ref, o_ref, group_acc, total = refs[n_groups:]
        n = pl.program_id(1)
        for g in range(n_groups):
            for k in range(3):
                @pl.when(n == 3 * g + k)
                def _():
                    part = _dot_nt(a_refs[g][...], w_ref[...])
                    for c in range(M // LANES):
                        lanes = slice(c * LANES, (c + 1) * LANES)
                        if k == 0:
                            group_acc[c] = part[:, lanes]
                        else:
                            group_acc[c] += part[:, lanes]
                        if k == 2:
                            _rows_from_dilated(group_acc.at[c], total.at[c], tm, dils[g], accumulate=g > 0)

        @pl.when(n == nk - 1)
        def _():
            for c in range(M // LANES):
                lanes = slice(c * LANES, (c + 1) * LANES)
                o_ref[:, lanes] = total[c] + ALPHA * r_ref[:, lanes]

    def a_spec(g):
        return pl.BlockSpec((None, tm, K), lambda i, n: (jnp.clip(n - 3 * g, 0, 2), i, 0))

    row = pl.BlockSpec((tm, M), lambda i, n: (i, 0))
    return pl.pallas_call(
        body, name="mm_nt_dilated",
        grid=(S // tm, nk),
        in_specs=[a_spec(g) for g in range(n_groups)] + [pl.BlockSpec((M, K), lambda i, n: (0, n)), row],
        out_specs=row,
        out_shape=jax.ShapeDtypeStruct((S, M), F32),
        scratch_shapes=[pltpu.VMEM((M // LANES, tm, LANES), F32), pltpu.VMEM((M // LANES, tm, LANES), F32)],
        compiler_params=_params(("parallel", "arbitrary")),
    )(*parts, w, resid)


def mm_tn(pairs, *, nblk, out_shape, out_block, out_index, alias=None, tk=512, name="mm_tn"):
    operands = []
    for a, b, _ in pairs:
        for t in (a, b):
            if not any(t is o for o in operands):
                operands.append(t)
    where = lambda t: next(i for i, o in enumerate(operands) if o is t)
    S = pairs[0][0].shape[-2]
    n_out, n_in = len(pairs), len(operands)
    n_alias = len(alias) if alias is not None else 0

    def spec(t):
        if t.ndim == 3:
            return pl.BlockSpec((None, tk, t.shape[-1]), lambda j, k: (j, k, 0))
        return pl.BlockSpec((tk, t.shape[-1]), lambda j, k: (k, 0))

    def body(*refs):
        refs = refs[n_alias:]
        in_refs, o_refs, accs = refs[:n_in], refs[n_in:n_in + n_out], refs[n_in + n_out:]
        k = pl.program_id(1)
        for (a, b, scale), o_ref, acc in zip(pairs, o_refs, accs):
            part = _dot_tn(in_refs[where(a)][...].astype(BF16), in_refs[where(b)][...].astype(BF16))

            @pl.when(k == 0)
            def _():
                acc[...] = part

            @pl.when(k > 0)
            def _():
                acc[...] += part

            @pl.when(k == S // tk - 1)
            def _():
                o_ref[...] = (scale * acc[...]).astype(BF16)

    out_spec = pl.BlockSpec(out_block, lambda j, k: out_index(j))
    outs = pl.pallas_call(
        body, name=name,
        grid=(nblk, S // tk),
        in_specs=[ANY] * n_alias + [spec(t) for t in operands],
        out_specs=[out_spec] * n_out,
        out_shape=[jax.ShapeDtypeStruct(out_shape, BF16)] * n_out,
        scratch_shapes=[pltpu.VMEM((a.shape[-1], b.shape[-1]), F32) for a, b, _ in pairs],
        input_output_aliases={i: i for i in range(n_alias)},
        compiler_params=_params(("parallel", "arbitrary")),
    )(*(tuple(alias) if alias is not None else ()), *operands)
    return list(outs)


def ffn_fwd(x, wg, wu, wd, gain, bias, tm=512):
    S, D = x.shape
    nb, FB = wg.shape[0], wg.shape[1]

    def body(x_ref, wg_ref, wu_ref, wd_ref, g_ref, b_ref, y_ref, xh_ref, rs_ref, a_ref, u_ref, acc):
        j = pl.program_id(1)
        xb = x_ref[...].astype(BF16)
        a = _dot_nt(xb, wg_ref[...])
        u = _dot_nt(xb, wu_ref[...])
        a_ref[...] = a.astype(BF16)
        u_ref[...] = u.astype(BF16)
        h = a * jax.nn.sigmoid(a) * u
        part = _dot(h.astype(BF16), wd_ref[...])

        @pl.when(j == 0)
        def _():
            acc[...] = part

        @pl.when(j > 0)
        def _():
            acc[...] += part

        @pl.when(j == nb - 1)
        def _():
            z = ALPHA * x_ref[...] + MACARON_WEIGHT * acc[...]
            y, xh, rs = _ln_fwd(z, g_ref[...], b_ref[...])
            y_ref[...] = y
            xh_ref[...] = xh
            rs_ref[...] = rs

    row = pl.BlockSpec((tm, D), lambda i, j: (i, 0))
    vec = pl.BlockSpec((1, D), lambda i, j: (0, 0))
    w_out = pl.BlockSpec((None, FB, D), lambda i, j: (j, 0, 0))
    act = pl.BlockSpec((None, tm, FB), lambda i, j: (j, i, 0))
    return pl.pallas_call(
        body, name="ffn_fwd",
        grid=(S // tm, nb),
        in_specs=[row, w_out, w_out, w_out, vec, vec],
        out_specs=[row, row, pl.BlockSpec((tm, 1), lambda i, j: (i, 0)), act, act],
        out_shape=[jax.ShapeDtypeStruct((S, D), F32), jax.ShapeDtypeStruct((S, D), F32), jax.ShapeDtypeStruct((S, 1), F32),
                   jax.ShapeDtypeStruct((nb, S, FB), BF16), jax.ShapeDtypeStruct((nb, S, FB), BF16)],
        scratch_shapes=[pltpu.VMEM((tm, D), F32)],
        compiler_params=_params(("parallel", "arbitrary")),
    )(x, wg, wu, wd, gain, bias)


def ffn_bwd(dz, a, u, wg, wu, wd, tm=512):
    S, D = dz.shape
    nb, FB = wg.shape[0], wg.shape[1]

    def body(dz_ref, a_ref, u_ref, wg_ref, wu_ref, wd_ref, dx_ref, h_ref, da_ref, du_ref, acc):
        j = pl.program_id(1)
        dzb = (MACARON_WEIGHT * dz_ref[...]).astype(BF16)
        dh = _dot_nt(dzb, wd_ref[...])
        av = a_ref[...].astype(F32)
        uv = u_ref[...].astype(F32)
        s = jax.nn.sigmoid(av)
        silu = av * s
        h_ref[...] = (silu * uv).astype(BF16)
        da = (dh * uv * (s * (1.0 + av * (1.0 - s)))).astype(BF16)
        du = (dh * silu).astype(BF16)
        da_ref[...] = da
        du_ref[...] = du
        part = _dot(da, wg_ref[...]) + _dot(du, wu_ref[...])

        @pl.when(j == 0)
        def _():
            acc[...] = part

        @pl.when(j > 0)
        def _():
            acc[...] += part

        @pl.when(j == nb - 1)
        def _():
            dx_ref[...] = ALPHA * dz_ref[...] + acc[...]

    row = pl.BlockSpec((tm, D), lambda i, j: (i, 0))
    w_out = pl.BlockSpec((None, FB, D), lambda i, j: (j, 0, 0))
    act = pl.BlockSpec((None, tm, FB), lambda i, j: (j, i, 0))
    act_shape = jax.ShapeDtypeStruct((nb, S, FB), BF16)
    return pl.pallas_call(
        body, name="ffn_bwd",
        grid=(S // tm, nb),
        in_specs=[row, act, act, w_out, w_out, w_out],
        out_specs=[row, act, act, act],
        out_shape=[jax.ShapeDtypeStruct((S, D), F32), act_shape, act_shape, act_shape],
        scratch_shapes=[pltpu.VMEM((tm, D), F32)],
        compiler_params=_params(("parallel", "arbitrary")),
    )(dz, a, u, wg, wu, wd)


def ln_bwd(dy, xhat, rstd, gain, tm=512):
    S, D = dy.shape

    def body(dy_ref, xh_ref, rs_ref, g_ref, dz_ref, dg_ref, db_ref):
        i = pl.program_id(0)
        dy = dy_ref[...]
        xh = xh_ref[...]
        dxh = dy * g_ref[...]
        m1 = jnp.mean(dxh, axis=-1, keepdims=True)
        m2 = jnp.mean(dxh * xh, axis=-1, keepdims=True)
        dz_ref[...] = rs_ref[...] * (dxh - m1 - xh * m2)
        dg = jnp.sum(dy * xh, axis=0, keepdims=True)
        db = jnp.sum(dy, axis=0, keepdims=True)

        @pl.when(i == 0)
        def _():
            dg_ref[...] = dg
            db_ref[...] = db

        @pl.when(i > 0)
        def _():
            dg_ref[...] += dg
            db_ref[...] += db

    row = pl.BlockSpec((tm, D), lambda i: (i, 0))
    vec = pl.BlockSpec((1, D), lambda i: (0, 0))
    return pl.pallas_call(
        body, name="ln_bwd",
        grid=(S // tm,),
        in_specs=[row, row, pl.BlockSpec((tm, 1), lambda i: (i, 0)), vec],
        out_specs=[row, vec, vec],
        out_shape=[jax.ShapeDtypeStruct((S, D), F32), jax.ShapeDtypeStruct((1, D), F32), jax.ShapeDtypeStruct((1, D), F32)],
        compiler_params=_params(("arbitrary",)),
    )(dy, xhat, rstd, gain)


def loss_head(y, target, tm=512):
    S, D = y.shape

    def body(y_ref, t_ref, dy_ref, l_ref):
        i = pl.program_id(0)
        e = y_ref[...] - t_ref[...]
        dy_ref[...] = e / D
        part = 0.5 * jnp.sum(jnp.mean(e * e, axis=-1, keepdims=True), axis=0, keepdims=True)

        @pl.when(i == 0)
        def _():
            l_ref[...] = part

        @pl.when(i > 0)
        def _():
            l_ref[...] += part

    row = pl.BlockSpec((tm, D), lambda i: (i, 0))
    return pl.pallas_call(
        body, name="loss_head",
        grid=(S // tm,),
        in_specs=[row, row],
        out_specs=[row, pl.BlockSpec((1, 1), lambda i: (0, 0))],
        out_shape=[jax.ShapeDtypeStruct((S, D), F32), jax.ShapeDtypeStruct((1, 1), F32)],
        compiler_params=_params(("arbitrary",)),
    )(y, target)


def _pool_window(xp, g):
    n = xp.shape[0]
    w = xp + pltpu.roll(xp, 1, 0)
    out = w
    for level, shift in enumerate((1, 2, 4), start=1):
        w = pltpu.roll(w, shift, 0) + pltpu.roll(w, n - shift, 0)
        out = jnp.where(g >= level, w, out)
    return out


def _pool_count(S, C, g):
    half = lax.shift_left(jnp.int32(1), g)
    t = lax.broadcasted_iota(jnp.int32, (S, C), 0)
    return (jnp.minimum(t + half, S) - jnp.maximum(t - half, 0)).astype(F32)


def pool_mix(u, wgrp, scale):
    S, D = u.shape
    G, C = wgrp.shape[0], wgrp.shape[1]

    def body(u_ref, w_ref, s_ref, mix_ref, v_ref, pad):
        g = pl.program_id(0)
        zeros = jnp.zeros((POOL_PAD, C), F32)
        pad[pl.ds(0, POOL_PAD), :] = zeros
        pad[pl.ds(POOL_PAD + S, POOL_PAD), :] = zeros
        pad[pl.ds(POOL_PAD, S), :] = u_ref[...]
        win = _pool_window(pad[...], g)[POOL_PAD:POOL_PAD + S]
        mixed = (win / _pool_count(S, C, g) - u_ref[...]).astype(BF16)
        mix_ref[...] = mixed
        v_ref[...] = _dot(mixed, w_ref[...]) * s_ref[...]

    col = pl.BlockSpec((S, C), lambda g: (0, g))
    return pl.pallas_call(
        body, name="pool_mix",
        grid=(G,),
        in_specs=[col, pl.BlockSpec((None, C, C), lambda g: (g, 0, 0)), pl.BlockSpec((1, C), lambda g: (0, g))],
        out_specs=[col, col],
        out_shape=[jax.ShapeDtypeStruct((S, D), BF16), jax.ShapeDtypeStruct((S, D), F32)],
        scratch_shapes=[pltpu.VMEM((S + 2 * POOL_PAD, C), F32)],
        compiler_params=_params(("arbitrary",)),
    )(u, wgrp, scale)


def pool_mix_bwd(dv, mixed, wgrp, scale):
    S, D = dv.shape
    G, C = wgrp.shape[0], wgrp.shape[1]

    def body(dv_ref, mix_ref, w_ref, s_ref, du_ref, dw_ref, ds_ref, pad):
        g = pl.program_id(0)
        mixed = mix_ref[...]
        dv = dv_ref[...]
        yg = _dot(mixed, w_ref[...])
        ds_ref[...] = jnp.sum(dv * yg, axis=0, keepdims=True)
        dyg = (dv * s_ref[...]).astype(BF16)
        dw_ref[...] = _dot_tn(mixed, dyg).astype(BF16)
        dmix = _dot_nt(dyg, w_ref[...])
        zeros = jnp.zeros((POOL_PAD, C), F32)
        pad[pl.ds(0, POOL_PAD), :] = zeros
        pad[pl.ds(POOL_PAD + S, POOL_PAD), :] = zeros
        pad[pl.ds(POOL_PAD, S), :] = dmix / _pool_count(S, C, g)
        win = _pool_window(pad[...], g)
        win = pltpu.roll(win, win.shape[0] - 1, 0)[POOL_PAD:POOL_PAD + S]
        du_ref[...] = win - dmix

    col = pl.BlockSpec((S, C), lambda g: (0, g))
    return pl.pallas_call(
        body, name="pool_mix_bwd",
        grid=(G,),
        in_specs=[col, col, pl.BlockSpec((None, C, C), lambda g: (g, 0, 0)), pl.BlockSpec((1, C), lambda g: (0, g))],
        out_specs=[col, pl.BlockSpec((None, C, C), lambda g: (g, 0, 0)), pl.BlockSpec((1, C), lambda g: (0, g))],
        out_shape=[jax.ShapeDtypeStruct((S, D), F32), jax.ShapeDtypeStruct((G, C, C), BF16), jax.ShapeDtypeStruct((1, D), F32)],
        scratch_shapes=[pltpu.VMEM((S + 2 * POOL_PAD, C), F32)],
        compiler_params=_params(("arbitrary",)),
    )(dv, mixed, wgrp, scale)


PERM_BLOCK = 256


def _dilated_runs(rows, d):
    n = PERM_BLOCK // d
    return [(c * PERM_BLOCK, r, n) for c in range(rows // PERM_BLOCK) for r in range(d)]


def _rows_to_dilated(src_ref, dst_ref, rows, d):
    for base, r, n in _dilated_runs(rows, d):
        dst_ref[pl.ds(base + r * n, n), :] = src_ref[pl.ds(base + r, n, stride=d), :].astype(dst_ref.dtype)


def _rows_from_dilated(src_ref, dst_ref, rows, d, accumulate=False):
    for base, r, n in _dilated_runs(rows, d):
        at = pl.ds(base + r, n, stride=d)
        v = src_ref[pl.ds(base + r * n, n), :]
        dst_ref[at, :] = dst_ref[at, :] + v if accumulate else v


def dilate_rows(x, dils, out_dtype, tm=1024):
    S, D = x.shape

    def body(x_ref, *o_refs):
        for d, o_ref in zip(dils, o_refs):
            _rows_to_dilated(x_ref, o_ref, tm, d)

    tile = pl.BlockSpec((tm, LANES), lambda i, j: (i, j))
    return pl.pallas_call(
        body, name="dilate_rows",
        grid=(S // tm, D // LANES),
        in_specs=[tile],
        out_specs=[tile] * len(dils),
        out_shape=[jax.ShapeDtypeStruct((S, D), out_dtype)] * len(dils),
        compiler_params=_params(("parallel", "parallel")),
    )(x)


def _slope_table(group, dilation):
    s = _alibi_slopes()[group].reshape(N_HEADS // 2, 2, 1, 1) * float(dilation)
    return jnp.asarray(np.broadcast_to(s, (N_HEADS // 2, 2, 1, ATTN_W)).copy())


def _residue_shape(S, D, d):
    return (S, D) if d == 1 else (S // PERM_BLOCK, d, PERM_BLOCK // d, D)


def _residue_view(x, d):
    return x.reshape(x.shape[:-2] + _residue_shape(x.shape[-2], x.shape[-1], d))


def _residue_spec(lead_block, lead_index, S, d):
    if d == 1:
        return pl.BlockSpec(lead_block + (S, LANES), lambda hp, r: lead_index + (0, hp))
    return pl.BlockSpec(lead_block + (S // PERM_BLOCK, None, PERM_BLOCK // d, LANES), lambda hp, r: lead_index + (0, r, 0, hp))


def _whole(ref, lead, L):
    return ref[lead + (slice(None),) * (len(ref.shape) - len(lead))].reshape(L, LANES)


def _query_rows(lead, i, d):
    if d == 1:
        return lead + (pl.ds(pl.multiple_of(i * ATTN_BQ, ATTN_BQ), ATTN_BQ), slice(None)), (ATTN_BQ, LANES)
    n = PERM_BLOCK // d
    return lead + (pl.ds(i * (ATTN_BQ // n), ATTN_BQ // n), slice(None), slice(None)), (ATTN_BQ // n, n, LANES)


def _load_query_rows(ref, lead, i, d):
    at, _ = _query_rows(lead, i, d)
    return ref[at].reshape(ATTN_BQ, LANES)


def _store_query_rows(ref, lead, i, d, value):
    at, shape = _query_rows(lead, i, d)
    ref[at] = value.reshape(shape)


def _stage_keys(dst, src_ref, L):
    rows = _whole(src_ref, (), L)
    lane = lax.broadcasted_iota(jnp.int32, (L, LANES), 1)
    zeros = jnp.zeros((ATTN_R, LANES), dst.dtype)
    for h in range(2):
        mine = (lane < HEAD_DIM) if h == 0 else (lane >= HEAD_DIM)
        dst[h, pl.ds(0, ATTN_R), :] = zeros
        dst[h, pl.ds(ATTN_R + L, ATTN_R), :] = zeros
        dst[h, pl.ds(ATTN_R, L), :] = jnp.where(mine, rows, jnp.zeros_like(rows))


def _fill_bias(bias, sl_ref):
    a = lax.broadcasted_iota(jnp.int32, (ATTN_BQ, ATTN_W), 0)
    c = lax.broadcasted_iota(jnp.int32, (ATTN_BQ, ATTN_W), 1)
    rel = jnp.abs(c - ATTN_R - a)
    band = rel <= ATTN_R
    after_start = c >= ATTN_R
    before_end = c < ATTN_BQ + ATTN_R
    for h in range(2):
        base = -(sl_ref[h] * rel.astype(F32))
        for variant in range(4):
            ok = band
            if variant & 1:
                ok = ok & after_start
            if variant & 2:
                ok = ok & before_end
            bias[variant, h] = jnp.where(ok, base, MASK_VALUE)


def _bias_variant(i, nq):
    return jnp.where(i == 0, 1, 0) + jnp.where(i == nq - 1, 2, 0)


def attn_fwd(qkv, group, dilation):
    _, S, D = qkv.shape
    d = dilation
    L = S // d
    nq = L // ATTN_BQ
    ncol = D // LANES
    view = _residue_view(qkv, d)
    slopes = _slope_table(group, d)
    scale = HEAD_DIM ** -0.5

    def body(q_ref, k_ref, v_ref, sl_ref, o_ref, lse_ref, k2, v2, bias):
        @pl.when(pl.program_id(1) == 0)
        def _():
            _fill_bias(bias, sl_ref)

        _stage_keys(k2, k_ref, L)
        _stage_keys(v2, v_ref, L)
        head0 = lax.broadcasted_iota(jnp.int32, (ATTN_BQ, LANES), 1) < HEAD_DIM

        def step(i, carry):
            variant = _bias_variant(i, nq)
            win = pl.ds(pl.multiple_of(i * ATTN_BQ, ATTN_BQ), ATTN_W)
            qs = _load_query_rows(q_ref, (), i, d) * jnp.asarray(scale, BF16)
            acc, ms, ls = None, [], []
            for h in range(2):
                s = _dot_nt(qs, k2[h, win, :]) + bias[variant, h]
                m = jnp.max(s, axis=-1, keepdims=True)
                e = jnp.exp(s - m)
                ls.append(jnp.sum(e, axis=-1, keepdims=True))
                ms.append(m)
                part = _dot(e.astype(BF16), v2[h, win, :])
                acc = part if acc is None else acc + part
            out = acc * jnp.where(head0, 1.0 / ls[0], 1.0 / ls[1])
            lse = jnp.where(head0, ms[0] + jnp.log(ls[0]), ms[1] + jnp.log(ls[1]))
            _store_query_rows(o_ref, (), i, d, out)
            _store_query_rows(lse_ref, (), i, d, lse)
            return carry

        lax.fori_loop(0, nq, step, 0, unroll=min(ATTN_UNROLL, nq))

    def col(which):
        return _residue_spec((None,), (which,), S, d)

    out = _residue_spec((), (), S, d)
    o, lse = pl.pallas_call(
        body, name=f"attn_fwd_g{group}",
        grid=(ncol, d),
        in_specs=[col(0), col(1), col(2), pl.BlockSpec((None, 2, 1, ATTN_W), lambda hp, r: (hp, 0, 0, 0))],
        out_specs=[out, out],
        out_shape=[jax.ShapeDtypeStruct(_residue_shape(S, D, d), F32)] * 2,
        scratch_shapes=[pltpu.VMEM((2, L + 2 * ATTN_R, LANES), BF16), pltpu.VMEM((2, L + 2 * ATTN_R, LANES), BF16),
                        pltpu.VMEM((4, 2, ATTN_BQ, ATTN_W), F32)],
        compiler_params=_params(("arbitrary", "arbitrary")),
    )(view, view, view, slopes)
    return o.reshape(S, D), lse.reshape(S, D)


def attn_combine(os, lses, dils, tm=1024):
    S, D = os[0].shape
    n = len(os)
    n_moved = sum(d > 1 for d in dils)

    def body(*refs):
        o_refs, l_refs, out_ref, lse_ref = list(refs[:n]), list(refs[n:2 * n]), refs[2 * n], refs[2 * n + 1]
        spare = list(refs[2 * n + 2:])
        for g, d in enumerate(dils):
            if d > 1:
                for which in (o_refs, l_refs):
                    token_order = spare.pop()
                    _rows_from_dilated(which[g], token_order, tm, d)
                    which[g] = token_order
        ls = [r[...] for r in l_refs]
        m = functools.reduce(jnp.maximum, ls)
        es = [jnp.exp(l - m) for l in ls]
        tot = functools.reduce(lambda x, y: x + y, es)
        inv = 1.0 / tot
        out_ref[...] = functools.reduce(lambda x, y: x + y, [(e * inv) * r[...] for e, r in zip(es, o_refs)])
        lse_ref[...] = m + jnp.log(tot)

    tile = pl.BlockSpec((tm, LANES), lambda i, j: (i, j))
    return pl.pallas_call(
        body, name="attn_combine",
        grid=(S // tm, D // LANES),
        in_specs=[tile] * (2 * n),
        out_specs=[tile, tile],
        out_shape=[jax.ShapeDtypeStruct((S, D), F32), jax.ShapeDtypeStruct((S, D), F32)],
        scratch_shapes=[pltpu.VMEM((tm, LANES), F32)] * (2 * n_moved),
        compiler_params=_params(("parallel", "parallel")),
    )(*os, *lses)


def attn_bwd(qkv, do, o, lse, group, dilation):
    _, S, D = qkv.shape
    d = dilation
    L = S // d
    nq = L // ATTN_BQ
    ncol = D // LANES
    view = _residue_view(qkv, d)
    slopes = _slope_table(group, d)
    scale = HEAD_DIM ** -0.5

    def body(q_ref, k_ref, v_ref, do_ref, o_ref, lse_ref, sl_ref, dx_ref, k2, v2, dkacc, dvacc, bias):
        @pl.when(pl.program_id(1) == 0)
        def _():
            _fill_bias(bias, sl_ref)

        _stage_keys(k2, k_ref, L)
        _stage_keys(v2, v_ref, L)
        dkacc[...] = jnp.zeros_like(dkacc)
        dvacc[...] = jnp.zeros_like(dvacc)
        lane = lax.broadcasted_iota(jnp.int32, (ATTN_BQ, LANES), 1)
        heads = (lane < HEAD_DIM, lane >= HEAD_DIM)
        key_head0 = lax.broadcasted_iota(jnp.int32, (ATTN_W, LANES), 1) < HEAD_DIM

        def step(i, carry):
            variant = _bias_variant(i, nq)
            win = pl.ds(pl.multiple_of(i * ATTN_BQ, ATTN_BQ), ATTN_W)
            q = _load_query_rows(q_ref, (), i, d)
            qs = q * jnp.asarray(scale, BF16)
            dov = _load_query_rows(do_ref, (), i, d)
            prod = dov * _load_query_rows(o_ref, (), i, d)
            lse_v = _load_query_rows(lse_ref, (), i, d)
            dob = dov.astype(BF16)
            dq, dks, dvs = None, [], []
            for h in range(2):
                s = _dot_nt(qs, k2[h, win, :]) + bias[variant, h]
                lse_h = jnp.max(jnp.where(heads[h], lse_v, -jnp.inf), axis=-1, keepdims=True)
                dterm = jnp.sum(jnp.where(heads[h], prod, 0.0), axis=-1, keepdims=True)
                p = jnp.exp(s - lse_h)
                dp = _dot_nt(dob, v2[h, win, :])
                ds = (p * (dp - dterm) * scale).astype(BF16)
                dvs.append(_dot_tn(p.astype(BF16), dob))
                dks.append(_dot_tn(ds, q))
                part = _dot(ds, k2[h, win, :])
                dq = part if dq is None else dq + part
            dvacc[win, :] += jnp.where(key_head0, dvs[0], dvs[1])
            dkacc[win, :] += jnp.where(key_head0, dks[0], dks[1])
            _store_query_rows(dx_ref, (0,), i, d, dq.astype(BF16))
            return carry

        lax.fori_loop(0, nq, step, 0, unroll=min(ATTN_UNROLL, nq))
        block_shape = dx_ref.shape[1:]
        dx_ref[1] = dkacc[pl.ds(ATTN_R, L), :].astype(BF16).reshape(block_shape)
        dx_ref[2] = dvacc[pl.ds(ATTN_R, L), :].astype(BF16).reshape(block_shape)

    def col(which):
        return _residue_spec((None,), (which,), S, d)

    act = _residue_spec((), (), S, d)
    out = pl.pallas_call(
        body, name=f"attn_bwd_g{group}",
        grid=(ncol, d),
        in_specs=[col(0), col(1), col(2), act, act, act, pl.BlockSpec((None, 2, 1, ATTN_W), lambda hp, r: (hp, 0, 0, 0))],
        out_specs=_residue_spec((3,), (0,), S, d),
        out_shape=jax.ShapeDtypeStruct((3,) + _residue_shape(S, D, d), BF16),
        scratch_shapes=[pltpu.VMEM((2, L + 2 * ATTN_R, LANES), BF16), pltpu.VMEM((2, L + 2 * ATTN_R, LANES), BF16),
                        pltpu.VMEM((L + 2 * ATTN_R, LANES), F32), pltpu.VMEM((L + 2 * ATTN_R, LANES), F32),
                        pltpu.VMEM((4, 2, ATTN_BQ, ATTN_W), F32)],
        compiler_params=_params(("arbitrary", "arbitrary")),
    )(view, view, view, _residue_view(do, d), _residue_view(o, d), _residue_view(lse, d), slopes)
    return out.reshape(3, S, D)


TILE_ELEMS = 256 * 1024


def _row_tile(R, C):
    if R * C <= TILE_ELEMS or R % 16:
        return R
    return max(t for t in range(16, R + 1, 16) if R % t == 0 and (t * C <= TILE_ELEMS or t == 16))


def pair_sum(core, g, recv):
    _, _, R, C = g.shape
    tr = _row_tile(R, C)

    def body(c_ref, g_ref, r_ref, o_ref):
        o_ref[...] = (g_ref[...].astype(F32) + r_ref[...].astype(F32)).astype(BF16)

    blk = pl.BlockSpec((None, tr, C), lambda d, i, c_ref: (d, i, 0))
    return pl.pallas_call(
        body, name="pair_sum",
        grid_spec=pltpu.PrefetchScalarGridSpec(
            num_scalar_prefetch=1, grid=(N_CHIPS, R // tr),
            in_specs=[pl.BlockSpec((None, None, tr, C), lambda d, i, c_ref: (d, c_ref[0], i, 0)), blk],
            out_specs=blk),
        out_shape=jax.ShapeDtypeStruct((N_CHIPS, R, C), BF16),
        compiler_params=_params(("parallel", "parallel")),
    )(core, g, recv)


def chip_sum(chip, own, recv):
    _, R, C = own.shape
    tr = _row_tile(R, C)
    slot_of_relation = {2: 0, 1: 1, 3: 2}

    def body(chip_ref, own_ref, r_ref, o_ref):
        me = chip_ref[0]
        mine = own_ref[...].astype(F32)
        theirs = {rel: r_ref[k].astype(F32) for rel, k in slot_of_relation.items()}
        acc = None
        for s in range(N_CHIPS):
            rel = jnp.bitwise_xor(me, s)
            part = jnp.where(rel == 0, mine, jnp.where(rel == 2, theirs[2], jnp.where(rel == 1, theirs[1], theirs[3])))
            acc = part if acc is None else acc + part
        o_ref[...] = acc

    return pl.pallas_call(
        body, name="chip_sum",
        grid_spec=pltpu.PrefetchScalarGridSpec(
            num_scalar_prefetch=1, grid=(R // tr,),
            in_specs=[pl.BlockSpec((None, tr, C), lambda i, chip_ref: (chip_ref[0], i, 0)),
                      pl.BlockSpec((N_CHIPS - 1, tr, C), lambda i, chip_ref: (0, i, 0))],
            out_specs=pl.BlockSpec((tr, C), lambda i, chip_ref: (i, 0))),
        out_shape=jax.ShapeDtypeStruct((R, C), F32),
        compiler_params=_params(("parallel",)),
    )(chip, own, recv)


def adamw(core, w, g_pairs, m, v):
    L, H, R, C = w.shape
    tr = _row_tile(R, C)

    def body(c_ref, w_ref, *rest):
        g_refs = rest[:2 * L]
        m_ref, v_ref, g_ref, d_ref, nm_ref, nv_ref = rest[2 * L:]
        mine = pl.program_id(1) == c_ref[0]
        g = None
        for l in range(L):
            g_l = jnp.where(mine, g_refs[2 * l][...], g_refs[2 * l + 1][...])
            g = g_l if g is None else jnp.where(pl.program_id(0) == l, g_l, g)
        m = ADAM_B1 * m_ref[...] + (1.0 - ADAM_B1) * g
        v = ADAM_B2 * v_ref[...] + (1.0 - ADAM_B2) * (g * g)
        m_hat = m / (1.0 - ADAM_B1 ** ADAM_STEP)
        v_hat = v / (1.0 - ADAM_B2 ** ADAM_STEP)
        g_ref[...] = g
        d_ref[...] = -ADAM_LR * (m_hat / (jnp.sqrt(v_hat) + ADAM_EPS) + ADAM_WD * w_ref[...])
        nm_ref[...] = m
        nv_ref[...] = v

    blk = pl.BlockSpec((None, None, tr, C), lambda l, h, i, c_ref: (l, h, i, 0))

    def half(layer):
        return pl.BlockSpec((tr, C), lambda l, h, i, c_ref: (jnp.where(l == layer, i, 0), 0))

    shape = jax.ShapeDtypeStruct((L, H, R, C), F32)
    return pl.pallas_call(
        body, name="adamw",
        grid_spec=pltpu.PrefetchScalarGridSpec(
            num_scalar_prefetch=1, grid=(L, H, R // tr),
            in_specs=[blk] + [half(l) for l in range(L) for _ in range(2)] + [blk, blk],
            out_specs=[blk] * 4),
        out_shape=[shape] * 4,
        compiler_params=_params(("parallel", "parallel", "parallel")),
    )(core, w, *[g for pair in g_pairs for g in pair], m, v)


def _place():
    return lax.axis_index("x"), lax.axis_index("y"), lax.axis_index("c")


def _other_chips(x, y):
    return [(2 * (1 - x) + y, (1 - x, y)), (2 * x + (1 - y), (x, 1 - y)), (2 * (1 - x) + (1 - y), (1 - x, 1 - y))]


def all_gather_shards(shards, placed):
    n = len(shards)

    def body(*refs):
        ins, outs = refs[:n], refs[2 * n:3 * n]
        send_sems, recv_sems = refs[3 * n:]
        x, y, c = _place()
        me = 2 * x + y
        sibling = (x, y, 1 - c)
        chips = _other_chips(x, y)

        def copy(a, k, src, dst, to):
            return pltpu.make_async_remote_copy(src_ref=src, dst_ref=dst, send_sem=send_sems.at[a, k], recv_sem=recv_sems.at[a, k],
                                                device_id=to, device_id_type=MESH)

        sends = []
        for a in range(n):
            for k, (_, (px, py)) in enumerate(chips):
                cp = copy(a, k, ins[a].at[c], outs[a].at[me, c], (px, py, c))
                cp.start()
                sends.append(cp)
        for a in range(n):
            for k, (chip, _) in enumerate(chips):
                landed = outs[a].at[chip, c]
                copy(a, k, landed, landed, sibling).wait_recv()
                cp = copy(a, 3 + k, landed, landed, sibling)
                cp.start()
                sends.append(cp)
        for a in range(n):
            for k, (chip, _) in enumerate(chips):
                other = outs[a].at[chip, 1 - c]
                copy(a, 3 + k, other, other, sibling).wait_recv()
        for cp in sends:
            cp.wait_send()

    return pl.pallas_call(
        body, name="all_gather_shards",
        in_specs=[ANY] * (2 * n),
        out_specs=[ANY] * n,
        out_shape=[jax.ShapeDtypeStruct(p.shape, p.dtype) for p in placed],
        scratch_shapes=[pltpu.SemaphoreType.DMA((n, 6)), pltpu.SemaphoreType.DMA((n, 6))],
        input_output_aliases={n + a: a for a in range(n)},
        compiler_params=pltpu.CompilerParams(has_side_effects=True),
    )(*shards, *placed)


def all_gather_shards_async(shards, placed, collective_id, name):
    n = len(shards)
    srcs = [jax.new_ref(s, memory_space=pltpu.MemorySpace.HBM) for s in shards]
    dsts = [jax.new_ref(p, memory_space=pltpu.MemorySpace.HBM) for p in placed]

    @pl.kernel(mesh=plsc.ScalarSubcoreMesh(axis_name="sequencer", num_cores=1), name=name,
               scratch_types=(pltpu.SemaphoreType.DMA((n, 6)), pltpu.SemaphoreType.DMA((n, 6))),
               compiler_params=pltpu.CompilerParams(collective_id=collective_id))
    def launch(send_sems, recv_sems):
        x, y, c = _place()
        me = 2 * x + y
        sibling = (x, y, 1 - c)
        chips = _other_chips(x, y)
        barrier = pltpu.get_barrier_semaphore()
        peers = [sibling] + [(px, py, c) for _, (px, py) in chips]
        for peer in peers:
            pl.semaphore_signal(barrier, inc=1, device_id=peer, device_id_type=MESH)
        pl.semaphore_wait(barrier, len(peers))

        def copy(a, k, src, dst, to):
            return pltpu.make_async_remote_copy(src_ref=src, dst_ref=dst, send_sem=send_sems.at[a, k], recv_sem=recv_sems.at[a, k],
                                                device_id=to, device_id_type=MESH)

        sends = []
        for a in range(n):
            for k, (_, (px, py)) in enumerate(chips):
                cp = copy(a, k, srcs[a].at[c], dsts[a].at[me, c], (px, py, c))
                cp.start()
                sends.append(cp)
        for a in range(n):
            for k, (chip, _) in enumerate(chips):
                landed = dsts[a].at[chip, c]
                copy(a, k, landed, landed, sibling).wait_recv()
                cp = copy(a, 3 + k, landed, landed, sibling)
                cp.start()
                sends.append(cp)
        for a in range(n):
            for k, (chip, _) in enumerate(chips):
                other = dsts[a].at[chip, 1 - c]
                copy(a, 3 + k, other, other, sibling).wait_recv()
        for cp in sends:
            cp.wait_send()

    launch()
    return [d[...] for d in dsts]


def _exchange(body, ins, out_shapes, sem_shapes, name, peers=None, collective_id=None):
    n_in, n_out = len(ins), len(out_shapes)
    sems = [pltpu.SemaphoreType.DMA(shape) for shape in sem_shapes]
    if collective_id is None:
        def tc_body(*refs):
            body(refs[:n_in], refs[n_in:n_in + n_out], *refs[n_in + n_out:])

        return pl.pallas_call(tc_body, name=name, in_specs=[ANY] * n_in, out_specs=[ANY] * n_out, out_shape=out_shapes,
                              scratch_shapes=sems, compiler_params=pltpu.CompilerParams(has_side_effects=True))(*ins)
    srcs = [jax.new_ref(a, memory_space=pltpu.MemorySpace.HBM) for a in ins]
    dsts = [jax.empty_ref(shape, memory_space=pltpu.MemorySpace.HBM) for shape in out_shapes]

    @pl.kernel(mesh=plsc.ScalarSubcoreMesh(axis_name="sequencer", num_cores=1), name=name, scratch_types=tuple(sems),
               compiler_params=pltpu.CompilerParams(collective_id=collective_id))
    def launch(*sem_refs):
        barrier = pltpu.get_barrier_semaphore()
        others = peers(*_place())
        for peer in others:
            pl.semaphore_signal(barrier, inc=1, device_id=peer, device_id_type=MESH)
        pl.semaphore_wait(barrier, len(others))
        body(srcs, dsts, *sem_refs)

    launch()
    return [d[...] for d in dsts]


def _sibling(x, y, c):
    return [(x, y, 1 - c)]


def _same_core_of_other_chips(x, y, c):
    return [(px, py, c) for _, (px, py) in _other_chips(x, y)]


def sibling_exchange_halves(grads, name="sibling_exchange_halves", collective_id=None):
    n = len(grads)

    def body(ins, outs, send_sems, recv_sems):
        x, y, c = _place()
        copies = [pltpu.make_async_remote_copy(src_ref=ins[a].at[:, 1 - c], dst_ref=outs[a], send_sem=send_sems.at[a],
                                               recv_sem=recv_sems.at[a], device_id=(x, y, 1 - c), device_id_type=MESH) for a in range(n)]
        for cp in copies:
            cp.start()
        for cp in copies:
            cp.wait()

    shapes = [jax.ShapeDtypeStruct((N_CHIPS,) + g.shape[2:], g.dtype) for g in grads]
    return _exchange(body, grads, shapes, [(n,), (n,)], name, _sibling, collective_id)


def chip_exchange(sums, name="chip_exchange", collective_id=None):
    n = len(sums)

    def body(ins, outs, send_sems, recv_sems):
        x, y, c = _place()
        copies = []
        for a in range(n):
            for k, (chip, (px, py)) in enumerate(_other_chips(x, y)):
                cp = pltpu.make_async_remote_copy(src_ref=ins[a].at[chip], dst_ref=outs[a].at[k], send_sem=send_sems.at[a, k],
                                                  recv_sem=recv_sems.at[a, k], device_id=(px, py, c), device_id_type=MESH)
                cp.start()
                copies.append(cp)
        for cp in copies:
            cp.wait()

    shapes = [jax.ShapeDtypeStruct((N_CHIPS - 1,) + s.shape[1:], s.dtype) for s in sums]
    return _exchange(body, sums, shapes, [(n, 3), (n, 3)], name, _same_core_of_other_chips, collective_id)


def sibling_share(halves, name="sibling_share", collective_id=None):
    n = len(halves)

    def body(ins, outs, send_sems, recv_sems):
        x, y, c = _place()
        copies = [pltpu.make_async_remote_copy(src_ref=ins[a], dst_ref=outs[a], send_sem=send_sems.at[a], recv_sem=recv_sems.at[a],
                                               device_id=(x, y, 1 - c), device_id_type=MESH) for a in range(n)]
        for cp in copies:
            cp.start()
        for cp in copies:
            cp.wait()

    shapes = [jax.ShapeDtypeStruct(h.shape, h.dtype) for h in halves]
    return _exchange(body, halves, shapes, [(n,), (n,)], name, _sibling, collective_id)


def all_reduce_small(v):
    R, C = v.shape
    n_dev = 8

    def body(v_ref, o_ref, buf, send_sems, recv_sems):
        x, y, c = _place()
        me = 4 * x + 2 * y + c
        buf[me] = v_ref[...]
        copies = []
        for rel in range(1, n_dev):
            fx, fy, fc = rel >> 2, (rel >> 1) & 1, rel & 1
            peer = (x ^ fx, y ^ fy, c ^ fc)
            cp = pltpu.make_async_remote_copy(src_ref=v_ref, dst_ref=buf.at[me], send_sem=send_sems.at[rel - 1],
                                              recv_sem=recv_sems.at[rel - 1], device_id=peer, device_id_type=MESH)
            cp.start()
            copies.append(cp)
        for cp in copies:
            cp.wait()
        acc = buf[0]
        for k in range(1, n_dev):
            acc = acc + buf[k]
        o_ref[...] = acc

    return pl.pallas_call(
        body, name="all_reduce_small",
        in_specs=[pl.BlockSpec(memory_space=pltpu.VMEM)],
        out_specs=pl.BlockSpec(memory_space=pltpu.VMEM),
        out_shape=jax.ShapeDtypeStruct((R, C), F32),
        scratch_shapes=[pltpu.VMEM((n_dev, R, C), F32), pltpu.SemaphoreType.DMA((n_dev - 1,)), pltpu.SemaphoreType.DMA((n_dev - 1,))],
        compiler_params=pltpu.CompilerParams(has_side_effects=True),
    )(v)


WEIGHT_NAMES = ("ffn1_w_gate", "ffn1_w_up", "ffn1_w_down", "ffn2_w_gate", "ffn2_w_up", "ffn2_w_down", "ln_gain", "ln_bias",
                "pool_w_in", "pool_w_group", "pool_scale", "pool_w_out", "attn_w_qkv", "attn_w_out")
MATRIX_NAMES = ("ffn1_w_gate", "ffn1_w_up", "ffn1_w_down", "ffn2_w_gate", "ffn2_w_up", "ffn2_w_down",
                "pool_w_in", "pool_w_group", "pool_w_out", "attn_w_qkv", "attn_w_out")


TRANSPOSED_NAMES = ("ffn1_w_gate", "ffn1_w_up", "ffn2_w_gate", "ffn2_w_up")


def _halves(name, w):
    if name in TRANSPOSED_NAMES:
        w = jnp.swapaxes(w, 1, 2)
    return w.reshape(2, -1, w.shape[-1])


def _unhalves(name, t, shape):
    if name in TRANSPOSED_NAMES:
        return jnp.swapaxes(t.reshape(shape[0], shape[2], shape[1]), 1, 2)
    return t.reshape(shape)


def kernel(x, ffn1_w_gate, ffn1_w_up, ffn1_w_down, ffn2_w_gate, ffn2_w_up, ffn2_w_down, ln_gain, ln_bias, pool_w_in, pool_w_group, pool_scale, pool_w_out, attn_w_qkv, attn_w_out, loss_target, m_ffn1_w_gate, m_ffn1_w_up, m_ffn1_w_down, m_ffn2_w_gate, m_ffn2_w_up, m_ffn2_w_down, m_ln_gain, m_ln_bias, m_pool_w_in, m_pool_w_group, m_pool_scale, m_pool_w_out, m_attn_w_qkv, m_attn_w_out, v_ffn1_w_gate, v_ffn1_w_up, v_ffn1_w_down, v_ffn2_w_gate, v_ffn2_w_up, v_ffn2_w_down, v_ln_gain, v_ln_bias, v_pool_w_in, v_pool_w_group, v_pool_scale, v_pool_w_out, v_attn_w_qkv, v_attn_w_out):
    weights = dict(zip(WEIGHT_NAMES, (ffn1_w_gate, ffn1_w_up, ffn1_w_down, ffn2_w_gate, ffn2_w_up, ffn2_w_down, ln_gain, ln_bias,
                                      pool_w_in, pool_w_group, pool_scale, pool_w_out, attn_w_qkv, attn_w_out)))
    moms = dict(zip(WEIGHT_NAMES, (m_ffn1_w_gate, m_ffn1_w_up, m_ffn1_w_down, m_ffn2_w_gate, m_ffn2_w_up, m_ffn2_w_down, m_ln_gain,
                                   m_ln_bias, m_pool_w_in, m_pool_w_group, m_pool_scale, m_pool_w_out, m_attn_w_qkv, m_attn_w_out)))
    vels = dict(zip(WEIGHT_NAMES, (v_ffn1_w_gate, v_ffn1_w_up, v_ffn1_w_down, v_ffn2_w_gate, v_ffn2_w_up, v_ffn2_w_down, v_ln_gain,
                                   v_ln_bias, v_pool_w_in, v_pool_w_group, v_pool_scale, v_pool_w_out, v_attn_w_qkv, v_attn_w_out)))
    S, D = x.shape[1], x.shape[2]
    FB = ffn1_w_gate.shape[2]
    QKV = attn_w_qkv.shape[2] * N_CHIPS
    G, CB = pool_w_group.shape[1], pool_w_group.shape[2]
    C = pool_w_group.shape[3]
    cx, cy, cc = _place()
    chip = 2 * cx + cy
    xs = x.reshape(S, D)
    target = loss_target.reshape(S, D)

    ln_rows = jnp.concatenate([ln_gain, ln_bias, jnp.zeros((DEPTH, 2, ln_gain.shape[2]), F32)], axis=1)
    shard = {n: _halves(n, weights[n]).astype(BF16) for n in MATRIX_NAMES}
    ffn_layer = lambda f, i: {f"{f}{s}@{i}": shard[f + s][i].reshape(2, FB // 2, D) for s in ("_w_gate", "_w_up", "_w_down")}
    pool = {n: shard[n] for n in ("pool_w_in", "pool_w_group", "pool_w_out")}
    attn = {n: shard[n] for n in ("attn_w_qkv", "attn_w_out")}
    groups = [dict(ffn_layer("ffn1", 0), ln=ln_rows), pool, ffn_layer("ffn2", 0), ffn_layer("ffn1", 1), attn, ffn_layer("ffn2", 1)]
    full = {}

    def launch(k, after=None):
        parts = list(groups[k].values())
        if after is not None:
            *parts, after = lax.optimization_barrier((*parts, after))
        placed = [lax.dynamic_update_slice(lax.empty((N_CHIPS,) + s.shape, s.dtype), s[None], (chip, 0, 0, 0)) for s in parts]
        full.update(zip(groups[k], all_gather_shards_async(parts, placed, GATHER_COLLECTIVE_ID, f"gather_weights_{k}")))
        return after

    for k in range(4):
        launch(k)
    ffn_w = lambda f, i: [full[f"{f}{s}@{i}"].reshape(N_CHIPS, FB, D) for s in ("_w_gate", "_w_up", "_w_down")]
    w_pool_in = full["pool_w_in"].reshape(D, D)
    w_pool_out = full["pool_w_out"].reshape(D, D)
    w_group = full["pool_w_group"].reshape(N_CHIPS, G, CB, C).transpose(1, 0, 2, 3).reshape(G, N_CHIPS * CB, C)
    ln_full = full["ln"].transpose(1, 2, 0, 3).reshape(DEPTH, 8, D)
    gain = lambda i, k: ln_full[i, k].reshape(1, D)
    bias = lambda i, k: ln_full[i, 3 + k].reshape(1, D)

    dils = [dil for _, dil in DIL_CONFIGS]
    moved_dils = [dil for dil in dils if dil > 1]

    saved = []
    y = xs
    for i in range(DEPTH):
        y_in = y
        y, xh, rs, a, u = ffn_fwd(y_in, *ffn_w("ffn1", i), gain(i, 0), bias(i, 0))
        f1 = (y_in, xh, rs, a, u)
        if i == 0:
            y = launch(4, after=y)
        y_mid = y
        if i % 2 == 0:
            pu = mm_nn(y_mid, w_pool_in, F32)[0]
            mixed, pv = pool_mix(pu, w_group, pool_scale)
            y, xh, rs = proj_ln(pv, w_pool_out, y_mid, gain(i, 1), bias(i, 1))
            mix = (y_mid, xh, rs, mixed, pv)
            if i == 0:
                y = launch(5, after=y)
        else:
            qkv_blocks, y_mid = lax.optimization_barrier((full["attn_w_qkv"], y_mid))
            w_qkv = chip_blocks_to_columns(qkv_blocks.reshape(N_CHIPS, D, QKV // N_CHIPS))
            w_attn_out = full["attn_w_out"].reshape(D, D)
            moved = dict(zip(moved_dils, dilate_rows(y_mid, moved_dils, BF16)))
            srcs = [moved.get(dil, y_mid) for dil in dils]
            qkvs = [mm_nn(src, w_qkv, BF16, first_block=3 * g, nb=3) for g, src in enumerate(srcs)]
            parts = [attn_fwd(qkv, g, dil) for g, (qkv, dil) in enumerate(zip(qkvs, dils))]
            ao, lse = attn_combine([p[0] for p in parts], [p[1] for p in parts], dils)
            y, xh, rs = proj_ln(ao, w_attn_out, y_mid, gain(i, 1), bias(i, 1))
            mix = (y_mid, xh, rs, srcs, qkvs, ao, lse, w_qkv, w_attn_out)
        y_in2 = y
        y, xh, rs, a, u = ffn_fwd(y_in2, *ffn_w("ffn2", i), gain(i, 2), bias(i, 2))
        f2 = (y_in2, xh, rs, a, u)
        saved.append((f1, mix, f2))

    dy, loss_part = loss_head(y, target)
    loss = lax.psum(loss_part[0, 0], ("x", "y", "c"))

    core = cc.reshape(1).astype(jnp.int32)
    chip_id = chip.reshape(1).astype(jnp.int32)
    dgain = [[None] * 3 for _ in range(DEPTH)]
    dbias = [[None] * 3 for _ in range(DEPTH)]
    dscale = None
    pieces = []
    own_half, other_half = {}, {}

    def tie(arrays, after):
        *arrays, after = lax.optimization_barrier((*arrays, after))
        return arrays, after

    def start_piece(keys, arrays):
        blocks = [g.reshape(N_CHIPS, 2, -1, g.shape[-1]) for g in arrays]
        k = len(pieces)
        pieces.append(dict(keys=keys, blocks=blocks, from_sibling=sibling_exchange_halves(
            blocks, name=f"reduce_halves_{k}", collective_id=SIBLING_COLLECTIVE_ID)))

    def pair_sums_and_chip_exchange(k, after):
        piece = pieces[k]
        received, after = tie(piece["from_sibling"], after)
        piece["pair_sums"] = [pair_sum(core, b, r) for b, r in zip(piece["blocks"], received)]
        piece["from_chips"] = chip_exchange(piece["pair_sums"], name=f"reduce_chips_{k}", collective_id=CHIPS_COLLECTIVE_ID)
        return after

    def chip_sums_and_share(k, after):
        piece = pieces[k]
        received, after = tie(piece["from_chips"], after)
        mine = [chip_sum(chip_id, p, r) for p, r in zip(piece["pair_sums"], received)]
        theirs = sibling_share(mine, name=f"reduce_share_{k}", collective_id=SIBLING_COLLECTIVE_ID)
        own_half.update(zip(piece["keys"], mine))
        other_half.update(zip(piece["keys"], theirs))
        return after

    def piece_done(keys, arrays, dy):
        start_piece(keys, arrays)
        k = len(pieces) - 1
        if k >= 1:
            dy = pair_sums_and_chip_exchange(k - 1, dy)
        if k >= 2:
            dy = chip_sums_and_share(k - 2, dy)
        return dy

    def ffn_backward(name, i, dy, state):
        y_in, xh, rs, a, u = state
        k = 0 if name == "ffn1" else 2
        dz, dgain[i][k], dbias[i][k] = ln_bwd(dy, xh, rs, gain(i, k))
        dx, h, da, du = ffn_bwd(dz, a, u, *ffn_w(name, i))
        outs = mm_tn([(da, y_in, 1.0), (du, y_in, 1.0), (h, dz, MACARON_WEIGHT)], nblk=N_CHIPS, out_shape=(N_CHIPS, FB, D),
                     out_block=(None, FB, D), out_index=lambda j: (j, 0, 0), name="ffn_wgrad")
        return piece_done([(name + s, i) for s in ("_w_gate", "_w_up", "_w_down")], outs, dx)

    def square_grad(a, b):
        return mm_tn([(a, b, 1.0)], nblk=1, out_shape=(D, D), out_block=(D, D), out_index=lambda j: (0, 0), name="square_wgrad")[0]

    for i in reversed(range(DEPTH)):
        f1, mix, f2 = saved[i]
        dy = ffn_backward("ffn2", i, dy, f2)
        dz, dgain[i][1], dbias[i][1] = ln_bwd(dy, mix[1], mix[2], gain(i, 1))
        if i % 2 == 0:
            y_mid, _, _, mixed, pv = mix
            g_out = square_grad(pv, dz)
            dv = mm_nt(dz, w_pool_out, None, a_blocked=False)
            du, dwg, dscale = pool_mix_bwd(dv, mixed, w_group, pool_scale)
            g_group = dwg.reshape(G, N_CHIPS, CB, C).transpose(1, 0, 2, 3)
            g_in = square_grad(y_mid, du)
            dy = mm_nt(du, w_pool_in, dz, a_blocked=False)
            dy = piece_done([("pool_w_out", 0), ("pool_w_group", 0), ("pool_w_in", 0)], [g_out, g_group, g_in], dy)
        else:
            y_mid, _, _, srcs, qkvs, ao, lse, w_qkv, w_attn_out = mix
            g_out = square_grad(ao, dz)
            dao = mm_nt(dz, w_attn_out, None, a_blocked=False)
            in_order = [dict(zip(moved_dils, dilate_rows(t, moved_dils, F32))) for t in (dao, ao, lse)]
            dqkvs = [attn_bwd(qkvs[g], *[m.get(dil, t) for m, t in zip(in_order, (dao, ao, lse))], g, dil)
                     for g, dil in enumerate(dils)]
            g_qkv = [mm_tn([(src, dqkv, 1.0)], nblk=3, out_shape=(D, 3 * D), out_block=(D, D), out_index=lambda j: (0, j),
                           name="qkv_wgrad")[0] for src, dqkv in zip(srcs, dqkvs)]
            g_qkv = jnp.concatenate(g_qkv, axis=1).reshape(D, N_CHIPS, QKV // N_CHIPS).transpose(1, 0, 2)
            dy = mm_nt_dilated(dqkvs, dils, w_qkv, dz)
            dy = piece_done([("attn_w_out", 0), ("attn_w_qkv", 0)], [g_out, g_qkv], dy)
        dy = ffn_backward("ffn1", i, dy, f1)
    grad_x = dy.reshape(x.shape)

    last = len(pieces) - 1
    small = jnp.concatenate([jnp.concatenate(dgain[i] + dbias[i], axis=0) for i in range(DEPTH)] + [dscale, jnp.zeros((3, D), F32)], axis=0)
    small = all_reduce_small(small)
    per_layer = small[:6 * DEPTH].reshape(DEPTH, 6, D)
    cols = D // N_CHIPS
    small_grads = {"ln_gain": lax.dynamic_slice_in_dim(per_layer[:, 0:3], chip * cols, cols, axis=2),
                   "ln_bias": lax.dynamic_slice_in_dim(per_layer[:, 3:6], chip * cols, cols, axis=2),
                   "pool_scale": small[6 * DEPTH:6 * DEPTH + 1]}

    grad_w, delta, new_m, new_v = {}, {}, {}, {}

    def update(n):
        shape = weights[n].shape
        if n in MATRIX_NAMES:
            layers = DEPTH if (n, 1) in own_half else 1
            as4 = lambda t: _halves(n, t).reshape(layers, 2, -1, shape[-1] if n not in TRANSPOSED_NAMES else shape[1])
            outs = adamw(core, as4(weights[n]), [(own_half[n, l], other_half[n, l]) for l in range(layers)], as4(moms[n]), as4(vels[n]))
            grad_w[n], delta[n], new_m[n], new_v[n] = [_unhalves(n, t, shape) for t in outs]
        else:
            as4 = lambda t: t.reshape(1, 1, -1, shape[-1])
            g2 = small_grads[n].reshape(-1, shape[-1])
            outs = adamw(core, as4(weights[n]), [(g2, g2)], as4(moms[n]), as4(vels[n]))
            grad_w[n], delta[n], new_m[n], new_v[n] = [t.reshape(shape) for t in outs]
        return outs[1]

    for n in ("ln_gain", "ln_bias", "pool_scale"):
        marker = update(n)
    marker = pair_sums_and_chip_exchange(last, marker)
    marker = chip_sums_and_share(last - 1, marker)
    for n in MATRIX_NAMES:
        if not n.startswith("ffn1"):
            marker = update(n)
    chip_sums_and_share(last, marker)
    for n in MATRIX_NAMES:
        if n.startswith("ffn1"):
            update(n)

    return (loss, grad_x, *[grad_w[n] for n in WEIGHT_NAMES], *[delta[n] for n in WEIGHT_NAMES],
            *[new_m[n] for n in WEIGHT_NAMES], *[new_v[n] for n in WEIGHT_NAMES])
```

```python
import functools
import math

import numpy as np
import jax
import jax.numpy as jnp
from jax import lax
from jax.experimental import pallas as pl
from jax.experimental.pallas import tpu as pltpu
from jax.experimental.pallas import tpu_sc as plsc

F32 = jnp.float32
BF16 = jnp.bfloat16

DEPTH = 2
ALPHA = (2.0 * DEPTH) ** 0.25
MACARON_WEIGHT = 0.5
LN_EPS = 1e-5
MASK_VALUE = -1e30
POOL_WINDOWS = (2, 4, 8, 16)
POOL_PAD = 16
HEAD_DIM = 64
N_HEADS = 16
DIL_CONFIGS = ((128, 1), (512, 4), (2048, 16))
ATTN_R = 64
ATTN_BQ = 128
ATTN_W = ATTN_BQ + 2 * ATTN_R
FFN_HIDDEN_TILE = 256
ATTN_UNROLL = 4
LANES = 128
ADAM_LR = 0.001
ADAM_B1 = 0.9
ADAM_B2 = 0.999
ADAM_EPS = 1e-08
ADAM_WD = 0.01
ADAM_STEP = 10
N_CHIPS = 4
GATHER_COLLECTIVE_ID = 1
SIBLING_COLLECTIVE_ID = 2
CHIPS_COLLECTIVE_ID = 3
VMEM_LIMIT = 56 * 1024 * 1024
MESH = pl.DeviceIdType.MESH
ANY = pl.BlockSpec(memory_space=pl.ANY)


def _params(sem=None, vmem=VMEM_LIMIT):
    return pltpu.CompilerParams(dimension_semantics=sem, vmem_limit_bytes=vmem)


def _alibi_slopes():
    n = len(DIL_CONFIGS) * N_HEADS
    s = 2.0 ** (-8.0 * np.arange(1, n + 1) / n)
    return s.reshape(len(DIL_CONFIGS), N_HEADS).astype(np.float32)


def _ln_fwd(z, g, b):
    mu = jnp.mean(z, axis=-1, keepdims=True)
    zc = z - mu
    var = jnp.mean(zc * zc, axis=-1, keepdims=True)
    rstd = lax.rsqrt(var + LN_EPS)
    xhat = zc * rstd
    return xhat * g + b, xhat, rstd


def _dot(a, b):
    return jnp.dot(a, b, preferred_element_type=F32)


def _dot_nt(a, b):
    return lax.dot_general(a, b, (((1,), (1,)), ((), ())), preferred_element_type=F32)


def _dot_tn(a, b):
    return lax.dot_general(a, b, (((0,), (0,)), ((), ())), preferred_element_type=F32)


def mm_nn(a, b, out_dtype, first_block=0, nb=None, tm=512):
    S, K = a.shape
    Nb = K
    nb = b.shape[1] // Nb if nb is None else nb

    def body(a_ref, b_ref, o_ref):
        o_ref[...] = _dot(a_ref[...].astype(BF16), b_ref[...]).astype(out_dtype)

    return pl.pallas_call(
        body, name="mm_nn",
        grid=(S // tm, nb),
        in_specs=[pl.BlockSpec((tm, K), lambda i, j: (i, 0)), pl.BlockSpec((K, Nb), lambda i, j: (0, first_block + j))],
        out_specs=pl.BlockSpec((None, tm, Nb), lambda i, j: (j, i, 0)),
        out_shape=jax.ShapeDtypeStruct((nb, S, Nb), out_dtype),
        compiler_params=_params(("parallel", "arbitrary")),
    )(a, b)


def chip_blocks_to_columns(w):
    nb, K, Nb = w.shape

    def body(w_ref, o_ref):
        o_ref[...] = w_ref[...]

    return pl.pallas_call(
        body, name="chip_blocks_to_columns",
        grid=(nb,),
        in_specs=[pl.BlockSpec((None, K, Nb), lambda b: (b, 0, 0))],
        out_specs=pl.BlockSpec((K, Nb), lambda b: (0, b)),
        out_shape=jax.ShapeDtypeStruct((K, nb * Nb), w.dtype),
        compiler_params=_params(("parallel",)),
    )(w)


def proj_ln(a, w, resid, gain, bias, tm=512):
    S, K = a.shape
    D = w.shape[1]

    def body(a_ref, w_ref, r_ref, g_ref, b_ref, y_ref, xh_ref, rs_ref):
        z = ALPHA * r_ref[...] + _dot(a_ref[...].astype(BF16), w_ref[...])
        y, xh, rs = _ln_fwd(z, g_ref[...], b_ref[...])
        y_ref[...] = y
        xh_ref[...] = xh
        rs_ref[...] = rs

    row = pl.BlockSpec((tm, D), lambda i: (i, 0))
    vec = pl.BlockSpec((1, D), lambda i: (0, 0))
    return pl.pallas_call(
        body, name="proj_ln",
        grid=(S // tm,),
        in_specs=[pl.BlockSpec((tm, K), lambda i: (i, 0)), pl.BlockSpec((K, D), lambda i: (0, 0)), row, vec, vec],
        out_specs=[row, row, pl.BlockSpec((tm, 1), lambda i: (i, 0))],
        out_shape=[jax.ShapeDtypeStruct((S, D), F32), jax.ShapeDtypeStruct((S, D), F32), jax.ShapeDtypeStruct((S, 1), F32)],
        compiler_params=_params(("parallel",)),
    )(a, w, resid, gain, bias)


def mm_nt(a, w, resid, a_blocked, out_dtype=F32, tm=512):
    if a_blocked:
        nk, S, Kb = a.shape
        a_spec = pl.BlockSpec((None, tm, Kb), lambda i, n: (n, i, 0))
    else:
        S, Kb = a.shape
        nk = 1
        a_spec = pl.BlockSpec((tm, Kb), lambda i, n: (i, 0))
    M = w.shape[0]
    has_resid = resid is not None

    def body(*refs):
        if has_resid:
            a_ref, w_ref, r_ref, o_ref, acc = refs
        else:
            a_ref, w_ref, o_ref, acc = refs
        n = pl.program_id(1)
        part = _dot_nt(a_ref[...].astype(BF16), w_ref[...])

        @pl.when(n == 0)
        def _():
            acc[...] = part

        @pl.when(n > 0)
        def _():
            acc[...] += part

        @pl.when(n == nk - 1)
        def _():
            out = acc[...]
            if has_resid:
                out = out + ALPHA * r_ref[...]
            o_ref[...] = out.astype(out_dtype)

    row = pl.BlockSpec((tm, M), lambda i, n: (i, 0))
    in_specs = [a_spec, pl.BlockSpec((M, Kb), lambda i, n: (0, n))] + ([row] if has_resid else [])
    args = (a, w) + ((resid,) if has_resid else ())
    return pl.pallas_call(
        body, name="mm_nt",
        grid=(S // tm, nk),
        in_specs=in_specs,
        out_specs=row,
        out_shape=jax.ShapeDtypeStruct((S, M), out_dtype),
        scratch_shapes=[pltpu.VMEM((tm, M), F32)],
        compiler_params=_params(("parallel", "arbitrary")),
    )(*args)


def mm_nt_dilated(parts, dils, w, resid, tm=512):
    n_groups = len(parts)
    _, S, K = parts[0].shape
    M = w.shape[0]
    nk = 3 * n_groups

    def body(*refs):
        a_refs = refs[:n_groups]
        w_ref, r_ref, o_ref, group_acc, total = refs[n_groups:]
        n = pl.program_id(1)
        for g in range(n_groups):
            for k in range(3):
                @pl.when(n == 3 * g + k)
                def _():
                    part = _dot_nt(a_refs[g][...], w_ref[...])
                    for c in range(M // LANES):
                        lanes = slice(c * LANES, (c + 1) * LANES)
                        if k == 0:
                            group_acc[c] = part[:, lanes]
                        else:
                            group_acc[c] += part[:, lanes]
                        if k == 2:
                            _rows_from_dilated(group_acc.at[c], total.at[c], tm, dils[g], accumulate=g > 0)

        @pl.when(n == nk - 1)
        def _():
            for c in range(M // LANES):
                lanes = slice(c * LANES, (c + 1) * LANES)
                o_ref[:, lanes] = total[c] + ALPHA * r_ref[:, lanes]

    def a_spec(g):
        return pl.BlockSpec((None, tm, K), lambda i, n: (jnp.clip(n - 3 * g, 0, 2), i, 0))

    row = pl.BlockSpec((tm, M), lambda i, n: (i, 0))
    return pl.pallas_call(
        body, name="mm_nt_dilated",
        grid=(S // tm, nk),
        in_specs=[a_spec(g) for g in range(n_groups)] + [pl.BlockSpec((M, K), lambda i, n: (0, n)), row],
        out_specs=row,
        out_shape=jax.ShapeDtypeStruct((S, M), F32),
        scratch_shapes=[pltpu.VMEM((M // LANES, tm, LANES), F32), pltpu.VMEM((M // LANES, tm, LANES), F32)],
        compiler_params=_params(("parallel", "arbitrary")),
    )(*parts, w, resid)


def mm_tn(pairs, *, nblk, out_shape, out_block, out_index, alias=None, tk=512, name="mm_tn"):
    operands = []
    for a, b, _ in pairs:
        for t in (a, b):
            if not any(t is o for o in operands):
                operands.append(t)
    where = lambda t: next(i for i, o in enumerate(operands) if o is t)
    S = pairs[0][0].shape[-2]
    n_out, n_in = len(pairs), len(operands)
    n_alias = len(alias) if alias is not None else 0

    def spec(t):
        if t.ndim == 3:
            return pl.BlockSpec((None, tk, t.shape[-1]), lambda j, k: (j, k, 0))
        return pl.BlockSpec((tk, t.shape[-1]), lambda j, k: (k, 0))

    def body(*refs):
        refs = refs[n_alias:]
        in_refs, o_refs, accs = refs[:n_in], refs[n_in:n_in + n_out], refs[n_in + n_out:]
        k = pl.program_id(1)
        for (a, b, scale), o_ref, acc in zip(pairs, o_refs, accs):
            part = _dot_tn(in_refs[where(a)][...].astype(BF16), in_refs[where(b)][...].astype(BF16))

            @pl.when(k == 0)
            def _():
                acc[...] = part

            @pl.when(k > 0)
            def _():
                acc[...] += part

            @pl.when(k == S // tk - 1)
            def _():
                o_ref[...] = (scale * acc[...]).astype(BF16)

    out_spec = pl.BlockSpec(out_block, lambda j, k: out_index(j))
    outs = pl.pallas_call(
        body, name=name,
        grid=(nblk, S // tk),
        in_specs=[ANY] * n_alias + [spec(t) for t in operands],
        out_specs=[out_spec] * n_out,
        out_shape=[jax.ShapeDtypeStruct(out_shape, BF16)] * n_out,
        scratch_shapes=[pltpu.VMEM((a.shape[-1], b.shape[-1]), F32) for a, b, _ in pairs],
        input_output_aliases={i: i for i in range(n_alias)},
        compiler_params=_params(("parallel", "arbitrary")),
    )(*(tuple(alias) if alias is not None else ()), *operands)
    return list(outs)


def ffn_fwd(x, wg, wu, wd, gain, bias, tm=512):
    S, D = x.shape
    nb, FB = wg.shape[0], wg.shape[1]

    def body(x_ref, wg_ref, wu_ref, wd_ref, g_ref, b_ref, y_ref, xh_ref, rs_ref, a_ref, u_ref, acc, xb_ref):
        j = pl.program_id(1)

        @pl.when(j == 0)
        def _():
            xb_ref[...] = x_ref[...].astype(BF16)
            acc[...] = jnp.zeros_like(acc)

        xb = xb_ref[...]
        total = None
        for t0 in range(0, FB, FFN_HIDDEN_TILE):
            cols = pl.ds(t0, min(FFN_HIDDEN_TILE, FB - t0))
            a = _dot_nt(xb, wg_ref[cols, :])
            u = _dot_nt(xb, wu_ref[cols, :])
            a_ref[:, cols] = a.astype(BF16)
            u_ref[:, cols] = u.astype(BF16)
            h = a * jax.nn.sigmoid(a) * u
            part = _dot(h.astype(BF16), wd_ref[cols, :])
            total = part if total is None else total + part
        acc[...] += total

        @pl.when(j == nb - 1)
        def _():
            z = ALPHA * x_ref[...] + MACARON_WEIGHT * acc[...]
            y, xh, rs = _ln_fwd(z, g_ref[...], b_ref[...])
            y_ref[...] = y
            xh_ref[...] = xh
            rs_ref[...] = rs

    row = pl.BlockSpec((tm, D), lambda i, j: (i, 0))
    vec = pl.BlockSpec((1, D), lambda i, j: (0, 0))
    w_out = pl.BlockSpec((None, FB, D), lambda i, j: (j, 0, 0))
    act = pl.BlockSpec((None, tm, FB), lambda i, j: (j, i, 0))
    return pl.pallas_call(
        body, name="ffn_fwd",
        grid=(S // tm, nb),
        in_specs=[row, w_out, w_out, w_out, vec, vec],
        out_specs=[row, row, pl.BlockSpec((tm, 1), lambda i, j: (i, 0)), act, act],
        out_shape=[jax.ShapeDtypeStruct((S, D), F32), jax.ShapeDtypeStruct((S, D), F32), jax.ShapeDtypeStruct((S, 1), F32),
                   jax.ShapeDtypeStruct((nb, S, FB), BF16), jax.ShapeDtypeStruct((nb, S, FB), BF16)],
        scratch_shapes=[pltpu.VMEM((tm, D), F32), pltpu.VMEM((tm, D), BF16)],
        compiler_params=_params(("parallel", "arbitrary")),
    )(x, wg, wu, wd, gain, bias)


def ffn_bwd(dz, a, u, wg, wu, wd, tm=512):
    S, D = dz.shape
    nb, FB = wg.shape[0], wg.shape[1]

    def body(dz_ref, a_ref, u_ref, wg_ref, wu_ref, wd_ref, dx_ref, h_ref, da_ref, du_ref, acc, dzb_ref):
        j = pl.program_id(1)

        @pl.when(j == 0)
        def _():
            dzb_ref[...] = (MACARON_WEIGHT * dz_ref[...]).astype(BF16)
            acc[...] = jnp.zeros_like(acc)

        dzb = dzb_ref[...]
        total = None
        for t0 in range(0, FB, FFN_HIDDEN_TILE):
            cols = pl.ds(t0, min(FFN_HIDDEN_TILE, FB - t0))
            dh = _dot_nt(dzb, wd_ref[cols, :])
            av = a_ref[:, cols].astype(F32)
            uv = u_ref[:, cols].astype(F32)
            s = jax.nn.sigmoid(av)
            silu = av * s
            h_ref[:, cols] = (silu * uv).astype(BF16)
            da = (dh * uv * (s * (1.0 + av * (1.0 - s)))).astype(BF16)
            du = (dh * silu).astype(BF16)
            da_ref[:, cols] = da
            du_ref[:, cols] = du
            part = _dot(da, wg_ref[cols, :]) + _dot(du, wu_ref[cols, :])
            total = part if total is None else total + part
        acc[...] += total

        @pl.when(j == nb - 1)
        def _():
            dx_ref[...] = ALPHA * dz_ref[...] + acc[...]

    row = pl.BlockSpec((tm, D), lambda i, j: (i, 0))
    w_out = pl.BlockSpec((None, FB, D), lambda i, j: (j, 0, 0))
    act = pl.BlockSpec((None, tm, FB), lambda i, j: (j, i, 0))
    act_shape = jax.ShapeDtypeStruct((nb, S, FB), BF16)
    return pl.pallas_call(
        body, name="ffn_bwd",
        grid=(S // tm, nb),
        in_specs=[row, act, act, w_out, w_out, w_out],
        out_specs=[row, act, act, act],
        out_shape=[jax.ShapeDtypeStruct((S, D), F32), act_shape, act_shape, act_shape],
        scratch_shapes=[pltpu.VMEM((tm, D), F32), pltpu.VMEM((tm, D), BF16)],
        compiler_params=_params(("parallel", "arbitrary")),
    )(dz, a, u, wg, wu, wd)


def ln_bwd(dy, xhat, rstd, gain, tm=512):
    S, D = dy.shape

    def body(dy_ref, xh_ref, rs_ref, g_ref, dz_ref, dg_ref, db_ref):
        i = pl.program_id(0)
        dy = dy_ref[...]
        xh = xh_ref[...]
        dxh = dy * g_ref[...]
        m1 = jnp.mean(dxh, axis=-1, keepdims=True)
        m2 = jnp.mean(dxh * xh, axis=-1, keepdims=True)
        dz_ref[...] = rs_ref[...] * (dxh - m1 - xh * m2)
        dg = jnp.sum(dy * xh, axis=0, keepdims=True)
        db = jnp.sum(dy, axis=0, keepdims=True)

        @pl.when(i == 0)
        def _():
            dg_ref[...] = dg
            db_ref[...] = db

        @pl.when(i > 0)
        def _():
            dg_ref[...] += dg
            db_ref[...] += db

    row = pl.BlockSpec((tm, D), lambda i: (i, 0))
    vec = pl.BlockSpec((1, D), lambda i: (0, 0))
    return pl.pallas_call(
        body, name="ln_bwd",
        grid=(S // tm,),
        in_specs=[row, row, pl.BlockSpec((tm, 1), lambda i: (i, 0)), vec],
        out_specs=[row, vec, vec],
        out_shape=[jax.ShapeDtypeStruct((S, D), F32), jax.ShapeDtypeStruct((1, D), F32), jax.ShapeDtypeStruct((1, D), F32)],
        compiler_params=_params(("arbitrary",)),
    )(dy, xhat, rstd, gain)


def loss_head(y, target, tm=512):
    S, D = y.shape

    def body(y_ref, t_ref, dy_ref, l_ref):
        i = pl.program_id(0)
        e = y_ref[...] - t_ref[...]
        dy_ref[...] = e / D
        part = 0.5 * jnp.sum(jnp.mean(e * e, axis=-1, keepdims=True), axis=0, keepdims=True)

        @pl.when(i == 0)
        def _():
            l_ref[...] = part

        @pl.when(i > 0)
        def _():
            l_ref[...] += part

    row = pl.BlockSpec((tm, D), lambda i: (i, 0))
    return pl.pallas_call(
        body, name="loss_head",
        grid=(S // tm,),
        in_specs=[row, row],
        out_specs=[row, pl.BlockSpec((1, 1), lambda i: (0, 0))],
        out_shape=[jax.ShapeDtypeStruct((S, D), F32), jax.ShapeDtypeStruct((1, 1), F32)],
        compiler_params=_params(("arbitrary",)),
    )(y, target)


def _pool_window(xp, g):
    n = xp.shape[0]
    w = xp + pltpu.roll(xp, 1, 0)
    out = w
    for level, shift in enumerate((1, 2, 4), start=1):
        w = pltpu.roll(w, shift, 0) + pltpu.roll(w, n - shift, 0)
        out = jnp.where(g >= level, w, out)
    return out


def _pool_count(S, C, g):
    half = lax.shift_left(jnp.int32(1), g)
    t = lax.broadcasted_iota(jnp.int32, (S, C), 0)
    return (jnp.minimum(t + half, S) - jnp.maximum(t - half, 0)).astype(F32)


def pool_mix(u, wgrp, scale):
    S, D = u.shape
    G, C = wgrp.shape[0], wgrp.shape[1]

    def body(u_ref, w_ref, s_ref, mix_ref, v_ref, pad):
        g = pl.program_id(0)
        zeros = jnp.zeros((POOL_PAD, C), F32)
        pad[pl.ds(0, POOL_PAD), :] = zeros
        pad[pl.ds(POOL_PAD + S, POOL_PAD), :] = zeros
        pad[pl.ds(POOL_PAD, S), :] = u_ref[...]
        win = _pool_window(pad[...], g)[POOL_PAD:POOL_PAD + S]
        mixed = (win / _pool_count(S, C, g) - u_ref[...]).astype(BF16)
        mix_ref[...] = mixed
        v_ref[...] = _dot(mixed, w_ref[...]) * s_ref[...]

    col = pl.BlockSpec((S, C), lambda g: (0, g))
    return pl.pallas_call(
        body, name="pool_mix",
        grid=(G,),
        in_specs=[col, pl.BlockSpec((None, C, C), lambda g: (g, 0, 0)), pl.BlockSpec((1, C), lambda g: (0, g))],
        out_specs=[col, col],
        out_shape=[jax.ShapeDtypeStruct((S, D), BF16), jax.ShapeDtypeStruct((S, D), F32)],
        scratch_shapes=[pltpu.VMEM((S + 2 * POOL_PAD, C), F32)],
        compiler_params=_params(("arbitrary",)),
    )(u, wgrp, scale)


def pool_mix_bwd(dv, mixed, wgrp, scale):
    S, D = dv.shape
    G, C = wgrp.shape[0], wgrp.shape[1]

    def body(dv_ref, mix_ref, w_ref, s_ref, du_ref, dw_ref, ds_ref, pad):
        g = pl.program_id(0)
        mixed = mix_ref[...]
        dv = dv_ref[...]
        yg = _dot(mixed, w_ref[...])
        ds_ref[...] = jnp.sum(dv * yg, axis=0, keepdims=True)
        dyg = (dv * s_ref[...]).astype(BF16)
        dw_ref[...] = _dot_tn(mixed, dyg).astype(BF16)
        dmix = _dot_nt(dyg, w_ref[...])
        zeros = jnp.zeros((POOL_PAD, C), F32)
        pad[pl.ds(0, POOL_PAD), :] = zeros
        pad[pl.ds(POOL_PAD + S, POOL_PAD), :] = zeros
        pad[pl.ds(POOL_PAD, S), :] = dmix / _pool_count(S, C, g)
        win = _pool_window(pad[...], g)
        win = pltpu.roll(win, win.shape[0] - 1, 0)[POOL_PAD:POOL_PAD + S]
        du_ref[...] = win - dmix

    col = pl.BlockSpec((S, C), lambda g: (0, g))
    return pl.pallas_call(
        body, name="pool_mix_bwd",
        grid=(G,),
        in_specs=[col, col, pl.BlockSpec((None, C, C), lambda g: (g, 0, 0)), pl.BlockSpec((1, C), lambda g: (0, g))],
        out_specs=[col, pl.BlockSpec((None, C, C), lambda g: (g, 0, 0)), pl.BlockSpec((1, C), lambda g: (0, g))],
        out_shape=[jax.ShapeDtypeStruct((S, D), F32), jax.ShapeDtypeStruct((G, C, C), BF16), jax.ShapeDtypeStruct((1, D), F32)],
        scratch_shapes=[pltpu.VMEM((S + 2 * POOL_PAD, C), F32)],
        compiler_params=_params(("arbitrary",)),
    )(dv, mixed, wgrp, scale)


PERM_BLOCK = 256


def _dilated_runs(rows, d):
    n = PERM_BLOCK // d
    return [(c * PERM_BLOCK, r, n) for c in range(rows // PERM_BLOCK) for r in range(d)]


def _rows_to_dilated(src_ref, dst_ref, rows, d):
    for base, r, n in _dilated_runs(rows, d):
        dst_ref[pl.ds(base + r * n, n), :] = src_ref[pl.ds(base + r, n, stride=d), :].astype(dst_ref.dtype)


def _rows_from_dilated(src_ref, dst_ref, rows, d, accumulate=False):
    for base, r, n in _dilated_runs(rows, d):
        at = pl.ds(base + r, n, stride=d)
        v = src_ref[pl.ds(base + r * n, n), :]
        dst_ref[at, :] = dst_ref[at, :] + v if accumulate else v


def dilate_rows(x, dils, out_dtype, tm=1024):
    S, D = x.shape

    def body(x_ref, *o_refs):
        for d, o_ref in zip(dils, o_refs):
            _rows_to_dilated(x_ref, o_ref, tm, d)

    tile = pl.BlockSpec((tm, LANES), lambda i, j: (i, j))
    return pl.pallas_call(
        body, name="dilate_rows",
        grid=(S // tm, D // LANES),
        in_specs=[tile],
        out_specs=[tile] * len(dils),
        out_shape=[jax.ShapeDtypeStruct((S, D), out_dtype)] * len(dils),
        compiler_params=_params(("parallel", "parallel")),
    )(x)


def _slope_table(group, dilation):
    s = _alibi_slopes()[group].reshape(N_HEADS // 2, 2, 1, 1) * float(dilation)
    return jnp.asarray(np.broadcast_to(s, (N_HEADS // 2, 2, 1, ATTN_W)).copy())


def _residue_shape(S, D, d):
    return (S, D) if d == 1 else (S // PERM_BLOCK, d, PERM_BLOCK // d, D)


def _residue_view(x, d):
    return x.reshape(x.shape[:-2] + _residue_shape(x.shape[-2], x.shape[-1], d))


def _residue_spec(lead_block, lead_index, S, d):
    if d == 1:
        return pl.BlockSpec(lead_block + (S, LANES), lambda hp, r: lead_index + (0, hp))
    return pl.BlockSpec(lead_block + (S // PERM_BLOCK, None, PERM_BLOCK // d, LANES), lambda hp, r: lead_index + (0, r, 0, hp))


def _whole(ref, lead, L):
    return ref[lead + (slice(None),) * (len(ref.shape) - len(lead))].reshape(L, LANES)


def _query_rows(lead, i, d):
    if d == 1:
        return lead + (pl.ds(pl.multiple_of(i * ATTN_BQ, ATTN_BQ), ATTN_BQ), slice(None)), (ATTN_BQ, LANES)
    n = PERM_BLOCK // d
    return lead + (pl.ds(i * (ATTN_BQ // n), ATTN_BQ // n), slice(None), slice(None)), (ATTN_BQ // n, n, LANES)


def _load_query_rows(ref, lead, i, d):
    at, _ = _query_rows(lead, i, d)
    return ref[at].reshape(ATTN_BQ, LANES)


def _store_query_rows(ref, lead, i, d, value):
    at, shape = _query_rows(lead, i, d)
    ref[at] = value.reshape(shape)


def _stage_keys(dst, src_ref, L):
    rows = _whole(src_ref, (), L)
    lane = lax.broadcasted_iota(jnp.int32, (L, LANES), 1)
    zeros = jnp.zeros((ATTN_R, LANES), dst.dtype)
    for h in range(2):
        mine = (lane < HEAD_DIM) if h == 0 else (lane >= HEAD_DIM)
        dst[h, pl.ds(0, ATTN_R), :] = zeros
        dst[h, pl.ds(ATTN_R + L, ATTN_R), :] = zeros
        dst[h, pl.ds(ATTN_R, L), :] = jnp.where(mine, rows, jnp.zeros_like(rows))


def _fill_bias(bias, sl_ref):
    a = lax.broadcasted_iota(jnp.int32, (ATTN_BQ, ATTN_W), 0)
    c = lax.broadcasted_iota(jnp.int32, (ATTN_BQ, ATTN_W), 1)
    rel = jnp.abs(c - ATTN_R - a)
    band = rel <= ATTN_R
    after_start = c >= ATTN_R
    before_end = c < ATTN_BQ + ATTN_R
    for h in range(2):
        base = -(sl_ref[h] * rel.astype(F32))
        for variant in range(4):
            ok = band
            if variant & 1:
                ok = ok & after_start
            if variant & 2:
                ok = ok & before_end
            bias[variant, h] = jnp.where(ok, base, MASK_VALUE)


def _bias_variant(i, nq):
    return jnp.where(i == 0, 1, 0) + jnp.where(i == nq - 1, 2, 0)


def attn_fwd(qkv, group, dilation):
    _, S, D = qkv.shape
    d = dilation
    L = S // d
    nq = L // ATTN_BQ
    ncol = D // LANES
    view = _residue_view(qkv, d)
    slopes = _slope_table(group, d)
    scale = HEAD_DIM ** -0.5

    def body(q_ref, k_ref, v_ref, sl_ref, o_ref, lse_ref, k2, v2, bias):
        @pl.when(pl.program_id(1) == 0)
        def _():
            _fill_bias(bias, sl_ref)

        _stage_keys(k2, k_ref, L)
        _stage_keys(v2, v_ref, L)
        head0 = lax.broadcasted_iota(jnp.int32, (ATTN_BQ, LANES), 1) < HEAD_DIM

        def step(i, carry):
            variant = _bias_variant(i, nq)
            win = pl.ds(pl.multiple_of(i * ATTN_BQ, ATTN_BQ), ATTN_W)
            qs = _load_query_rows(q_ref, (), i, d) * jnp.asarray(scale, BF16)
            acc, ms, ls = None, [], []
            for h in range(2):
                s = _dot_nt(qs, k2[h, win, :]) + bias[variant, h]
                m = jnp.max(s, axis=-1, keepdims=True)
                e = jnp.exp(s - m)
                ls.append(jnp.sum(e, axis=-1, keepdims=True))
                ms.append(m)
                part = _dot(e.astype(BF16), v2[h, win, :])
                acc = part if acc is None else acc + part
            out = acc * jnp.where(head0, 1.0 / ls[0], 1.0 / ls[1])
            lse = jnp.where(head0, ms[0] + jnp.log(ls[0]), ms[1] + jnp.log(ls[1]))
            _store_query_rows(o_ref, (), i, d, out)
            _store_query_rows(lse_ref, (), i, d, lse)
            return carry

        lax.fori_loop(0, nq, step, 0, unroll=min(ATTN_UNROLL, nq))

    def col(which):
        return _residue_spec((None,), (which,), S, d)

    out = _residue_spec((), (), S, d)
    o, lse = pl.pallas_call(
        body, name=f"attn_fwd_g{group}",
        grid=(ncol, d),
        in_specs=[col(0), col(1), col(2), pl.BlockSpec((None, 2, 1, ATTN_W), lambda hp, r: (hp, 0, 0, 0))],
        out_specs=[out, out],
        out_shape=[jax.ShapeDtypeStruct(_residue_shape(S, D, d), F32)] * 2,
        scratch_shapes=[pltpu.VMEM((2, L + 2 * ATTN_R, LANES), BF16), pltpu.VMEM((2, L + 2 * ATTN_R, LANES), BF16),
                        pltpu.VMEM((4, 2, ATTN_BQ, ATTN_W), F32)],
        compiler_params=_params(("arbitrary", "arbitrary")),
    )(view, view, view, slopes)
    return o.reshape(S, D), lse.reshape(S, D)


def attn_combine(os, lses, dils, tm=1024):
    S, D = os[0].shape
    n = len(os)
    n_moved = sum(d > 1 for d in dils)

    def body(*refs):
        o_refs, l_refs, out_ref, lse_ref = list(refs[:n]), list(refs[n:2 * n]), refs[2 * n], refs[2 * n + 1]
        spare = list(refs[2 * n + 2:])
        for g, d in enumerate(dils):
            if d > 1:
                for which in (o_refs, l_refs):
                    token_order = spare.pop()
                    _rows_from_dilated(which[g], token_order, tm, d)
                    which[g] = token_order
        ls = [r[...] for r in l_refs]
        m = functools.reduce(jnp.maximum, ls)
        es = [jnp.exp(l - m) for l in ls]
        tot = functools.reduce(lambda x, y: x + y, es)
        inv = 1.0 / tot
        out_ref[...] = functools.reduce(lambda x, y: x + y, [(e * inv) * r[...] for e, r in zip(es, o_refs)])
        lse_ref[...] = m + jnp.log(tot)

    tile = pl.BlockSpec((tm, LANES), lambda i, j: (i, j))
    return pl.pallas_call(
        body, name="attn_combine",
        grid=(S // tm, D // LANES),
        in_specs=[tile] * (2 * n),
        out_specs=[tile, tile],
        out_shape=[jax.ShapeDtypeStruct((S, D), F32), jax.ShapeDtypeStruct((S, D), F32)],
        scratch_shapes=[pltpu.VMEM((tm, LANES), F32)] * (2 * n_moved),
        compiler_params=_params(("parallel", "parallel")),
    )(*os, *lses)


def attn_bwd(qkv, do, o, lse, group, dilation):
    _, S, D = qkv.shape
    d = dilation
    L = S // d
    nq = L // ATTN_BQ
    ncol = D // LANES
    view = _residue_view(qkv, d)
    slopes = _slope_table(group, d)
    scale = HEAD_DIM ** -0.5

    def body(q_ref, k_ref, v_ref, do_ref, o_ref, lse_ref, sl_ref, dx_ref, k2, v2, dkacc, dvacc, bias):
        @pl.when(pl.program_id(1) == 0)
        def _():
            _fill_bias(bias, sl_ref)

        _stage_keys(k2, k_ref, L)
        _stage_keys(v2, v_ref, L)
        dkacc[...] = jnp.zeros_like(dkacc)
        dvacc[...] = jnp.zeros_like(dvacc)
        lane = lax.broadcasted_iota(jnp.int32, (ATTN_BQ, LANES), 1)
        heads = (lane < HEAD_DIM, lane >= HEAD_DIM)
        key_head0 = lax.broadcasted_iota(jnp.int32, (ATTN_W, LANES), 1) < HEAD_DIM

        def step(i, carry):
            variant = _bias_variant(i, nq)
            win = pl.ds(pl.multiple_of(i * ATTN_BQ, ATTN_BQ), ATTN_W)
            q = _load_query_rows(q_ref, (), i, d)
            qs = q * jnp.asarray(scale, BF16)
            dov = _load_query_rows(do_ref, (), i, d)
            prod = dov * _load_query_rows(o_ref, (), i, d)
            lse_v = _load_query_rows(lse_ref, (), i, d)
            dob = dov.astype(BF16)
            dq, dks, dvs = None, [], []
            for h in range(2):
                s = _dot_nt(qs, k2[h, win, :]) + bias[variant, h]
                lse_h = jnp.max(jnp.where(heads[h], lse_v, -jnp.inf), axis=-1, keepdims=True)
                dterm = jnp.sum(jnp.where(heads[h], prod, 0.0), axis=-1, keepdims=True)
                p = jnp.exp(s - lse_h)
                dp = _dot_nt(dob, v2[h, win, :])
                ds = (p * (dp - dterm) * scale).astype(BF16)
                dvs.append(_dot_tn(p.astype(BF16), dob))
                dks.append(_dot_tn(ds, q))
                part = _dot(ds, k2[h, win, :])
                dq = part if dq is None else dq + part
            dvacc[win, :] += jnp.where(key_head0, dvs[0], dvs[1])
            dkacc[win, :] += jnp.where(key_head0, dks[0], dks[1])
            _store_query_rows(dx_ref, (0,), i, d, dq.astype(BF16))
            return carry

        lax.fori_loop(0, nq, step, 0, unroll=min(ATTN_UNROLL, nq))
        block_shape = dx_ref.shape[1:]
        dx_ref[1] = dkacc[pl.ds(ATTN_R, L), :].astype(BF16).reshape(block_shape)
        dx_ref[2] = dvacc[pl.ds(ATTN_R, L), :].astype(BF16).reshape(block_shape)

    def col(which):
        return _residue_spec((None,), (which,), S, d)

    act = _residue_spec((), (), S, d)
    out = pl.pallas_call(
        body, name=f"attn_bwd_g{group}",
        grid=(ncol, d),
        in_specs=[col(0), col(1), col(2), act, act, act, pl.BlockSpec((None, 2, 1, ATTN_W), lambda hp, r: (hp, 0, 0, 0))],
        out_specs=_residue_spec((3,), (0,), S, d),
        out_shape=jax.ShapeDtypeStruct((3,) + _residue_shape(S, D, d), BF16),
        scratch_shapes=[pltpu.VMEM((2, L + 2 * ATTN_R, LANES), BF16), pltpu.VMEM((2, L + 2 * ATTN_R, LANES), BF16),
                        pltpu.VMEM((L + 2 * ATTN_R, LANES), F32), pltpu.VMEM((L + 2 * ATTN_R, LANES), F32),
                        pltpu.VMEM((4, 2, ATTN_BQ, ATTN_W), F32)],
        compiler_params=_params(("arbitrary", "arbitrary")),
    )(view, view, view, _residue_view(do, d), _residue_view(o, d), _residue_view(lse, d), slopes)
    return out.reshape(3, S, D)


TILE_ELEMS = 256 * 1024


def _row_tile(R, C):
    if R * C <= TILE_ELEMS or R % 16:
        return R
    return max(t for t in range(16, R + 1, 16) if R % t == 0 and (t * C <= TILE_ELEMS or t == 16))


def pair_sum(core, g, recv):
    _, _, R, C = g.shape
    tr = _row_tile(R, C)

    def body(c_ref, g_ref, r_ref, o_ref):
        o_ref[...] = (g_ref[...].astype(F32) + r_ref[...].astype(F32)).astype(BF16)

    blk = pl.BlockSpec((None, tr, C), lambda d, i, c_ref: (d, i, 0))
    return pl.pallas_call(
        body, name="pair_sum",
        grid_spec=pltpu.PrefetchScalarGridSpec(
            num_scalar_prefetch=1, grid=(N_CHIPS, R // tr),
            in_specs=[pl.BlockSpec((None, None, tr, C), lambda d, i, c_ref: (d, c_ref[0], i, 0)), blk],
            out_specs=blk),
        out_shape=jax.ShapeDtypeStruct((N_CHIPS, R, C), BF16),
        compiler_params=_params(("parallel", "parallel")),
    )(core, g, recv)


def chip_sum(chip, own, recv):
    _, R, C = own.shape
    tr = _row_tile(R, C)
    slot_of_relation = {2: 0, 1: 1, 3: 2}

    def body(chip_ref, own_ref, r_ref, o_ref):
        me = chip_ref[0]
        mine = own_ref[...].astype(F32)
        theirs = {rel: r_ref[k].astype(F32) for rel, k in slot_of_relation.items()}
        acc = None
        for s in range(N_CHIPS):
            rel = jnp.bitwise_xor(me, s)
            part = jnp.where(rel == 0, mine, jnp.where(rel == 2, theirs[2], jnp.where(rel == 1, theirs[1], theirs[3])))
            acc = part if acc is None else acc + part
        o_ref[...] = acc

    return pl.pallas_call(
        body, name="chip_sum",
        grid_spec=pltpu.PrefetchScalarGridSpec(
            num_scalar_prefetch=1, grid=(R // tr,),
            in_specs=[pl.BlockSpec((None, tr, C), lambda i, chip_ref: (chip_ref[0], i, 0)),
                      pl.BlockSpec((N_CHIPS - 1, tr, C), lambda i, chip_ref: (0, i, 0))],
            out_specs=pl.BlockSpec((tr, C), lambda i, chip_ref: (i, 0))),
        out_shape=jax.ShapeDtypeStruct((R, C), F32),
        compiler_params=_params(("parallel",)),
    )(chip, own, recv)


def adamw(core, w, g_pairs, m, v):
    L, H, R, C = w.shape
    tr = _row_tile(R, C)

    def body(c_ref, w_ref, *rest):
        g_refs = rest[:2 * L]
        m_ref, v_ref, g_ref, d_ref, nm_ref, nv_ref = rest[2 * L:]
        mine = pl.program_id(1) == c_ref[0]
        g = None
        for l in range(L):
            g_l = jnp.where(mine, g_refs[2 * l][...], g_refs[2 * l + 1][...])
            g = g_l if g is None else jnp.where(pl.program_id(0) == l, g_l, g)
        m = ADAM_B1 * m_ref[...] + (1.0 - ADAM_B1) * g
        v = ADAM_B2 * v_ref[...] + (1.0 - ADAM_B2) * (g * g)
        m_hat = m / (1.0 - ADAM_B1 ** ADAM_STEP)
        v_hat = v / (1.0 - ADAM_B2 ** ADAM_STEP)
        g_ref[...] = g
        d_ref[...] = -ADAM_LR * (m_hat / (jnp.sqrt(v_hat) + ADAM_EPS) + ADAM_WD * w_ref[...])
        nm_ref[...] = m
        nv_ref[...] = v

    blk = pl.BlockSpec((None, None, tr, C), lambda l, h, i, c_ref: (l, h, i, 0))

    def half(layer):
        return pl.BlockSpec((tr, C), lambda l, h, i, c_ref: (jnp.where(l == layer, i, 0), 0))

    shape = jax.ShapeDtypeStruct((L, H, R, C), F32)
    return pl.pallas_call(
        body, name="adamw",
        grid_spec=pltpu.PrefetchScalarGridSpec(
            num_scalar_prefetch=1, grid=(L, H, R // tr),
            in_specs=[blk] + [half(l) for l in range(L) for _ in range(2)] + [blk, blk],
            out_specs=[blk] * 4),
        out_shape=[shape] * 4,
        compiler_params=_params(("parallel", "parallel", "parallel")),
    )(core, w, *[g for pair in g_pairs for g in pair], m, v)


def _place():
    return lax.axis_index("x"), lax.axis_index("y"), lax.axis_index("c")


def _other_chips(x, y):
    return [(2 * (1 - x) + y, (1 - x, y)), (2 * x + (1 - y), (x, 1 - y)), (2 * (1 - x) + (1 - y), (1 - x, 1 - y))]


def all_gather_shards(shards, placed):
    n = len(shards)

    def body(*refs):
        ins, outs = refs[:n], refs[2 * n:3 * n]
        send_sems, recv_sems = refs[3 * n:]
        x, y, c = _place()
        me = 2 * x + y
        sibling = (x, y, 1 - c)
        chips = _other_chips(x, y)

        def copy(a, k, src, dst, to):
            return pltpu.make_async_remote_copy(src_ref=src, dst_ref=dst, send_sem=send_sems.at[a, k], recv_sem=recv_sems.at[a, k],
                                                device_id=to, device_id_type=MESH)

        sends = []
        for a in range(n):
            for k, (_, (px, py)) in enumerate(chips):
                cp = copy(a, k, ins[a].at[c], outs[a].at[me, c], (px, py, c))
                cp.start()
                sends.append(cp)
        for a in range(n):
            for k, (chip, _) in enumerate(chips):
                landed = outs[a].at[chip, c]
                copy(a, k, landed, landed, sibling).wait_recv()
                cp = copy(a, 3 + k, landed, landed, sibling)
                cp.start()
                sends.append(cp)
        for a in range(n):
            for k, (chip, _) in enumerate(chips):
                other = outs[a].at[chip, 1 - c]
                copy(a, 3 + k, other, other, sibling).wait_recv()
        for cp in sends:
            cp.wait_send()

    return pl.pallas_call(
        body, name="all_gather_shards",
        in_specs=[ANY] * (2 * n),
        out_specs=[ANY] * n,
        out_shape=[jax.ShapeDtypeStruct(p.shape, p.dtype) for p in placed],
        scratch_shapes=[pltpu.SemaphoreType.DMA((n, 6)), pltpu.SemaphoreType.DMA((n, 6))],
        input_output_aliases={n + a: a for a in range(n)},
        compiler_params=pltpu.CompilerParams(has_side_effects=True),
    )(*shards, *placed)


def all_gather_shards_async(shards, placed, collective_id, name):
    n = len(shards)
    srcs = [jax.new_ref(s, memory_space=pltpu.MemorySpace.HBM) for s in shards]
    dsts = [jax.new_ref(p, memory_space=pltpu.MemorySpace.HBM) for p in placed]

    @pl.kernel(mesh=plsc.ScalarSubcoreMesh(axis_name="sequencer", num_cores=1), name=name,
               scratch_types=(pltpu.SemaphoreType.DMA((n, 6)), pltpu.SemaphoreType.DMA((n, 6))),
               compiler_params=pltpu.CompilerParams(collective_id=collective_id))
    def launch(send_sems, recv_sems):
        x, y, c = _place()
        me = 2 * x + y
        sibling = (x, y, 1 - c)
        chips = _other_chips(x, y)
        barrier = pltpu.get_barrier_semaphore()
        peers = [sibling] + [(px, py, c) for _, (px, py) in chips]
        for peer in peers:
            pl.semaphore_signal(barrier, inc=1, device_id=peer, device_id_type=MESH)
        pl.semaphore_wait(barrier, len(peers))

        def copy(a, k, src, dst, to):
            return pltpu.make_async_remote_copy(src_ref=src, dst_ref=dst, send_sem=send_sems.at[a, k], recv_sem=recv_sems.at[a, k],
                                                device_id=to, device_id_type=MESH)

        sends = []
        for a in range(n):
            for k, (_, (px, py)) in enumerate(chips):
                cp = copy(a, k, srcs[a].at[c], dsts[a].at[me, c], (px, py, c))
                cp.start()
                sends.append(cp)
        for a in range(n):
            for k, (chip, _) in enumerate(chips):
                landed = dsts[a].at[chip, c]
                copy(a, k, landed, landed, sibling).wait_recv()
                cp = copy(a, 3 + k, landed, landed, sibling)
                cp.start()
                sends.append(cp)
        for a in range(n):
            for k, (chip, _) in enumerate(chips):
                other = dsts[a].at[chip, 1 - c]
                copy(a, 3 + k, other, other, sibling).wait_recv()
        for cp in sends:
            cp.wait_send()

    launch()
    return [d[...] for d in dsts]


def _exchange(body, ins, out_shapes, sem_shapes, name, peers=None, collective_id=None):
    n_in, n_out = len(ins), len(out_shapes)
    sems = [pltpu.SemaphoreType.DMA(shape) for shape in sem_shapes]
    if collective_id is None:
        def tc_body(*refs):
            body(refs[:n_in], refs[n_in:n_in + n_out], *refs[n_in + n_out:])

        return pl.pallas_call(tc_body, name=name, in_specs=[ANY] * n_in, out_specs=[ANY] * n_out, out_shape=out_shapes,
                              scratch_shapes=sems, compiler_params=pltpu.CompilerParams(has_side_effects=True))(*ins)
    srcs = [jax.new_ref(a, memory_space=pltpu.MemorySpace.HBM) for a in ins]
    dsts = [jax.empty_ref(shape, memory_space=pltpu.MemorySpace.HBM) for shape in out_shapes]

    @pl.kernel(mesh=plsc.ScalarSubcoreMesh(axis_name="sequencer", num_cores=1), name=name, scratch_types=tuple(sems),
               compiler_params=pltpu.CompilerParams(collective_id=collective_id))
    def launch(*sem_refs):
        barrier = pltpu.get_barrier_semaphore()
        others = peers(*_place())
        for peer in others:
            pl.semaphore_signal(barrier, inc=1, device_id=peer, device_id_type=MESH)
        pl.semaphore_wait(barrier, len(others))
        body(srcs, dsts, *sem_refs)

    launch()
    return [d[...] for d in dsts]


def _sibling(x, y, c):
    return [(x, y, 1 - c)]


def _same_core_of_other_chips(x, y, c):
    return [(px, py, c) for _, (px, py) in _other_chips(x, y)]


def sibling_exchange_halves(grads, name="sibling_exchange_halves", collective_id=None):
    n = len(grads)

    def body(ins, outs, send_sems, recv_sems):
        x, y, c = _place()
        copies = [pltpu.make_async_remote_copy(src_ref=ins[a].at[:, 1 - c], dst_ref=outs[a], send_sem=send_sems.at[a],
                                               recv_sem=recv_sems.at[a], device_id=(x, y, 1 - c), device_id_type=MESH) for a in range(n)]
        for cp in copies:
            cp.start()
        for cp in copies:
            cp.wait()

    shapes = [jax.ShapeDtypeStruct((N_CHIPS,) + g.shape[2:], g.dtype) for g in grads]
    return _exchange(body, grads, shapes, [(n,), (n,)], name, _sibling, collective_id)


def chip_exchange(sums, name="chip_exchange", collective_id=None):
    n = len(sums)

    def body(ins, outs, send_sems, recv_sems):
        x, y, c = _place()
        copies = []
        for a in range(n):
            for k, (chip, (px, py)) in enumerate(_other_chips(x, y)):
                cp = pltpu.make_async_remote_copy(src_ref=ins[a].at[chip], dst_ref=outs[a].at[k], send_sem=send_sems.at[a, k],
                                                  recv_sem=recv_sems.at[a, k], device_id=(px, py, c), device_id_type=MESH)
                cp.start()
                copies.append(cp)
        for cp in copies:
            cp.wait()

    shapes = [jax.ShapeDtypeStruct((N_CHIPS - 1,) + s.shape[1:], s.dtype) for s in sums]
    return _exchange(body, sums, shapes, [(n, 3), (n, 3)], name, _same_core_of_other_chips, collective_id)


def sibling_share(halves, name="sibling_share", collective_id=None):
    n = len(halves)

    def body(ins, outs, send_sems, recv_sems):
        x, y, c = _place()
        copies = [pltpu.make_async_remote_copy(src_ref=ins[a], dst_ref=outs[a], send_sem=send_sems.at[a], recv_sem=recv_sems.at[a],
                                               device_id=(x, y, 1 - c), device_id_type=MESH) for a in range(n)]
        for cp in copies:
            cp.start()
        for cp in copies:
            cp.wait()

    shapes = [jax.ShapeDtypeStruct(h.shape, h.dtype) for h in halves]
    return _exchange(body, halves, shapes, [(n,), (n,)], name, _sibling, collective_id)


def all_reduce_small(v):
    R, C = v.shape
    n_dev = 8

    def body(v_ref, o_ref, buf, send_sems, recv_sems):
        x, y, c = _place()
        me = 4 * x + 2 * y + c
        buf[me] = v_ref[...]
        copies = []
        for rel in range(1, n_dev):
            fx, fy, fc = rel >> 2, (rel >> 1) & 1, rel & 1
            peer = (x ^ fx, y ^ fy, c ^ fc)
            cp = pltpu.make_async_remote_copy(src_ref=v_ref, dst_ref=buf.at[me], send_sem=send_sems.at[rel - 1],
                                              recv_sem=recv_sems.at[rel - 1], device_id=peer, device_id_type=MESH)
            cp.start()
            copies.append(cp)
        for cp in copies:
            cp.wait()
        acc = buf[0]
        for k in range(1, n_dev):
            acc = acc + buf[k]
        o_ref[...] = acc

    return pl.pallas_call(
        body, name="all_reduce_small",
        in_specs=[pl.BlockSpec(memory_space=pltpu.VMEM)],
        out_specs=pl.BlockSpec(memory_space=pltpu.VMEM),
        out_shape=jax.ShapeDtypeStruct((R, C), F32),
        scratch_shapes=[pltpu.VMEM((n_dev, R, C), F32), pltpu.SemaphoreType.DMA((n_dev - 1,)), pltpu.SemaphoreType.DMA((n_dev - 1,))],
        compiler_params=pltpu.CompilerParams(has_side_effects=True),
    )(v)


WEIGHT_NAMES = ("ffn1_w_gate", "ffn1_w_up", "ffn1_w_down", "ffn2_w_gate", "ffn2_w_up", "ffn2_w_down", "ln_gain", "ln_bias",
                "pool_w_in", "pool_w_group", "pool_scale", "pool_w_out", "attn_w_qkv", "attn_w_out")
MATRIX_NAMES = ("ffn1_w_gate", "ffn1_w_up", "ffn1_w_down", "ffn2_w_gate", "ffn2_w_up", "ffn2_w_down",
                "pool_w_in", "pool_w_group", "pool_w_out", "attn_w_qkv", "attn_w_out")


TRANSPOSED_NAMES = ("ffn1_w_gate", "ffn1_w_up", "ffn2_w_gate", "ffn2_w_up")


def _halves(name, w):
    if name in TRANSPOSED_NAMES:
        w = jnp.swapaxes(w, 1, 2)
    return w.reshape(2, -1, w.shape[-1])


def _unhalves(name, t, shape):
    if name in TRANSPOSED_NAMES:
        return jnp.swapaxes(t.reshape(shape[0], shape[2], shape[1]), 1, 2)
    return t.reshape(shape)


def kernel(x, ffn1_w_gate, ffn1_w_up, ffn1_w_down, ffn2_w_gate, ffn2_w_up, ffn2_w_down, ln_gain, ln_bias, pool_w_in, pool_w_group, pool_scale, pool_w_out, attn_w_qkv, attn_w_out, loss_target, m_ffn1_w_gate, m_ffn1_w_up, m_ffn1_w_down, m_ffn2_w_gate, m_ffn2_w_up, m_ffn2_w_down, m_ln_gain, m_ln_bias, m_pool_w_in, m_pool_w_group, m_pool_scale, m_pool_w_out, m_attn_w_qkv, m_attn_w_out, v_ffn1_w_gate, v_ffn1_w_up, v_ffn1_w_down, v_ffn2_w_gate, v_ffn2_w_up, v_ffn2_w_down, v_ln_gain, v_ln_bias, v_pool_w_in, v_pool_w_group, v_pool_scale, v_pool_w_out, v_attn_w_qkv, v_attn_w_out):
    weights = dict(zip(WEIGHT_NAMES, (ffn1_w_gate, ffn1_w_up, ffn1_w_down, ffn2_w_gate, ffn2_w_up, ffn2_w_down, ln_gain, ln_bias,
                                      pool_w_in, pool_w_group, pool_scale, pool_w_out, attn_w_qkv, attn_w_out)))
    moms = dict(zip(WEIGHT_NAMES, (m_ffn1_w_gate, m_ffn1_w_up, m_ffn1_w_down, m_ffn2_w_gate, m_ffn2_w_up, m_ffn2_w_down, m_ln_gain,
                                   m_ln_bias, m_pool_w_in, m_pool_w_group, m_pool_scale, m_pool_w_out, m_attn_w_qkv, m_attn_w_out)))
    vels = dict(zip(WEIGHT_NAMES, (v_ffn1_w_gate, v_ffn1_w_up, v_ffn1_w_down, v_ffn2_w_gate, v_ffn2_w_up, v_ffn2_w_down, v_ln_gain,
                                   v_ln_bias, v_pool_w_in, v_pool_w_group, v_pool_scale, v_pool_w_out, v_attn_w_qkv, v_attn_w_out)))
    S, D = x.shape[1], x.shape[2]
    FB = ffn1_w_gate.shape[2]
    QKV = attn_w_qkv.shape[2] * N_CHIPS
    G, CB = pool_w_group.shape[1], pool_w_group.shape[2]
    C = pool_w_group.shape[3]
    cx, cy, cc = _place()
    chip = 2 * cx + cy
    xs = x.reshape(S, D)
    target = loss_target.reshape(S, D)

    ln_rows = jnp.concatenate([ln_gain, ln_bias, jnp.zeros((DEPTH, 2, ln_gain.shape[2]), F32)], axis=1)
    shard = {n: _halves(n, weights[n]).astype(BF16) for n in MATRIX_NAMES}
    ffn_layer = lambda f, i: {f"{f}{s}@{i}": shard[f + s][i].reshape(2, FB // 2, D) for s in ("_w_gate", "_w_up", "_w_down")}
    pool = {n: shard[n] for n in ("pool_w_in", "pool_w_group", "pool_w_out")}
    attn = {n: shard[n] for n in ("attn_w_qkv", "attn_w_out")}
    groups = [dict(ffn_layer("ffn1", 0), ln=ln_rows), pool, ffn_layer("ffn2", 0), ffn_layer("ffn1", 1), attn, ffn_layer("ffn2", 1)]
    full = {}

    def launch(k, after=None):
        parts = list(groups[k].values())
        if after is not None:
            *parts, after = lax.optimization_barrier((*parts, after))
        placed = [lax.dynamic_update_slice(lax.empty((N_CHIPS,) + s.shape, s.dtype), s[None], (chip, 0, 0, 0)) for s in parts]
        full.update(zip(groups[k], all_gather_shards_async(parts, placed, GATHER_COLLECTIVE_ID, f"gather_weights_{k}")))
        return after

    for k in range(4):
        launch(k)
    ffn_w = lambda f, i: [full[f"{f}{s}@{i}"].reshape(N_CHIPS, FB, D) for s in ("_w_gate", "_w_up", "_w_down")]
    w_pool_in = full["pool_w_in"].reshape(D, D)
    w_pool_out = full["pool_w_out"].reshape(D, D)
    w_group = full["pool_w_group"].reshape(N_CHIPS, G, CB, C).transpose(1, 0, 2, 3).reshape(G, N_CHIPS * CB, C)
    ln_full = full["ln"].transpose(1, 2, 0, 3).reshape(DEPTH, 8, D)
    gain = lambda i, k: ln_full[i, k].reshape(1, D)
    bias = lambda i, k: ln_full[i, 3 + k].reshape(1, D)

    dils = [dil for _, dil in DIL_CONFIGS]
    moved_dils = [dil for dil in dils if dil > 1]

    saved = []
    y = xs
    for i in range(DEPTH):
        y_in = y
        y, xh, rs, a, u = ffn_fwd(y_in, *ffn_w("ffn1", i), gain(i, 0), bias(i, 0))
        f1 = (y_in, xh, rs, a, u)
        if i == 0:
            y = launch(4, after=y)
        y_mid = y
        if i % 2 == 0:
            pu = mm_nn(y_mid, w_pool_in, F32)[0]
            mixed, pv = pool_mix(pu, w_group, pool_scale)
            y, xh, rs = proj_ln(pv, w_pool_out, y_mid, gain(i, 1), bias(i, 1))
            mix = (y_mid, xh, rs, mixed, pv)
            if i == 0:
                y = launch(5, after=y)
        else:
            qkv_blocks, y_mid = lax.optimization_barrier((full["attn_w_qkv"], y_mid))
            w_qkv = chip_blocks_to_columns(qkv_blocks.reshape(N_CHIPS, D, QKV // N_CHIPS))
            w_attn_out = full["attn_w_out"].reshape(D, D)
            moved = dict(zip(moved_dils, dilate_rows(y_mid, moved_dils, BF16)))
            srcs = [moved.get(dil, y_mid) for dil in dils]
            qkvs = [mm_nn(src, w_qkv, BF16, first_block=3 * g, nb=3) for g, src in enumerate(srcs)]
            parts = [attn_fwd(qkv, g, dil) for g, (qkv, dil) in enumerate(zip(qkvs, dils))]
            ao, lse = attn_combine([p[0] for p in parts], [p[1] for p in parts], dils)
            y, xh, rs = proj_ln(ao, w_attn_out, y_mid, gain(i, 1), bias(i, 1))
            mix = (y_mid, xh, rs, srcs, qkvs, ao, lse, w_qkv, w_attn_out)
        y_in2 = y
        y, xh, rs, a, u = ffn_fwd(y_in2, *ffn_w("ffn2", i), gain(i, 2), bias(i, 2))
        f2 = (y_in2, xh, rs, a, u)
        saved.append((f1, mix, f2))

    dy, loss_part = loss_head(y, target)
    loss = lax.psum(loss_part[0, 0], ("x", "y", "c"))

    core = cc.reshape(1).astype(jnp.int32)
    chip_id = chip.reshape(1).astype(jnp.int32)
    dgain = [[None] * 3 for _ in range(DEPTH)]
    dbias = [[None] * 3 for _ in range(DEPTH)]
    dscale = None
    pieces = []
    own_half, other_half = {}, {}

    def tie(arrays, after):
        *arrays, after = lax.optimization_barrier((*arrays, after))
        return arrays, after

    def start_piece(keys, arrays):
        blocks = [g.reshape(N_CHIPS, 2, -1, g.shape[-1]) for g in arrays]
        k = len(pieces)
        pieces.append(dict(keys=keys, blocks=blocks, from_sibling=sibling_exchange_halves(
            blocks, name=f"reduce_halves_{k}", collective_id=SIBLING_COLLECTIVE_ID)))

    def pair_sums_and_chip_exchange(k, after):
        piece = pieces[k]
        received, after = tie(piece["from_sibling"], after)
        piece["pair_sums"] = [pair_sum(core, b, r) for b, r in zip(piece["blocks"], received)]
        piece["from_chips"] = chip_exchange(piece["pair_sums"], name=f"reduce_chips_{k}", collective_id=CHIPS_COLLECTIVE_ID)
        return after

    def chip_sums_and_share(k, after):
        piece = pieces[k]
        received, after = tie(piece["from_chips"], after)
        mine = [chip_sum(chip_id, p, r) for p, r in zip(piece["pair_sums"], received)]
        theirs = sibling_share(mine, name=f"reduce_share_{k}", collective_id=SIBLING_COLLECTIVE_ID)
        own_half.update(zip(piece["keys"], mine))
        other_half.update(zip(piece["keys"], theirs))
        return after

    def piece_done(keys, arrays, dy):
        start_piece(keys, arrays)
        k = len(pieces) - 1
        if k >= 1:
            dy = pair_sums_and_chip_exchange(k - 1, dy)
        if k >= 2:
            dy = chip_sums_and_share(k - 2, dy)
        return dy

    def ffn_backward(name, i, dy, state):
        y_in, xh, rs, a, u = state
        k = 0 if name == "ffn1" else 2
        dz, dgain[i][k], dbias[i][k] = ln_bwd(dy, xh, rs, gain(i, k))
        dx, h, da, du = ffn_bwd(dz, a, u, *ffn_w(name, i))
        outs = mm_tn([(da, y_in, 1.0), (du, y_in, 1.0), (h, dz, MACARON_WEIGHT)], nblk=N_CHIPS, out_shape=(N_CHIPS, FB, D),
                     out_block=(None, FB, D), out_index=lambda j: (j, 0, 0), name="ffn_wgrad")
        return piece_done([(name + s, i) for s in ("_w_gate", "_w_up", "_w_down")], outs, dx)

    def square_grad(a, b):
        return mm_tn([(a, b, 1.0)], nblk=1, out_shape=(D, D), out_block=(D, D), out_index=lambda j: (0, 0), name="square_wgrad")[0]

    for i in reversed(range(DEPTH)):
        f1, mix, f2 = saved[i]
        dy = ffn_backward("ffn2", i, dy, f2)
        dz, dgain[i][1], dbias[i][1] = ln_bwd(dy, mix[1], mix[2], gain(i, 1))
        if i % 2 == 0:
            y_mid, _, _, mixed, pv = mix
            g_out = square_grad(pv, dz)
            dv = mm_nt(dz, w_pool_out, None, a_blocked=False)
            du, dwg, dscale = pool_mix_bwd(dv, mixed, w_group, pool_scale)
            g_group = dwg.reshape(G, N_CHIPS, CB, C).transpose(1, 0, 2, 3)
            g_in = square_grad(y_mid, du)
            dy = mm_nt(du, w_pool_in, dz, a_blocked=False)
            dy = piece_done([("pool_w_out", 0), ("pool_w_group", 0), ("pool_w_in", 0)], [g_out, g_group, g_in], dy)
        else:
            y_mid, _, _, srcs, qkvs, ao, lse, w_qkv, w_attn_out = mix
            g_out = square_grad(ao, dz)
            dao = mm_nt(dz, w_attn_out, None, a_blocked=False)
            in_order = [dict(zip(moved_dils, dilate_rows(t, moved_dils, F32))) for t in (dao, ao, lse)]
            dqkvs = [attn_bwd(qkvs[g], *[m.get(dil, t) for m, t in zip(in_order, (dao, ao, lse))], g, dil)
                     for g, dil in enumerate(dils)]
            g_qkv = [mm_tn([(src, dqkv, 1.0)], nblk=3, out_shape=(D, 3 * D), out_block=(D, D), out_index=lambda j: (0, j),
                           name="qkv_wgrad")[0] for src, dqkv in zip(srcs, dqkvs)]
            g_qkv = jnp.concatenate(g_qkv, axis=1).reshape(D, N_CHIPS, QKV // N_CHIPS).transpose(1, 0, 2)
            dy = mm_nt_dilated(dqkvs, dils, w_qkv, dz)
            dy = piece_done([("attn_w_out", 0), ("attn_w_qkv", 0)], [g_out, g_qkv], dy)
        dy = ffn_backward("ffn1", i, dy, f1)
    grad_x = dy.reshape(x.shape)

    last = len(pieces) - 1
    small = jnp.concatenate([jnp.concatenate(dgain[i] + dbias[i], axis=0) for i in range(DEPTH)] + [dscale, jnp.zeros((3, D), F32)], axis=0)
    small = all_reduce_small(small)
    per_layer = small[:6 * DEPTH].reshape(DEPTH, 6, D)
    cols = D // N_CHIPS
    small_grads = {"ln_gain": lax.dynamic_slice_in_dim(per_layer[:, 0:3], chip * cols, cols, axis=2),
                   "ln_bias": lax.dynamic_slice_in_dim(per_layer[:, 3:6], chip * cols, cols, axis=2),
                   "pool_scale": small[6 * DEPTH:6 * DEPTH + 1]}

    grad_w, delta, new_m, new_v = {}, {}, {}, {}

    def update(n, after):
        shape = weights[n].shape
        if n in MATRIX_NAMES:
            layers = DEPTH if (n, 1) in own_half else 1
            as4 = lambda t: _halves(n, t).reshape(layers, 2, -1, shape[-1] if n not in TRANSPOSED_NAMES else shape[1])
            mine, after = tie([own_half[n, l] for l in range(layers)], after)
            outs = adamw(core, as4(weights[n]), [(mine[l], other_half[n, l]) for l in range(layers)], as4(moms[n]), as4(vels[n]))
            grad_w[n], delta[n], new_m[n], new_v[n] = [_unhalves(n, t, shape) for t in outs]
        else:
            as4 = lambda t: t.reshape(1, 1, -1, shape[-1])
            g2 = small_grads[n].reshape(-1, shape[-1])
            outs = adamw(core, as4(weights[n]), [(g2, g2)], as4(moms[n]), as4(vels[n]))
            grad_w[n], delta[n], new_m[n], new_v[n] = [t.reshape(shape) for t in outs]
        return outs[1]

    marker = small
    for n in ("ln_gain", "ln_bias", "pool_scale"):
        marker = update(n, marker)
    marker = pair_sums_and_chip_exchange(last, marker)
    marker = chip_sums_and_share(last - 1, marker)
    for n in MATRIX_NAMES:
        if not n.startswith("ffn1"):
            marker = update(n, marker)
    marker = chip_sums_and_share(last, marker)
    for n in MATRIX_NAMES:
        if n.startswith("ffn1"):
            marker = update(n, marker)

    return (loss, grad_x, *[grad_w[n] for n in WEIGHT_NAMES], *[delta[n] for n in WEIGHT_NAMES],
            *[new_m[n] for n in WEIGHT_NAMES], *[new_v[n] for n in WEIGHT_NAMES])
```

```python
import functools
import math

import numpy as np
import jax
import jax.numpy as jnp
from jax import lax
from jax.experimental import pallas as pl
from jax.experimental.pallas import tpu as pltpu
from jax.experimental.pallas import tpu_sc as plsc

F32 = jnp.float32
BF16 = jnp.bfloat16

DEPTH = 2
ALPHA = (2.0 * DEPTH) ** 0.25
MACARON_WEIGHT = 0.5
LN_EPS = 1e-5
MASK_VALUE = -1e30
POOL_WINDOWS = (2, 4, 8, 16)
POOL_PAD = 16
HEAD_DIM = 64
N_HEADS = 16
DIL_CONFIGS = ((128, 1), (512, 4), (2048, 16))
ATTN_R = 64
ATTN_BQ = 128
ATTN_W = ATTN_BQ + 2 * ATTN_R
FFN_HIDDEN_TILE = 256
ATTN_UNROLL = 4
LANES = 128
ADAM_LR = 0.001
ADAM_B1 = 0.9
ADAM_B2 = 0.999
ADAM_EPS = 1e-08
ADAM_WD = 0.01
ADAM_STEP = 10
N_CHIPS = 4
GATHER_COLLECTIVE_ID = 1
SIBLING_COLLECTIVE_ID = 2
CHIPS_COLLECTIVE_ID = 3
VMEM_LIMIT = 56 * 1024 * 1024
MESH = pl.DeviceIdType.MESH
ANY = pl.BlockSpec(memory_space=pl.ANY)


def _params(sem=None, vmem=VMEM_LIMIT):
    return pltpu.CompilerParams(dimension_semantics=sem, vmem_limit_bytes=vmem)


def _alibi_slopes():
    n = len(DIL_CONFIGS) * N_HEADS
    s = 2.0 ** (-8.0 * np.arange(1, n + 1) / n)
    return s.reshape(len(DIL_CONFIGS), N_HEADS).astype(np.float32)


def _ln_fwd(z, g, b):
    mu = jnp.mean(z, axis=-1, keepdims=True)
    zc = z - mu
    var = jnp.mean(zc * zc, axis=-1, keepdims=True)
    rstd = lax.rsqrt(var + LN_EPS)
    xhat = zc * rstd
    return xhat * g + b, xhat, rstd


def _dot(a, b):
    return jnp.dot(a, b, preferred_element_type=F32)


def _dot_nt(a, b):
    return lax.dot_general(a, b, (((1,), (1,)), ((), ())), preferred_element_type=F32)


def _dot_tn(a, b):
    return lax.dot_general(a, b, (((0,), (0,)), ((), ())), preferred_element_type=F32)


def mm_nn(a, b, out_dtype, first_block=0, nb=None, tm=512):
    S, K = a.shape
    Nb = K
    nb = b.shape[1] // Nb if nb is None else nb

    def body(a_ref, b_ref, o_ref):
        o_ref[...] = _dot(a_ref[...].astype(BF16), b_ref[...]).astype(out_dtype)

    return pl.pallas_call(
        body, name="mm_nn",
        grid=(S // tm, nb),
        in_specs=[pl.BlockSpec((tm, K), lambda i, j: (i, 0)), pl.BlockSpec((K, Nb), lambda i, j: (0, first_block + j))],
        out_specs=pl.BlockSpec((None, tm, Nb), lambda i, j: (j, i, 0)),
        out_shape=jax.ShapeDtypeStruct((nb, S, Nb), out_dtype),
        compiler_params=_params(("parallel", "arbitrary")),
    )(a, b)


def chip_blocks_to_columns(w):
    nb, K, Nb = w.shape

    def body(w_ref, o_ref):
        o_ref[...] = w_ref[...]

    return pl.pallas_call(
        body, name="chip_blocks_to_columns",
        grid=(nb,),
        in_specs=[pl.BlockSpec((None, K, Nb), lambda b: (b, 0, 0))],
        out_specs=pl.BlockSpec((K, Nb), lambda b: (0, b)),
        out_shape=jax.ShapeDtypeStruct((K, nb * Nb), w.dtype),
        compiler_params=_params(("parallel",)),
    )(w)


def proj_ln(a, w, resid, gain, bias, tm=512):
    S, K = a.shape
    D = w.shape[1]

    def body(a_ref, w_ref, r_ref, g_ref, b_ref, y_ref, xh_ref, rs_ref):
        z = ALPHA * r_ref[...] + _dot(a_ref[...].astype(BF16), w_ref[...])
        y, xh, rs = _ln_fwd(z, g_ref[...], b_ref[...])
        y_ref[...] = y
        xh_ref[...] = xh
        rs_ref[...] = rs

    row = pl.BlockSpec((tm, D), lambda i: (i, 0))
    vec = pl.BlockSpec((1, D), lambda i: (0, 0))
    return pl.pallas_call(
        body, name="proj_ln",
        grid=(S // tm,),
        in_specs=[pl.BlockSpec((tm, K), lambda i: (i, 0)), pl.BlockSpec((K, D), lambda i: (0, 0)), row, vec, vec],
        out_specs=[row, row, pl.BlockSpec((tm, 1), lambda i: (i, 0))],
        out_shape=[jax.ShapeDtypeStruct((S, D), F32), jax.ShapeDtypeStruct((S, D), F32), jax.ShapeDtypeStruct((S, 1), F32)],
        compiler_params=_params(("parallel",)),
    )(a, w, resid, gain, bias)


def mm_nt(a, w, resid, a_blocked, out_dtype=F32, tm=512):
    if a_blocked:
        nk, S, Kb = a.shape
        a_spec = pl.BlockSpec((None, tm, Kb), lambda i, n: (n, i, 0))
    else:
        S, Kb = a.shape
        nk = 1
        a_spec = pl.BlockSpec((tm, Kb), lambda i, n: (i, 0))
    M = w.shape[0]
    has_resid = resid is not None

    def body(*refs):
        if has_resid:
            a_ref, w_ref, r_ref, o_ref, acc = refs
        else:
            a_ref, w_ref, o_ref, acc = refs
        n = pl.program_id(1)
        part = _dot_nt(a_ref[...].astype(BF16), w_ref[...])

        @pl.when(n == 0)
        def _():
            acc[...] = part

        @pl.when(n > 0)
        def _():
            acc[...] += part

        @pl.when(n == nk - 1)
        def _():
            out = acc[...]
            if has_resid:
                out = out + ALPHA * r_ref[...]
            o_ref[...] = out.astype(out_dtype)

    row = pl.BlockSpec((tm, M), lambda i, n: (i, 0))
    in_specs = [a_spec, pl.BlockSpec((M, Kb), lambda i, n: (0, n))] + ([row] if has_resid else [])
    args = (a, w) + ((resid,) if has_resid else ())
    return pl.pallas_call(
        body, name="mm_nt",
        grid=(S // tm, nk),
        in_specs=in_specs,
        out_specs=row,
        out_shape=jax.ShapeDtypeStruct((S, M), out_dtype),
        scratch_shapes=[pltpu.VMEM((tm, M), F32)],
        compiler_params=_params(("parallel", "arbitrary")),
    )(*args)


def mm_nt_dilated(parts, dils, w, resid, tm=512):
    n_groups = len(parts)
    _, S, K = parts[0].shape
    M = w.shape[0]
    nk = 3 * n_groups

    def body(*refs):
        a_refs = refs[:n_groups]
        w_ref, r_ref, o_ref, group_acc, total = refs[n_groups:]
        n = pl.program_id(1)
        for g in range(n_groups):
            for k in range(3):
                @pl.when(n == 3 * g + k)
                def _():
                    part = _dot_nt(a_refs[g][...], w_ref[...])
                    for c in range(M // LANES):
                        lanes = slice(c * LANES, (c + 1) * LANES)
                        if k == 0:
                            group_acc[c] = part[:, lanes]
                        else:
                            group_acc[c] += part[:, lanes]
                        if k == 2:
                            _rows_from_dilated(group_acc.at[c], total.at[c], tm, dils[g], accumulate=g > 0)

        @pl.when(n == nk - 1)
        def _():
            for c in range(M // LANES):
                lanes = slice(c * LANES, (c + 1) * LANES)
                o_ref[:, lanes] = total[c] + ALPHA * r_ref[:, lanes]

    def a_spec(g):
        return pl.BlockSpec((None, tm, K), lambda i, n: (jnp.clip(n - 3 * g, 0, 2), i, 0))

    row = pl.BlockSpec((tm, M), lambda i, n: (i, 0))
    return pl.pallas_call(
        body, name="mm_nt_dilated",
        grid=(S // tm, nk),
        in_specs=[a_spec(g) for g in range(n_groups)] + [pl.BlockSpec((M, K), lambda i, n: (0, n)), row],
        out_specs=row,
        out_shape=jax.ShapeDtypeStruct((S, M), F32),
        scratch_shapes=[pltpu.VMEM((M // LANES, tm, LANES), F32), pltpu.VMEM((M // LANES, tm, LANES), F32)],
        compiler_params=_params(("parallel", "arbitrary")),
    )(*parts, w, resid)


def mm_tn(pairs, *, nblk, out_shape, out_block, out_index, alias=None, tk=1024, name="mm_tn"):
    operands = []
    for a, b, _ in pairs:
        for t in (a, b):
            if not any(t is o for o in operands):
                operands.append(t)
    where = lambda t: next(i for i, o in enumerate(operands) if o is t)
    S = pairs[0][0].shape[-2]
    n_out, n_in = len(pairs), len(operands)
    n_alias = len(alias) if alias is not None else 0

    def spec(t):
        if t.ndim == 3:
            return pl.BlockSpec((None, tk, t.shape[-1]), lambda j, k: (j, k, 0))
        return pl.BlockSpec((tk, t.shape[-1]), lambda j, k: (k, 0))

    def body(*refs):
        refs = refs[n_alias:]
        in_refs, o_refs, accs = refs[:n_in], refs[n_in:n_in + n_out], refs[n_in + n_out:]
        k = pl.program_id(1)
        for (a, b, scale), o_ref, acc in zip(pairs, o_refs, accs):
            part = _dot_tn(in_refs[where(a)][...].astype(BF16), in_refs[where(b)][...].astype(BF16))

            @pl.when(k == 0)
            def _():
                acc[...] = part

            @pl.when(k > 0)
            def _():
                acc[...] += part

            @pl.when(k == S // tk - 1)
            def _():
                o_ref[...] = (scale * acc[...]).astype(BF16)

    out_spec = pl.BlockSpec(out_block, lambda j, k: out_index(j))
    outs = pl.pallas_call(
        body, name=name,
        grid=(nblk, S // tk),
        in_specs=[ANY] * n_alias + [spec(t) for t in operands],
        out_specs=[out_spec] * n_out,
        out_shape=[jax.ShapeDtypeStruct(out_shape, BF16)] * n_out,
        scratch_shapes=[pltpu.VMEM((a.shape[-1], b.shape[-1]), F32) for a, b, _ in pairs],
        input_output_aliases={i: i for i in range(n_alias)},
        compiler_params=_params(("parallel", "arbitrary")),
    )(*(tuple(alias) if alias is not None else ()), *operands)
    return list(outs)


def ffn_fwd(x, wg, wu, wd, gain, bias, tm=1024):
    S, D = x.shape
    nb, FB = wg.shape[0], wg.shape[1]

    def body(x_ref, wg_ref, wu_ref, wd_ref, g_ref, b_ref, y_ref, xh_ref, rs_ref, a_ref, u_ref, acc, xb_ref):
        j = pl.program_id(1)

        @pl.when(j == 0)
        def _():
            xb_ref[...] = x_ref[...].astype(BF16)
            acc[...] = jnp.zeros_like(acc)

        xb = xb_ref[...]
        total = None
        for t0 in range(0, FB, FFN_HIDDEN_TILE):
            cols = pl.ds(t0, min(FFN_HIDDEN_TILE, FB - t0))
            a = _dot_nt(xb, wg_ref[cols, :])
            u = _dot_nt(xb, wu_ref[cols, :])
            a_ref[:, cols] = a.astype(BF16)
            u_ref[:, cols] = u.astype(BF16)
            h = a * jax.nn.sigmoid(a) * u
            part = _dot(h.astype(BF16), wd_ref[cols, :])
            total = part if total is None else total + part
        acc[...] += total

        @pl.when(j == nb - 1)
        def _():
            z = ALPHA * x_ref[...] + MACARON_WEIGHT * acc[...]
            y, xh, rs = _ln_fwd(z, g_ref[...], b_ref[...])
            y_ref[...] = y
            xh_ref[...] = xh
            rs_ref[...] = rs

    row = pl.BlockSpec((tm, D), lambda i, j: (i, 0))
    vec = pl.BlockSpec((1, D), lambda i, j: (0, 0))
    w_out = pl.BlockSpec((None, FB, D), lambda i, j: (j, 0, 0))
    act = pl.BlockSpec((None, tm, FB), lambda i, j: (j, i, 0))
    return pl.pallas_call(
        body, name="ffn_fwd",
        grid=(S // tm, nb),
        in_specs=[row, w_out, w_out, w_out, vec, vec],
        out_specs=[row, row, pl.BlockSpec((tm, 1), lambda i, j: (i, 0)), act, act],
        out_shape=[jax.ShapeDtypeStruct((S, D), F32), jax.ShapeDtypeStruct((S, D), F32), jax.ShapeDtypeStruct((S, 1), F32),
                   jax.ShapeDtypeStruct((nb, S, FB), BF16), jax.ShapeDtypeStruct((nb, S, FB), BF16)],
        scratch_shapes=[pltpu.VMEM((tm, D), F32), pltpu.VMEM((tm, D), BF16)],
        compiler_params=_params(("parallel", "arbitrary")),
    )(x, wg, wu, wd, gain, bias)


def ffn_bwd(dz, a, u, wg, wu, wd, tm=1024):
    S, D = dz.shape
    nb, FB = wg.shape[0], wg.shape[1]

    def body(dz_ref, a_ref, u_ref, wg_ref, wu_ref, wd_ref, dx_ref, h_ref, da_ref, du_ref, acc, dzb_ref):
        j = pl.program_id(1)

        @pl.when(j == 0)
        def _():
            dzb_ref[...] = (MACARON_WEIGHT * dz_ref[...]).astype(BF16)
            acc[...] = jnp.zeros_like(acc)

        dzb = dzb_ref[...]
        total = None
        for t0 in range(0, FB, FFN_HIDDEN_TILE):
            cols = pl.ds(t0, min(FFN_HIDDEN_TILE, FB - t0))
            dh = _dot_nt(dzb, wd_ref[cols, :])
            av = a_ref[:, cols].astype(F32)
            uv = u_ref[:, cols].astype(F32)
            s = jax.nn.sigmoid(av)
            silu = av * s
            h_ref[:, cols] = (silu * uv).astype(BF16)
            da = (dh * uv * (s * (1.0 + av * (1.0 - s)))).astype(BF16)
            du = (dh * silu).astype(BF16)
            da_ref[:, cols] = da
            du_ref[:, cols] = du
            part = _dot(da, wg_ref[cols, :]) + _dot(du, wu_ref[cols, :])
            total = part if total is None else total + part
        acc[...] += total

        @pl.when(j == nb - 1)
        def _():
            dx_ref[...] = ALPHA * dz_ref[...] + acc[...]

    row = pl.BlockSpec((tm, D), lambda i, j: (i, 0))
    w_out = pl.BlockSpec((None, FB, D), lambda i, j: (j, 0, 0))
    act = pl.BlockSpec((None, tm, FB), lambda i, j: (j, i, 0))
    act_shape = jax.ShapeDtypeStruct((nb, S, FB), BF16)
    return pl.pallas_call(
        body, name="ffn_bwd",
        grid=(S // tm, nb),
        in_specs=[row, act, act, w_out, w_out, w_out],
        out_specs=[row, act, act, act],
        out_shape=[jax.ShapeDtypeStruct((S, D), F32), act_shape, act_shape, act_shape],
        scratch_shapes=[pltpu.VMEM((tm, D), F32), pltpu.VMEM((tm, D), BF16)],
        compiler_params=_params(("parallel", "arbitrary")),
    )(dz, a, u, wg, wu, wd)


def ln_bwd(dy, xhat, rstd, gain, tm=512):
    S, D = dy.shape

    def body(dy_ref, xh_ref, rs_ref, g_ref, dz_ref, dg_ref, db_ref):
        i = pl.program_id(0)
        dy = dy_ref[...]
        xh = xh_ref[...]
        dxh = dy * g_ref[...]
        m1 = jnp.mean(dxh, axis=-1, keepdims=True)
        m2 = jnp.mean(dxh * xh, axis=-1, keepdims=True)
        dz_ref[...] = rs_ref[...] * (dxh - m1 - xh * m2)
        dg = jnp.sum(dy * xh, axis=0, keepdims=True)
        db = jnp.sum(dy, axis=0, keepdims=True)

        @pl.when(i == 0)
        def _():
            dg_ref[...] = dg
            db_ref[...] = db

        @pl.when(i > 0)
        def _():
            dg_ref[...] += dg
            db_ref[...] += db

    row = pl.BlockSpec((tm, D), lambda i: (i, 0))
    vec = pl.BlockSpec((1, D), lambda i: (0, 0))
    return pl.pallas_call(
        body, name="ln_bwd",
        grid=(S // tm,),
        in_specs=[row, row, pl.BlockSpec((tm, 1), lambda i: (i, 0)), vec],
        out_specs=[row, vec, vec],
        out_shape=[jax.ShapeDtypeStruct((S, D), F32), jax.ShapeDtypeStruct((1, D), F32), jax.ShapeDtypeStruct((1, D), F32)],
        compiler_params=_params(("arbitrary",)),
    )(dy, xhat, rstd, gain)


def loss_head(y, target, tm=512):
    S, D = y.shape

    def body(y_ref, t_ref, dy_ref, l_ref):
        i = pl.program_id(0)
        e = y_ref[...] - t_ref[...]
        dy_ref[...] = e / D
        part = 0.5 * jnp.sum(jnp.mean(e * e, axis=-1, keepdims=True), axis=0, keepdims=True)

        @pl.when(i == 0)
        def _():
            l_ref[...] = part

        @pl.when(i > 0)
        def _():
            l_ref[...] += part

    row = pl.BlockSpec((tm, D), lambda i: (i, 0))
    return pl.pallas_call(
        body, name="loss_head",
        grid=(S // tm,),
        in_specs=[row, row],
        out_specs=[row, pl.BlockSpec((1, 1), lambda i: (0, 0))],
        out_shape=[jax.ShapeDtypeStruct((S, D), F32), jax.ShapeDtypeStruct((1, 1), F32)],
        compiler_params=_params(("arbitrary",)),
    )(y, target)


def _pool_window(xp, g):
    n = xp.shape[0]
    w = xp + pltpu.roll(xp, 1, 0)
    out = w
    for level, shift in enumerate((1, 2, 4), start=1):
        w = pltpu.roll(w, shift, 0) + pltpu.roll(w, n - shift, 0)
        out = jnp.where(g >= level, w, out)
    return out


def _pool_count(S, C, g):
    half = lax.shift_left(jnp.int32(1), g)
    t = lax.broadcasted_iota(jnp.int32, (S, C), 0)
    return (jnp.minimum(t + half, S) - jnp.maximum(t - half, 0)).astype(F32)


def pool_mix(u, wgrp, scale):
    S, D = u.shape
    G, C = wgrp.shape[0], wgrp.shape[1]

    def body(u_ref, w_ref, s_ref, mix_ref, v_ref, pad):
        g = pl.program_id(0)
        zeros = jnp.zeros((POOL_PAD, C), F32)
        pad[pl.ds(0, POOL_PAD), :] = zeros
        pad[pl.ds(POOL_PAD + S, POOL_PAD), :] = zeros
        pad[pl.ds(POOL_PAD, S), :] = u_ref[...]
        win = _pool_window(pad[...], g)[POOL_PAD:POOL_PAD + S]
        mixed = (win / _pool_count(S, C, g) - u_ref[...]).astype(BF16)
        mix_ref[...] = mixed
        v_ref[...] = _dot(mixed, w_ref[...]) * s_ref[...]

    col = pl.BlockSpec((S, C), lambda g: (0, g))
    return pl.pallas_call(
        body, name="pool_mix",
        grid=(G,),
        in_specs=[col, pl.BlockSpec((None, C, C), lambda g: (g, 0, 0)), pl.BlockSpec((1, C), lambda g: (0, g))],
        out_specs=[col, col],
        out_shape=[jax.ShapeDtypeStruct((S, D), BF16), jax.ShapeDtypeStruct((S, D), F32)],
        scratch_shapes=[pltpu.VMEM((S + 2 * POOL_PAD, C), F32)],
        compiler_params=_params(("arbitrary",)),
    )(u, wgrp, scale)


def pool_mix_bwd(dv, mixed, wgrp, scale):
    S, D = dv.shape
    G, C = wgrp.shape[0], wgrp.shape[1]

    def body(dv_ref, mix_ref, w_ref, s_ref, du_ref, dw_ref, ds_ref, pad):
        g = pl.program_id(0)
        mixed = mix_ref[...]
        dv = dv_ref[...]
        yg = _dot(mixed, w_ref[...])
        ds_ref[...] = jnp.sum(dv * yg, axis=0, keepdims=True)
        dyg = (dv * s_ref[...]).astype(BF16)
        dw_ref[...] = _dot_tn(mixed, dyg).astype(BF16)
        dmix = _dot_nt(dyg, w_ref[...])
        zeros = jnp.zeros((POOL_PAD, C), F32)
        pad[pl.ds(0, POOL_PAD), :] = zeros
        pad[pl.ds(POOL_PAD + S, POOL_PAD), :] = zeros
        pad[pl.ds(POOL_PAD, S), :] = dmix / _pool_count(S, C, g)
        win = _pool_window(pad[...], g)
        win = pltpu.roll(win, win.shape[0] - 1, 0)[POOL_PAD:POOL_PAD + S]
        du_ref[...] = win - dmix

    col = pl.BlockSpec((S, C), lambda g: (0, g))
    return pl.pallas_call(
        body, name="pool_mix_bwd",
        grid=(G,),
        in_specs=[col, col, pl.BlockSpec((None, C, C), lambda g: (g, 0, 0)), pl.BlockSpec((1, C), lambda g: (0, g))],
        out_specs=[col, pl.BlockSpec((None, C, C), lambda g: (g, 0, 0)), pl.BlockSpec((1, C), lambda g: (0, g))],
        out_shape=[jax.ShapeDtypeStruct((S, D), F32), jax.ShapeDtypeStruct((G, C, C), BF16), jax.ShapeDtypeStruct((1, D), F32)],
        scratch_shapes=[pltpu.VMEM((S + 2 * POOL_PAD, C), F32)],
        compiler_params=_params(("arbitrary",)),
    )(dv, mixed, wgrp, scale)


PERM_BLOCK = 256


def _dilated_runs(rows, d):
    n = PERM_BLOCK // d
    return [(c * PERM_BLOCK, r, n) for c in range(rows // PERM_BLOCK) for r in range(d)]


def _rows_to_dilated(src_ref, dst_ref, rows, d):
    for base, r, n in _dilated_runs(rows, d):
        dst_ref[pl.ds(base + r * n, n), :] = src_ref[pl.ds(base + r, n, stride=d), :].astype(dst_ref.dtype)


def _rows_from_dilated(src_ref, dst_ref, rows, d, accumulate=False):
    for base, r, n in _dilated_runs(rows, d):
        at = pl.ds(base + r, n, stride=d)
        v = src_ref[pl.ds(base + r * n, n), :]
        dst_ref[at, :] = dst_ref[at, :] + v if accumulate else v


def dilate_rows(x, dils, out_dtype, tm=1024):
    S, D = x.shape

    def body(x_ref, *o_refs):
        for d, o_ref in zip(dils, o_refs):
            _rows_to_dilated(x_ref, o_ref, tm, d)

    tile = pl.BlockSpec((tm, LANES), lambda i, j: (i, j))
    return pl.pallas_call(
        body, name="dilate_rows",
        grid=(S // tm, D // LANES),
        in_specs=[tile],
        out_specs=[tile] * len(dils),
        out_shape=[jax.ShapeDtypeStruct((S, D), out_dtype)] * len(dils),
        compiler_params=_params(("parallel", "parallel")),
    )(x)


def _slope_table(group, dilation):
    s = _alibi_slopes()[group].reshape(N_HEADS // 2, 2, 1, 1) * float(dilation)
    return jnp.asarray(np.broadcast_to(s, (N_HEADS // 2, 2, 1, ATTN_W)).copy())


def _residue_shape(S, D, d):
    return (S, D) if d == 1 else (S // PERM_BLOCK, d, PERM_BLOCK // d, D)


def _residue_view(x, d):
    return x.reshape(x.shape[:-2] + _residue_shape(x.shape[-2], x.shape[-1], d))


def _residue_spec(lead_block, lead_index, S, d):
    if d == 1:
        return pl.BlockSpec(lead_block + (S, LANES), lambda hp, r: lead_index + (0, hp))
    return pl.BlockSpec(lead_block + (S // PERM_BLOCK, None, PERM_BLOCK // d, LANES), lambda hp, r: lead_index + (0, r, 0, hp))


def _whole(ref, lead, L):
    return ref[lead + (slice(None),) * (len(ref.shape) - len(lead))].reshape(L, LANES)


def _query_rows(lead, i, d):
    if d == 1:
        return lead + (pl.ds(pl.multiple_of(i * ATTN_BQ, ATTN_BQ), ATTN_BQ), slice(None)), (ATTN_BQ, LANES)
    n = PERM_BLOCK // d
    return lead + (pl.ds(i * (ATTN_BQ // n), ATTN_BQ // n), slice(None), slice(None)), (ATTN_BQ // n, n, LANES)


def _load_query_rows(ref, lead, i, d):
    at, _ = _query_rows(lead, i, d)
    return ref[at].reshape(ATTN_BQ, LANES)


def _store_query_rows(ref, lead, i, d, value):
    at, shape = _query_rows(lead, i, d)
    ref[at] = value.reshape(shape)


def _stage_keys(dst, src_ref, L):
    rows = _whole(src_ref, (), L)
    lane = lax.broadcasted_iota(jnp.int32, (L, LANES), 1)
    zeros = jnp.zeros((ATTN_R, LANES), dst.dtype)
    for h in range(2):
        mine = (lane < HEAD_DIM) if h == 0 else (lane >= HEAD_DIM)
        dst[h, pl.ds(0, ATTN_R), :] = zeros
        dst[h, pl.ds(ATTN_R + L, ATTN_R), :] = zeros
        dst[h, pl.ds(ATTN_R, L), :] = jnp.where(mine, rows, jnp.zeros_like(rows))


def _fill_bias(bias, sl_ref):
    a = lax.broadcasted_iota(jnp.int32, (ATTN_BQ, ATTN_W), 0)
    c = lax.broadcasted_iota(jnp.int32, (ATTN_BQ, ATTN_W), 1)
    rel = jnp.abs(c - ATTN_R - a)
    band = rel <= ATTN_R
    after_start = c >= ATTN_R
    before_end = c < ATTN_BQ + ATTN_R
    for h in range(2):
        base = -(sl_ref[h] * rel.astype(F32))
        for variant in range(4):
            ok = band
            if variant & 1:
                ok = ok & after_start
            if variant & 2:
                ok = ok & before_end
            bias[variant, h] = jnp.where(ok, base, MASK_VALUE)


def _bias_variant(i, nq):
    return jnp.where(i == 0, 1, 0) + jnp.where(i == nq - 1, 2, 0)


def attn_fwd(qkv, group, dilation):
    _, S, D = qkv.shape
    d = dilation
    L = S // d
    nq = L // ATTN_BQ
    ncol = D // LANES
    view = _residue_view(qkv, d)
    slopes = _slope_table(group, d)
    scale = HEAD_DIM ** -0.5

    def body(q_ref, k_ref, v_ref, sl_ref, o_ref, lse_ref, k2, v2, bias):
        @pl.when(pl.program_id(1) == 0)
        def _():
            _fill_bias(bias, sl_ref)

        _stage_keys(k2, k_ref, L)
        _stage_keys(v2, v_ref, L)
        head0 = lax.broadcasted_iota(jnp.int32, (ATTN_BQ, LANES), 1) < HEAD_DIM

        def step(i, carry):
            variant = _bias_variant(i, nq)
            win = pl.ds(pl.multiple_of(i * ATTN_BQ, ATTN_BQ), ATTN_W)
            qs = _load_query_rows(q_ref, (), i, d) * jnp.asarray(scale, BF16)
            acc, ms, ls = None, [], []
            for h in range(2):
                s = _dot_nt(qs, k2[h, win, :]) + bias[variant, h]
                m = jnp.max(s, axis=-1, keepdims=True)
                e = jnp.exp(s - m)
                ls.append(jnp.sum(e, axis=-1, keepdims=True))
                ms.append(m)
                part = _dot(e.astype(BF16), v2[h, win, :])
                acc = part if acc is None else acc + part
            out = acc * jnp.where(head0, 1.0 / ls[0], 1.0 / ls[1])
            lse = jnp.where(head0, ms[0] + jnp.log(ls[0]), ms[1] + jnp.log(ls[1]))
            _store_query_rows(o_ref, (), i, d, out)
            _store_query_rows(lse_ref, (), i, d, lse)
            return carry

        lax.fori_loop(0, nq, step, 0, unroll=min(ATTN_UNROLL, nq))

    def col(which):
        return _residue_spec((None,), (which,), S, d)

    out = _residue_spec((), (), S, d)
    o, lse = pl.pallas_call(
        body, name=f"attn_fwd_g{group}",
        grid=(ncol, d),
        in_specs=[col(0), col(1), col(2), pl.BlockSpec((None, 2, 1, ATTN_W), lambda hp, r: (hp, 0, 0, 0))],
        out_specs=[out, out],
        out_shape=[jax.ShapeDtypeStruct(_residue_shape(S, D, d), F32)] * 2,
        scratch_shapes=[pltpu.VMEM((2, L + 2 * ATTN_R, LANES), BF16), pltpu.VMEM((2, L + 2 * ATTN_R, LANES), BF16),
                        pltpu.VMEM((4, 2, ATTN_BQ, ATTN_W), F32)],
        compiler_params=_params(("arbitrary", "arbitrary")),
    )(view, view, view, slopes)
    return o.reshape(S, D), lse.reshape(S, D)


def attn_combine(os, lses, dils, tm=1024):
    S, D = os[0].shape
    n = len(os)
    n_moved = sum(d > 1 for d in dils)

    def body(*refs):
        o_refs, l_refs, out_ref, lse_ref = list(refs[:n]), list(refs[n:2 * n]), refs[2 * n], refs[2 * n + 1]
        spare = list(refs[2 * n + 2:])
        for g, d in enumerate(dils):
            if d > 1:
                for which in (o_refs, l_refs):
                    token_order = spare.pop()
                    _rows_from_dilated(which[g], token_order, tm, d)
                    which[g] = token_order
        ls = [r[...] for r in l_refs]
        m = functools.reduce(jnp.maximum, ls)
        es = [jnp.exp(l - m) for l in ls]
        tot = functools.reduce(lambda x, y: x + y, es)
        inv = 1.0 / tot
        out_ref[...] = functools.reduce(lambda x, y: x + y, [(e * inv) * r[...] for e, r in zip(es, o_refs)])
        lse_ref[...] = m + jnp.log(tot)

    tile = pl.BlockSpec((tm, LANES), lambda i, j: (i, j))
    return pl.pallas_call(
        body, name="attn_combine",
        grid=(S // tm, D // LANES),
        in_specs=[tile] * (2 * n),
        out_specs=[tile, tile],
        out_shape=[jax.ShapeDtypeStruct((S, D), F32), jax.ShapeDtypeStruct((S, D), F32)],
        scratch_shapes=[pltpu.VMEM((tm, LANES), F32)] * (2 * n_moved),
        compiler_params=_params(("parallel", "parallel")),
    )(*os, *lses)


def attn_bwd(qkv, do, o, lse, group, dilation):
    _, S, D = qkv.shape
    d = dilation
    L = S // d
    nq = L // ATTN_BQ
    ncol = D // LANES
    view = _residue_view(qkv, d)
    slopes = _slope_table(group, d)
    scale = HEAD_DIM ** -0.5

    def body(q_ref, k_ref, v_ref, do_ref, o_ref, lse_ref, sl_ref, dx_ref, k2, v2, dkacc, dvacc, bias):
        @pl.when(pl.program_id(1) == 0)
        def _():
            _fill_bias(bias, sl_ref)

        _stage_keys(k2, k_ref, L)
        _stage_keys(v2, v_ref, L)
        dkacc[...] = jnp.zeros_like(dkacc)
        dvacc[...] = jnp.zeros_like(dvacc)
        lane = lax.broadcasted_iota(jnp.int32, (ATTN_BQ, LANES), 1)
        heads = (lane < HEAD_DIM, lane >= HEAD_DIM)
        key_head0 = lax.broadcasted_iota(jnp.int32, (ATTN_W, LANES), 1) < HEAD_DIM

        def step(i, carry):
            variant = _bias_variant(i, nq)
            win = pl.ds(pl.multiple_of(i * ATTN_BQ, ATTN_BQ), ATTN_W)
            q = _load_query_rows(q_ref, (), i, d)
            qs = q * jnp.asarray(scale, BF16)
            dov = _load_query_rows(do_ref, (), i, d)
            prod = dov * _load_query_rows(o_ref, (), i, d)
            lse_v = _load_query_rows(lse_ref, (), i, d)
            dob = dov.astype(BF16)
            dq, dks, dvs = None, [], []
            for h in range(2):
                s = _dot_nt(qs, k2[h, win, :]) + bias[variant, h]
                lse_h = jnp.max(jnp.where(heads[h], lse_v, -jnp.inf), axis=-1, keepdims=True)
                dterm = jnp.sum(jnp.where(heads[h], prod, 0.0), axis=-1, keepdims=True)
                p = jnp.exp(s - lse_h)
                dp = _dot_nt(dob, v2[h, win, :])
                ds = (p * (dp - dterm) * scale).astype(BF16)
                dvs.append(_dot_tn(p.astype(BF16), dob))
                dks.append(_dot_tn(ds, q))
                part = _dot(ds, k2[h, win, :])
                dq = part if dq is None else dq + part
            dvacc[win, :] += jnp.where(key_head0, dvs[0], dvs[1])
            dkacc[win, :] += jnp.where(key_head0, dks[0], dks[1])
            _store_query_rows(dx_ref, (0,), i, d, dq.astype(BF16))
            return carry

        lax.fori_loop(0, nq, step, 0, unroll=min(ATTN_UNROLL, nq))
        block_shape = dx_ref.shape[1:]
        dx_ref[1] = dkacc[pl.ds(ATTN_R, L), :].astype(BF16).reshape(block_shape)
        dx_ref[2] = dvacc[pl.ds(ATTN_R, L), :].astype(BF16).reshape(block_shape)

    def col(which):
        return _residue_spec((None,), (which,), S, d)

    act = _residue_spec((), (), S, d)
    out = pl.pallas_call(
        body, name=f"attn_bwd_g{group}",
        grid=(ncol, d),
        in_specs=[col(0), col(1), col(2), act, act, act, pl.BlockSpec((None, 2, 1, ATTN_W), lambda hp, r: (hp, 0, 0, 0))],
        out_specs=_residue_spec((3,), (0,), S, d),
        out_shape=jax.ShapeDtypeStruct((3,) + _residue_shape(S, D, d), BF16),
        scratch_shapes=[pltpu.VMEM((2, L + 2 * ATTN_R, LANES), BF16), pltpu.VMEM((2, L + 2 * ATTN_R, LANES), BF16),
                        pltpu.VMEM((L + 2 * ATTN_R, LANES), F32), pltpu.VMEM((L + 2 * ATTN_R, LANES), F32),
                        pltpu.VMEM((4, 2, ATTN_BQ, ATTN_W), F32)],
        compiler_params=_params(("arbitrary", "arbitrary")),
    )(view, view, view, _residue_view(do, d), _residue_view(o, d), _residue_view(lse, d), slopes)
    return out.reshape(3, S, D)


TILE_ELEMS = 256 * 1024


def _row_tile(R, C):
    if R * C <= TILE_ELEMS or R % 16:
        return R
    return max(t for t in range(16, R + 1, 16) if R % t == 0 and (t * C <= TILE_ELEMS or t == 16))


def pair_sum(core, g, recv):
    _, _, R, C = g.shape
    tr = _row_tile(R, C)

    def body(c_ref, g_ref, r_ref, o_ref):
        o_ref[...] = (g_ref[...].astype(F32) + r_ref[...].astype(F32)).astype(BF16)

    blk = pl.BlockSpec((None, tr, C), lambda d, i, c_ref: (d, i, 0))
    return pl.pallas_call(
        body, name="pair_sum",
        grid_spec=pltpu.PrefetchScalarGridSpec(
            num_scalar_prefetch=1, grid=(N_CHIPS, R // tr),
            in_specs=[pl.BlockSpec((None, None, tr, C), lambda d, i, c_ref: (d, c_ref[0], i, 0)), blk],
            out_specs=blk),
        out_shape=jax.ShapeDtypeStruct((N_CHIPS, R, C), BF16),
        compiler_params=_params(("parallel", "parallel")),
    )(core, g, recv)


def chip_sum(chip, own, recv):
    _, R, C = own.shape
    tr = _row_tile(R, C)
    slot_of_relation = {2: 0, 1: 1, 3: 2}

    def body(chip_ref, own_ref, r_ref, o_ref):
        me = chip_ref[0]
        mine = own_ref[...].astype(F32)
        theirs = {rel: r_ref[k].astype(F32) for rel, k in slot_of_relation.items()}
        acc = None
        for s in range(N_CHIPS):
            rel = jnp.bitwise_xor(me, s)
            part = jnp.where(rel == 0, mine, jnp.where(rel == 2, theirs[2], jnp.where(rel == 1, theirs[1], theirs[3])))
            acc = part if acc is None else acc + part
        o_ref[...] = acc

    return pl.pallas_call(
        body, name="chip_sum",
        grid_spec=pltpu.PrefetchScalarGridSpec(
            num_scalar_prefetch=1, grid=(R // tr,),
            in_specs=[pl.BlockSpec((None, tr, C), lambda i, chip_ref: (chip_ref[0], i, 0)),
                      pl.BlockSpec((N_CHIPS - 1, tr, C), lambda i, chip_ref: (0, i, 0))],
            out_specs=pl.BlockSpec((tr, C), lambda i, chip_ref: (i, 0))),
        out_shape=jax.ShapeDtypeStruct((R, C), F32),
        compiler_params=_params(("parallel",)),
    )(chip, own, recv)


def adamw(core, w, g_pairs, m, v):
    L, H, R, C = w.shape
    tr = _row_tile(R, C)

    def body(c_ref, w_ref, *rest):
        g_refs = rest[:2 * L]
        m_ref, v_ref, g_ref, d_ref, nm_ref, nv_ref = rest[2 * L:]
        mine = pl.program_id(1) == c_ref[0]
        g = None
        for l in range(L):
            g_l = jnp.where(mine, g_refs[2 * l][...], g_refs[2 * l + 1][...])
            g = g_l if g is None else jnp.where(pl.program_id(0) == l, g_l, g)
        m = ADAM_B1 * m_ref[...] + (1.0 - ADAM_B1) * g
        v = ADAM_B2 * v_ref[...] + (1.0 - ADAM_B2) * (g * g)
        m_hat = m / (1.0 - ADAM_B1 ** ADAM_STEP)
        v_hat = v / (1.0 - ADAM_B2 ** ADAM_STEP)
        g_ref[...] = g
        d_ref[...] = -ADAM_LR * (m_hat / (jnp.sqrt(v_hat) + ADAM_EPS) + ADAM_WD * w_ref[...])
        nm_ref[...] = m
        nv_ref[...] = v

    blk = pl.BlockSpec((None, None, tr, C), lambda l, h, i, c_ref: (l, h, i, 0))

    def half(layer):
        return pl.BlockSpec((tr, C), lambda l, h, i, c_ref: (jnp.where(l == layer, i, 0), 0))

    shape = jax.ShapeDtypeStruct((L, H, R, C), F32)
    return pl.pallas_call(
        body, name="adamw",
        grid_spec=pltpu.PrefetchScalarGridSpec(
            num_scalar_prefetch=1, grid=(L, H, R // tr),
            in_specs=[blk] + [half(l) for l in range(L) for _ in range(2)] + [blk, blk],
            out_specs=[blk] * 4),
        out_shape=[shape] * 4,
        compiler_params=_params(("parallel", "parallel", "parallel")),
    )(core, w, *[g for pair in g_pairs for g in pair], m, v)


def _place():
    return lax.axis_index("x"), lax.axis_index("y"), lax.axis_index("c")


def _other_chips(x, y):
    return [(2 * (1 - x) + y, (1 - x, y)), (2 * x + (1 - y), (x, 1 - y)), (2 * (1 - x) + (1 - y), (1 - x, 1 - y))]


def all_gather_shards(shards, placed):
    n = len(shards)

    def body(*refs):
        ins, outs = refs[:n], refs[2 * n:3 * n]
        send_sems, recv_sems = refs[3 * n:]
        x, y, c = _place()
        me = 2 * x + y
        sibling = (x, y, 1 - c)
        chips = _other_chips(x, y)

        def copy(a, k, src, dst, to):
            return pltpu.make_async_remote_copy(src_ref=src, dst_ref=dst, send_sem=send_sems.at[a, k], recv_sem=recv_sems.at[a, k],
                                                device_id=to, device_id_type=MESH)

        sends = []
        for a in range(n):
            for k, (_, (px, py)) in enumerate(chips):
                cp = copy(a, k, ins[a].at[c], outs[a].at[me, c], (px, py, c))
                cp.start()
                sends.append(cp)
        for a in range(n):
            for k, (chip, _) in enumerate(chips):
                landed = outs[a].at[chip, c]
                copy(a, k, landed, landed, sibling).wait_recv()
                cp = copy(a, 3 + k, landed, landed, sibling)
                cp.start()
                sends.append(cp)
        for a in range(n):
            for k, (chip, _) in enumerate(chips):
                other = outs[a].at[chip, 1 - c]
                copy(a, 3 + k, other, other, sibling).wait_recv()
        for cp in sends:
            cp.wait_send()

    return pl.pallas_call(
        body, name="all_gather_shards",
        in_specs=[ANY] * (2 * n),
        out_specs=[ANY] * n,
        out_shape=[jax.ShapeDtypeStruct(p.shape, p.dtype) for p in placed],
        scratch_shapes=[pltpu.SemaphoreType.DMA((n, 6)), pltpu.SemaphoreType.DMA((n, 6))],
        input_output_aliases={n + a: a for a in range(n)},
        compiler_params=pltpu.CompilerParams(has_side_effects=True),
    )(*shards, *placed)


def all_gather_shards_async(shards, placed, collective_id, name):
    n = len(shards)
    srcs = [jax.new_ref(s, memory_space=pltpu.MemorySpace.HBM) for s in shards]
    dsts = [jax.new_ref(p, memory_space=pltpu.MemorySpace.HBM) for p in placed]

    @pl.kernel(mesh=plsc.ScalarSubcoreMesh(axis_name="sequencer", num_cores=1), name=name,
               scratch_types=(pltpu.SemaphoreType.DMA((n, 6)), pltpu.SemaphoreType.DMA((n, 6))),
               compiler_params=pltpu.CompilerParams(collective_id=collective_id))
    def launch(send_sems, recv_sems):
        x, y, c = _place()
        me = 2 * x + y
        sibling = (x, y, 1 - c)
        chips = _other_chips(x, y)
        barrier = pltpu.get_barrier_semaphore()
        peers = [sibling] + [(px, py, c) for _, (px, py) in chips]
        for peer in peers:
            pl.semaphore_signal(barrier, inc=1, device_id=peer, device_id_type=MESH)
        pl.semaphore_wait(barrier, len(peers))

        def copy(a, k, src, dst, to):
            return pltpu.make_async_remote_copy(src_ref=src, dst_ref=dst, send_sem=send_sems.at[a, k], recv_sem=recv_sems.at[a, k],
                                                device_id=to, device_id_type=MESH)

        sends = []
        for a in range(n):
            for k, (_, (px, py)) in enumerate(chips):
                cp = copy(a, k, srcs[a].at[c], dsts[a].at[me, c], (px, py, c))
                cp.start()
                sends.append(cp)
        for a in range(n):
            for k, (chip, _) in enumerate(chips):
                landed = dsts[a].at[chip, c]
                copy(a, k, landed, landed, sibling).wait_recv()
                cp = copy(a, 3 + k, landed, landed, sibling)
                cp.start()
                sends.append(cp)
        for a in range(n):
            for k, (chip, _) in enumerate(chips):
                other = dsts[a].at[chip, 1 - c]
                copy(a, 3 + k, other, other, sibling).wait_recv()
        for cp in sends:
            cp.wait_send()

    launch()
    return [d[...] for d in dsts]


def _exchange(body, ins, out_shapes, sem_shapes, name, peers=None, collective_id=None):
    n_in, n_out = len(ins), len(out_shapes)
    sems = [pltpu.SemaphoreType.DMA(shape) for shape in sem_shapes]
    if collective_id is None:
        def tc_body(*refs):
            body(refs[:n_in], refs[n_in:n_in + n_out], *refs[n_in + n_out:])

        return pl.pallas_call(tc_body, name=name, in_specs=[ANY] * n_in, out_specs=[ANY] * n_out, out_shape=out_shapes,
                              scratch_shapes=sems, compiler_params=pltpu.CompilerParams(has_side_effects=True))(*ins)
    srcs = [jax.new_ref(a, memory_space=pltpu.MemorySpace.HBM) for a in ins]
    dsts = [jax.empty_ref(shape, memory_space=pltpu.MemorySpace.HBM) for shape in out_shapes]

    @pl.kernel(mesh=plsc.ScalarSubcoreMesh(axis_name="sequencer", num_cores=1), name=name, scratch_types=tuple(sems),
               compiler_params=pltpu.CompilerParams(collective_id=collective_id))
    def launch(*sem_refs):
        barrier = pltpu.get_barrier_semaphore()
        others = peers(*_place())
        for peer in others:
            pl.semaphore_signal(barrier, inc=1, device_id=peer, device_id_type=MESH)
        pl.semaphore_wait(barrier, len(others))
        body(srcs, dsts, *sem_refs)

    launch()
    return [d[...] for d in dsts]


def _sibling(x, y, c):
    return [(x, y, 1 - c)]


def _same_core_of_other_chips(x, y, c):
    return [(px, py, c) for _, (px, py) in _other_chips(x, y)]


def sibling_exchange_halves(grads, name="sibling_exchange_halves", collective_id=None):
    n = len(grads)

    def body(ins, outs, send_sems, recv_sems):
        x, y, c = _place()
        copies = [pltpu.make_async_remote_copy(src_ref=ins[a].at[:, 1 - c], dst_ref=outs[a], send_sem=send_sems.at[a],
                                               recv_sem=recv_sems.at[a], device_id=(x, y, 1 - c), device_id_type=MESH) for a in range(n)]
        for cp in copies:
            cp.start()
        for cp in copies:
            cp.wait()

    shapes = [jax.ShapeDtypeStruct((N_CHIPS,) + g.shape[2:], g.dtype) for g in grads]
    return _exchange(body, grads, shapes, [(n,), (n,)], name, _sibling, collective_id)


def chip_exchange(sums, name="chip_exchange", collective_id=None):
    n = len(sums)

    def body(ins, outs, send_sems, recv_sems):
        x, y, c = _place()
        copies = []
        for a in range(n):
            for k, (chip, (px, py)) in enumerate(_other_chips(x, y)):
                cp = pltpu.make_async_remote_copy(src_ref=ins[a].at[chip], dst_ref=outs[a].at[k], send_sem=send_sems.at[a, k],
                                                  recv_sem=recv_sems.at[a, k], device_id=(px, py, c), device_id_type=MESH)
                cp.start()
                copies.append(cp)
        for cp in copies:
            cp.wait()

    shapes = [jax.ShapeDtypeStruct((N_CHIPS - 1,) + s.shape[1:], s.dtype) for s in sums]
    return _exchange(body, sums, shapes, [(n, 3), (n, 3)], name, _same_core_of_other_chips, collective_id)


def sibling_share(halves, name="sibling_share", collective_id=None):
    n = len(halves)

    def body(ins, outs, send_sems, recv_sems):
        x, y, c = _place()
        copies = [pltpu.make_async_remote_copy(src_ref=ins[a], dst_ref=outs[a], send_sem=send_sems.at[a], recv_sem=recv_sems.at[a],
                                               device_id=(x, y, 1 - c), device_id_type=MESH) for a in range(n)]
        for cp in copies:
            cp.start()
        for cp in copies:
            cp.wait()

    shapes = [jax.ShapeDtypeStruct(h.shape, h.dtype) for h in halves]
    return _exchange(body, halves, shapes, [(n,), (n,)], name, _sibling, collective_id)


def all_reduce_small(v):
    R, C = v.shape
    n_dev = 8

    def body(v_ref, o_ref, buf, send_sems, recv_sems):
        x, y, c = _place()
        me = 4 * x + 2 * y + c
        buf[me] = v_ref[...]
        copies = []
        for rel in range(1, n_dev):
            fx, fy, fc = rel >> 2, (rel >> 1) & 1, rel & 1
            peer = (x ^ fx, y ^ fy, c ^ fc)
            cp = pltpu.make_async_remote_copy(src_ref=v_ref, dst_ref=buf.at[me], send_sem=send_sems.at[rel - 1],
                                              recv_sem=recv_sems.at[rel - 1], device_id=peer, device_id_type=MESH)
            cp.start()
            copies.append(cp)
        for cp in copies:
            cp.wait()
        acc = buf[0]
        for k in range(1, n_dev):
            acc = acc + buf[k]
        o_ref[...] = acc

    return pl.pallas_call(
        body, name="all_reduce_small",
        in_specs=[pl.BlockSpec(memory_space=pltpu.VMEM)],
        out_specs=pl.BlockSpec(memory_space=pltpu.VMEM),
        out_shape=jax.ShapeDtypeStruct((R, C), F32),
        scratch_shapes=[pltpu.VMEM((n_dev, R, C), F32), pltpu.SemaphoreType.DMA((n_dev - 1,)), pltpu.SemaphoreType.DMA((n_dev - 1,))],
        compiler_params=pltpu.CompilerParams(has_side_effects=True),
    )(v)


WEIGHT_NAMES = ("ffn1_w_gate", "ffn1_w_up", "ffn1_w_down", "ffn2_w_gate", "ffn2_w_up", "ffn2_w_down", "ln_gain", "ln_bias",
                "pool_w_in", "pool_w_group", "pool_scale", "pool_w_out", "attn_w_qkv", "attn_w_out")
MATRIX_NAMES = ("ffn1_w_gate", "ffn1_w_up", "ffn1_w_down", "ffn2_w_gate", "ffn2_w_up", "ffn2_w_down",
                "pool_w_in", "pool_w_group", "pool_w_out", "attn_w_qkv", "attn_w_out")


TRANSPOSED_NAMES = ("ffn1_w_gate", "ffn1_w_up", "ffn2_w_gate", "ffn2_w_up")


def _halves(name, w):
    if name in TRANSPOSED_NAMES:
        w = jnp.swapaxes(w, 1, 2)
    return w.reshape(2, -1, w.shape[-1])


def _unhalves(name, t, shape):
    if name in TRANSPOSED_NAMES:
        return jnp.swapaxes(t.reshape(shape[0], shape[2], shape[1]), 1, 2)
    return t.reshape(shape)


def kernel(x, ffn1_w_gate, ffn1_w_up, ffn1_w_down, ffn2_w_gate, ffn2_w_up, ffn2_w_down, ln_gain, ln_bias, pool_w_in, pool_w_group, pool_scale, pool_w_out, attn_w_qkv, attn_w_out, loss_target, m_ffn1_w_gate, m_ffn1_w_up, m_ffn1_w_down, m_ffn2_w_gate, m_ffn2_w_up, m_ffn2_w_down, m_ln_gain, m_ln_bias, m_pool_w_in, m_pool_w_group, m_pool_scale, m_pool_w_out, m_attn_w_qkv, m_attn_w_out, v_ffn1_w_gate, v_ffn1_w_up, v_ffn1_w_down, v_ffn2_w_gate, v_ffn2_w_up, v_ffn2_w_down, v_ln_gain, v_ln_bias, v_pool_w_in, v_pool_w_group, v_pool_scale, v_pool_w_out, v_attn_w_qkv, v_attn_w_out):
    weights = dict(zip(WEIGHT_NAMES, (ffn1_w_gate, ffn1_w_up, ffn1_w_down, ffn2_w_gate, ffn2_w_up, ffn2_w_down, ln_gain, ln_bias,
                                      pool_w_in, pool_w_group, pool_scale, pool_w_out, attn_w_qkv, attn_w_out)))
    moms = dict(zip(WEIGHT_NAMES, (m_ffn1_w_gate, m_ffn1_w_up, m_ffn1_w_down, m_ffn2_w_gate, m_ffn2_w_up, m_ffn2_w_down, m_ln_gain,
                                   m_ln_bias, m_pool_w_in, m_pool_w_group, m_pool_scale, m_pool_w_out, m_attn_w_qkv, m_attn_w_out)))
    vels = dict(zip(WEIGHT_NAMES, (v_ffn1_w_gate, v_ffn1_w_up, v_ffn1_w_down, v_ffn2_w_gate, v_ffn2_w_up, v_ffn2_w_down, v_ln_gain,
                                   v_ln_bias, v_pool_w_in, v_pool_w_group, v_pool_scale, v_pool_w_out, v_attn_w_qkv, v_attn_w_out)))
    S, D = x.shape[1], x.shape[2]
    FB = ffn1_w_gate.shape[2]
    QKV = attn_w_qkv.shape[2] * N_CHIPS
    G, CB = pool_w_group.shape[1], pool_w_group.shape[2]
    C = pool_w_group.shape[3]
    cx, cy, cc = _place()
    chip = 2 * cx + cy
    xs = x.reshape(S, D)
    target = loss_target.reshape(S, D)

    ln_rows = jnp.concatenate([ln_gain, ln_bias, jnp.zeros((DEPTH, 2, ln_gain.shape[2]), F32)], axis=1)
    shard = {n: _halves(n, weights[n]).astype(BF16) for n in MATRIX_NAMES}
    ffn_layer = lambda f, i: {f"{f}{s}@{i}": shard[f + s][i].reshape(2, FB // 2, D) for s in ("_w_gate", "_w_up", "_w_down")}
    pool = {n: shard[n] for n in ("pool_w_in", "pool_w_group", "pool_w_out")}
    attn = {n: shard[n] for n in ("attn_w_qkv", "attn_w_out")}
    groups = [dict(ffn_layer("ffn1", 0), ln=ln_rows), pool, ffn_layer("ffn2", 0), ffn_layer("ffn1", 1), attn, ffn_layer("ffn2", 1)]
    full = {}

    def launch(k, after=None):
        parts = list(groups[k].values())
        if after is not None:
            *parts, after = lax.optimization_barrier((*parts, after))
        placed = [lax.dynamic_update_slice(lax.empty((N_CHIPS,) + s.shape, s.dtype), s[None], (chip, 0, 0, 0)) for s in parts]
        full.update(zip(groups[k], all_gather_shards_async(parts, placed, GATHER_COLLECTIVE_ID, f"gather_weights_{k}")))
        return after

    for k in range(4):
        launch(k)
    ffn_w = lambda f, i: [full[f"{f}{s}@{i}"].reshape(N_CHIPS, FB, D) for s in ("_w_gate", "_w_up", "_w_down")]
    w_pool_in = full["pool_w_in"].reshape(D, D)
    w_pool_out = full["pool_w_out"].reshape(D, D)
    w_group = full["pool_w_group"].reshape(N_CHIPS, G, CB, C).transpose(1, 0, 2, 3).reshape(G, N_CHIPS * CB, C)
    ln_full = full["ln"].transpose(1, 2, 0, 3).reshape(DEPTH, 8, D)
    gain = lambda i, k: ln_full[i, k].reshape(1, D)
    bias = lambda i, k: ln_full[i, 3 + k].reshape(1, D)

    dils = [dil for _, dil in DIL_CONFIGS]
    moved_dils = [dil for dil in dils if dil > 1]

    saved = []
    y = xs
    for i in range(DEPTH):
        y_in = y
        y, xh, rs, a, u = ffn_fwd(y_in, *ffn_w("ffn1", i), gain(i, 0), bias(i, 0))
        f1 = (y_in, xh, rs, a, u)
        if i == 0:
            y = launch(4, after=y)
        y_mid = y
        if i % 2 == 0:
            pu = mm_nn(y_mid, w_pool_in, F32)[0]
            mixed, pv = pool_mix(pu, w_group, pool_scale)
            y, xh, rs = proj_ln(pv, w_pool_out, y_mid, gain(i, 1), bias(i, 1))
            mix = (y_mid, xh, rs, mixed, pv)
            if i == 0:
                y = launch(5, after=y)
        else:
            qkv_blocks, y_mid = lax.optimization_barrier((full["attn_w_qkv"], y_mid))
            w_qkv = chip_blocks_to_columns(qkv_blocks.reshape(N_CHIPS, D, QKV // N_CHIPS))
            w_attn_out = full["attn_w_out"].reshape(D, D)
            moved = dict(zip(moved_dils, dilate_rows(y_mid, moved_dils, BF16)))
            srcs = [moved.get(dil, y_mid) for dil in dils]
            qkvs = [mm_nn(src, w_qkv, BF16, first_block=3 * g, nb=3) for g, src in enumerate(srcs)]
            parts = [attn_fwd(qkv, g, dil) for g, (qkv, dil) in enumerate(zip(qkvs, dils))]
            ao, lse = attn_combine([p[0] for p in parts], [p[1] for p in parts], dils)
            y, xh, rs = proj_ln(ao, w_attn_out, y_mid, gain(i, 1), bias(i, 1))
            mix = (y_mid, xh, rs, srcs, qkvs, ao, lse, w_qkv, w_attn_out)
        y_in2 = y
        y, xh, rs, a, u = ffn_fwd(y_in2, *ffn_w("ffn2", i), gain(i, 2), bias(i, 2))
        f2 = (y_in2, xh, rs, a, u)
        saved.append((f1, mix, f2))

    dy, loss_part = loss_head(y, target)
    loss = lax.psum(loss_part[0, 0], ("x", "y", "c"))

    core = cc.reshape(1).astype(jnp.int32)
    chip_id = chip.reshape(1).astype(jnp.int32)
    dgain = [[None] * 3 for _ in range(DEPTH)]
    dbias = [[None] * 3 for _ in range(DEPTH)]
    dscale = None
    pieces = []
    own_half, other_half = {}, {}

    def tie(arrays, after):
        *arrays, after = lax.optimization_barrier((*arrays, after))
        return arrays, after

    def start_piece(keys, arrays):
        blocks = [g.reshape(N_CHIPS, 2, -1, g.shape[-1]) for g in arrays]
        k = len(pieces)
        pieces.append(dict(keys=keys, blocks=blocks, from_sibling=sibling_exchange_halves(
            blocks, name=f"reduce_halves_{k}", collective_id=SIBLING_COLLECTIVE_ID)))

    def pair_sums_and_chip_exchange(k, after):
        piece = pieces[k]
        received, after = tie(piece["from_sibling"], after)
        piece["pair_sums"] = [pair_sum(core, b, r) for b, r in zip(piece["blocks"], received)]
        piece["from_chips"] = chip_exchange(piece["pair_sums"], name=f"reduce_chips_{k}", collective_id=CHIPS_COLLECTIVE_ID)
        return after

    def chip_sums_and_share(k, after):
        piece = pieces[k]
        received, after = tie(piece["from_chips"], after)
        mine = [chip_sum(chip_id, p, r) for p, r in zip(piece["pair_sums"], received)]
        theirs = sibling_share(mine, name=f"reduce_share_{k}", collective_id=SIBLING_COLLECTIVE_ID)
        own_half.update(zip(piece["keys"], mine))
        other_half.update(zip(piece["keys"], theirs))
        return after

    def piece_done(keys, arrays, dy):
        start_piece(keys, arrays)
        k = len(pieces) - 1
        if k >= 1:
            dy = pair_sums_and_chip_exchange(k - 1, dy)
        if k >= 2:
            dy = chip_sums_and_share(k - 2, dy)
        return dy

    def ffn_backward(name, i, dy, state):
        y_in, xh, rs, a, u = state
        k = 0 if name == "ffn1" else 2
        dz, dgain[i][k], dbias[i][k] = ln_bwd(dy, xh, rs, gain(i, k))
        dx, h, da, du = ffn_bwd(dz, a, u, *ffn_w(name, i))
        outs = mm_tn([(da, y_in, 1.0), (du, y_in, 1.0), (h, dz, MACARON_WEIGHT)], nblk=N_CHIPS, out_shape=(N_CHIPS, FB, D),
                     out_block=(None, FB, D), out_index=lambda j: (j, 0, 0), name="ffn_wgrad")
        return piece_done([(name + s, i) for s in ("_w_gate", "_w_up", "_w_down")], outs, dx)

    def square_grad(a, b):
        return mm_tn([(a, b, 1.0)], nblk=1, out_shape=(D, D), out_block=(D, D), out_index=lambda j: (0, 0), name="square_wgrad")[0]

    for i in reversed(range(DEPTH)):
        f1, mix, f2 = saved[i]
        dy = ffn_backward("ffn2", i, dy, f2)
        dz, dgain[i][1], dbias[i][1] = ln_bwd(dy, mix[1], mix[2], gain(i, 1))
        if i % 2 == 0:
            y_mid, _, _, mixed, pv = mix
            g_out = square_grad(pv, dz)
            dv = mm_nt(dz, w_pool_out, None, a_blocked=False)
            du, dwg, dscale = pool_mix_bwd(dv, mixed, w_group, pool_scale)
            g_group = dwg.reshape(G, N_CHIPS, CB, C).transpose(1, 0, 2, 3)
            g_in = square_grad(y_mid, du)
            dy = mm_nt(du, w_pool_in, dz, a_blocked=False)
            dy = piece_done([("pool_w_out", 0), ("pool_w_group", 0), ("pool_w_in", 0)], [g_out, g_group, g_in], dy)
        else:
            y_mid, _, _, srcs, qkvs, ao, lse, w_qkv, w_attn_out = mix
            g_out = square_grad(ao, dz)
            dao = mm_nt(dz, w_attn_out, None, a_blocked=False)
            in_order = [dict(zip(moved_dils, dilate_rows(t, moved_dils, F32))) for t in (dao, ao, lse)]
            dqkvs = [attn_bwd(qkvs[g], *[m.get(dil, t) for m, t in zip(in_order, (dao, ao, lse))], g, dil)
                     for g, dil in enumerate(dils)]
            g_qkv = [mm_tn([(src, dqkv, 1.0)], nblk=3, out_shape=(D, 3 * D), out_block=(D, D), out_index=lambda j: (0, j),
                           name="qkv_wgrad")[0] for src, dqkv in zip(srcs, dqkvs)]
            g_qkv = jnp.concatenate(g_qkv, axis=1).reshape(D, N_CHIPS, QKV // N_CHIPS).transpose(1, 0, 2)
            dy = mm_nt_dilated(dqkvs, dils, w_qkv, dz)
            dy = piece_done([("attn_w_out", 0), ("attn_w_qkv", 0)], [g_out, g_qkv], dy)
        dy = ffn_backward("ffn1", i, dy, f1)
    grad_x = dy.reshape(x.shape)

    last = len(pieces) - 1
    small = jnp.concatenate([jnp.concatenate(dgain[i] + dbias[i], axis=0) for i in range(DEPTH)] + [dscale, jnp.zeros((3, D), F32)], axis=0)
    small = all_reduce_small(small)
    per_layer = small[:6 * DEPTH].reshape(DEPTH, 6, D)
    cols = D // N_CHIPS
    small_grads = {"ln_gain": lax.dynamic_slice_in_dim(per_layer[:, 0:3], chip * cols, cols, axis=2),
                   "ln_bias": lax.dynamic_slice_in_dim(per_layer[:, 3:6], chip * cols, cols, axis=2),
                   "pool_scale": small[6 * DEPTH:6 * DEPTH + 1]}

    grad_w, delta, new_m, new_v = {}, {}, {}, {}

    def update(n, after):
        shape = weights[n].shape
        if n in MATRIX_NAMES:
            layers = DEPTH if (n, 1) in own_half else 1
            as4 = lambda t: _halves(n, t).reshape(layers, 2, -1, shape[-1] if n not in TRANSPOSED_NAMES else shape[1])
            mine, after = tie([own_half[n, l] for l in range(layers)], after)
            outs = adamw(core, as4(weights[n]), [(mine[l], other_half[n, l]) for l in range(layers)], as4(moms[n]), as4(vels[n]))
            grad_w[n], delta[n], new_m[n], new_v[n] = [_unhalves(n, t, shape) for t in outs]
        else:
            as4 = lambda t: t.reshape(1, 1, -1, shape[-1])
            g2 = small_grads[n].reshape(-1, shape[-1])
            outs = adamw(core, as4(weights[n]), [(g2, g2)], as4(moms[n]), as4(vels[n]))
            grad_w[n], delta[n], new_m[n], new_v[n] = [t.reshape(shape) for t in outs]
        return outs[1]

    marker = small
    for n in ("ln_gain", "ln_bias", "pool_scale"):
        marker = update(n, marker)
    marker = pair_sums_and_chip_exchange(last, marker)
    marker = chip_sums_and_share(last - 1, marker)
    for n in MATRIX_NAMES:
        if not n.startswith("ffn1"):
            marker = update(n, marker)
    marker = chip_sums_and_share(last, marker)
    for n in MATRIX_NAMES:
        if n.startswith("ffn1"):
            marker = update(n, marker)

    return (loss, grad_x, *[grad_w[n] for n in WEIGHT_NAMES], *[delta[n] for n in WEIGHT_NAMES],
            *[new_m[n] for n in WEIGHT_NAMES], *[new_v[n] for n in WEIGHT_NAMES])
```

```python
import functools
import math

import numpy as np
import jax
import jax.numpy as jnp
from jax import lax
from jax.experimental import pallas as pl
from jax.experimental.pallas import tpu as pltpu
from jax.experimental.pallas import tpu_sc as plsc

F32 = jnp.float32
BF16 = jnp.bfloat16

DEPTH = 2
ALPHA = (2.0 * DEPTH) ** 0.25
MACARON_WEIGHT = 0.5
LN_EPS = 1e-5
MASK_VALUE = -1e30
POOL_WINDOWS = (2, 4, 8, 16)
POOL_PAD = 16
HEAD_DIM = 64
N_HEADS = 16
DIL_CONFIGS = ((128, 1), (512, 4), (2048, 16))
ATTN_R = 64
ATTN_BQ = 128
ATTN_W = ATTN_BQ + 2 * ATTN_R
FFN_HIDDEN_TILE = 256
ATTN_UNROLL = 8
LANES = 128
ADAM_LR = 0.001
ADAM_B1 = 0.9
ADAM_B2 = 0.999
ADAM_EPS = 1e-08
ADAM_WD = 0.01
ADAM_STEP = 10
N_CHIPS = 4
GATHER_COLLECTIVE_ID = 1
SIBLING_COLLECTIVE_ID = 2
CHIPS_COLLECTIVE_ID = 3
VMEM_LIMIT = 56 * 1024 * 1024
MESH = pl.DeviceIdType.MESH
ANY = pl.BlockSpec(memory_space=pl.ANY)


def _params(sem=None, vmem=VMEM_LIMIT):
    return pltpu.CompilerParams(dimension_semantics=sem, vmem_limit_bytes=vmem)


def _alibi_slopes():
    n = len(DIL_CONFIGS) * N_HEADS
    s = 2.0 ** (-8.0 * np.arange(1, n + 1) / n)
    return s.reshape(len(DIL_CONFIGS), N_HEADS).astype(np.float32)


def _ln_fwd(z, g, b):
    mu = jnp.mean(z, axis=-1, keepdims=True)
    zc = z - mu
    var = jnp.mean(zc * zc, axis=-1, keepdims=True)
    rstd = lax.rsqrt(var + LN_EPS)
    xhat = zc * rstd
    return xhat * g + b, xhat, rstd


def _dot(a, b):
    return jnp.dot(a, b, preferred_element_type=F32)


def _dot_nt(a, b):
    return lax.dot_general(a, b, (((1,), (1,)), ((), ())), preferred_element_type=F32)


def _dot_tn(a, b):
    return lax.dot_general(a, b, (((0,), (0,)), ((), ())), preferred_element_type=F32)


def mm_nn(a, b, out_dtype, first_block=0, nb=None, tm=1024):
    S, K = a.shape
    Nb = K
    nb = b.shape[1] // Nb if nb is None else nb

    def body(a_ref, b_ref, o_ref):
        o_ref[...] = _dot(a_ref[...].astype(BF16), b_ref[...]).astype(out_dtype)

    return pl.pallas_call(
        body, name="mm_nn",
        grid=(S // tm, nb),
        in_specs=[pl.BlockSpec((tm, K), lambda i, j: (i, 0)), pl.BlockSpec((K, Nb), lambda i, j: (0, first_block + j))],
        out_specs=pl.BlockSpec((None, tm, Nb), lambda i, j: (j, i, 0)),
        out_shape=jax.ShapeDtypeStruct((nb, S, Nb), out_dtype),
        compiler_params=_params(("parallel", "arbitrary")),
    )(a, b)


def chip_blocks_to_columns(w):
    nb, K, Nb = w.shape

    def body(w_ref, o_ref):
        o_ref[...] = w_ref[...]

    return pl.pallas_call(
        body, name="chip_blocks_to_columns",
        grid=(nb,),
        in_specs=[pl.BlockSpec((None, K, Nb), lambda b: (b, 0, 0))],
        out_specs=pl.BlockSpec((K, Nb), lambda b: (0, b)),
        out_shape=jax.ShapeDtypeStruct((K, nb * Nb), w.dtype),
        compiler_params=_params(("parallel",)),
    )(w)


def proj_ln(a, w, resid, gain, bias, tm=1024):
    S, K = a.shape
    D = w.shape[1]

    def body(a_ref, w_ref, r_ref, g_ref, b_ref, y_ref, xh_ref, rs_ref):
        z = ALPHA * r_ref[...] + _dot(a_ref[...].astype(BF16), w_ref[...])
        y, xh, rs = _ln_fwd(z, g_ref[...], b_ref[...])
        y_ref[...] = y
        xh_ref[...] = xh
        rs_ref[...] = rs

    row = pl.BlockSpec((tm, D), lambda i: (i, 0))
    vec = pl.BlockSpec((1, D), lambda i: (0, 0))
    return pl.pallas_call(
        body, name="proj_ln",
        grid=(S // tm,),
        in_specs=[pl.BlockSpec((tm, K), lambda i: (i, 0)), pl.BlockSpec((K, D), lambda i: (0, 0)), row, vec, vec],
        out_specs=[row, row, pl.BlockSpec((tm, 1), lambda i: (i, 0))],
        out_shape=[jax.ShapeDtypeStruct((S, D), F32), jax.ShapeDtypeStruct((S, D), F32), jax.ShapeDtypeStruct((S, 1), F32)],
        compiler_params=_params(("parallel",)),
    )(a, w, resid, gain, bias)


def mm_nt(a, w, resid, a_blocked, out_dtype=F32, tm=1024):
    if a_blocked:
        nk, S, Kb = a.shape
        a_spec = pl.BlockSpec((None, tm, Kb), lambda i, n: (n, i, 0))
    else:
        S, Kb = a.shape
        nk = 1
        a_spec = pl.BlockSpec((tm, Kb), lambda i, n: (i, 0))
    M = w.shape[0]
    has_resid = resid is not None

    def body(*refs):
        if has_resid:
            a_ref, w_ref, r_ref, o_ref, acc = refs
        else:
            a_ref, w_ref, o_ref, acc = refs
        n = pl.program_id(1)
        part = _dot_nt(a_ref[...].astype(BF16), w_ref[...])

        @pl.when(n == 0)
        def _():
            acc[...] = part

        @pl.when(n > 0)
        def _():
            acc[...] += part

        @pl.when(n == nk - 1)
        def _():
            out = acc[...]
            if has_resid:
                out = out + ALPHA * r_ref[...]
            o_ref[...] = out.astype(out_dtype)

    row = pl.BlockSpec((tm, M), lambda i, n: (i, 0))
    in_specs = [a_spec, pl.BlockSpec((M, Kb), lambda i, n: (0, n))] + ([row] if has_resid else [])
    args = (a, w) + ((resid,) if has_resid else ())
    return pl.pallas_call(
        body, name="mm_nt",
        grid=(S // tm, nk),
        in_specs=in_specs,
        out_specs=row,
        out_shape=jax.ShapeDtypeStruct((S, M), out_dtype),
        scratch_shapes=[pltpu.VMEM((tm, M), F32)],
        compiler_params=_params(("parallel", "arbitrary")),
    )(*args)


def mm_nt_dilated(parts, dils, w, resid, tm=1024):
    n_groups = len(parts)
    _, S, K = parts[0].shape
    M = w.shape[0]
    nk = 3 * n_groups

    def body(*refs):
        a_refs = refs[:n_groups]
        w_ref, r_ref, o_ref, group_acc, total = refs[n_groups:]
        n = pl.program_id(1)
        for g in range(n_groups):
            for k in range(3):
                @pl.when(n == 3 * g + k)
                def _():
                    part = _dot_nt(a_refs[g][...], w_ref[...])
                    for c in range(M // LANES):
                        lanes = slice(c * LANES, (c + 1) * LANES)
                        if k == 0:
                            group_acc[c] = part[:, lanes]
                        else:
                            group_acc[c] += part[:, lanes]
                        if k == 2:
                            _rows_from_dilated(group_acc.at[c], total.at[c], tm, dils[g], accumulate=g > 0)

        @pl.when(n == nk - 1)
        def _():
            for c in range(M // LANES):
                lanes = slice(c * LANES, (c + 1) * LANES)
                o_ref[:, lanes] = total[c] + ALPHA * r_ref[:, lanes]

    def a_spec(g):
        return pl.BlockSpec((None, tm, K), lambda i, n: (jnp.clip(n - 3 * g, 0, 2), i, 0))

    row = pl.BlockSpec((tm, M), lambda i, n: (i, 0))
    return pl.pallas_call(
        body, name="mm_nt_dilated",
        grid=(S // tm, nk),
        in_specs=[a_spec(g) for g in range(n_groups)] + [pl.BlockSpec((M, K), lambda i, n: (0, n)), row],
        out_specs=row,
        out_shape=jax.ShapeDtypeStruct((S, M), F32),
        scratch_shapes=[pltpu.VMEM((M // LANES, tm, LANES), F32), pltpu.VMEM((M // LANES, tm, LANES), F32)],
        compiler_params=_params(("parallel", "arbitrary")),
    )(*parts, w, resid)


def mm_tn(pairs, *, nblk, out_shape, out_block, out_index, alias=None, tk=1024, name="mm_tn"):
    operands = []
    for a, b, _ in pairs:
        for t in (a, b):
            if not any(t is o for o in operands):
                operands.append(t)
    where = lambda t: next(i for i, o in enumerate(operands) if o is t)
    S = pairs[0][0].shape[-2]
    n_out, n_in = len(pairs), len(operands)
    n_alias = len(alias) if alias is not None else 0

    def spec(t):
        if t.ndim == 3:
            return pl.BlockSpec((None, tk, t.shape[-1]), lambda j, k: (j, k, 0))
        return pl.BlockSpec((tk, t.shape[-1]), lambda j, k: (k, 0))

    def body(*refs):
        refs = refs[n_alias:]
        in_refs, o_refs, accs = refs[:n_in], refs[n_in:n_in + n_out], refs[n_in + n_out:]
        k = pl.program_id(1)
        for (a, b, scale), o_ref, acc in zip(pairs, o_refs, accs):
            part = _dot_tn(in_refs[where(a)][...].astype(BF16), in_refs[where(b)][...].astype(BF16))

            @pl.when(k == 0)
            def _():
                acc[...] = part

            @pl.when(k > 0)
            def _():
                acc[...] += part

            @pl.when(k == S // tk - 1)
            def _():
                o_ref[...] = (scale * acc[...]).astype(BF16)

    out_spec = pl.BlockSpec(out_block, lambda j, k: out_index(j))
    outs = pl.pallas_call(
        body, name=name,
        grid=(nblk, S // tk),
        in_specs=[ANY] * n_alias + [spec(t) for t in operands],
        out_specs=[out_spec] * n_out,
        out_shape=[jax.ShapeDtypeStruct(out_shape, BF16)] * n_out,
        scratch_shapes=[pltpu.VMEM((a.shape[-1], b.shape[-1]), F32) for a, b, _ in pairs],
        input_output_aliases={i: i for i in range(n_alias)},
        compiler_params=_params(("parallel", "arbitrary")),
    )(*(tuple(alias) if alias is not None else ()), *operands)
    return list(outs)


def ffn_fwd(x, wg, wu, wd, gain, bias, tm=1024):
    S, D = x.shape
    nb, FB = wg.shape[0], wg.shape[1]

    def body(x_ref, wg_ref, wu_ref, wd_ref, g_ref, b_ref, y_ref, xh_ref, rs_ref, a_ref, u_ref, acc, xb_ref):
        j = pl.program_id(1)

        @pl.when(j == 0)
        def _():
            xb_ref[...] = x_ref[...].astype(BF16)
            acc[...] = jnp.zeros_like(acc)

        xb = xb_ref[...]
        total = None
        for t0 in range(0, FB, FFN_HIDDEN_TILE):
            cols = pl.ds(t0, min(FFN_HIDDEN_TILE, FB - t0))
            a = _dot_nt(xb, wg_ref[cols, :])
            u = _dot_nt(xb, wu_ref[cols, :])
            a_ref[:, cols] = a.astype(BF16)
            u_ref[:, cols] = u.astype(BF16)
            h = a * jax.nn.sigmoid(a) * u
            part = _dot(h.astype(BF16), wd_ref[cols, :])
            total = part if total is None else total + part
        acc[...] += total

        @pl.when(j == nb - 1)
        def _():
            z = ALPHA * x_ref[...] + MACARON_WEIGHT * acc[...]
            y, xh, rs = _ln_fwd(z, g_ref[...], b_ref[...])
            y_ref[...] = y
            xh_ref[...] = xh
            rs_ref[...] = rs

    row = pl.BlockSpec((tm, D), lambda i, j: (i, 0))
    vec = pl.BlockSpec((1, D), lambda i, j: (0, 0))
    w_out = pl.BlockSpec((None, FB, D), lambda i, j: (j, 0, 0))
    act = pl.BlockSpec((None, tm, FB), lambda i, j: (j, i, 0))
    return pl.pallas_call(
        body, name="ffn_fwd",
        grid=(S // tm, nb),
        in_specs=[row, w_out, w_out, w_out, vec, vec],
        out_specs=[row, row, pl.BlockSpec((tm, 1), lambda i, j: (i, 0)), act, act],
        out_shape=[jax.ShapeDtypeStruct((S, D), F32), jax.ShapeDtypeStruct((S, D), F32), jax.ShapeDtypeStruct((S, 1), F32),
                   jax.ShapeDtypeStruct((nb, S, FB), BF16), jax.ShapeDtypeStruct((nb, S, FB), BF16)],
        scratch_shapes=[pltpu.VMEM((tm, D), F32), pltpu.VMEM((tm, D), BF16)],
        compiler_params=_params(("parallel", "arbitrary")),
    )(x, wg, wu, wd, gain, bias)


def ffn_bwd(dz, a, u, wg, wu, wd, tm=1024):
    S, D = dz.shape
    nb, FB = wg.shape[0], wg.shape[1]

    def body(dz_ref, a_ref, u_ref, wg_ref, wu_ref, wd_ref, dx_ref, h_ref, da_ref, du_ref, acc, dzb_ref):
        j = pl.program_id(1)

        @pl.when(j == 0)
        def _():
            dzb_ref[...] = (MACARON_WEIGHT * dz_ref[...]).astype(BF16)
            acc[...] = jnp.zeros_like(acc)

        dzb = dzb_ref[...]
        total = None
        for t0 in range(0, FB, FFN_HIDDEN_TILE):
            cols = pl.ds(t0, min(FFN_HIDDEN_TILE, FB - t0))
            dh = _dot_nt(dzb, wd_ref[cols, :])
            av = a_ref[:, cols].astype(F32)
            uv = u_ref[:, cols].astype(F32)
            s = jax.nn.sigmoid(av)
            silu = av * s
            h_ref[:, cols] = (silu * uv).astype(BF16)
            da = (dh * uv * (s * (1.0 + av * (1.0 - s)))).astype(BF16)
            du = (dh * silu).astype(BF16)
            da_ref[:, cols] = da
            du_ref[:, cols] = du
            both = jnp.concatenate([da, du], axis=1)
            weights = jnp.concatenate([wg_ref[cols, :], wu_ref[cols, :]], axis=0)
            part = _dot(both, weights)
            total = part if total is None else total + part
        acc[...] += total

        @pl.when(j == nb - 1)
        def _():
            dx_ref[...] = ALPHA * dz_ref[...] + acc[...]

    row = pl.BlockSpec((tm, D), lambda i, j: (i, 0))
    w_out = pl.BlockSpec((None, FB, D), lambda i, j: (j, 0, 0))
    act = pl.BlockSpec((None, tm, FB), lambda i, j: (j, i, 0))
    act_shape = jax.ShapeDtypeStruct((nb, S, FB), BF16)
    return pl.pallas_call(
        body, name="ffn_bwd",
        grid=(S // tm, nb),
        in_specs=[row, act, act, w_out, w_out, w_out],
        out_specs=[row, act, act, act],
        out_shape=[jax.ShapeDtypeStruct((S, D), F32), act_shape, act_shape, act_shape],
        scratch_shapes=[pltpu.VMEM((tm, D), F32), pltpu.VMEM((tm, D), BF16)],
        compiler_params=_params(("parallel", "arbitrary")),
    )(dz, a, u, wg, wu, wd)


def ln_bwd(dy, xhat, rstd, gain, tm=512):
    S, D = dy.shape

    def body(dy_ref, xh_ref, rs_ref, g_ref, dz_ref, dg_ref, db_ref):
        i = pl.program_id(0)
        dy = dy_ref[...]
        xh = xh_ref[...]
        dxh = dy * g_ref[...]
        m1 = jnp.mean(dxh, axis=-1, keepdims=True)
        m2 = jnp.mean(dxh * xh, axis=-1, keepdims=True)
        dz_ref[...] = rs_ref[...] * (dxh - m1 - xh * m2)
        dg = jnp.sum(dy * xh, axis=0, keepdims=True)
        db = jnp.sum(dy, axis=0, keepdims=True)

        @pl.when(i == 0)
        def _():
            dg_ref[...] = dg
            db_ref[...] = db

        @pl.when(i > 0)
        def _():
            dg_ref[...] += dg
            db_ref[...] += db

    row = pl.BlockSpec((tm, D), lambda i: (i, 0))
    vec = pl.BlockSpec((1, D), lambda i: (0, 0))
    return pl.pallas_call(
        body, name="ln_bwd",
        grid=(S // tm,),
        in_specs=[row, row, pl.BlockSpec((tm, 1), lambda i: (i, 0)), vec],
        out_specs=[row, vec, vec],
        out_shape=[jax.ShapeDtypeStruct((S, D), F32), jax.ShapeDtypeStruct((1, D), F32), jax.ShapeDtypeStruct((1, D), F32)],
        compiler_params=_params(("arbitrary",)),
    )(dy, xhat, rstd, gain)


def loss_head(y, target, tm=512):
    S, D = y.shape

    def body(y_ref, t_ref, dy_ref, l_ref):
        i = pl.program_id(0)
        e = y_ref[...] - t_ref[...]
        dy_ref[...] = e / D
        part = 0.5 * jnp.sum(jnp.mean(e * e, axis=-1, keepdims=True), axis=0, keepdims=True)

        @pl.when(i == 0)
        def _():
            l_ref[...] = part

        @pl.when(i > 0)
        def _():
            l_ref[...] += part

    row = pl.BlockSpec((tm, D), lambda i: (i, 0))
    return pl.pallas_call(
        body, name="loss_head",
        grid=(S // tm,),
        in_specs=[row, row],
        out_specs=[row, pl.BlockSpec((1, 1), lambda i: (0, 0))],
        out_shape=[jax.ShapeDtypeStruct((S, D), F32), jax.ShapeDtypeStruct((1, 1), F32)],
        compiler_params=_params(("arbitrary",)),
    )(y, target)


def _pool_window(xp, g):
    n = xp.shape[0]
    w = xp + pltpu.roll(xp, 1, 0)
    out = w
    for level, shift in enumerate((1, 2, 4), start=1):
        w = pltpu.roll(w, shift, 0) + pltpu.roll(w, n - shift, 0)
        out = jnp.where(g >= level, w, out)
    return out


def _pool_count(S, C, g):
    half = lax.shift_left(jnp.int32(1), g)
    t = lax.broadcasted_iota(jnp.int32, (S, C), 0)
    return (jnp.minimum(t + half, S) - jnp.maximum(t - half, 0)).astype(F32)


def pool_mix(u, wgrp, scale):
    S, D = u.shape
    G, C = wgrp.shape[0], wgrp.shape[1]

    def body(u_ref, w_ref, s_ref, mix_ref, v_ref, pad):
        g = pl.program_id(0)
        zeros = jnp.zeros((POOL_PAD, C), F32)
        pad[pl.ds(0, POOL_PAD), :] = zeros
        pad[pl.ds(POOL_PAD + S, POOL_PAD), :] = zeros
        pad[pl.ds(POOL_PAD, S), :] = u_ref[...]
        win = _pool_window(pad[...], g)[POOL_PAD:POOL_PAD + S]
        mixed = (win / _pool_count(S, C, g) - u_ref[...]).astype(BF16)
        mix_ref[...] = mixed
        v_ref[...] = _dot(mixed, w_ref[...]) * s_ref[...]

    col = pl.BlockSpec((S, C), lambda g: (0, g))
    return pl.pallas_call(
        body, name="pool_mix",
        grid=(G,),
        in_specs=[col, pl.BlockSpec((None, C, C), lambda g: (g, 0, 0)), pl.BlockSpec((1, C), lambda g: (0, g))],
        out_specs=[col, col],
        out_shape=[jax.ShapeDtypeStruct((S, D), BF16), jax.ShapeDtypeStruct((S, D), F32)],
        scratch_shapes=[pltpu.VMEM((S + 2 * POOL_PAD, C), F32)],
        compiler_params=_params(("arbitrary",)),
    )(u, wgrp, scale)


def pool_mix_bwd(dv, mixed, wgrp, scale):
    S, D = dv.shape
    G, C = wgrp.shape[0], wgrp.shape[1]

    def body(dv_ref, mix_ref, w_ref, s_ref, du_ref, dw_ref, ds_ref, pad):
        g = pl.program_id(0)
        mixed = mix_ref[...]
        dv = dv_ref[...]
        yg = _dot(mixed, w_ref[...])
        ds_ref[...] = jnp.sum(dv * yg, axis=0, keepdims=True)
        dyg = (dv * s_ref[...]).astype(BF16)
        dw_ref[...] = _dot_tn(mixed, dyg).astype(BF16)
        dmix = _dot_nt(dyg, w_ref[...])
        zeros = jnp.zeros((POOL_PAD, C), F32)
        pad[pl.ds(0, POOL_PAD), :] = zeros
        pad[pl.ds(POOL_PAD + S, POOL_PAD), :] = zeros
        pad[pl.ds(POOL_PAD, S), :] = dmix / _pool_count(S, C, g)
        win = _pool_window(pad[...], g)
        win = pltpu.roll(win, win.shape[0] - 1, 0)[POOL_PAD:POOL_PAD + S]
        du_ref[...] = win - dmix

    col = pl.BlockSpec((S, C), lambda g: (0, g))
    return pl.pallas_call(
        body, name="pool_mix_bwd",
        grid=(G,),
        in_specs=[col, col, pl.BlockSpec((None, C, C), lambda g: (g, 0, 0)), pl.BlockSpec((1, C), lambda g: (0, g))],
        out_specs=[col, pl.BlockSpec((None, C, C), lambda g: (g, 0, 0)), pl.BlockSpec((1, C), lambda g: (0, g))],
        out_shape=[jax.ShapeDtypeStruct((S, D), F32), jax.ShapeDtypeStruct((G, C, C), BF16), jax.ShapeDtypeStruct((1, D), F32)],
        scratch_shapes=[pltpu.VMEM((S + 2 * POOL_PAD, C), F32)],
        compiler_params=_params(("arbitrary",)),
    )(dv, mixed, wgrp, scale)


PERM_BLOCK = 256


def _dilated_runs(rows, d):
    n = PERM_BLOCK // d
    return [(c * PERM_BLOCK, r, n) for c in range(rows // PERM_BLOCK) for r in range(d)]


def _rows_to_dilated(src_ref, dst_ref, rows, d):
    for base, r, n in _dilated_runs(rows, d):
        dst_ref[pl.ds(base + r * n, n), :] = src_ref[pl.ds(base + r, n, stride=d), :].astype(dst_ref.dtype)


def _rows_from_dilated(src_ref, dst_ref, rows, d, accumulate=False):
    for base, r, n in _dilated_runs(rows, d):
        at = pl.ds(base + r, n, stride=d)
        v = src_ref[pl.ds(base + r * n, n), :]
        dst_ref[at, :] = dst_ref[at, :] + v if accumulate else v


def dilate_rows(x, dils, out_dtype, tm=1024):
    S, D = x.shape

    def body(x_ref, *o_refs):
        for d, o_ref in zip(dils, o_refs):
            _rows_to_dilated(x_ref, o_ref, tm, d)

    tile = pl.BlockSpec((tm, LANES), lambda i, j: (i, j))
    return pl.pallas_call(
        body, name="dilate_rows",
        grid=(S // tm, D // LANES),
        in_specs=[tile],
        out_specs=[tile] * len(dils),
        out_shape=[jax.ShapeDtypeStruct((S, D), out_dtype)] * len(dils),
        compiler_params=_params(("parallel", "parallel")),
    )(x)


def _slope_table(group, dilation):
    s = _alibi_slopes()[group].reshape(N_HEADS // 2, 2, 1, 1) * float(dilation)
    return jnp.asarray(np.broadcast_to(s, (N_HEADS // 2, 2, 1, ATTN_W)).copy())


def _residue_shape(S, D, d):
    return (S, D) if d == 1 else (S // PERM_BLOCK, d, PERM_BLOCK // d, D)


def _residue_view(x, d):
    return x.reshape(x.shape[:-2] + _residue_shape(x.shape[-2], x.shape[-1], d))


def _residue_spec(lead_block, lead_index, S, d):
    if d == 1:
        return pl.BlockSpec(lead_block + (S, LANES), lambda hp, r: lead_index + (0, hp))
    return pl.BlockSpec(lead_block + (S // PERM_BLOCK, None, PERM_BLOCK // d, LANES), lambda hp, r: lead_index + (0, r, 0, hp))


def _whole(ref, lead, L):
    return ref[lead + (slice(None),) * (len(ref.shape) - len(lead))].reshape(L, LANES)


def _query_rows(lead, i, d):
    if d == 1:
        return lead + (pl.ds(pl.multiple_of(i * ATTN_BQ, ATTN_BQ), ATTN_BQ), slice(None)), (ATTN_BQ, LANES)
    n = PERM_BLOCK // d
    return lead + (pl.ds(i * (ATTN_BQ // n), ATTN_BQ // n), slice(None), slice(None)), (ATTN_BQ // n, n, LANES)


def _load_query_rows(ref, lead, i, d):
    at, _ = _query_rows(lead, i, d)
    return ref[at].reshape(ATTN_BQ, LANES)


def _store_query_rows(ref, lead, i, d, value):
    at, shape = _query_rows(lead, i, d)
    ref[at] = value.reshape(shape)


def _stage_keys(dst, src_ref, L):
    rows = _whole(src_ref, (), L)
    lane = lax.broadcasted_iota(jnp.int32, (L, LANES), 1)
    zeros = jnp.zeros((ATTN_R, LANES), dst.dtype)
    for h in range(2):
        mine = (lane < HEAD_DIM) if h == 0 else (lane >= HEAD_DIM)
        dst[h, pl.ds(0, ATTN_R), :] = zeros
        dst[h, pl.ds(ATTN_R + L, ATTN_R), :] = zeros
        dst[h, pl.ds(ATTN_R, L), :] = jnp.where(mine, rows, jnp.zeros_like(rows))


def _fill_bias(bias, sl_ref):
    a = lax.broadcasted_iota(jnp.int32, (ATTN_BQ, ATTN_W), 0)
    c = lax.broadcasted_iota(jnp.int32, (ATTN_BQ, ATTN_W), 1)
    rel = jnp.abs(c - ATTN_R - a)
    band = rel <= ATTN_R
    after_start = c >= ATTN_R
    before_end = c < ATTN_BQ + ATTN_R
    for h in range(2):
        base = -(sl_ref[h] * rel.astype(F32))
        for variant in range(4):
            ok = band
            if variant & 1:
                ok = ok & after_start
            if variant & 2:
                ok = ok & before_end
            bias[variant, h] = jnp.where(ok, base, MASK_VALUE)


def _bias_variant(i, nq):
    return jnp.where(i == 0, 1, 0) + jnp.where(i == nq - 1, 2, 0)


def attn_fwd(qkv, group, dilation):
    _, S, D = qkv.shape
    d = dilation
    L = S // d
    nq = L // ATTN_BQ
    ncol = D // LANES
    view = _residue_view(qkv, d)
    slopes = _slope_table(group, d)
    scale = HEAD_DIM ** -0.5

    def body(q_ref, k_ref, v_ref, sl_ref, o_ref, lse_ref, k2, v2, bias):
        @pl.when(pl.program_id(1) == 0)
        def _():
            _fill_bias(bias, sl_ref)

        _stage_keys(k2, k_ref, L)
        _stage_keys(v2, v_ref, L)
        head0 = lax.broadcasted_iota(jnp.int32, (ATTN_BQ, LANES), 1) < HEAD_DIM

        def step(i, carry):
            variant = _bias_variant(i, nq)
            win = pl.ds(pl.multiple_of(i * ATTN_BQ, ATTN_BQ), ATTN_W)
            qs = _load_query_rows(q_ref, (), i, d) * jnp.asarray(scale, BF16)
            es, ms, ls = [], [], []
            for h in range(2):
                s = _dot_nt(qs, k2[h, win, :]) + bias[variant, h]
                m = jnp.max(s, axis=-1, keepdims=True)
                e = jnp.exp(s - m)
                ls.append(jnp.sum(e, axis=-1, keepdims=True))
                ms.append(m)
                es.append(e.astype(BF16))
            acc = _dot(jnp.concatenate(es, axis=1), jnp.concatenate([v2[0, win, :], v2[1, win, :]], axis=0))
            out = acc * jnp.where(head0, 1.0 / ls[0], 1.0 / ls[1])
            lse = jnp.where(head0, ms[0] + jnp.log(ls[0]), ms[1] + jnp.log(ls[1]))
            _store_query_rows(o_ref, (), i, d, out)
            _store_query_rows(lse_ref, (), i, d, lse)
            return carry

        lax.fori_loop(0, nq, step, 0, unroll=min(ATTN_UNROLL, nq))

    def col(which):
        return _residue_spec((None,), (which,), S, d)

    out = _residue_spec((), (), S, d)
    o, lse = pl.pallas_call(
        body, name=f"attn_fwd_g{group}",
        grid=(ncol, d),
        in_specs=[col(0), col(1), col(2), pl.BlockSpec((None, 2, 1, ATTN_W), lambda hp, r: (hp, 0, 0, 0))],
        out_specs=[out, out],
        out_shape=[jax.ShapeDtypeStruct(_residue_shape(S, D, d), F32)] * 2,
        scratch_shapes=[pltpu.VMEM((2, L + 2 * ATTN_R, LANES), BF16), pltpu.VMEM((2, L + 2 * ATTN_R, LANES), BF16),
                        pltpu.VMEM((4, 2, ATTN_BQ, ATTN_W), F32)],
        compiler_params=_params(("arbitrary", "arbitrary")),
    )(view, view, view, slopes)
    return o.reshape(S, D), lse.reshape(S, D)


def attn_combine(os, lses, dils, tm=1024):
    S, D = os[0].shape
    n = len(os)
    n_moved = sum(d > 1 for d in dils)

    def body(*refs):
        o_refs, l_refs, out_ref, lse_ref = list(refs[:n]), list(refs[n:2 * n]), refs[2 * n], refs[2 * n + 1]
        spare = list(refs[2 * n + 2:])
        for g, d in enumerate(dils):
            if d > 1:
                for which in (o_refs, l_refs):
                    token_order = spare.pop()
                    _rows_from_dilated(which[g], token_order, tm, d)
                    which[g] = token_order
        ls = [r[...] for r in l_refs]
        m = functools.reduce(jnp.maximum, ls)
        es = [jnp.exp(l - m) for l in ls]
        tot = functools.reduce(lambda x, y: x + y, es)
        inv = 1.0 / tot
        out_ref[...] = functools.reduce(lambda x, y: x + y, [(e * inv) * r[...] for e, r in zip(es, o_refs)])
        lse_ref[...] = m + jnp.log(tot)

    tile = pl.BlockSpec((tm, LANES), lambda i, j: (i, j))
    return pl.pallas_call(
        body, name="attn_combine",
        grid=(S // tm, D // LANES),
        in_specs=[tile] * (2 * n),
        out_specs=[tile, tile],
        out_shape=[jax.ShapeDtypeStruct((S, D), F32), jax.ShapeDtypeStruct((S, D), F32)],
        scratch_shapes=[pltpu.VMEM((tm, LANES), F32)] * (2 * n_moved),
        compiler_params=_params(("parallel", "parallel")),
    )(*os, *lses)


def attn_bwd(qkv, do, o, lse, group, dilation):
    _, S, D = qkv.shape
    d = dilation
    L = S // d
    nq = L // ATTN_BQ
    ncol = D // LANES
    view = _residue_view(qkv, d)
    slopes = _slope_table(group, d)
    scale = HEAD_DIM ** -0.5

    def body(q_ref, k_ref, v_ref, do_ref, o_ref, lse_ref, sl_ref, dx_ref, k2, v2, dkacc, dvacc, bias):
        @pl.when(pl.program_id(1) == 0)
        def _():
            _fill_bias(bias, sl_ref)

        _stage_keys(k2, k_ref, L)
        _stage_keys(v2, v_ref, L)
        dkacc[...] = jnp.zeros_like(dkacc)
        dvacc[...] = jnp.zeros_like(dvacc)
        lane = lax.broadcasted_iota(jnp.int32, (ATTN_BQ, LANES), 1)
        heads = (lane < HEAD_DIM, lane >= HEAD_DIM)
        key_head0 = lax.broadcasted_iota(jnp.int32, (ATTN_W, LANES), 1) < HEAD_DIM

        def step(i, carry):
            variant = _bias_variant(i, nq)
            win = pl.ds(pl.multiple_of(i * ATTN_BQ, ATTN_BQ), ATTN_W)
            q = _load_query_rows(q_ref, (), i, d)
            qs = q * jnp.asarray(scale, BF16)
            dov = _load_query_rows(do_ref, (), i, d)
            prod = dov * _load_query_rows(o_ref, (), i, d)
            lse_v = _load_query_rows(lse_ref, (), i, d)
            dob = dov.astype(BF16)
            dss, dks, dvs = [], [], []
            for h in range(2):
                s = _dot_nt(qs, k2[h, win, :]) + bias[variant, h]
                lse_h = jnp.max(jnp.where(heads[h], lse_v, -jnp.inf), axis=-1, keepdims=True)
                dterm = jnp.sum(jnp.where(heads[h], prod, 0.0), axis=-1, keepdims=True)
                p = jnp.exp(s - lse_h)
                dp = _dot_nt(dob, v2[h, win, :])
                ds = (p * (dp - dterm) * scale).astype(BF16)
                dvs.append(_dot_tn(p.astype(BF16), dob))
                dks.append(_dot_tn(ds, q))
                dss.append(ds)
            dq = _dot(jnp.concatenate(dss, axis=1), jnp.concatenate([k2[0, win, :], k2[1, win, :]], axis=0))
            dvacc[win, :] += jnp.where(key_head0, dvs[0], dvs[1])
            dkacc[win, :] += jnp.where(key_head0, dks[0], dks[1])
            _store_query_rows(dx_ref, (0,), i, d, dq.astype(BF16))
            return carry

        lax.fori_loop(0, nq, step, 0, unroll=min(ATTN_UNROLL, nq))
        block_shape = dx_ref.shape[1:]
        dx_ref[1] = dkacc[pl.ds(ATTN_R, L), :].astype(BF16).reshape(block_shape)
        dx_ref[2] = dvacc[pl.ds(ATTN_R, L), :].astype(BF16).reshape(block_shape)

    def col(which):
        return _residue_spec((None,), (which,), S, d)

    act = _residue_spec((), (), S, d)
    out = pl.pallas_call(
        body, name=f"attn_bwd_g{group}",
        grid=(ncol, d),
        in_specs=[col(0), col(1), col(2), act, act, act, pl.BlockSpec((None, 2, 1, ATTN_W), lambda hp, r: (hp, 0, 0, 0))],
        out_specs=_residue_spec((3,), (0,), S, d),
        out_shape=jax.ShapeDtypeStruct((3,) + _residue_shape(S, D, d), BF16),
        scratch_shapes=[pltpu.VMEM((2, L + 2 * ATTN_R, LANES), BF16), pltpu.VMEM((2, L + 2 * ATTN_R, LANES), BF16),
                        pltpu.VMEM((L + 2 * ATTN_R, LANES), F32), pltpu.VMEM((L + 2 * ATTN_R, LANES), F32),
                        pltpu.VMEM((4, 2, ATTN_BQ, ATTN_W), F32)],
        compiler_params=_params(("arbitrary", "arbitrary")),
    )(view, view, view, _residue_view(do, d), _residue_view(o, d), _residue_view(lse, d), slopes)
    return out.reshape(3, S, D)


TILE_ELEMS = 256 * 1024


def _row_tile(R, C):
    if R * C <= TILE_ELEMS or R % 16:
        return R
    return max(t for t in range(16, R + 1, 16) if R % t == 0 and (t * C <= TILE_ELEMS or t == 16))


def pair_sum(core, g, recv):
    _, _, R, C = g.shape
    tr = _row_tile(R, C)

    def body(c_ref, g_ref, r_ref, o_ref):
        o_ref[...] = (g_ref[...].astype(F32) + r_ref[...].astype(F32)).astype(BF16)

    blk = pl.BlockSpec((None, tr, C), lambda d, i, c_ref: (d, i, 0))
    return pl.pallas_call(
        body, name="pair_sum",
        grid_spec=pltpu.PrefetchScalarGridSpec(
            num_scalar_prefetch=1, grid=(N_CHIPS, R // tr),
            in_specs=[pl.BlockSpec((None, None, tr, C), lambda d, i, c_ref: (d, c_ref[0], i, 0)), blk],
            out_specs=blk),
        out_shape=jax.ShapeDtypeStruct((N_CHIPS, R, C), BF16),
        compiler_params=_params(("parallel", "parallel")),
    )(core, g, recv)


def chip_sum(chip, own, recv):
    _, R, C = own.shape
    tr = _row_tile(R, C)
    slot_of_relation = {2: 0, 1: 1, 3: 2}

    def body(chip_ref, own_ref, r_ref, o_ref):
        me = chip_ref[0]
        mine = own_ref[...].astype(F32)
        theirs = {rel: r_ref[k].astype(F32) for rel, k in slot_of_relation.items()}
        acc = None
        for s in range(N_CHIPS):
            rel = jnp.bitwise_xor(me, s)
            part = jnp.where(rel == 0, mine, jnp.where(rel == 2, theirs[2], jnp.where(rel == 1, theirs[1], theirs[3])))
            acc = part if acc is None else acc + part
        o_ref[...] = acc

    return pl.pallas_call(
        body, name="chip_sum",
        grid_spec=pltpu.PrefetchScalarGridSpec(
            num_scalar_prefetch=1, grid=(R // tr,),
            in_specs=[pl.BlockSpec((None, tr, C), lambda i, chip_ref: (chip_ref[0], i, 0)),
                      pl.BlockSpec((N_CHIPS - 1, tr, C), lambda i, chip_ref: (0, i, 0))],
            out_specs=pl.BlockSpec((tr, C), lambda i, chip_ref: (i, 0))),
        out_shape=jax.ShapeDtypeStruct((R, C), F32),
        compiler_params=_params(("parallel",)),
    )(chip, own, recv)


def adamw(core, w, g_pairs, m, v):
    L, H, R, C = w.shape
    tr = _row_tile(R, C)

    def body(c_ref, w_ref, *rest):
        g_refs = rest[:2 * L]
        m_ref, v_ref, g_ref, d_ref, nm_ref, nv_ref = rest[2 * L:]
        mine = pl.program_id(1) == c_ref[0]
        g = None
        for l in range(L):
            g_l = jnp.where(mine, g_refs[2 * l][...], g_refs[2 * l + 1][...])
            g = g_l if g is None else jnp.where(pl.program_id(0) == l, g_l, g)
        m = ADAM_B1 * m_ref[...] + (1.0 - ADAM_B1) * g
        v = ADAM_B2 * v_ref[...] + (1.0 - ADAM_B2) * (g * g)
        m_hat = m / (1.0 - ADAM_B1 ** ADAM_STEP)
        v_hat = v / (1.0 - ADAM_B2 ** ADAM_STEP)
        g_ref[...] = g
        d_ref[...] = -ADAM_LR * (m_hat / (jnp.sqrt(v_hat) + ADAM_EPS) + ADAM_WD * w_ref[...])
        nm_ref[...] = m
        nv_ref[...] = v

    blk = pl.BlockSpec((None, None, tr, C), lambda l, h, i, c_ref: (l, h, i, 0))

    def half(layer):
        return pl.BlockSpec((tr, C), lambda l, h, i, c_ref: (jnp.where(l == layer, i, 0), 0))

    shape = jax.ShapeDtypeStruct((L, H, R, C), F32)
    return pl.pallas_call(
        body, name="adamw",
        grid_spec=pltpu.PrefetchScalarGridSpec(
            num_scalar_prefetch=1, grid=(L, H, R // tr),
            in_specs=[blk] + [half(l) for l in range(L) for _ in range(2)] + [blk, blk],
            out_specs=[blk] * 4),
        out_shape=[shape] * 4,
        compiler_params=_params(("parallel", "parallel", "parallel")),
    )(core, w, *[g for pair in g_pairs for g in pair], m, v)


def _place():
    return lax.axis_index("x"), lax.axis_index("y"), lax.axis_index("c")


def _other_chips(x, y):
    return [(2 * (1 - x) + y, (1 - x, y)), (2 * x + (1 - y), (x, 1 - y)), (2 * (1 - x) + (1 - y), (1 - x, 1 - y))]


def all_gather_shards(shards, placed):
    n = len(shards)

    def body(*refs):
        ins, outs = refs[:n], refs[2 * n:3 * n]
        send_sems, recv_sems = refs[3 * n:]
        x, y, c = _place()
        me = 2 * x + y
        sibling = (x, y, 1 - c)
        chips = _other_chips(x, y)

        def copy(a, k, src, dst, to):
            return pltpu.make_async_remote_copy(src_ref=src, dst_ref=dst, send_sem=send_sems.at[a, k], recv_sem=recv_sems.at[a, k],
                                                device_id=to, device_id_type=MESH)

        sends = []
        for a in range(n):
            for k, (_, (px, py)) in enumerate(chips):
                cp = copy(a, k, ins[a].at[c], outs[a].at[me, c], (px, py, c))
                cp.start()
                sends.append(cp)
        for a in range(n):
            for k, (chip, _) in enumerate(chips):
                landed = outs[a].at[chip, c]
                copy(a, k, landed, landed, sibling).wait_recv()
                cp = copy(a, 3 + k, landed, landed, sibling)
                cp.start()
                sends.append(cp)
        for a in range(n):
            for k, (chip, _) in enumerate(chips):
                other = outs[a].at[chip, 1 - c]
                copy(a, 3 + k, other, other, sibling).wait_recv()
        for cp in sends:
            cp.wait_send()

    return pl.pallas_call(
        body, name="all_gather_shards",
        in_specs=[ANY] * (2 * n),
        out_specs=[ANY] * n,
        out_shape=[jax.ShapeDtypeStruct(p.shape, p.dtype) for p in placed],
        scratch_shapes=[pltpu.SemaphoreType.DMA((n, 6)), pltpu.SemaphoreType.DMA((n, 6))],
        input_output_aliases={n + a: a for a in range(n)},
        compiler_params=pltpu.CompilerParams(has_side_effects=True),
    )(*shards, *placed)


def all_gather_shards_async(shards, placed, collective_id, name):
    n = len(shards)
    srcs = [jax.new_ref(s, memory_space=pltpu.MemorySpace.HBM) for s in shards]
    dsts = [jax.new_ref(p, memory_space=pltpu.MemorySpace.HBM) for p in placed]

    @pl.kernel(mesh=plsc.ScalarSubcoreMesh(axis_name="sequencer", num_cores=1), name=name,
               scratch_types=(pltpu.SemaphoreType.DMA((n, 6)), pltpu.SemaphoreType.DMA((n, 6))),
               compiler_params=pltpu.CompilerParams(collective_id=collective_id))
    def launch(send_sems, recv_sems):
        x, y, c = _place()
        me = 2 * x + y
        sibling = (x, y, 1 - c)
        chips = _other_chips(x, y)
        barrier = pltpu.get_barrier_semaphore()
        peers = [sibling] + [(px, py, c) for _, (px, py) in chips]
        for peer in peers:
            pl.semaphore_signal(barrier, inc=1, device_id=peer, device_id_type=MESH)
        pl.semaphore_wait(barrier, len(peers))

        def copy(a, k, src, dst, to):
            return pltpu.make_async_remote_copy(src_ref=src, dst_ref=dst, send_sem=send_sems.at[a, k], recv_sem=recv_sems.at[a, k],
                                                device_id=to, device_id_type=MESH)

        sends = []
        for a in range(n):
            for k, (_, (px, py)) in enumerate(chips):
                cp = copy(a, k, srcs[a].at[c], dsts[a].at[me, c], (px, py, c))
                cp.start()
                sends.append(cp)
        for a in range(n):
            for k, (chip, _) in enumerate(chips):
                landed = dsts[a].at[chip, c]
                copy(a, k, landed, landed, sibling).wait_recv()
                cp = copy(a, 3 + k, landed, landed, sibling)
                cp.start()
                sends.append(cp)
        for a in range(n):
            for k, (chip, _) in enumerate(chips):
                other = dsts[a].at[chip, 1 - c]
                copy(a, 3 + k, other, other, sibling).wait_recv()
        for cp in sends:
            cp.wait_send()

    launch()
    return [d[...] for d in dsts]


def _exchange(body, ins, out_shapes, sem_shapes, name, peers=None, collective_id=None):
    n_in, n_out = len(ins), len(out_shapes)
    sems = [pltpu.SemaphoreType.DMA(shape) for shape in sem_shapes]
    if collective_id is None:
        def tc_body(*refs):
            body(refs[:n_in], refs[n_in:n_in + n_out], *refs[n_in + n_out:])

        return pl.pallas_call(tc_body, name=name, in_specs=[ANY] * n_in, out_specs=[ANY] * n_out, out_shape=out_shapes,
                              scratch_shapes=sems, compiler_params=pltpu.CompilerParams(has_side_effects=True))(*ins)
    srcs = [jax.new_ref(a, memory_space=pltpu.MemorySpace.HBM) for a in ins]
    dsts = [jax.empty_ref(shape, memory_space=pltpu.MemorySpace.HBM) for shape in out_shapes]

    @pl.kernel(mesh=plsc.ScalarSubcoreMesh(axis_name="sequencer", num_cores=1), name=name, scratch_types=tuple(sems),
               compiler_params=pltpu.CompilerParams(collective_id=collective_id))
    def launch(*sem_refs):
        barrier = pltpu.get_barrier_semaphore()
        others = peers(*_place())
        for peer in others:
            pl.semaphore_signal(barrier, inc=1, device_id=peer, device_id_type=MESH)
        pl.semaphore_wait(barrier, len(others))
        body(srcs, dsts, *sem_refs)

    launch()
    return [d[...] for d in dsts]


def _sibling(x, y, c):
    return [(x, y, 1 - c)]


def _same_core_of_other_chips(x, y, c):
    return [(px, py, c) for _, (px, py) in _other_chips(x, y)]


def sibling_exchange_halves(grads, name="sibling_exchange_halves", collective_id=None):
    n = len(grads)

    def body(ins, outs, send_sems, recv_sems):
        x, y, c = _place()
        copies = [pltpu.make_async_remote_copy(src_ref=ins[a].at[:, 1 - c], dst_ref=outs[a], send_sem=send_sems.at[a],
                                               recv_sem=recv_sems.at[a], device_id=(x, y, 1 - c), device_id_type=MESH) for a in range(n)]
        for cp in copies:
            cp.start()
        for cp in copies:
            cp.wait()

    shapes = [jax.ShapeDtypeStruct((N_CHIPS,) + g.shape[2:], g.dtype) for g in grads]
    return _exchange(body, grads, shapes, [(n,), (n,)], name, _sibling, collective_id)


def chip_exchange(sums, name="chip_exchange", collective_id=None):
    n = len(sums)

    def body(ins, outs, send_sems, recv_sems):
        x, y, c = _place()
        copies = []
        for a in range(n):
            for k, (chip, (px, py)) in enumerate(_other_chips(x, y)):
                cp = pltpu.make_async_remote_copy(src_ref=ins[a].at[chip], dst_ref=outs[a].at[k], send_sem=send_sems.at[a, k],
                                                  recv_sem=recv_sems.at[a, k], device_id=(px, py, c), device_id_type=MESH)
                cp.start()
                copies.append(cp)
        for cp in copies:
            cp.wait()

    shapes = [jax.ShapeDtypeStruct((N_CHIPS - 1,) + s.shape[1:], s.dtype) for s in sums]
    return _exchange(body, sums, shapes, [(n, 3), (n, 3)], name, _same_core_of_other_chips, collective_id)


def sibling_share(halves, name="sibling_share", collective_id=None):
    n = len(halves)

    def body(ins, outs, send_sems, recv_sems):
        x, y, c = _place()
        copies = [pltpu.make_async_remote_copy(src_ref=ins[a], dst_ref=outs[a], send_sem=send_sems.at[a], recv_sem=recv_sems.at[a],
                                               device_id=(x, y, 1 - c), device_id_type=MESH) for a in range(n)]
        for cp in copies:
            cp.start()
        for cp in copies:
            cp.wait()

    shapes = [jax.ShapeDtypeStruct(h.shape, h.dtype) for h in halves]
    return _exchange(body, halves, shapes, [(n,), (n,)], name, _sibling, collective_id)


def all_reduce_small(v):
    R, C = v.shape
    n_dev = 8

    def body(v_ref, o_ref, buf, send_sems, recv_sems):
        x, y, c = _place()
        me = 4 * x + 2 * y + c
        buf[me] = v_ref[...]
        copies = []
        for rel in range(1, n_dev):
            fx, fy, fc = rel >> 2, (rel >> 1) & 1, rel & 1
            peer = (x ^ fx, y ^ fy, c ^ fc)
            cp = pltpu.make_async_remote_copy(src_ref=v_ref, dst_ref=buf.at[me], send_sem=send_sems.at[rel - 1],
                                              recv_sem=recv_sems.at[rel - 1], device_id=peer, device_id_type=MESH)
            cp.start()
            copies.append(cp)
        for cp in copies:
            cp.wait()
        acc = buf[0]
        for k in range(1, n_dev):
            acc = acc + buf[k]
        o_ref[...] = acc

    return pl.pallas_call(
        body, name="all_reduce_small",
        in_specs=[pl.BlockSpec(memory_space=pltpu.VMEM)],
        out_specs=pl.BlockSpec(memory_space=pltpu.VMEM),
        out_shape=jax.ShapeDtypeStruct((R, C), F32),
        scratch_shapes=[pltpu.VMEM((n_dev, R, C), F32), pltpu.SemaphoreType.DMA((n_dev - 1,)), pltpu.SemaphoreType.DMA((n_dev - 1,))],
        compiler_params=pltpu.CompilerParams(has_side_effects=True),
    )(v)


WEIGHT_NAMES = ("ffn1_w_gate", "ffn1_w_up", "ffn1_w_down", "ffn2_w_gate", "ffn2_w_up", "ffn2_w_down", "ln_gain", "ln_bias",
                "pool_w_in", "pool_w_group", "pool_scale", "pool_w_out", "attn_w_qkv", "attn_w_out")
MATRIX_NAMES = ("ffn1_w_gate", "ffn1_w_up", "ffn1_w_down", "ffn2_w_gate", "ffn2_w_up", "ffn2_w_down",
                "pool_w_in", "pool_w_group", "pool_w_out", "attn_w_qkv", "attn_w_out")


TRANSPOSED_NAMES = ("ffn1_w_gate", "ffn1_w_up", "ffn2_w_gate", "ffn2_w_up")


def _halves(name, w):
    if name in TRANSPOSED_NAMES:
        w = jnp.swapaxes(w, 1, 2)
    return w.reshape(2, -1, w.shape[-1])


def _unhalves(name, t, shape):
    if name in TRANSPOSED_NAMES:
        return jnp.swapaxes(t.reshape(shape[0], shape[2], shape[1]), 1, 2)
    return t.reshape(shape)


def kernel(x, ffn1_w_gate, ffn1_w_up, ffn1_w_down, ffn2_w_gate, ffn2_w_up, ffn2_w_down, ln_gain, ln_bias, pool_w_in, pool_w_group, pool_scale, pool_w_out, attn_w_qkv, attn_w_out, loss_target, m_ffn1_w_gate, m_ffn1_w_up, m_ffn1_w_down, m_ffn2_w_gate, m_ffn2_w_up, m_ffn2_w_down, m_ln_gain, m_ln_bias, m_pool_w_in, m_pool_w_group, m_pool_scale, m_pool_w_out, m_attn_w_qkv, m_attn_w_out, v_ffn1_w_gate, v_ffn1_w_up, v_ffn1_w_down, v_ffn2_w_gate, v_ffn2_w_up, v_ffn2_w_down, v_ln_gain, v_ln_bias, v_pool_w_in, v_pool_w_group, v_pool_scale, v_pool_w_out, v_attn_w_qkv, v_attn_w_out):
    weights = dict(zip(WEIGHT_NAMES, (ffn1_w_gate, ffn1_w_up, ffn1_w_down, ffn2_w_gate, ffn2_w_up, ffn2_w_down, ln_gain, ln_bias,
                                      pool_w_in, pool_w_group, pool_scale, pool_w_out, attn_w_qkv, attn_w_out)))
    moms = dict(zip(WEIGHT_NAMES, (m_ffn1_w_gate, m_ffn1_w_up, m_ffn1_w_down, m_ffn2_w_gate, m_ffn2_w_up, m_ffn2_w_down, m_ln_gain,
                                   m_ln_bias, m_pool_w_in, m_pool_w_group, m_pool_scale, m_pool_w_out, m_attn_w_qkv, m_attn_w_out)))
    vels = dict(zip(WEIGHT_NAMES, (v_ffn1_w_gate, v_ffn1_w_up, v_ffn1_w_down, v_ffn2_w_gate, v_ffn2_w_up, v_ffn2_w_down, v_ln_gain,
                                   v_ln_bias, v_pool_w_in, v_pool_w_group, v_pool_scale, v_pool_w_out, v_attn_w_qkv, v_attn_w_out)))
    S, D = x.shape[1], x.shape[2]
    FB = ffn1_w_gate.shape[2]
    QKV = attn_w_qkv.shape[2] * N_CHIPS
    G, CB = pool_w_group.shape[1], pool_w_group.shape[2]
    C = pool_w_group.shape[3]
    cx, cy, cc = _place()
    chip = 2 * cx + cy
    xs = x.reshape(S, D)
    target = loss_target.reshape(S, D)

    ln_rows = jnp.concatenate([ln_gain, ln_bias, jnp.zeros((DEPTH, 2, ln_gain.shape[2]), F32)], axis=1)
    shard = {n: _halves(n, weights[n]).astype(BF16) for n in MATRIX_NAMES}
    ffn_layer = lambda f, i: {f"{f}{s}@{i}": shard[f + s][i].reshape(2, FB // 2, D) for s in ("_w_gate", "_w_up", "_w_down")}
    pool = {n: shard[n] for n in ("pool_w_in", "pool_w_group", "pool_w_out")}
    attn = {n: shard[n] for n in ("attn_w_qkv", "attn_w_out")}
    groups = [dict(ffn_layer("ffn1", 0), ln=ln_rows), pool, ffn_layer("ffn2", 0), ffn_layer("ffn1", 1), attn, ffn_layer("ffn2", 1)]
    full = {}

    def launch(k, after=None):
        parts = list(groups[k].values())
        if after is not None:
            *parts, after = lax.optimization_barrier((*parts, after))
        placed = [lax.dynamic_update_slice(lax.empty((N_CHIPS,) + s.shape, s.dtype), s[None], (chip, 0, 0, 0)) for s in parts]
        full.update(zip(groups[k], all_gather_shards_async(parts, placed, GATHER_COLLECTIVE_ID, f"gather_weights_{k}")))
        return after

    for k in range(4):
        launch(k)
    ffn_w = lambda f, i: [full[f"{f}{s}@{i}"].reshape(N_CHIPS, FB, D) for s in ("_w_gate", "_w_up", "_w_down")]
    w_pool_in = full["pool_w_in"].reshape(D, D)
    w_pool_out = full["pool_w_out"].reshape(D, D)
    w_group = full["pool_w_group"].reshape(N_CHIPS, G, CB, C).transpose(1, 0, 2, 3).reshape(G, N_CHIPS * CB, C)
    ln_full = full["ln"].transpose(1, 2, 0, 3).reshape(DEPTH, 8, D)
    gain = lambda i, k: ln_full[i, k].reshape(1, D)
    bias = lambda i, k: ln_full[i, 3 + k].reshape(1, D)

    dils = [dil for _, dil in DIL_CONFIGS]
    moved_dils = [dil for dil in dils if dil > 1]

    saved = []
    y = xs
    for i in range(DEPTH):
        y_in = y
        y, xh, rs, a, u = ffn_fwd(y_in, *ffn_w("ffn1", i), gain(i, 0), bias(i, 0))
        f1 = (y_in, xh, rs, a, u)
        if i == 0:
            y = launch(4, after=y)
        y_mid = y
        if i % 2 == 0:
            pu = mm_nn(y_mid, w_pool_in, F32)[0]
            mixed, pv = pool_mix(pu, w_group, pool_scale)
            y, xh, rs = proj_ln(pv, w_pool_out, y_mid, gain(i, 1), bias(i, 1))
            mix = (y_mid, xh, rs, mixed, pv)
            if i == 0:
                y = launch(5, after=y)
        else:
            qkv_blocks, y_mid = lax.optimization_barrier((full["attn_w_qkv"], y_mid))
            w_qkv = chip_blocks_to_columns(qkv_blocks.reshape(N_CHIPS, D, QKV // N_CHIPS))
            w_attn_out = full["attn_w_out"].reshape(D, D)
            moved = dict(zip(moved_dils, dilate_rows(y_mid, moved_dils, BF16)))
            srcs = [moved.get(dil, y_mid) for dil in dils]
            qkvs = [mm_nn(src, w_qkv, BF16, first_block=3 * g, nb=3) for g, src in enumerate(srcs)]
            parts = [attn_fwd(qkv, g, dil) for g, (qkv, dil) in enumerate(zip(qkvs, dils))]
            ao, lse = attn_combine([p[0] for p in parts], [p[1] for p in parts], dils)
            y, xh, rs = proj_ln(ao, w_attn_out, y_mid, gain(i, 1), bias(i, 1))
            mix = (y_mid, xh, rs, srcs, qkvs, ao, lse, w_qkv, w_attn_out)
        y_in2 = y
        y, xh, rs, a, u = ffn_fwd(y_in2, *ffn_w("ffn2", i), gain(i, 2), bias(i, 2))
        f2 = (y_in2, xh, rs, a, u)
        saved.append((f1, mix, f2))

    dy, loss_part = loss_head(y, target)
    loss = lax.psum(loss_part[0, 0], ("x", "y", "c"))

    core = cc.reshape(1).astype(jnp.int32)
    chip_id = chip.reshape(1).astype(jnp.int32)
    dgain = [[None] * 3 for _ in range(DEPTH)]
    dbias = [[None] * 3 for _ in range(DEPTH)]
    dscale = None
    pieces = []
    own_half, other_half = {}, {}

    def tie(arrays, after):
        *arrays, after = lax.optimization_barrier((*arrays, after))
        return arrays, after

    def start_piece(keys, arrays):
        blocks = [g.reshape(N_CHIPS, 2, -1, g.shape[-1]) for g in arrays]
        k = len(pieces)
        pieces.append(dict(keys=keys, blocks=blocks, from_sibling=sibling_exchange_halves(
            blocks, name=f"reduce_halves_{k}", collective_id=SIBLING_COLLECTIVE_ID)))

    def pair_sums_and_chip_exchange(k, after):
        piece = pieces[k]
        received, after = tie(piece["from_sibling"], after)
        piece["pair_sums"] = [pair_sum(core, b, r) for b, r in zip(piece["blocks"], received)]
        piece["from_chips"] = chip_exchange(piece["pair_sums"], name=f"reduce_chips_{k}", collective_id=CHIPS_COLLECTIVE_ID)
        return after

    def chip_sums_and_share(k, after):
        piece = pieces[k]
        received, after = tie(piece["from_chips"], after)
        mine = [chip_sum(chip_id, p, r) for p, r in zip(piece["pair_sums"], received)]
        theirs = sibling_share(mine, name=f"reduce_share_{k}", collective_id=SIBLING_COLLECTIVE_ID)
        own_half.update(zip(piece["keys"], mine))
        other_half.update(zip(piece["keys"], theirs))
        return after

    def piece_done(keys, arrays, dy):
        start_piece(keys, arrays)
        k = len(pieces) - 1
        if k >= 1:
            dy = pair_sums_and_chip_exchange(k - 1, dy)
        if k >= 2:
            dy = chip_sums_and_share(k - 2, dy)
        return dy

    def ffn_backward(name, i, dy, state):
        y_in, xh, rs, a, u = state
        k = 0 if name == "ffn1" else 2
        dz, dgain[i][k], dbias[i][k] = ln_bwd(dy, xh, rs, gain(i, k))
        dx, h, da, du = ffn_bwd(dz, a, u, *ffn_w(name, i))
        outs = mm_tn([(da, y_in, 1.0), (du, y_in, 1.0), (h, dz, MACARON_WEIGHT)], nblk=N_CHIPS, out_shape=(N_CHIPS, FB, D),
                     out_block=(None, FB, D), out_index=lambda j: (j, 0, 0), name="ffn_wgrad")
        return piece_done([(name + s, i) for s in ("_w_gate", "_w_up", "_w_down")], outs, dx)

    def square_grad(a, b):
        return mm_tn([(a, b, 1.0)], nblk=1, out_shape=(D, D), out_block=(D, D), out_index=lambda j: (0, 0), name="square_wgrad")[0]

    for i in reversed(range(DEPTH)):
        f1, mix, f2 = saved[i]
        dy = ffn_backward("ffn2", i, dy, f2)
        dz, dgain[i][1], dbias[i][1] = ln_bwd(dy, mix[1], mix[2], gain(i, 1))
        if i % 2 == 0:
            y_mid, _, _, mixed, pv = mix
            g_out = square_grad(pv, dz)
            dv = mm_nt(dz, w_pool_out, None, a_blocked=False)
            du, dwg, dscale = pool_mix_bwd(dv, mixed, w_group, pool_scale)
            g_group = dwg.reshape(G, N_CHIPS, CB, C).transpose(1, 0, 2, 3)
            g_in = square_grad(y_mid, du)
            dy = mm_nt(du, w_pool_in, dz, a_blocked=False)
            dy = piece_done([("pool_w_out", 0), ("pool_w_group", 0), ("pool_w_in", 0)], [g_out, g_group, g_in], dy)
        else:
            y_mid, _, _, srcs, qkvs, ao, lse, w_qkv, w_attn_out = mix
            g_out = square_grad(ao, dz)
            dao = mm_nt(dz, w_attn_out, None, a_blocked=False)
            in_order = [dict(zip(moved_dils, dilate_rows(t, moved_dils, F32))) for t in (dao, ao, lse)]
            dqkvs = [attn_bwd(qkvs[g], *[m.get(dil, t) for m, t in zip(in_order, (dao, ao, lse))], g, dil)
                     for g, dil in enumerate(dils)]
            g_qkv = [mm_tn([(src, dqkv, 1.0)], nblk=3, out_shape=(D, 3 * D), out_block=(D, D), out_index=lambda j: (0, j),
                           name="qkv_wgrad")[0] for src, dqkv in zip(srcs, dqkvs)]
            g_qkv = jnp.concatenate(g_qkv, axis=1).reshape(D, N_CHIPS, QKV // N_CHIPS).transpose(1, 0, 2)
            dy = mm_nt_dilated(dqkvs, dils, w_qkv, dz)
            dy = piece_done([("attn_w_out", 0), ("attn_w_qkv", 0)], [g_out, g_qkv], dy)
        dy = ffn_backward("ffn1", i, dy, f1)
    grad_x = dy.reshape(x.shape)

    last = len(pieces) - 1
    small = jnp.concatenate([jnp.concatenate(dgain[i] + dbias[i], axis=0) for i in range(DEPTH)] + [dscale, jnp.zeros((3, D), F32)], axis=0)
    small = all_reduce_small(small)
    per_layer = small[:6 * DEPTH].reshape(DEPTH, 6, D)
    cols = D // N_CHIPS
    small_grads = {"ln_gain": lax.dynamic_slice_in_dim(per_layer[:, 0:3], chip * cols, cols, axis=2),
                   "ln_bias": lax.dynamic_slice_in_dim(per_layer[:, 3:6], chip * cols, cols, axis=2),
                   "pool_scale": small[6 * DEPTH:6 * DEPTH + 1]}

    grad_w, delta, new_m, new_v = {}, {}, {}, {}

    def update(n, after):
        shape = weights[n].shape
        if n in MATRIX_NAMES:
            layers = DEPTH if (n, 1) in own_half else 1
            as4 = lambda t: _halves(n, t).reshape(layers, 2, -1, shape[-1] if n not in TRANSPOSED_NAMES else shape[1])
            mine, after = tie([own_half[n, l] for l in range(layers)], after)
            outs = adamw(core, as4(weights[n]), [(mine[l], other_half[n, l]) for l in range(layers)], as4(moms[n]), as4(vels[n]))
            grad_w[n], delta[n], new_m[n], new_v[n] = [_unhalves(n, t, shape) for t in outs]
        else:
            as4 = lambda t: t.reshape(1, 1, -1, shape[-1])
            g2 = small_grads[n].reshape(-1, shape[-1])
            outs = adamw(core, as4(weights[n]), [(g2, g2)], as4(moms[n]), as4(vels[n]))
            grad_w[n], delta[n], new_m[n], new_v[n] = [t.reshape(shape) for t in outs]
        return outs[1]

    marker = small
    for n in ("ln_gain", "ln_bias", "pool_scale"):
        marker = update(n, marker)
    marker = pair_sums_and_chip_exchange(last, marker)
    marker = chip_sums_and_share(last - 1, marker)
    for n in MATRIX_NAMES:
        if not n.startswith("ffn1"):
            marker = update(n, marker)
    marker = chip_sums_and_share(last, marker)
    for n in MATRIX_NAMES:
        if n.startswith("ffn1"):
            marker = update(n, marker)

    return (loss, grad_x, *[grad_w[n] for n in WEIGHT_NAMES], *[delta[n] for n in WEIGHT_NAMES],
            *[new_m[n] for n in WEIGHT_NAMES], *[new_v[n] for n in WEIGHT_NAMES])
```

```python
import functools
import math

import numpy as np
import jax
import jax.numpy as jnp
from jax import lax
from jax.experimental import pallas as pl
from jax.experimental.pallas import tpu as pltpu
from jax.experimental.pallas import tpu_sc as plsc

F32 = jnp.float32
BF16 = jnp.bfloat16

DEPTH = 2
ALPHA = (2.0 * DEPTH) ** 0.25
MACARON_WEIGHT = 0.5
LN_EPS = 1e-5
MASK_VALUE = -1e30
POOL_WINDOWS = (2, 4, 8, 16)
POOL_PAD = 16
HEAD_DIM = 64
N_HEADS = 16
DIL_CONFIGS = ((128, 1), (512, 4), (2048, 16))
ATTN_R = 64
ATTN_BQ = 128
ATTN_W = ATTN_BQ + 2 * ATTN_R
FFN_HIDDEN_TILE = 256
ATTN_UNROLL = 8
LANES = 128
ADAM_LR = 0.001
ADAM_B1 = 0.9
ADAM_B2 = 0.999
ADAM_EPS = 1e-08
ADAM_WD = 0.01
ADAM_STEP = 10
N_CHIPS = 4
GATHER_COLLECTIVE_ID = 1
SIBLING_COLLECTIVE_ID = 2
CHIPS_COLLECTIVE_ID = 3
VMEM_LIMIT = 56 * 1024 * 1024
MESH = pl.DeviceIdType.MESH
ANY = pl.BlockSpec(memory_space=pl.ANY)


def _params(sem=None, vmem=VMEM_LIMIT):
    return pltpu.CompilerParams(dimension_semantics=sem, vmem_limit_bytes=vmem)


def _alibi_slopes():
    n = len(DIL_CONFIGS) * N_HEADS
    s = 2.0 ** (-8.0 * np.arange(1, n + 1) / n)
    return s.reshape(len(DIL_CONFIGS), N_HEADS).astype(np.float32)


def _ln_fwd(z, g, b):
    mu = jnp.mean(z, axis=-1, keepdims=True)
    zc = z - mu
    var = jnp.mean(zc * zc, axis=-1, keepdims=True)
    rstd = lax.rsqrt(var + LN_EPS)
    xhat = zc * rstd
    return xhat * g + b, xhat, rstd


def _dot(a, b):
    return jnp.dot(a, b, preferred_element_type=F32)


def _dot_nt(a, b):
    return lax.dot_general(a, b, (((1,), (1,)), ((), ())), preferred_element_type=F32)


def _dot_tn(a, b):
    return lax.dot_general(a, b, (((0,), (0,)), ((), ())), preferred_element_type=F32)


def mm_nn(a, b, out_dtype, first_block=0, nb=None, tm=1024):
    S, K = a.shape
    Nb = K
    nb = b.shape[1] // Nb if nb is None else nb

    def body(a_ref, b_ref, o_ref):
        o_ref[...] = _dot(a_ref[...].astype(BF16), b_ref[...]).astype(out_dtype)

    return pl.pallas_call(
        body, name="mm_nn",
        grid=(S // tm, nb),
        in_specs=[pl.BlockSpec((tm, K), lambda i, j: (i, 0)), pl.BlockSpec((K, Nb), lambda i, j: (0, first_block + j))],
        out_specs=pl.BlockSpec((None, tm, Nb), lambda i, j: (j, i, 0)),
        out_shape=jax.ShapeDtypeStruct((nb, S, Nb), out_dtype),
        compiler_params=_params(("parallel", "arbitrary")),
    )(a, b)


def chip_blocks_to_columns(w):
    nb, K, Nb = w.shape

    def body(w_ref, o_ref):
        o_ref[...] = w_ref[...]

    return pl.pallas_call(
        body, name="chip_blocks_to_columns",
        grid=(nb,),
        in_specs=[pl.BlockSpec((None, K, Nb), lambda b: (b, 0, 0))],
        out_specs=pl.BlockSpec((K, Nb), lambda b: (0, b)),
        out_shape=jax.ShapeDtypeStruct((K, nb * Nb), w.dtype),
        compiler_params=_params(("parallel",)),
    )(w)


def proj_ln(a, w, resid, gain, bias, tm=1024):
    S, K = a.shape
    D = w.shape[1]

    def body(a_ref, w_ref, r_ref, g_ref, b_ref, y_ref, xh_ref, rs_ref):
        z = ALPHA * r_ref[...] + _dot(a_ref[...].astype(BF16), w_ref[...])
        y, xh, rs = _ln_fwd(z, g_ref[...], b_ref[...])
        y_ref[...] = y
        xh_ref[...] = xh
        rs_ref[...] = rs

    row = pl.BlockSpec((tm, D), lambda i: (i, 0))
    vec = pl.BlockSpec((1, D), lambda i: (0, 0))
    return pl.pallas_call(
        body, name="proj_ln",
        grid=(S // tm,),
        in_specs=[pl.BlockSpec((tm, K), lambda i: (i, 0)), pl.BlockSpec((K, D), lambda i: (0, 0)), row, vec, vec],
        out_specs=[row, row, pl.BlockSpec((tm, 1), lambda i: (i, 0))],
        out_shape=[jax.ShapeDtypeStruct((S, D), F32), jax.ShapeDtypeStruct((S, D), F32), jax.ShapeDtypeStruct((S, 1), F32)],
        compiler_params=_params(("parallel",)),
    )(a, w, resid, gain, bias)


def mm_nt(a, w, resid, a_blocked, out_dtype=F32, tm=1024):
    if a_blocked:
        nk, S, Kb = a.shape
        a_spec = pl.BlockSpec((None, tm, Kb), lambda i, n: (n, i, 0))
    else:
        S, Kb = a.shape
        nk = 1
        a_spec = pl.BlockSpec((tm, Kb), lambda i, n: (i, 0))
    M = w.shape[0]
    has_resid = resid is not None

    def body(*refs):
        if has_resid:
            a_ref, w_ref, r_ref, o_ref, acc = refs
        else:
            a_ref, w_ref, o_ref, acc = refs
        n = pl.program_id(1)
        part = _dot_nt(a_ref[...].astype(BF16), w_ref[...])

        @pl.when(n == 0)
        def _():
            acc[...] = part

        @pl.when(n > 0)
        def _():
            acc[...] += part

        @pl.when(n == nk - 1)
        def _():
            out = acc[...]
            if has_resid:
                out = out + ALPHA * r_ref[...]
            o_ref[...] = out.astype(out_dtype)

    row = pl.BlockSpec((tm, M), lambda i, n: (i, 0))
    in_specs = [a_spec, pl.BlockSpec((M, Kb), lambda i, n: (0, n))] + ([row] if has_resid else [])
    args = (a, w) + ((resid,) if has_resid else ())
    return pl.pallas_call(
        body, name="mm_nt",
        grid=(S // tm, nk),
        in_specs=in_specs,
        out_specs=row,
        out_shape=jax.ShapeDtypeStruct((S, M), out_dtype),
        scratch_shapes=[pltpu.VMEM((tm, M), F32)],
        compiler_params=_params(("parallel", "arbitrary")),
    )(*args)


def mm_nt_dilated(parts, dils, w, resid, tm=1024):
    n_groups = len(parts)
    _, S, K = parts[0].shape
    M = w.shape[0]
    nk = 3 * n_groups

    def body(*refs):
        a_refs = refs[:n_groups]
        w_ref, r_ref, o_ref, group_acc, total = refs[n_groups:]
        n = pl.program_id(1)
        for g in range(n_groups):
            for k in range(3):
                @pl.when(n == 3 * g + k)
                def _():
                    part = _dot_nt(a_refs[g][...], w_ref[...])
                    for c in range(M // LANES):
                        lanes = slice(c * LANES, (c + 1) * LANES)
                        if k == 0:
                            group_acc[c] = part[:, lanes]
                        else:
                            group_acc[c] += part[:, lanes]
                        if k == 2:
                            _rows_from_dilated(group_acc.at[c], total.at[c], tm, dils[g], accumulate=g > 0)

        @pl.when(n == nk - 1)
        def _():
            for c in range(M // LANES):
                lanes = slice(c * LANES, (c + 1) * LANES)
                o_ref[:, lanes] = total[c] + ALPHA * r_ref[:, lanes]

    def a_spec(g):
        return pl.BlockSpec((None, tm, K), lambda i, n: (jnp.clip(n - 3 * g, 0, 2), i, 0))

    row = pl.BlockSpec((tm, M), lambda i, n: (i, 0))
    return pl.pallas_call(
        body, name="mm_nt_dilated",
        grid=(S // tm, nk),
        in_specs=[a_spec(g) for g in range(n_groups)] + [pl.BlockSpec((M, K), lambda i, n: (0, n)), row],
        out_specs=row,
        out_shape=jax.ShapeDtypeStruct((S, M), F32),
        scratch_shapes=[pltpu.VMEM((M // LANES, tm, LANES), F32), pltpu.VMEM((M // LANES, tm, LANES), F32)],
        compiler_params=_params(("parallel", "arbitrary")),
    )(*parts, w, resid)


def mm_tn(pairs, *, nblk, out_shape, out_block, out_index, alias=None, tk=1024, name="mm_tn"):
    operands = []
    for a, b, _ in pairs:
        for t in (a, b):
            if not any(t is o for o in operands):
                operands.append(t)
    where = lambda t: next(i for i, o in enumerate(operands) if o is t)
    S = pairs[0][0].shape[-2]
    n_out, n_in = len(pairs), len(operands)
    n_alias = len(alias) if alias is not None else 0

    def spec(t):
        if t.ndim == 3:
            return pl.BlockSpec((None, tk, t.shape[-1]), lambda j, k: (j, k, 0))
        return pl.BlockSpec((tk, t.shape[-1]), lambda j, k: (k, 0))

    def body(*refs):
        refs = refs[n_alias:]
        in_refs, o_refs, accs = refs[:n_in], refs[n_in:n_in + n_out], refs[n_in + n_out:]
        k = pl.program_id(1)
        for (a, b, scale), o_ref, acc in zip(pairs, o_refs, accs):
            part = _dot_tn(in_refs[where(a)][...].astype(BF16), in_refs[where(b)][...].astype(BF16))

            @pl.when(k == 0)
            def _():
                acc[...] = part

            @pl.when(k > 0)
            def _():
                acc[...] += part

            @pl.when(k == S // tk - 1)
            def _():
                o_ref[...] = (scale * acc[...]).astype(BF16)

    out_spec = pl.BlockSpec(out_block, lambda j, k: out_index(j))
    outs = pl.pallas_call(
        body, name=name,
        grid=(nblk, S // tk),
        in_specs=[ANY] * n_alias + [spec(t) for t in operands],
        out_specs=[out_spec] * n_out,
        out_shape=[jax.ShapeDtypeStruct(out_shape, BF16)] * n_out,
        scratch_shapes=[pltpu.VMEM((a.shape[-1], b.shape[-1]), F32) for a, b, _ in pairs],
        input_output_aliases={i: i for i in range(n_alias)},
        compiler_params=_params(("parallel", "arbitrary")),
    )(*(tuple(alias) if alias is not None else ()), *operands)
    return list(outs)


def ffn_fwd(x, wg, wu, wd, gain, bias, tm=1024):
    S, D = x.shape
    nb, FB = wg.shape[0], wg.shape[1]

    def body(x_ref, wg_ref, wu_ref, wd_ref, g_ref, b_ref, y_ref, xh_ref, rs_ref, a_ref, u_ref, acc, xb_ref):
        j = pl.program_id(1)

        @pl.when(j == 0)
        def _():
            xb_ref[...] = x_ref[...].astype(BF16)
            acc[...] = jnp.zeros_like(acc)

        xb = xb_ref[...]
        total = None
        for t0 in range(0, FB, FFN_HIDDEN_TILE):
            cols = pl.ds(t0, min(FFN_HIDDEN_TILE, FB - t0))
            a = _dot_nt(xb, wg_ref[cols, :])
            u = _dot_nt(xb, wu_ref[cols, :])
            a_ref[:, cols] = a.astype(BF16)
            u_ref[:, cols] = u.astype(BF16)
            h = a * jax.nn.sigmoid(a) * u
            part = _dot(h.astype(BF16), wd_ref[cols, :])
            total = part if total is None else total + part
        acc[...] += total

        @pl.when(j == nb - 1)
        def _():
            z = ALPHA * x_ref[...] + MACARON_WEIGHT * acc[...]
            y, xh, rs = _ln_fwd(z, g_ref[...], b_ref[...])
            y_ref[...] = y
            xh_ref[...] = xh
            rs_ref[...] = rs

    row = pl.BlockSpec((tm, D), lambda i, j: (i, 0))
    vec = pl.BlockSpec((1, D), lambda i, j: (0, 0))
    w_out = pl.BlockSpec((None, FB, D), lambda i, j: (j, 0, 0))
    act = pl.BlockSpec((None, tm, FB), lambda i, j: (j, i, 0))
    return pl.pallas_call(
        body, name="ffn_fwd",
        grid=(S // tm, nb),
        in_specs=[row, w_out, w_out, w_out, vec, vec],
        out_specs=[row, row, pl.BlockSpec((tm, 1), lambda i, j: (i, 0)), act, act],
        out_shape=[jax.ShapeDtypeStruct((S, D), F32), jax.ShapeDtypeStruct((S, D), F32), jax.ShapeDtypeStruct((S, 1), F32),
                   jax.ShapeDtypeStruct((nb, S, FB), BF16), jax.ShapeDtypeStruct((nb, S, FB), BF16)],
        scratch_shapes=[pltpu.VMEM((tm, D), F32), pltpu.VMEM((tm, D), BF16)],
        compiler_params=_params(("parallel", "arbitrary")),
    )(x, wg, wu, wd, gain, bias)


def ffn_bwd(dz, a, u, wg, wu, wd, tm=1024):
    S, D = dz.shape
    nb, FB = wg.shape[0], wg.shape[1]

    def body(dz_ref, a_ref, u_ref, wg_ref, wu_ref, wd_ref, dx_ref, h_ref, da_ref, du_ref, acc, dzb_ref):
        j = pl.program_id(1)

        @pl.when(j == 0)
        def _():
            dzb_ref[...] = (MACARON_WEIGHT * dz_ref[...]).astype(BF16)
            acc[...] = jnp.zeros_like(acc)

        dzb = dzb_ref[...]
        total = None
        for t0 in range(0, FB, FFN_HIDDEN_TILE):
            cols = pl.ds(t0, min(FFN_HIDDEN_TILE, FB - t0))
            dh = _dot_nt(dzb, wd_ref[cols, :])
            av = a_ref[:, cols].astype(F32)
            uv = u_ref[:, cols].astype(F32)
            s = jax.nn.sigmoid(av)
            silu = av * s
            h_ref[:, cols] = (silu * uv).astype(BF16)
            da = (dh * uv * (s * (1.0 + av * (1.0 - s)))).astype(BF16)
            du = (dh * silu).astype(BF16)
            da_ref[:, cols] = da
            du_ref[:, cols] = du
            both = jnp.concatenate([da, du], axis=1)
            weights = jnp.concatenate([wg_ref[cols, :], wu_ref[cols, :]], axis=0)
            part = _dot(both, weights)
            total = part if total is None else total + part
        acc[...] += total

        @pl.when(j == nb - 1)
        def _():
            dx_ref[...] = ALPHA * dz_ref[...] + acc[...]

    row = pl.BlockSpec((tm, D), lambda i, j: (i, 0))
    w_out = pl.BlockSpec((None, FB, D), lambda i, j: (j, 0, 0))
    act = pl.BlockSpec((None, tm, FB), lambda i, j: (j, i, 0))
    act_shape = jax.ShapeDtypeStruct((nb, S, FB), BF16)
    return pl.pallas_call(
        body, name="ffn_bwd",
        grid=(S // tm, nb),
        in_specs=[row, act, act, w_out, w_out, w_out],
        out_specs=[row, act, act, act],
        out_shape=[jax.ShapeDtypeStruct((S, D), F32), act_shape, act_shape, act_shape],
        scratch_shapes=[pltpu.VMEM((tm, D), F32), pltpu.VMEM((tm, D), BF16)],
        compiler_params=_params(("parallel", "arbitrary")),
    )(dz, a, u, wg, wu, wd)


def ln_bwd(dy, xhat, rstd, gain, tm=1024):
    S, D = dy.shape

    def body(dy_ref, xh_ref, rs_ref, g_ref, dz_ref, dg_ref, db_ref):
        i = pl.program_id(0)
        dy = dy_ref[...]
        xh = xh_ref[...]
        dxh = dy * g_ref[...]
        m1 = jnp.mean(dxh, axis=-1, keepdims=True)
        m2 = jnp.mean(dxh * xh, axis=-1, keepdims=True)
        dz_ref[...] = rs_ref[...] * (dxh - m1 - xh * m2)
        dg = jnp.sum(dy * xh, axis=0, keepdims=True)
        db = jnp.sum(dy, axis=0, keepdims=True)

        @pl.when(i == 0)
        def _():
            dg_ref[...] = dg
            db_ref[...] = db

        @pl.when(i > 0)
        def _():
            dg_ref[...] += dg
            db_ref[...] += db

    row = pl.BlockSpec((tm, D), lambda i: (i, 0))
    vec = pl.BlockSpec((1, D), lambda i: (0, 0))
    return pl.pallas_call(
        body, name="ln_bwd",
        grid=(S // tm,),
        in_specs=[row, row, pl.BlockSpec((tm, 1), lambda i: (i, 0)), vec],
        out_specs=[row, vec, vec],
        out_shape=[jax.ShapeDtypeStruct((S, D), F32), jax.ShapeDtypeStruct((1, D), F32), jax.ShapeDtypeStruct((1, D), F32)],
        compiler_params=_params(("arbitrary",)),
    )(dy, xhat, rstd, gain)


def loss_head(y, target, tm=1024):
    S, D = y.shape

    def body(y_ref, t_ref, dy_ref, l_ref):
        i = pl.program_id(0)
        e = y_ref[...] - t_ref[...]
        dy_ref[...] = e / D
        part = 0.5 * jnp.sum(jnp.mean(e * e, axis=-1, keepdims=True), axis=0, keepdims=True)

        @pl.when(i == 0)
        def _():
            l_ref[...] = part

        @pl.when(i > 0)
        def _():
            l_ref[...] += part

    row = pl.BlockSpec((tm, D), lambda i: (i, 0))
    return pl.pallas_call(
        body, name="loss_head",
        grid=(S // tm,),
        in_specs=[row, row],
        out_specs=[row, pl.BlockSpec((1, 1), lambda i: (0, 0))],
        out_shape=[jax.ShapeDtypeStruct((S, D), F32), jax.ShapeDtypeStruct((1, 1), F32)],
        compiler_params=_params(("arbitrary",)),
    )(y, target)


def _pool_window(xp, g):
    n = xp.shape[0]
    w = xp + pltpu.roll(xp, 1, 0)
    out = w
    for level, shift in enumerate((1, 2, 4), start=1):
        w = pltpu.roll(w, shift, 0) + pltpu.roll(w, n - shift, 0)
        out = jnp.where(g >= level, w, out)
    return out


def _pool_count(S, C, g):
    half = lax.shift_left(jnp.int32(1), g)
    t = lax.broadcasted_iota(jnp.int32, (S, C), 0)
    return (jnp.minimum(t + half, S) - jnp.maximum(t - half, 0)).astype(F32)


def pool_mix(u, wgrp, scale):
    S, D = u.shape
    G, C = wgrp.shape[0], wgrp.shape[1]

    def body(u_ref, w_ref, s_ref, mix_ref, v_ref, pad):
        g = pl.program_id(0)
        zeros = jnp.zeros((POOL_PAD, C), F32)
        pad[pl.ds(0, POOL_PAD), :] = zeros
        pad[pl.ds(POOL_PAD + S, POOL_PAD), :] = zeros
        pad[pl.ds(POOL_PAD, S), :] = u_ref[...]
        win = _pool_window(pad[...], g)[POOL_PAD:POOL_PAD + S]
        mixed = (win / _pool_count(S, C, g) - u_ref[...]).astype(BF16)
        mix_ref[...] = mixed
        v_ref[...] = _dot(mixed, w_ref[...]) * s_ref[...]

    col = pl.BlockSpec((S, C), lambda g: (0, g))
    return pl.pallas_call(
        body, name="pool_mix",
        grid=(G,),
        in_specs=[col, pl.BlockSpec((None, C, C), lambda g: (g, 0, 0)), pl.BlockSpec((1, C), lambda g: (0, g))],
        out_specs=[col, col],
        out_shape=[jax.ShapeDtypeStruct((S, D), BF16), jax.ShapeDtypeStruct((S, D), F32)],
        scratch_shapes=[pltpu.VMEM((S + 2 * POOL_PAD, C), F32)],
        compiler_params=_params(("arbitrary",)),
    )(u, wgrp, scale)


def pool_mix_bwd(dv, mixed, wgrp, scale):
    S, D = dv.shape
    G, C = wgrp.shape[0], wgrp.shape[1]

    def body(dv_ref, mix_ref, w_ref, s_ref, du_ref, dw_ref, ds_ref, pad):
        g = pl.program_id(0)
        mixed = mix_ref[...]
        dv = dv_ref[...]
        yg = _dot(mixed, w_ref[...])
        ds_ref[...] = jnp.sum(dv * yg, axis=0, keepdims=True)
        dyg = (dv * s_ref[...]).astype(BF16)
        dw_ref[...] = _dot_tn(mixed, dyg).astype(BF16)
        dmix = _dot_nt(dyg, w_ref[...])
        zeros = jnp.zeros((POOL_PAD, C), F32)
        pad[pl.ds(0, POOL_PAD), :] = zeros
        pad[pl.ds(POOL_PAD + S, POOL_PAD), :] = zeros
        pad[pl.ds(POOL_PAD, S), :] = dmix / _pool_count(S, C, g)
        win = _pool_window(pad[...], g)
        win = pltpu.roll(win, win.shape[0] - 1, 0)[POOL_PAD:POOL_PAD + S]
        du_ref[...] = win - dmix

    col = pl.BlockSpec((S, C), lambda g: (0, g))
    return pl.pallas_call(
        body, name="pool_mix_bwd",
        grid=(G,),
        in_specs=[col, col, pl.BlockSpec((None, C, C), lambda g: (g, 0, 0)), pl.BlockSpec((1, C), lambda g: (0, g))],
        out_specs=[col, pl.BlockSpec((None, C, C), lambda g: (g, 0, 0)), pl.BlockSpec((1, C), lambda g: (0, g))],
        out_shape=[jax.ShapeDtypeStruct((S, D), F32), jax.ShapeDtypeStruct((G, C, C), BF16), jax.ShapeDtypeStruct((1, D), F32)],
        scratch_shapes=[pltpu.VMEM((S + 2 * POOL_PAD, C), F32)],
        compiler_params=_params(("arbitrary",)),
    )(dv, mixed, wgrp, scale)


PERM_BLOCK = 256


def _dilated_runs(rows, d):
    n = PERM_BLOCK // d
    return [(c * PERM_BLOCK, r, n) for c in range(rows // PERM_BLOCK) for r in range(d)]


def _rows_to_dilated(src_ref, dst_ref, rows, d):
    for base, r, n in _dilated_runs(rows, d):
        dst_ref[pl.ds(base + r * n, n), :] = src_ref[pl.ds(base + r, n, stride=d), :].astype(dst_ref.dtype)


def _rows_from_dilated(src_ref, dst_ref, rows, d, accumulate=False):
    for base, r, n in _dilated_runs(rows, d):
        at = pl.ds(base + r, n, stride=d)
        v = src_ref[pl.ds(base + r * n, n), :]
        dst_ref[at, :] = dst_ref[at, :] + v if accumulate else v


def dilate_rows(x, dils, out_dtype, tm=1024):
    S, D = x.shape

    def body(x_ref, *o_refs):
        for d, o_ref in zip(dils, o_refs):
            _rows_to_dilated(x_ref, o_ref, tm, d)

    tile = pl.BlockSpec((tm, LANES), lambda i, j: (i, j))
    return pl.pallas_call(
        body, name="dilate_rows",
        grid=(S // tm, D // LANES),
        in_specs=[tile],
        out_specs=[tile] * len(dils),
        out_shape=[jax.ShapeDtypeStruct((S, D), out_dtype)] * len(dils),
        compiler_params=_params(("parallel", "parallel")),
    )(x)


def _slope_table(group, dilation):
    s = _alibi_slopes()[group].reshape(N_HEADS // 2, 2, 1, 1) * float(dilation)
    return jnp.asarray(np.broadcast_to(s, (N_HEADS // 2, 2, 1, ATTN_W)).copy())


def _residue_shape(S, D, d):
    return (S, D) if d == 1 else (S // PERM_BLOCK, d, PERM_BLOCK // d, D)


def _residue_view(x, d):
    return x.reshape(x.shape[:-2] + _residue_shape(x.shape[-2], x.shape[-1], d))


def _residue_spec(lead_block, lead_index, S, d):
    if d == 1:
        return pl.BlockSpec(lead_block + (S, LANES), lambda hp, r: lead_index + (0, hp))
    return pl.BlockSpec(lead_block + (S // PERM_BLOCK, None, PERM_BLOCK // d, LANES), lambda hp, r: lead_index + (0, r, 0, hp))


def _whole(ref, lead, L):
    return ref[lead + (slice(None),) * (len(ref.shape) - len(lead))].reshape(L, LANES)


def _query_rows(lead, i, d):
    if d == 1:
        return lead + (pl.ds(pl.multiple_of(i * ATTN_BQ, ATTN_BQ), ATTN_BQ), slice(None)), (ATTN_BQ, LANES)
    n = PERM_BLOCK // d
    return lead + (pl.ds(i * (ATTN_BQ // n), ATTN_BQ // n), slice(None), slice(None)), (ATTN_BQ // n, n, LANES)


def _load_query_rows(ref, lead, i, d):
    at, _ = _query_rows(lead, i, d)
    return ref[at].reshape(ATTN_BQ, LANES)


def _store_query_rows(ref, lead, i, d, value):
    at, shape = _query_rows(lead, i, d)
    ref[at] = value.reshape(shape)


def _stage_keys(dst, src_ref, L):
    rows = _whole(src_ref, (), L)
    lane = lax.broadcasted_iota(jnp.int32, (L, LANES), 1)
    zeros = jnp.zeros((ATTN_R, LANES), dst.dtype)
    for h in range(2):
        mine = (lane < HEAD_DIM) if h == 0 else (lane >= HEAD_DIM)
        dst[h, pl.ds(0, ATTN_R), :] = zeros
        dst[h, pl.ds(ATTN_R + L, ATTN_R), :] = zeros
        dst[h, pl.ds(ATTN_R, L), :] = jnp.where(mine, rows, jnp.zeros_like(rows))


def _fill_bias(bias, sl_ref):
    a = lax.broadcasted_iota(jnp.int32, (ATTN_BQ, ATTN_W), 0)
    c = lax.broadcasted_iota(jnp.int32, (ATTN_BQ, ATTN_W), 1)
    rel = jnp.abs(c - ATTN_R - a)
    band = rel <= ATTN_R
    after_start = c >= ATTN_R
    before_end = c < ATTN_BQ + ATTN_R
    for h in range(2):
        base = -(sl_ref[h] * rel.astype(F32))
        for variant in range(4):
            ok = band
            if variant & 1:
                ok = ok & after_start
            if variant & 2:
                ok = ok & before_end
            bias[variant, h] = jnp.where(ok, base, MASK_VALUE)


def _bias_variant(i, nq):
    return jnp.where(i == 0, 1, 0) + jnp.where(i == nq - 1, 2, 0)


def attn_fwd(qkv, group, dilation):
    _, S, D = qkv.shape
    d = dilation
    L = S // d
    nq = L // ATTN_BQ
    ncol = D // LANES
    view = _residue_view(qkv, d)
    slopes = _slope_table(group, d)
    scale = HEAD_DIM ** -0.5

    def body(q_ref, k_ref, v_ref, sl_ref, o_ref, lse_ref, k2, v2, bias):
        @pl.when(pl.program_id(1) == 0)
        def _():
            _fill_bias(bias, sl_ref)

        _stage_keys(k2, k_ref, L)
        _stage_keys(v2, v_ref, L)
        head0 = lax.broadcasted_iota(jnp.int32, (ATTN_BQ, LANES), 1) < HEAD_DIM

        def block(i):
            variant = _bias_variant(i, nq)
            win = pl.ds(pl.multiple_of(i * ATTN_BQ, ATTN_BQ), ATTN_W)
            qs = _load_query_rows(q_ref, (), i, d) * jnp.asarray(scale, BF16)
            es, ms, ls = [], [], []
            for h in range(2):
                s = _dot_nt(qs, k2[h, win, :]) + bias[variant, h]
                m = jnp.max(s, axis=-1, keepdims=True)
                e = jnp.exp(s - m)
                ls.append(jnp.sum(e, axis=-1, keepdims=True))
                ms.append(m)
                es.append(e.astype(BF16))
            acc = _dot(jnp.concatenate(es, axis=1), jnp.concatenate([v2[0, win, :], v2[1, win, :]], axis=0))
            out = acc * jnp.where(head0, 1.0 / ls[0], 1.0 / ls[1])
            lse = jnp.where(head0, ms[0] + jnp.log(ls[0]), ms[1] + jnp.log(ls[1]))
            return out, lse

        def step(t, carry):
            results = [block(t * group + b) for b in range(group)]
            for b, (out, lse) in enumerate(results):
                _store_query_rows(o_ref, (), t * group + b, d, out)
                _store_query_rows(lse_ref, (), t * group + b, d, lse)
            return carry

        group = min(ATTN_UNROLL, nq)
        lax.fori_loop(0, nq // group, step, 0)

    def col(which):
        return _residue_spec((None,), (which,), S, d)

    out = _residue_spec((), (), S, d)
    o, lse = pl.pallas_call(
        body, name=f"attn_fwd_g{group}",
        grid=(ncol, d),
        in_specs=[col(0), col(1), col(2), pl.BlockSpec((None, 2, 1, ATTN_W), lambda hp, r: (hp, 0, 0, 0))],
        out_specs=[out, out],
        out_shape=[jax.ShapeDtypeStruct(_residue_shape(S, D, d), F32)] * 2,
        scratch_shapes=[pltpu.VMEM((2, L + 2 * ATTN_R, LANES), BF16), pltpu.VMEM((2, L + 2 * ATTN_R, LANES), BF16),
                        pltpu.VMEM((4, 2, ATTN_BQ, ATTN_W), F32)],
        compiler_params=_params(("arbitrary", "arbitrary")),
    )(view, view, view, slopes)
    return o.reshape(S, D), lse.reshape(S, D)


def attn_combine(os, lses, dils, tm=1024):
    S, D = os[0].shape
    n = len(os)
    n_moved = sum(d > 1 for d in dils)

    def body(*refs):
        o_refs, l_refs, out_ref, lse_ref = list(refs[:n]), list(refs[n:2 * n]), refs[2 * n], refs[2 * n + 1]
        spare = list(refs[2 * n + 2:])
        for g, d in enumerate(dils):
            if d > 1:
                for which in (o_refs, l_refs):
                    token_order = spare.pop()
                    _rows_from_dilated(which[g], token_order, tm, d)
                    which[g] = token_order
        ls = [r[...] for r in l_refs]
        m = functools.reduce(jnp.maximum, ls)
        es = [jnp.exp(l - m) for l in ls]
        tot = functools.reduce(lambda x, y: x + y, es)
        inv = 1.0 / tot
        out_ref[...] = functools.reduce(lambda x, y: x + y, [(e * inv) * r[...] for e, r in zip(es, o_refs)])
        lse_ref[...] = m + jnp.log(tot)

    tile = pl.BlockSpec((tm, LANES), lambda i, j: (i, j))
    return pl.pallas_call(
        body, name="attn_combine",
        grid=(S // tm, D // LANES),
        in_specs=[tile] * (2 * n),
        out_specs=[tile, tile],
        out_shape=[jax.ShapeDtypeStruct((S, D), F32), jax.ShapeDtypeStruct((S, D), F32)],
        scratch_shapes=[pltpu.VMEM((tm, LANES), F32)] * (2 * n_moved),
        compiler_params=_params(("parallel", "parallel")),
    )(*os, *lses)


def attn_bwd(qkv, do, o, lse, group, dilation):
    _, S, D = qkv.shape
    d = dilation
    L = S // d
    nq = L // ATTN_BQ
    ncol = D // LANES
    view = _residue_view(qkv, d)
    slopes = _slope_table(group, d)
    scale = HEAD_DIM ** -0.5

    def body(q_ref, k_ref, v_ref, do_ref, o_ref, lse_ref, sl_ref, dx_ref, k2, v2, dkacc, dvacc, bias):
        @pl.when(pl.program_id(1) == 0)
        def _():
            _fill_bias(bias, sl_ref)

        _stage_keys(k2, k_ref, L)
        _stage_keys(v2, v_ref, L)
        dkacc[...] = jnp.zeros_like(dkacc)
        dvacc[...] = jnp.zeros_like(dvacc)
        lane = lax.broadcasted_iota(jnp.int32, (ATTN_BQ, LANES), 1)
        heads = (lane < HEAD_DIM, lane >= HEAD_DIM)
        key_head0 = lax.broadcasted_iota(jnp.int32, (ATTN_W, LANES), 1) < HEAD_DIM

        def step(i, carry):
            variant = _bias_variant(i, nq)
            win = pl.ds(pl.multiple_of(i * ATTN_BQ, ATTN_BQ), ATTN_W)
            q = _load_query_rows(q_ref, (), i, d)
            qs = q * jnp.asarray(scale, BF16)
            dov = _load_query_rows(do_ref, (), i, d)
            prod = dov * _load_query_rows(o_ref, (), i, d)
            lse_v = _load_query_rows(lse_ref, (), i, d)
            dob = dov.astype(BF16)
            dss, dks, dvs = [], [], []
            for h in range(2):
                s = _dot_nt(qs, k2[h, win, :]) + bias[variant, h]
                lse_h = jnp.max(jnp.where(heads[h], lse_v, -jnp.inf), axis=-1, keepdims=True)
                dterm = jnp.sum(jnp.where(heads[h], prod, 0.0), axis=-1, keepdims=True)
                p = jnp.exp(s - lse_h)
                dp = _dot_nt(dob, v2[h, win, :])
                ds = (p * (dp - dterm) * scale).astype(BF16)
                dvs.append(_dot_tn(p.astype(BF16), dob))
                dks.append(_dot_tn(ds, q))
                dss.append(ds)
            dq = _dot(jnp.concatenate(dss, axis=1), jnp.concatenate([k2[0, win, :], k2[1, win, :]], axis=0))
            dvacc[win, :] += jnp.where(key_head0, dvs[0], dvs[1])
            dkacc[win, :] += jnp.where(key_head0, dks[0], dks[1])
            _store_query_rows(dx_ref, (0,), i, d, dq.astype(BF16))
            return carry

        lax.fori_loop(0, nq, step, 0, unroll=min(ATTN_UNROLL, nq))
        block_shape = dx_ref.shape[1:]
        dx_ref[1] = dkacc[pl.ds(ATTN_R, L), :].astype(BF16).reshape(block_shape)
        dx_ref[2] = dvacc[pl.ds(ATTN_R, L), :].astype(BF16).reshape(block_shape)

    def col(which):
        return _residue_spec((None,), (which,), S, d)

    act = _residue_spec((), (), S, d)
    out = pl.pallas_call(
        body, name=f"attn_bwd_g{group}",
        grid=(ncol, d),
        in_specs=[col(0), col(1), col(2), act, act, act, pl.BlockSpec((None, 2, 1, ATTN_W), lambda hp, r: (hp, 0, 0, 0))],
        out_specs=_residue_spec((3,), (0,), S, d),
        out_shape=jax.ShapeDtypeStruct((3,) + _residue_shape(S, D, d), BF16),
        scratch_shapes=[pltpu.VMEM((2, L + 2 * ATTN_R, LANES), BF16), pltpu.VMEM((2, L + 2 * ATTN_R, LANES), BF16),
                        pltpu.VMEM((L + 2 * ATTN_R, LANES), F32), pltpu.VMEM((L + 2 * ATTN_R, LANES), F32),
                        pltpu.VMEM((4, 2, ATTN_BQ, ATTN_W), F32)],
        compiler_params=_params(("arbitrary", "arbitrary")),
    )(view, view, view, _residue_view(do, d), _residue_view(o, d), _residue_view(lse, d), slopes)
    return out.reshape(3, S, D)


TILE_ELEMS = 256 * 1024


def _row_tile(R, C):
    if R * C <= TILE_ELEMS or R % 16:
        return R
    return max(t for t in range(16, R + 1, 16) if R % t == 0 and (t * C <= TILE_ELEMS or t == 16))


def pair_sum(core, gs, recvs):
    n = len(gs)
    _, _, R, C = gs[0].shape
    tr = _row_tile(R, C)

    def body(c_ref, *refs):
        for g_ref, r_ref, o_ref in zip(refs[:n], refs[n:2 * n], refs[2 * n:]):
            o_ref[...] = (g_ref[...].astype(F32) + r_ref[...].astype(F32)).astype(BF16)

    blk = pl.BlockSpec((None, tr, C), lambda d, i, c_ref: (d, i, 0))
    mine = pl.BlockSpec((None, None, tr, C), lambda d, i, c_ref: (d, c_ref[0], i, 0))
    return pl.pallas_call(
        body, name="pair_sum",
        grid_spec=pltpu.PrefetchScalarGridSpec(
            num_scalar_prefetch=1, grid=(N_CHIPS, R // tr),
            in_specs=[mine] * n + [blk] * n,
            out_specs=[blk] * n),
        out_shape=[jax.ShapeDtypeStruct((N_CHIPS, R, C), BF16)] * n,
        compiler_params=_params(("parallel", "parallel")),
    )(core, *gs, *recvs)


def chip_sum(chip, owns, recvs):
    n = len(owns)
    _, R, C = owns[0].shape
    tr = _row_tile(R, C)
    slot_of_relation = {2: 0, 1: 1, 3: 2}

    def body(chip_ref, *refs):
        me = chip_ref[0]
        for own_ref, r_ref, o_ref in zip(refs[:n], refs[n:2 * n], refs[2 * n:]):
            mine = own_ref[...].astype(F32)
            theirs = {rel: r_ref[k].astype(F32) for rel, k in slot_of_relation.items()}
            acc = None
            for s in range(N_CHIPS):
                rel = jnp.bitwise_xor(me, s)
                part = jnp.where(rel == 0, mine, jnp.where(rel == 2, theirs[2], jnp.where(rel == 1, theirs[1], theirs[3])))
                acc = part if acc is None else acc + part
            o_ref[...] = acc

    return pl.pallas_call(
        body, name="chip_sum",
        grid_spec=pltpu.PrefetchScalarGridSpec(
            num_scalar_prefetch=1, grid=(R // tr,),
            in_specs=[pl.BlockSpec((None, tr, C), lambda i, chip_ref: (chip_ref[0], i, 0))] * n
            + [pl.BlockSpec((N_CHIPS - 1, tr, C), lambda i, chip_ref: (0, i, 0))] * n,
            out_specs=[pl.BlockSpec((tr, C), lambda i, chip_ref: (i, 0))] * n),
        out_shape=[jax.ShapeDtypeStruct((R, C), F32)] * n,
        compiler_params=_params(("parallel",)),
    )(chip, *owns, *recvs)


def adamw(core, w, g_pairs, m, v):
    L, H, R, C = w.shape
    tr = _row_tile(R, C)

    def body(c_ref, w_ref, *rest):
        g_refs = rest[:2 * L]
        m_ref, v_ref, g_ref, d_ref, nm_ref, nv_ref = rest[2 * L:]
        mine = pl.program_id(1) == c_ref[0]
        g = None
        for l in range(L):
            g_l = jnp.where(mine, g_refs[2 * l][...], g_refs[2 * l + 1][...])
            g = g_l if g is None else jnp.where(pl.program_id(0) == l, g_l, g)
        m = ADAM_B1 * m_ref[...] + (1.0 - ADAM_B1) * g
        v = ADAM_B2 * v_ref[...] + (1.0 - ADAM_B2) * (g * g)
        m_hat = m / (1.0 - ADAM_B1 ** ADAM_STEP)
        v_hat = v / (1.0 - ADAM_B2 ** ADAM_STEP)
        g_ref[...] = g
        d_ref[...] = -ADAM_LR * (m_hat / (jnp.sqrt(v_hat) + ADAM_EPS) + ADAM_WD * w_ref[...])
        nm_ref[...] = m
        nv_ref[...] = v

    blk = pl.BlockSpec((None, None, tr, C), lambda l, h, i, c_ref: (l, h, i, 0))

    def half(layer):
        return pl.BlockSpec((tr, C), lambda l, h, i, c_ref: (jnp.where(l == layer, i, 0), 0))

    shape = jax.ShapeDtypeStruct((L, H, R, C), F32)
    return pl.pallas_call(
        body, name="adamw",
        grid_spec=pltpu.PrefetchScalarGridSpec(
            num_scalar_prefetch=1, grid=(L, H, R // tr),
            in_specs=[blk] + [half(l) for l in range(L) for _ in range(2)] + [blk, blk],
            out_specs=[blk] * 4),
        out_shape=[shape] * 4,
        compiler_params=_params(("parallel", "parallel", "parallel")),
    )(core, w, *[g for pair in g_pairs for g in pair], m, v)


def _place():
    return lax.axis_index("x"), lax.axis_index("y"), lax.axis_index("c")


def _other_chips(x, y):
    return [(2 * (1 - x) + y, (1 - x, y)), (2 * x + (1 - y), (x, 1 - y)), (2 * (1 - x) + (1 - y), (1 - x, 1 - y))]


def all_gather_shards(shards, placed):
    n = len(shards)

    def body(*refs):
        ins, outs = refs[:n], refs[2 * n:3 * n]
        send_sems, recv_sems = refs[3 * n:]
        x, y, c = _place()
        me = 2 * x + y
        sibling = (x, y, 1 - c)
        chips = _other_chips(x, y)

        def copy(a, k, src, dst, to):
            return pltpu.make_async_remote_copy(src_ref=src, dst_ref=dst, send_sem=send_sems.at[a, k], recv_sem=recv_sems.at[a, k],
                                                device_id=to, device_id_type=MESH)

        sends = []
        for a in range(n):
            for k, (_, (px, py)) in enumerate(chips):
                cp = copy(a, k, ins[a].at[c], outs[a].at[me, c], (px, py, c))
                cp.start()
                sends.append(cp)
        for a in range(n):
            for k, (chip, _) in enumerate(chips):
                landed = outs[a].at[chip, c]
                copy(a, k, landed, landed, sibling).wait_recv()
                cp = copy(a, 3 + k, landed, landed, sibling)
                cp.start()
                sends.append(cp)
        for a in range(n):
            for k, (chip, _) in enumerate(chips):
                other = outs[a].at[chip, 1 - c]
                copy(a, 3 + k, other, other, sibling).wait_recv()
        for cp in sends:
            cp.wait_send()

    return pl.pallas_call(
        body, name="all_gather_shards",
        in_specs=[ANY] * (2 * n),
        out_specs=[ANY] * n,
        out_shape=[jax.ShapeDtypeStruct(p.shape, p.dtype) for p in placed],
        scratch_shapes=[pltpu.SemaphoreType.DMA((n, 6)), pltpu.SemaphoreType.DMA((n, 6))],
        input_output_aliases={n + a: a for a in range(n)},
        compiler_params=pltpu.CompilerParams(has_side_effects=True),
    )(*shards, *placed)


def all_gather_shards_async(shards, placed, collective_id, name):
    n = len(shards)
    srcs = [jax.new_ref(s, memory_space=pltpu.MemorySpace.HBM) for s in shards]
    dsts = [jax.new_ref(p, memory_space=pltpu.MemorySpace.HBM) for p in placed]

    @pl.kernel(mesh=plsc.ScalarSubcoreMesh(axis_name="sequencer", num_cores=1), name=name,
               scratch_types=(pltpu.SemaphoreType.DMA((n, 6)), pltpu.SemaphoreType.DMA((n, 6))),
               compiler_params=pltpu.CompilerParams(collective_id=collective_id))
    def launch(send_sems, recv_sems):
        x, y, c = _place()
        me = 2 * x + y
        sibling = (x, y, 1 - c)
        chips = _other_chips(x, y)
        barrier = pltpu.get_barrier_semaphore()
        peers = [sibling] + [(px, py, c) for _, (px, py) in chips]
        for peer in peers:
            pl.semaphore_signal(barrier, inc=1, device_id=peer, device_id_type=MESH)
        pl.semaphore_wait(barrier, len(peers))

        def copy(a, k, src, dst, to):
            return pltpu.make_async_remote_copy(src_ref=src, dst_ref=dst, send_sem=send_sems.at[a, k], recv_sem=recv_sems.at[a, k],
                                                device_id=to, device_id_type=MESH)

        sends = []
        for a in range(n):
            for k, (_, (px, py)) in enumerate(chips):
                cp = copy(a, k, srcs[a].at[c], dsts[a].at[me, c], (px, py, c))
                cp.start()
                sends.append(cp)
        for a in range(n):
            for k, (chip, _) in enumerate(chips):
                landed = dsts[a].at[chip, c]
                copy(a, k, landed, landed, sibling).wait_recv()
                cp = copy(a, 3 + k, landed, landed, sibling)
                cp.start()
                sends.append(cp)
        for a in range(n):
            for k, (chip, _) in enumerate(chips):
                other = dsts[a].at[chip, 1 - c]
                copy(a, 3 + k, other, other, sibling).wait_recv()
        for cp in sends:
            cp.wait_send()

    launch()
    return [d[...] for d in dsts]


def _exchange(body, ins, out_shapes, sem_shapes, name, peers=None, collective_id=None):
    n_in, n_out = len(ins), len(out_shapes)
    sems = [pltpu.SemaphoreType.DMA(shape) for shape in sem_shapes]
    if collective_id is None:
        def tc_body(*refs):
            body(refs[:n_in], refs[n_in:n_in + n_out], *refs[n_in + n_out:])

        return pl.pallas_call(tc_body, name=name, in_specs=[ANY] * n_in, out_specs=[ANY] * n_out, out_shape=out_shapes,
                              scratch_shapes=sems, compiler_params=pltpu.CompilerParams(has_side_effects=True))(*ins)
    srcs = [jax.new_ref(a, memory_space=pltpu.MemorySpace.HBM) for a in ins]
    dsts = [jax.empty_ref(shape, memory_space=pltpu.MemorySpace.HBM) for shape in out_shapes]

    @pl.kernel(mesh=plsc.ScalarSubcoreMesh(axis_name="sequencer", num_cores=1), name=name, scratch_types=tuple(sems),
               compiler_params=pltpu.CompilerParams(collective_id=collective_id))
    def launch(*sem_refs):
        barrier = pltpu.get_barrier_semaphore()
        others = peers(*_place())
        for peer in others:
            pl.semaphore_signal(barrier, inc=1, device_id=peer, device_id_type=MESH)
        pl.semaphore_wait(barrier, len(others))
        body(srcs, dsts, *sem_refs)

    launch()
    return [d[...] for d in dsts]


def _sibling(x, y, c):
    return [(x, y, 1 - c)]


def _same_core_of_other_chips(x, y, c):
    return [(px, py, c) for _, (px, py) in _other_chips(x, y)]


def sibling_exchange_halves(grads, name="sibling_exchange_halves", collective_id=None):
    n = len(grads)

    def body(ins, outs, send_sems, recv_sems):
        x, y, c = _place()
        copies = [pltpu.make_async_remote_copy(src_ref=ins[a].at[:, 1 - c], dst_ref=outs[a], send_sem=send_sems.at[a],
                                               recv_sem=recv_sems.at[a], device_id=(x, y, 1 - c), device_id_type=MESH) for a in range(n)]
        for cp in copies:
            cp.start()
        for cp in copies:
            cp.wait()

    shapes = [jax.ShapeDtypeStruct((N_CHIPS,) + g.shape[2:], g.dtype) for g in grads]
    return _exchange(body, grads, shapes, [(n,), (n,)], name, _sibling, collective_id)


def chip_exchange(sums, name="chip_exchange", collective_id=None):
    n = len(sums)

    def body(ins, outs, send_sems, recv_sems):
        x, y, c = _place()
        copies = []
        for a in range(n):
            for k, (chip, (px, py)) in enumerate(_other_chips(x, y)):
                cp = pltpu.make_async_remote_copy(src_ref=ins[a].at[chip], dst_ref=outs[a].at[k], send_sem=send_sems.at[a, k],
                                                  recv_sem=recv_sems.at[a, k], device_id=(px, py, c), device_id_type=MESH)
                cp.start()
                copies.append(cp)
        for cp in copies:
            cp.wait()

    shapes = [jax.ShapeDtypeStruct((N_CHIPS - 1,) + s.shape[1:], s.dtype) for s in sums]
    return _exchange(body, sums, shapes, [(n, 3), (n, 3)], name, _same_core_of_other_chips, collective_id)


def sibling_share(halves, name="sibling_share", collective_id=None):
    n = len(halves)

    def body(ins, outs, send_sems, recv_sems):
        x, y, c = _place()
        copies = [pltpu.make_async_remote_copy(src_ref=ins[a], dst_ref=outs[a], send_sem=send_sems.at[a], recv_sem=recv_sems.at[a],
                                               device_id=(x, y, 1 - c), device_id_type=MESH) for a in range(n)]
        for cp in copies:
            cp.start()
        for cp in copies:
            cp.wait()

    shapes = [jax.ShapeDtypeStruct(h.shape, h.dtype) for h in halves]
    return _exchange(body, halves, shapes, [(n,), (n,)], name, _sibling, collective_id)


def all_reduce_small(v):
    R, C = v.shape
    n_dev = 8

    def body(v_ref, o_ref, buf, send_sems, recv_sems):
        x, y, c = _place()
        me = 4 * x + 2 * y + c
        buf[me] = v_ref[...]
        copies = []
        for rel in range(1, n_dev):
            fx, fy, fc = rel >> 2, (rel >> 1) & 1, rel & 1
            peer = (x ^ fx, y ^ fy, c ^ fc)
            cp = pltpu.make_async_remote_copy(src_ref=v_ref, dst_ref=buf.at[me], send_sem=send_sems.at[rel - 1],
                                              recv_sem=recv_sems.at[rel - 1], device_id=peer, device_id_type=MESH)
            cp.start()
            copies.append(cp)
        for cp in copies:
            cp.wait()
        acc = buf[0]
        for k in range(1, n_dev):
            acc = acc + buf[k]
        o_ref[...] = acc

    return pl.pallas_call(
        body, name="all_reduce_small",
        in_specs=[pl.BlockSpec(memory_space=pltpu.VMEM)],
        out_specs=pl.BlockSpec(memory_space=pltpu.VMEM),
        out_shape=jax.ShapeDtypeStruct((R, C), F32),
        scratch_shapes=[pltpu.VMEM((n_dev, R, C), F32), pltpu.SemaphoreType.DMA((n_dev - 1,)), pltpu.SemaphoreType.DMA((n_dev - 1,))],
        compiler_params=pltpu.CompilerParams(has_side_effects=True),
    )(v)


WEIGHT_NAMES = ("ffn1_w_gate", "ffn1_w_up", "ffn1_w_down", "ffn2_w_gate", "ffn2_w_up", "ffn2_w_down", "ln_gain", "ln_bias",
                "pool_w_in", "pool_w_group", "pool_scale", "pool_w_out", "attn_w_qkv", "attn_w_out")
MATRIX_NAMES = ("ffn1_w_gate", "ffn1_w_up", "ffn1_w_down", "ffn2_w_gate", "ffn2_w_up", "ffn2_w_down",
                "pool_w_in", "pool_w_group", "pool_w_out", "attn_w_qkv", "attn_w_out")


TRANSPOSED_NAMES = ("ffn1_w_gate", "ffn1_w_up", "ffn2_w_gate", "ffn2_w_up")


def _halves(name, w):
    if name in TRANSPOSED_NAMES:
        w = jnp.swapaxes(w, 1, 2)
    return w.reshape(2, -1, w.shape[-1])


def _unhalves(name, t, shape):
    if name in TRANSPOSED_NAMES:
        return jnp.swapaxes(t.reshape(shape[0], shape[2], shape[1]), 1, 2)
    return t.reshape(shape)


def kernel(x, ffn1_w_gate, ffn1_w_up, ffn1_w_down, ffn2_w_gate, ffn2_w_up, ffn2_w_down, ln_gain, ln_bias, pool_w_in, pool_w_group, pool_scale, pool_w_out, attn_w_qkv, attn_w_out, loss_target, m_ffn1_w_gate, m_ffn1_w_up, m_ffn1_w_down, m_ffn2_w_gate, m_ffn2_w_up, m_ffn2_w_down, m_ln_gain, m_ln_bias, m_pool_w_in, m_pool_w_group, m_pool_scale, m_pool_w_out, m_attn_w_qkv, m_attn_w_out, v_ffn1_w_gate, v_ffn1_w_up, v_ffn1_w_down, v_ffn2_w_gate, v_ffn2_w_up, v_ffn2_w_down, v_ln_gain, v_ln_bias, v_pool_w_in, v_pool_w_group, v_pool_scale, v_pool_w_out, v_attn_w_qkv, v_attn_w_out):
    weights = dict(zip(WEIGHT_NAMES, (ffn1_w_gate, ffn1_w_up, ffn1_w_down, ffn2_w_gate, ffn2_w_up, ffn2_w_down, ln_gain, ln_bias,
                                      pool_w_in, pool_w_group, pool_scale, pool_w_out, attn_w_qkv, attn_w_out)))
    moms = dict(zip(WEIGHT_NAMES, (m_ffn1_w_gate, m_ffn1_w_up, m_ffn1_w_down, m_ffn2_w_gate, m_ffn2_w_up, m_ffn2_w_down, m_ln_gain,
                                   m_ln_bias, m_pool_w_in, m_pool_w_group, m_pool_scale, m_pool_w_out, m_attn_w_qkv, m_attn_w_out)))
    vels = dict(zip(WEIGHT_NAMES, (v_ffn1_w_gate, v_ffn1_w_up, v_ffn1_w_down, v_ffn2_w_gate, v_ffn2_w_up, v_ffn2_w_down, v_ln_gain,
                                   v_ln_bias, v_pool_w_in, v_pool_w_group, v_pool_scale, v_pool_w_out, v_attn_w_qkv, v_attn_w_out)))
    S, D = x.shape[1], x.shape[2]
    FB = ffn1_w_gate.shape[2]
    QKV = attn_w_qkv.shape[2] * N_CHIPS
    G, CB = pool_w_group.shape[1], pool_w_group.shape[2]
    C = pool_w_group.shape[3]
    cx, cy, cc = _place()
    chip = 2 * cx + cy
    xs = x.reshape(S, D)
    target = loss_target.reshape(S, D)

    ln_rows = jnp.concatenate([ln_gain, ln_bias, jnp.zeros((DEPTH, 2, ln_gain.shape[2]), F32)], axis=1)
    shard = {n: _halves(n, weights[n]).astype(BF16) for n in MATRIX_NAMES}
    ffn_layer = lambda f, i: {f"{f}{s}@{i}": shard[f + s][i].reshape(2, FB // 2, D) for s in ("_w_gate", "_w_up", "_w_down")}
    pool = {n: shard[n] for n in ("pool_w_in", "pool_w_group", "pool_w_out")}
    attn = {n: shard[n] for n in ("attn_w_qkv", "attn_w_out")}
    groups = [dict(ffn_layer("ffn1", 0), ln=ln_rows), pool, ffn_layer("ffn2", 0), ffn_layer("ffn1", 1), attn, ffn_layer("ffn2", 1)]
    full = {}

    def launch(k, after=None):
        parts = list(groups[k].values())
        if after is not None:
            *parts, after = lax.optimization_barrier((*parts, after))
        placed = [lax.dynamic_update_slice(lax.empty((N_CHIPS,) + s.shape, s.dtype), s[None], (chip, 0, 0, 0)) for s in parts]
        full.update(zip(groups[k], all_gather_shards_async(parts, placed, GATHER_COLLECTIVE_ID, f"gather_weights_{k}")))
        return after

    for k in range(4):
        launch(k)
    ffn_w = lambda f, i: [full[f"{f}{s}@{i}"].reshape(N_CHIPS, FB, D) for s in ("_w_gate", "_w_up", "_w_down")]
    w_pool_in = full["pool_w_in"].reshape(D, D)
    w_pool_out = full["pool_w_out"].reshape(D, D)
    w_group = full["pool_w_group"].reshape(N_CHIPS, G, CB, C).transpose(1, 0, 2, 3).reshape(G, N_CHIPS * CB, C)
    ln_full = full["ln"].transpose(1, 2, 0, 3).reshape(DEPTH, 8, D)
    gain = lambda i, k: ln_full[i, k].reshape(1, D)
    bias = lambda i, k: ln_full[i, 3 + k].reshape(1, D)

    dils = [dil for _, dil in DIL_CONFIGS]
    moved_dils = [dil for dil in dils if dil > 1]

    saved = []
    y = xs
    for i in range(DEPTH):
        y_in = y
        y, xh, rs, a, u = ffn_fwd(y_in, *ffn_w("ffn1", i), gain(i, 0), bias(i, 0))
        f1 = (y_in, xh, rs, a, u)
        if i == 0:
            y = launch(4, after=y)
        y_mid = y
        if i % 2 == 0:
            pu = mm_nn(y_mid, w_pool_in, F32)[0]
            mixed, pv = pool_mix(pu, w_group, pool_scale)
            y, xh, rs = proj_ln(pv, w_pool_out, y_mid, gain(i, 1), bias(i, 1))
            mix = (y_mid, xh, rs, mixed, pv)
            if i == 0:
                y = launch(5, after=y)
        else:
            qkv_blocks, y_mid = lax.optimization_barrier((full["attn_w_qkv"], y_mid))
            w_qkv = chip_blocks_to_columns(qkv_blocks.reshape(N_CHIPS, D, QKV // N_CHIPS))
            w_attn_out = full["attn_w_out"].reshape(D, D)
            moved = dict(zip(moved_dils, dilate_rows(y_mid, moved_dils, BF16)))
            srcs = [moved.get(dil, y_mid) for dil in dils]
            qkvs = [mm_nn(src, w_qkv, BF16, first_block=3 * g, nb=3) for g, src in enumerate(srcs)]
            parts = [attn_fwd(qkv, g, dil) for g, (qkv, dil) in enumerate(zip(qkvs, dils))]
            ao, lse = attn_combine([p[0] for p in parts], [p[1] for p in parts], dils)
            y, xh, rs = proj_ln(ao, w_attn_out, y_mid, gain(i, 1), bias(i, 1))
            mix = (y_mid, xh, rs, srcs, qkvs, ao, lse, w_qkv, w_attn_out)
        y_in2 = y
        y, xh, rs, a, u = ffn_fwd(y_in2, *ffn_w("ffn2", i), gain(i, 2), bias(i, 2))
        f2 = (y_in2, xh, rs, a, u)
        saved.append((f1, mix, f2))

    dy, loss_part = loss_head(y, target)
    loss = lax.psum(loss_part[0, 0], ("x", "y", "c"))

    core = cc.reshape(1).astype(jnp.int32)
    chip_id = chip.reshape(1).astype(jnp.int32)
    dgain = [[None] * 3 for _ in range(DEPTH)]
    dbias = [[None] * 3 for _ in range(DEPTH)]
    dscale = None
    pieces = []
    own_half, other_half = {}, {}

    def tie(arrays, after):
        *arrays, after = lax.optimization_barrier((*arrays, after))
        return arrays, after

    def by_shape(fn, xs, ys):
        out = [None] * len(xs)
        for shape in dict.fromkeys(x.shape for x in xs):
            idx = [k for k, x in enumerate(xs) if x.shape == shape]
            for k, res in zip(idx, fn([xs[k] for k in idx], [ys[k] for k in idx])):
                out[k] = res
        return out

    def start_piece(keys, arrays):
        blocks = [g.reshape(N_CHIPS, 2, -1, g.shape[-1]) for g in arrays]
        k = len(pieces)
        pieces.append(dict(keys=keys, blocks=blocks, from_sibling=sibling_exchange_halves(
            blocks, name=f"reduce_halves_{k}", collective_id=SIBLING_COLLECTIVE_ID)))

    def pair_sums_and_chip_exchange(k, after):
        piece = pieces[k]
        received, after = tie(piece["from_sibling"], after)
        piece["pair_sums"] = by_shape(lambda gs, rs: pair_sum(core, gs, rs), piece["blocks"], received)
        piece["from_chips"] = chip_exchange(piece["pair_sums"], name=f"reduce_chips_{k}", collective_id=CHIPS_COLLECTIVE_ID)
        return after

    def chip_sums_and_share(k, after):
        piece = pieces[k]
        received, after = tie(piece["from_chips"], after)
        mine = by_shape(lambda ps, rs: chip_sum(chip_id, ps, rs), piece["pair_sums"], received)
        theirs = sibling_share(mine, name=f"reduce_share_{k}", collective_id=SIBLING_COLLECTIVE_ID)
        own_half.update(zip(piece["keys"], mine))
        other_half.update(zip(piece["keys"], theirs))
        return after

    def piece_done(keys, arrays, dy):
        start_piece(keys, arrays)
        k = len(pieces) - 1
        if k >= 1:
            dy = pair_sums_and_chip_exchange(k - 1, dy)
        if k >= 2:
            dy = chip_sums_and_share(k - 2, dy)
        return dy

    def ffn_backward(name, i, dy, state):
        y_in, xh, rs, a, u = state
        k = 0 if name == "ffn1" else 2
        dz, dgain[i][k], dbias[i][k] = ln_bwd(dy, xh, rs, gain(i, k))
        dx, h, da, du = ffn_bwd(dz, a, u, *ffn_w(name, i))
        outs = mm_tn([(da, y_in, 1.0), (du, y_in, 1.0), (h, dz, MACARON_WEIGHT)], nblk=N_CHIPS, out_shape=(N_CHIPS, FB, D),
                     out_block=(None, FB, D), out_index=lambda j: (j, 0, 0), name="ffn_wgrad")
        return piece_done([(name + s, i) for s in ("_w_gate", "_w_up", "_w_down")], outs, dx)

    def square_grad(a, b):
        return mm_tn([(a, b, 1.0)], nblk=1, out_shape=(D, D), out_block=(D, D), out_index=lambda j: (0, 0), name="square_wgrad")[0]

    for i in reversed(range(DEPTH)):
        f1, mix, f2 = saved[i]
        dy = ffn_backward("ffn2", i, dy, f2)
        dz, dgain[i][1], dbias[i][1] = ln_bwd(dy, mix[1], mix[2], gain(i, 1))
        if i % 2 == 0:
            y_mid, _, _, mixed, pv = mix
            g_out = square_grad(pv, dz)
            dv = mm_nt(dz, w_pool_out, None, a_blocked=False)
            du, dwg, dscale = pool_mix_bwd(dv, mixed, w_group, pool_scale)
            g_group = dwg.reshape(G, N_CHIPS, CB, C).transpose(1, 0, 2, 3)
            g_in = square_grad(y_mid, du)
            dy = mm_nt(du, w_pool_in, dz, a_blocked=False)
            dy = piece_done([("pool_w_out", 0), ("pool_w_group", 0), ("pool_w_in", 0)], [g_out, g_group, g_in], dy)
        else:
            y_mid, _, _, srcs, qkvs, ao, lse, w_qkv, w_attn_out = mix
            g_out = square_grad(ao, dz)
            dao = mm_nt(dz, w_attn_out, None, a_blocked=False)
            in_order = [dict(zip(moved_dils, dilate_rows(t, moved_dils, F32))) for t in (dao, ao, lse)]
            dqkvs = [attn_bwd(qkvs[g], *[m.get(dil, t) for m, t in zip(in_order, (dao, ao, lse))], g, dil)
                     for g, dil in enumerate(dils)]
            g_qkv = [mm_tn([(src, dqkv, 1.0)], nblk=3, out_shape=(D, 3 * D), out_block=(D, D), out_index=lambda j: (0, j),
                           name="qkv_wgrad")[0] for src, dqkv in zip(srcs, dqkvs)]
            g_qkv = jnp.concatenate(g_qkv, axis=1).reshape(D, N_CHIPS, QKV // N_CHIPS).transpose(1, 0, 2)
            dy = mm_nt_dilated(dqkvs, dils, w_qkv, dz)
            dy = piece_done([("attn_w_out", 0), ("attn_w_qkv", 0)], [g_out, g_qkv], dy)
        dy = ffn_backward("ffn1", i, dy, f1)
    grad_x = dy.reshape(x.shape)

    last = len(pieces) - 1
    small = jnp.concatenate([jnp.concatenate(dgain[i] + dbias[i], axis=0) for i in range(DEPTH)] + [dscale, jnp.zeros((3, D), F32)], axis=0)
    small = all_reduce_small(small)
    per_layer = small[:6 * DEPTH].reshape(DEPTH, 6, D)
    cols = D // N_CHIPS
    small_grads = {"ln_gain": lax.dynamic_slice_in_dim(per_layer[:, 0:3], chip * cols, cols, axis=2),
                   "ln_bias": lax.dynamic_slice_in_dim(per_layer[:, 3:6], chip * cols, cols, axis=2),
                   "pool_scale": small[6 * DEPTH:6 * DEPTH + 1]}

    grad_w, delta, new_m, new_v = {}, {}, {}, {}

    def update(n, after):
        shape = weights[n].shape
        if n in MATRIX_NAMES:
            layers = DEPTH if (n, 1) in own_half else 1
            as4 = lambda t: _halves(n, t).reshape(layers, 2, -1, shape[-1] if n not in TRANSPOSED_NAMES else shape[1])
            mine, after = tie([own_half[n, l] for l in range(layers)], after)
            outs = adamw(core, as4(weights[n]), [(mine[l], other_half[n, l]) for l in range(layers)], as4(moms[n]), as4(vels[n]))
            grad_w[n], delta[n], new_m[n], new_v[n] = [_unhalves(n, t, shape) for t in outs]
        else:
            as4 = lambda t: t.reshape(1, 1, -1, shape[-1])
            g2 = small_grads[n].reshape(-1, shape[-1])
            outs = adamw(core, as4(weights[n]), [(g2, g2)], as4(moms[n]), as4(vels[n]))
            grad_w[n], delta[n], new_m[n], new_v[n] = [t.reshape(shape) for t in outs]
        return outs[1]

    marker = small
    for n in ("ln_gain", "ln_bias", "pool_scale"):
        marker = update(n, marker)
    marker = pair_sums_and_chip_exchange(last, marker)
    marker = chip_sums_and_share(last - 1, marker)
    for n in MATRIX_NAMES:
        if not n.startswith("ffn1"):
            marker = update(n, marker)
    marker = chip_sums_and_share(last, marker)
    for n in MATRIX_NAMES:
        if n.startswith("ffn1"):
            marker = update(n, marker)

    return (loss, grad_x, *[grad_w[n] for n in WEIGHT_NAMES], *[delta[n] for n in WEIGHT_NAMES],
            *[new_m[n] for n in WEIGHT_NAMES], *[new_v[n] for n in WEIGHT_NAMES])
```

```python
import functools
import math

import numpy as np
import jax
import jax.numpy as jnp
from jax import lax
from jax.experimental import pallas as pl
from jax.experimental.pallas import tpu as pltpu
from jax.experimental.pallas import tpu_sc as plsc

F32 = jnp.float32
BF16 = jnp.bfloat16

DEPTH = 2
ALPHA = (2.0 * DEPTH) ** 0.25
MACARON_WEIGHT = 0.5
LN_EPS = 1e-5
MASK_VALUE = -1e30
POOL_WINDOWS = (2, 4, 8, 16)
POOL_PAD = 16
HEAD_DIM = 64
N_HEADS = 16
DIL_CONFIGS = ((128, 1), (512, 4), (2048, 16))
ATTN_R = 64
ATTN_BQ = 128
ATTN_W = ATTN_BQ + 2 * ATTN_R
FFN_HIDDEN_TILE = 256
ATTN_UNROLL = 8
LANES = 128
ADAM_LR = 0.001
ADAM_B1 = 0.9
ADAM_B2 = 0.999
ADAM_EPS = 1e-08
ADAM_WD = 0.01
ADAM_STEP = 10
N_CHIPS = 4
GATHER_COLLECTIVE_ID = 1
SIBLING_COLLECTIVE_ID = 2
CHIPS_COLLECTIVE_ID = 3
VMEM_LIMIT = 56 * 1024 * 1024
MESH = pl.DeviceIdType.MESH
ANY = pl.BlockSpec(memory_space=pl.ANY)


def _params(sem=None, vmem=VMEM_LIMIT):
    return pltpu.CompilerParams(dimension_semantics=sem, vmem_limit_bytes=vmem)


def _alibi_slopes():
    n = len(DIL_CONFIGS) * N_HEADS
    s = 2.0 ** (-8.0 * np.arange(1, n + 1) / n)
    return s.reshape(len(DIL_CONFIGS), N_HEADS).astype(np.float32)


def _ln_fwd(z, g, b):
    mu = jnp.mean(z, axis=-1, keepdims=True)
    zc = z - mu
    var = jnp.mean(zc * zc, axis=-1, keepdims=True)
    rstd = lax.rsqrt(var + LN_EPS)
    xhat = zc * rstd
    return xhat * g + b, xhat, rstd


def _dot(a, b):
    return jnp.dot(a, b, preferred_element_type=F32)


def _dot_nt(a, b):
    return lax.dot_general(a, b, (((1,), (1,)), ((), ())), preferred_element_type=F32)


def _dot_tn(a, b):
    return lax.dot_general(a, b, (((0,), (0,)), ((), ())), preferred_element_type=F32)


def mm_nn(a, b, out_dtype, first_block=0, nb=None, tm=1024):
    S, K = a.shape
    Nb = K
    nb = b.shape[1] // Nb if nb is None else nb

    def body(a_ref, b_ref, o_ref):
        o_ref[...] = _dot(a_ref[...].astype(BF16), b_ref[...]).astype(out_dtype)

    return pl.pallas_call(
        body, name="mm_nn",
        grid=(S // tm, nb),
        in_specs=[pl.BlockSpec((tm, K), lambda i, j: (i, 0)), pl.BlockSpec((K, Nb), lambda i, j: (0, first_block + j))],
        out_specs=pl.BlockSpec((None, tm, Nb), lambda i, j: (j, i, 0)),
        out_shape=jax.ShapeDtypeStruct((nb, S, Nb), out_dtype),
        compiler_params=_params(("parallel", "arbitrary")),
    )(a, b)


def chip_blocks_to_columns(w):
    nb, K, Nb = w.shape

    def body(w_ref, o_ref):
        o_ref[...] = w_ref[...]

    return pl.pallas_call(
        body, name="chip_blocks_to_columns",
        grid=(nb,),
        in_specs=[pl.BlockSpec((None, K, Nb), lambda b: (b, 0, 0))],
        out_specs=pl.BlockSpec((K, Nb), lambda b: (0, b)),
        out_shape=jax.ShapeDtypeStruct((K, nb * Nb), w.dtype),
        compiler_params=_params(("parallel",)),
    )(w)


def proj_ln(a, w, resid, gain, bias, tm=1024):
    S, K = a.shape
    D = w.shape[1]

    def body(a_ref, w_ref, r_ref, g_ref, b_ref, y_ref, xh_ref, rs_ref):
        z = ALPHA * r_ref[...] + _dot(a_ref[...].astype(BF16), w_ref[...])
        y, xh, rs = _ln_fwd(z, g_ref[...], b_ref[...])
        y_ref[...] = y
        xh_ref[...] = xh
        rs_ref[...] = rs

    row = pl.BlockSpec((tm, D), lambda i: (i, 0))
    vec = pl.BlockSpec((1, D), lambda i: (0, 0))
    return pl.pallas_call(
        body, name="proj_ln",
        grid=(S // tm,),
        in_specs=[pl.BlockSpec((tm, K), lambda i: (i, 0)), pl.BlockSpec((K, D), lambda i: (0, 0)), row, vec, vec],
        out_specs=[row, row, pl.BlockSpec((tm, 1), lambda i: (i, 0))],
        out_shape=[jax.ShapeDtypeStruct((S, D), F32), jax.ShapeDtypeStruct((S, D), F32), jax.ShapeDtypeStruct((S, 1), F32)],
        compiler_params=_params(("parallel",)),
    )(a, w, resid, gain, bias)


def mm_nt(a, w, resid, a_blocked, out_dtype=F32, tm=1024):
    if a_blocked:
        nk, S, Kb = a.shape
        a_spec = pl.BlockSpec((None, tm, Kb), lambda i, n: (n, i, 0))
    else:
        S, Kb = a.shape
        nk = 1
        a_spec = pl.BlockSpec((tm, Kb), lambda i, n: (i, 0))
    M = w.shape[0]
    has_resid = resid is not None

    def body(*refs):
        if has_resid:
            a_ref, w_ref, r_ref, o_ref, acc = refs
        else:
            a_ref, w_ref, o_ref, acc = refs
        n = pl.program_id(1)
        part = _dot_nt(a_ref[...].astype(BF16), w_ref[...])

        @pl.when(n == 0)
        def _():
            acc[...] = part

        @pl.when(n > 0)
        def _():
            acc[...] += part

        @pl.when(n == nk - 1)
        def _():
            out = acc[...]
            if has_resid:
                out = out + ALPHA * r_ref[...]
            o_ref[...] = out.astype(out_dtype)

    row = pl.BlockSpec((tm, M), lambda i, n: (i, 0))
    in_specs = [a_spec, pl.BlockSpec((M, Kb), lambda i, n: (0, n))] + ([row] if has_resid else [])
    args = (a, w) + ((resid,) if has_resid else ())
    return pl.pallas_call(
        body, name="mm_nt",
        grid=(S // tm, nk),
        in_specs=in_specs,
        out_specs=row,
        out_shape=jax.ShapeDtypeStruct((S, M), out_dtype),
        scratch_shapes=[pltpu.VMEM((tm, M), F32)],
        compiler_params=_params(("parallel", "arbitrary")),
    )(*args)


def mm_nt_dilated(parts, dils, w, resid, tm=1024):
    n_groups = len(parts)
    _, S, K = parts[0].shape
    M = w.shape[0]
    nk = 3 * n_groups

    def body(*refs):
        a_refs = refs[:n_groups]
        w_ref, r_ref, o_ref, group_acc, total = refs[n_groups:]
        n = pl.program_id(1)
        for g in range(n_groups):
            for k in range(3):
                @pl.when(n == 3 * g + k)
                def _():
                    part = _dot_nt(a_refs[g][...], w_ref[...])
                    for c in range(M // LANES):
                        lanes = slice(c * LANES, (c + 1) * LANES)
                        if k == 0:
                            group_acc[c] = part[:, lanes]
                        else:
                            group_acc[c] += part[:, lanes]
                        if k == 2:
                            _rows_from_dilated(group_acc.at[c], total.at[c], tm, dils[g], accumulate=g > 0)

        @pl.when(n == nk - 1)
        def _():
            for c in range(M // LANES):
                lanes = slice(c * LANES, (c + 1) * LANES)
                o_ref[:, lanes] = total[c] + ALPHA * r_ref[:, lanes]

    def a_spec(g):
        return pl.BlockSpec((None, tm, K), lambda i, n: (jnp.clip(n - 3 * g, 0, 2), i, 0))

    row = pl.BlockSpec((tm, M), lambda i, n: (i, 0))
    return pl.pallas_call(
        body, name="mm_nt_dilated",
        grid=(S // tm, nk),
        in_specs=[a_spec(g) for g in range(n_groups)] + [pl.BlockSpec((M, K), lambda i, n: (0, n)), row],
        out_specs=row,
        out_shape=jax.ShapeDtypeStruct((S, M), F32),
        scratch_shapes=[pltpu.VMEM((M // LANES, tm, LANES), F32), pltpu.VMEM((M // LANES, tm, LANES), F32)],
        compiler_params=_params(("parallel", "arbitrary")),
    )(*parts, w, resid)


def mm_tn(pairs, *, nblk, out_shape, out_block, out_index, alias=None, tk=1024, name="mm_tn"):
    operands = []
    for a, b, _ in pairs:
        for t in (a, b):
            if not any(t is o for o in operands):
                operands.append(t)
    where = lambda t: next(i for i, o in enumerate(operands) if o is t)
    S = pairs[0][0].shape[-2]
    n_out, n_in = len(pairs), len(operands)
    n_alias = len(alias) if alias is not None else 0

    def spec(t):
        if t.ndim == 3:
            return pl.BlockSpec((None, tk, t.shape[-1]), lambda j, k: (j, k, 0))
        return pl.BlockSpec((tk, t.shape[-1]), lambda j, k: (k, 0))

    def body(*refs):
        refs = refs[n_alias:]
        in_refs, o_refs, accs = refs[:n_in], refs[n_in:n_in + n_out], refs[n_in + n_out:]
        k = pl.program_id(1)
        for (a, b, scale), o_ref, acc in zip(pairs, o_refs, accs):
            part = _dot_tn(in_refs[where(a)][...].astype(BF16), in_refs[where(b)][...].astype(BF16))

            @pl.when(k == 0)
            def _():
                acc[...] = part

            @pl.when(k > 0)
            def _():
                acc[...] += part

            @pl.when(k == S // tk - 1)
            def _():
                o_ref[...] = (scale * acc[...]).astype(BF16)

    out_spec = pl.BlockSpec(out_block, lambda j, k: out_index(j))
    outs = pl.pallas_call(
        body, name=name,
        grid=(nblk, S // tk),
        in_specs=[ANY] * n_alias + [spec(t) for t in operands],
        out_specs=[out_spec] * n_out,
        out_shape=[jax.ShapeDtypeStruct(out_shape, BF16)] * n_out,
        scratch_shapes=[pltpu.VMEM((a.shape[-1], b.shape[-1]), F32) for a, b, _ in pairs],
        input_output_aliases={i: i for i in range(n_alias)},
        compiler_params=_params(("parallel", "arbitrary")),
    )(*(tuple(alias) if alias is not None else ()), *operands)
    return list(outs)


def ffn_fwd(x, wg, wu, wd, gain, bias, tm=1024):
    S, D = x.shape
    nb, FB = wg.shape[0], wg.shape[1]

    def body(x_ref, wg_ref, wu_ref, wd_ref, g_ref, b_ref, y_ref, xh_ref, rs_ref, a_ref, u_ref, acc, xb_ref):
        j = pl.program_id(1)

        @pl.when(j == 0)
        def _():
            xb_ref[...] = x_ref[...].astype(BF16)
            acc[...] = jnp.zeros_like(acc)

        xb = xb_ref[...]
        total = None
        for t0 in range(0, FB, FFN_HIDDEN_TILE):
            cols = pl.ds(t0, min(FFN_HIDDEN_TILE, FB - t0))
            a = _dot_nt(xb, wg_ref[cols, :])
            u = _dot_nt(xb, wu_ref[cols, :])
            a_ref[:, cols] = a.astype(BF16)
            u_ref[:, cols] = u.astype(BF16)
            h = a * jax.nn.sigmoid(a) * u
            part = _dot(h.astype(BF16), wd_ref[cols, :])
            total = part if total is None else total + part
        acc[...] += total

        @pl.when(j == nb - 1)
        def _():
            z = ALPHA * x_ref[...] + MACARON_WEIGHT * acc[...]
            y, xh, rs = _ln_fwd(z, g_ref[...], b_ref[...])
            y_ref[...] = y
            xh_ref[...] = xh
            rs_ref[...] = rs

    row = pl.BlockSpec((tm, D), lambda i, j: (i, 0))
    vec = pl.BlockSpec((1, D), lambda i, j: (0, 0))
    w_out = pl.BlockSpec((None, FB, D), lambda i, j: (j, 0, 0))
    act = pl.BlockSpec((None, tm, FB), lambda i, j: (j, i, 0))
    return pl.pallas_call(
        body, name="ffn_fwd",
        grid=(S // tm, nb),
        in_specs=[row, w_out, w_out, w_out, vec, vec],
        out_specs=[row, row, pl.BlockSpec((tm, 1), lambda i, j: (i, 0)), act, act],
        out_shape=[jax.ShapeDtypeStruct((S, D), F32), jax.ShapeDtypeStruct((S, D), F32), jax.ShapeDtypeStruct((S, 1), F32),
                   jax.ShapeDtypeStruct((nb, S, FB), BF16), jax.ShapeDtypeStruct((nb, S, FB), BF16)],
        scratch_shapes=[pltpu.VMEM((tm, D), F32), pltpu.VMEM((tm, D), BF16)],
        compiler_params=_params(("parallel", "arbitrary")),
    )(x, wg, wu, wd, gain, bias)


def ffn_bwd(dy, xhat, rstd, gain, a, u, wg, wu, wd, tm=512):
    S, D = dy.shape
    nb, FB = wg.shape[0], wg.shape[1]

    def body(dy_ref, xh_ref, rs_ref, g_ref, a_ref, u_ref, wg_ref, wu_ref, wd_ref,
             dx_ref, dz_ref, h_ref, da_ref, du_ref, dg_ref, db_ref, acc, dzb_ref):
        i = pl.program_id(0)
        j = pl.program_id(1)

        @pl.when(j == 0)
        def _():
            dy = dy_ref[...]
            xh = xh_ref[...]
            dxh = dy * g_ref[...]
            m1 = jnp.mean(dxh, axis=-1, keepdims=True)
            m2 = jnp.mean(dxh * xh, axis=-1, keepdims=True)
            dz = rs_ref[...] * (dxh - m1 - xh * m2)
            dz_ref[...] = dz
            dzb_ref[...] = (MACARON_WEIGHT * dz).astype(BF16)
            acc[...] = jnp.zeros_like(acc)
            dg = jnp.sum(dy * xh, axis=0, keepdims=True)
            db = jnp.sum(dy, axis=0, keepdims=True)

            @pl.when(i == 0)
            def _():
                dg_ref[...] = dg
                db_ref[...] = db

            @pl.when(i > 0)
            def _():
                dg_ref[...] += dg
                db_ref[...] += db

        dzb = dzb_ref[...]
        total = None
        for t0 in range(0, FB, FFN_HIDDEN_TILE):
            cols = pl.ds(t0, min(FFN_HIDDEN_TILE, FB - t0))
            dh = _dot_nt(dzb, wd_ref[cols, :])
            av = a_ref[:, cols].astype(F32)
            uv = u_ref[:, cols].astype(F32)
            s = jax.nn.sigmoid(av)
            silu = av * s
            h_ref[:, cols] = (silu * uv).astype(BF16)
            da = (dh * uv * (s * (1.0 + av * (1.0 - s)))).astype(BF16)
            du = (dh * silu).astype(BF16)
            da_ref[:, cols] = da
            du_ref[:, cols] = du
            both = jnp.concatenate([da, du], axis=1)
            weights = jnp.concatenate([wg_ref[cols, :], wu_ref[cols, :]], axis=0)
            part = _dot(both, weights)
            total = part if total is None else total + part
        acc[...] += total

        @pl.when(j == nb - 1)
        def _():
            dx_ref[...] = ALPHA * dz_ref[...] + acc[...]

    row = pl.BlockSpec((tm, D), lambda i, j: (i, 0))
    vec = pl.BlockSpec((1, D), lambda i, j: (0, 0))
    w_out = pl.BlockSpec((None, FB, D), lambda i, j: (j, 0, 0))
    act = pl.BlockSpec((None, tm, FB), lambda i, j: (j, i, 0))
    act_shape = jax.ShapeDtypeStruct((nb, S, FB), BF16)
    full, one = jax.ShapeDtypeStruct((S, D), F32), jax.ShapeDtypeStruct((1, D), F32)
    return pl.pallas_call(
        body, name="ffn_bwd",
        grid=(S // tm, nb),
        in_specs=[row, row, pl.BlockSpec((tm, 1), lambda i, j: (i, 0)), vec, act, act, w_out, w_out, w_out],
        out_specs=[row, row, act, act, act, vec, vec],
        out_shape=[full, full, act_shape, act_shape, act_shape, one, one],
        scratch_shapes=[pltpu.VMEM((tm, D), F32), pltpu.VMEM((tm, D), BF16)],
        compiler_params=_params(("arbitrary", "arbitrary")),
    )(dy, xhat, rstd, gain, a, u, wg, wu, wd)


def ln_bwd(dy, xhat, rstd, gain, tm=1024):
    S, D = dy.shape

    def body(dy_ref, xh_ref, rs_ref, g_ref, dz_ref, dg_ref, db_ref):
        i = pl.program_id(0)
        dy = dy_ref[...]
        xh = xh_ref[...]
        dxh = dy * g_ref[...]
        m1 = jnp.mean(dxh, axis=-1, keepdims=True)
        m2 = jnp.mean(dxh * xh, axis=-1, keepdims=True)
        dz_ref[...] = rs_ref[...] * (dxh - m1 - xh * m2)
        dg = jnp.sum(dy * xh, axis=0, keepdims=True)
        db = jnp.sum(dy, axis=0, keepdims=True)

        @pl.when(i == 0)
        def _():
            dg_ref[...] = dg
            db_ref[...] = db

        @pl.when(i > 0)
        def _():
            dg_ref[...] += dg
            db_ref[...] += db

    row = pl.BlockSpec((tm, D), lambda i: (i, 0))
    vec = pl.BlockSpec((1, D), lambda i: (0, 0))
    return pl.pallas_call(
        body, name="ln_bwd",
        grid=(S // tm,),
        in_specs=[row, row, pl.BlockSpec((tm, 1), lambda i: (i, 0)), vec],
        out_specs=[row, vec, vec],
        out_shape=[jax.ShapeDtypeStruct((S, D), F32), jax.ShapeDtypeStruct((1, D), F32), jax.ShapeDtypeStruct((1, D), F32)],
        compiler_params=_params(("arbitrary",)),
    )(dy, xhat, rstd, gain)


def loss_head(y, target, tm=1024):
    S, D = y.shape

    def body(y_ref, t_ref, dy_ref, l_ref):
        i = pl.program_id(0)
        e = y_ref[...] - t_ref[...]
        dy_ref[...] = e / D
        part = 0.5 * jnp.sum(jnp.mean(e * e, axis=-1, keepdims=True), axis=0, keepdims=True)

        @pl.when(i == 0)
        def _():
            l_ref[...] = part

        @pl.when(i > 0)
        def _():
            l_ref[...] += part

    row = pl.BlockSpec((tm, D), lambda i: (i, 0))
    return pl.pallas_call(
        body, name="loss_head",
        grid=(S // tm,),
        in_specs=[row, row],
        out_specs=[row, pl.BlockSpec((1, 1), lambda i: (0, 0))],
        out_shape=[jax.ShapeDtypeStruct((S, D), F32), jax.ShapeDtypeStruct((1, 1), F32)],
        compiler_params=_params(("arbitrary",)),
    )(y, target)


def _pool_window(xp, g):
    n = xp.shape[0]
    w = xp + pltpu.roll(xp, 1, 0)
    out = w
    for level, shift in enumerate((1, 2, 4), start=1):
        w = pltpu.roll(w, shift, 0) + pltpu.roll(w, n - shift, 0)
        out = jnp.where(g >= level, w, out)
    return out


def _pool_count(S, C, g):
    half = lax.shift_left(jnp.int32(1), g)
    t = lax.broadcasted_iota(jnp.int32, (S, C), 0)
    return (jnp.minimum(t + half, S) - jnp.maximum(t - half, 0)).astype(F32)


def pool_mix(u, wgrp, scale):
    S, D = u.shape
    G, C = wgrp.shape[0], wgrp.shape[1]

    def body(u_ref, w_ref, s_ref, mix_ref, v_ref, pad):
        g = pl.program_id(0)
        zeros = jnp.zeros((POOL_PAD, C), F32)
        pad[pl.ds(0, POOL_PAD), :] = zeros
        pad[pl.ds(POOL_PAD + S, POOL_PAD), :] = zeros
        pad[pl.ds(POOL_PAD, S), :] = u_ref[...]
        win = _pool_window(pad[...], g)[POOL_PAD:POOL_PAD + S]
        mixed = (win / _pool_count(S, C, g) - u_ref[...]).astype(BF16)
        mix_ref[...] = mixed
        v_ref[...] = _dot(mixed, w_ref[...]) * s_ref[...]

    col = pl.BlockSpec((S, C), lambda g: (0, g))
    return pl.pallas_call(
        body, name="pool_mix",
        grid=(G,),
        in_specs=[col, pl.BlockSpec((None, C, C), lambda g: (g, 0, 0)), pl.BlockSpec((1, C), lambda g: (0, g))],
        out_specs=[col, col],
        out_shape=[jax.ShapeDtypeStruct((S, D), BF16), jax.ShapeDtypeStruct((S, D), F32)],
        scratch_shapes=[pltpu.VMEM((S + 2 * POOL_PAD, C), F32)],
        compiler_params=_params(("arbitrary",)),
    )(u, wgrp, scale)


def pool_mix_bwd(dv, mixed, wgrp, scale):
    S, D = dv.shape
    G, C = wgrp.shape[0], wgrp.shape[1]

    def body(dv_ref, mix_ref, w_ref, s_ref, du_ref, dw_ref, ds_ref, pad):
        g = pl.program_id(0)
        mixed = mix_ref[...]
        dv = dv_ref[...]
        yg = _dot(mixed, w_ref[...])
        ds_ref[...] = jnp.sum(dv * yg, axis=0, keepdims=True)
        dyg = (dv * s_ref[...]).astype(BF16)
        dw_ref[...] = _dot_tn(mixed, dyg).astype(BF16)
        dmix = _dot_nt(dyg, w_ref[...])
        zeros = jnp.zeros((POOL_PAD, C), F32)
        pad[pl.ds(0, POOL_PAD), :] = zeros
        pad[pl.ds(POOL_PAD + S, POOL_PAD), :] = zeros
        pad[pl.ds(POOL_PAD, S), :] = dmix / _pool_count(S, C, g)
        win = _pool_window(pad[...], g)
        win = pltpu.roll(win, win.shape[0] - 1, 0)[POOL_PAD:POOL_PAD + S]
        du_ref[...] = win - dmix

    col = pl.BlockSpec((S, C), lambda g: (0, g))
    return pl.pallas_call(
        body, name="pool_mix_bwd",
        grid=(G,),
        in_specs=[col, col, pl.BlockSpec((None, C, C), lambda g: (g, 0, 0)), pl.BlockSpec((1, C), lambda g: (0, g))],
        out_specs=[col, pl.BlockSpec((None, C, C), lambda g: (g, 0, 0)), pl.BlockSpec((1, C), lambda g: (0, g))],
        out_shape=[jax.ShapeDtypeStruct((S, D), F32), jax.ShapeDtypeStruct((G, C, C), BF16), jax.ShapeDtypeStruct((1, D), F32)],
        scratch_shapes=[pltpu.VMEM((S + 2 * POOL_PAD, C), F32)],
        compiler_params=_params(("arbitrary",)),
    )(dv, mixed, wgrp, scale)


PERM_BLOCK = 256


def _dilated_runs(rows, d):
    n = PERM_BLOCK // d
    return [(c * PERM_BLOCK, r, n) for c in range(rows // PERM_BLOCK) for r in range(d)]


def _rows_to_dilated(src_ref, dst_ref, rows, d):
    for base, r, n in _dilated_runs(rows, d):
        dst_ref[pl.ds(base + r * n, n), :] = src_ref[pl.ds(base + r, n, stride=d), :].astype(dst_ref.dtype)


def _rows_from_dilated(src_ref, dst_ref, rows, d, accumulate=False):
    for base, r, n in _dilated_runs(rows, d):
        at = pl.ds(base + r, n, stride=d)
        v = src_ref[pl.ds(base + r * n, n), :]
        dst_ref[at, :] = dst_ref[at, :] + v if accumulate else v


def dilate_rows(x, dils, out_dtype, tm=1024):
    S, D = x.shape

    def body(x_ref, *o_refs):
        for d, o_ref in zip(dils, o_refs):
            _rows_to_dilated(x_ref, o_ref, tm, d)

    tile = pl.BlockSpec((tm, LANES), lambda i, j: (i, j))
    return pl.pallas_call(
        body, name="dilate_rows",
        grid=(S // tm, D // LANES),
        in_specs=[tile],
        out_specs=[tile] * len(dils),
        out_shape=[jax.ShapeDtypeStruct((S, D), out_dtype)] * len(dils),
        compiler_params=_params(("parallel", "parallel")),
    )(x)


def _slope_table(group, dilation):
    s = _alibi_slopes()[group].reshape(N_HEADS // 2, 2, 1, 1) * float(dilation)
    return jnp.asarray(np.broadcast_to(s, (N_HEADS // 2, 2, 1, ATTN_W)).copy())


def _residue_shape(S, D, d):
    return (S, D) if d == 1 else (S // PERM_BLOCK, d, PERM_BLOCK // d, D)


def _residue_view(x, d):
    return x.reshape(x.shape[:-2] + _residue_shape(x.shape[-2], x.shape[-1], d))


def _residue_spec(lead_block, lead_index, S, d):
    if d == 1:
        return pl.BlockSpec(lead_block + (S, LANES), lambda hp, r: lead_index + (0, hp))
    return pl.BlockSpec(lead_block + (S // PERM_BLOCK, None, PERM_BLOCK // d, LANES), lambda hp, r: lead_index + (0, r, 0, hp))


def _whole(ref, lead, L):
    return ref[lead + (slice(None),) * (len(ref.shape) - len(lead))].reshape(L, LANES)


def _query_rows(lead, i, d):
    if d == 1:
        return lead + (pl.ds(pl.multiple_of(i * ATTN_BQ, ATTN_BQ), ATTN_BQ), slice(None)), (ATTN_BQ, LANES)
    n = PERM_BLOCK // d
    return lead + (pl.ds(i * (ATTN_BQ // n), ATTN_BQ // n), slice(None), slice(None)), (ATTN_BQ // n, n, LANES)


def _load_query_rows(ref, lead, i, d):
    at, _ = _query_rows(lead, i, d)
    return ref[at].reshape(ATTN_BQ, LANES)


def _store_query_rows(ref, lead, i, d, value):
    at, shape = _query_rows(lead, i, d)
    ref[at] = value.reshape(shape)


def _stage_keys(dst, src_ref, L):
    rows = _whole(src_ref, (), L)
    lane = lax.broadcasted_iota(jnp.int32, (L, LANES), 1)
    zeros = jnp.zeros((ATTN_R, LANES), dst.dtype)
    for h in range(2):
        mine = (lane < HEAD_DIM) if h == 0 else (lane >= HEAD_DIM)
        dst[h, pl.ds(0, ATTN_R), :] = zeros
        dst[h, pl.ds(ATTN_R + L, ATTN_R), :] = zeros
        dst[h, pl.ds(ATTN_R, L), :] = jnp.where(mine, rows, jnp.zeros_like(rows))


def _fill_bias(bias, sl_ref):
    a = lax.broadcasted_iota(jnp.int32, (ATTN_BQ, ATTN_W), 0)
    c = lax.broadcasted_iota(jnp.int32, (ATTN_BQ, ATTN_W), 1)
    rel = jnp.abs(c - ATTN_R - a)
    band = rel <= ATTN_R
    after_start = c >= ATTN_R
    before_end = c < ATTN_BQ + ATTN_R
    for h in range(2):
        base = -(sl_ref[h] * rel.astype(F32))
        for variant in range(4):
            ok = band
            if variant & 1:
                ok = ok & after_start
            if variant & 2:
                ok = ok & before_end
            bias[variant, h] = jnp.where(ok, base, MASK_VALUE)


def _bias_variant(i, nq):
    return jnp.where(i == 0, 1, 0) + jnp.where(i == nq - 1, 2, 0)


def attn_fwd(qkv, group, dilation):
    _, S, D = qkv.shape
    d = dilation
    L = S // d
    nq = L // ATTN_BQ
    ncol = D // LANES
    view = _residue_view(qkv, d)
    slopes = _slope_table(group, d)
    scale = HEAD_DIM ** -0.5

    def body(q_ref, k_ref, v_ref, sl_ref, o_ref, lse_ref, k2, v2, bias):
        @pl.when(pl.program_id(1) == 0)
        def _():
            _fill_bias(bias, sl_ref)

        _stage_keys(k2, k_ref, L)
        _stage_keys(v2, v_ref, L)
        head0 = lax.broadcasted_iota(jnp.int32, (ATTN_BQ, LANES), 1) < HEAD_DIM

        def block(i):
            variant = _bias_variant(i, nq)
            win = pl.ds(pl.multiple_of(i * ATTN_BQ, ATTN_BQ), ATTN_W)
            qs = _load_query_rows(q_ref, (), i, d) * jnp.asarray(scale, BF16)
            es, ms, ls = [], [], []
            for h in range(2):
                s = _dot_nt(qs, k2[h, win, :]) + bias[variant, h]
                m = jnp.max(s, axis=-1, keepdims=True)
                e = jnp.exp(s - m)
                ls.append(jnp.sum(e, axis=-1, keepdims=True))
                ms.append(m)
                es.append(e.astype(BF16))
            acc = _dot(jnp.concatenate(es, axis=1), jnp.concatenate([v2[0, win, :], v2[1, win, :]], axis=0))
            out = acc * jnp.where(head0, 1.0 / ls[0], 1.0 / ls[1])
            lse = jnp.where(head0, ms[0] + jnp.log(ls[0]), ms[1] + jnp.log(ls[1]))
            return out, lse

        def step(t, carry):
            results = [block(t * group + b) for b in range(group)]
            for b, (out, lse) in enumerate(results):
                _store_query_rows(o_ref, (), t * group + b, d, out)
                _store_query_rows(lse_ref, (), t * group + b, d, lse)
            return carry

        group = min(ATTN_UNROLL, nq)
        lax.fori_loop(0, nq // group, step, 0)

    def col(which):
        return _residue_spec((None,), (which,), S, d)

    out = _residue_spec((), (), S, d)
    o, lse = pl.pallas_call(
        body, name=f"attn_fwd_g{group}",
        grid=(ncol, d),
        in_specs=[col(0), col(1), col(2), pl.BlockSpec((None, 2, 1, ATTN_W), lambda hp, r: (hp, 0, 0, 0))],
        out_specs=[out, out],
        out_shape=[jax.ShapeDtypeStruct(_residue_shape(S, D, d), F32)] * 2,
        scratch_shapes=[pltpu.VMEM((2, L + 2 * ATTN_R, LANES), BF16), pltpu.VMEM((2, L + 2 * ATTN_R, LANES), BF16),
                        pltpu.VMEM((4, 2, ATTN_BQ, ATTN_W), F32)],
        compiler_params=_params(("arbitrary", "arbitrary")),
    )(view, view, view, slopes)
    return o.reshape(S, D), lse.reshape(S, D)


def attn_combine(os, lses, dils, tm=1024):
    S, D = os[0].shape
    n = len(os)
    n_moved = sum(d > 1 for d in dils)

    def body(*refs):
        o_refs, l_refs, out_ref, lse_ref = list(refs[:n]), list(refs[n:2 * n]), refs[2 * n], refs[2 * n + 1]
        spare = list(refs[2 * n + 2:])
        for g, d in enumerate(dils):
            if d > 1:
                for which in (o_refs, l_refs):
                    token_order = spare.pop()
                    _rows_from_dilated(which[g], token_order, tm, d)
                    which[g] = token_order
        ls = [r[...] for r in l_refs]
        m = functools.reduce(jnp.maximum, ls)
        es = [jnp.exp(l - m) for l in ls]
        tot = functools.reduce(lambda x, y: x + y, es)
        inv = 1.0 / tot
        out_ref[...] = functools.reduce(lambda x, y: x + y, [(e * inv) * r[...] for e, r in zip(es, o_refs)])
        lse_ref[...] = m + jnp.log(tot)

    tile = pl.BlockSpec((tm, LANES), lambda i, j: (i, j))
    return pl.pallas_call(
        body, name="attn_combine",
        grid=(S // tm, D // LANES),
        in_specs=[tile] * (2 * n),
        out_specs=[tile, tile],
        out_shape=[jax.ShapeDtypeStruct((S, D), F32), jax.ShapeDtypeStruct((S, D), F32)],
        scratch_shapes=[pltpu.VMEM((tm, LANES), F32)] * (2 * n_moved),
        compiler_params=_params(("parallel", "parallel")),
    )(*os, *lses)


def attn_bwd(qkv, do, o, lse, group, dilation):
    _, S, D = qkv.shape
    d = dilation
    L = S // d
    nq = L // ATTN_BQ
    ncol = D // LANES
    view = _residue_view(qkv, d)
    slopes = _slope_table(group, d)
    scale = HEAD_DIM ** -0.5

    def body(q_ref, k_ref, v_ref, do_ref, o_ref, lse_ref, sl_ref, dx_ref, k2, v2, dkacc, dvacc, bias):
        @pl.when(pl.program_id(1) == 0)
        def _():
            _fill_bias(bias, sl_ref)

        _stage_keys(k2, k_ref, L)
        _stage_keys(v2, v_ref, L)
        dkacc[...] = jnp.zeros_like(dkacc)
        dvacc[...] = jnp.zeros_like(dvacc)
        lane = lax.broadcasted_iota(jnp.int32, (ATTN_BQ, LANES), 1)
        heads = (lane < HEAD_DIM, lane >= HEAD_DIM)
        key_head0 = lax.broadcasted_iota(jnp.int32, (ATTN_W, LANES), 1) < HEAD_DIM

        def step(i, carry):
            variant = _bias_variant(i, nq)
            win = pl.ds(pl.multiple_of(i * ATTN_BQ, ATTN_BQ), ATTN_W)
            q = _load_query_rows(q_ref, (), i, d)
            qs = q * jnp.asarray(scale, BF16)
            dov = _load_query_rows(do_ref, (), i, d)
            prod = dov * _load_query_rows(o_ref, (), i, d)
            lse_v = _load_query_rows(lse_ref, (), i, d)
            dob = dov.astype(BF16)
            dss, dks, dvs = [], [], []
            for h in range(2):
                s = _dot_nt(qs, k2[h, win, :]) + bias[variant, h]
                lse_h = jnp.max(jnp.where(heads[h], lse_v, -jnp.inf), axis=-1, keepdims=True)
                dterm = jnp.sum(jnp.where(heads[h], prod, 0.0), axis=-1, keepdims=True)
                p = jnp.exp(s - lse_h)
                dp = _dot_nt(dob, v2[h, win, :])
                ds = (p * (dp - dterm) * scale).astype(BF16)
                dvs.append(_dot_tn(p.astype(BF16), dob))
                dks.append(_dot_tn(ds, q))
                dss.append(ds)
            dq = _dot(jnp.concatenate(dss, axis=1), jnp.concatenate([k2[0, win, :], k2[1, win, :]], axis=0))
            dvacc[win, :] += jnp.where(key_head0, dvs[0], dvs[1])
            dkacc[win, :] += jnp.where(key_head0, dks[0], dks[1])
            _store_query_rows(dx_ref, (0,), i, d, dq.astype(BF16))
            return carry

        lax.fori_loop(0, nq, step, 0, unroll=min(ATTN_UNROLL, nq))
        block_shape = dx_ref.shape[1:]
        dx_ref[1] = dkacc[pl.ds(ATTN_R, L), :].astype(BF16).reshape(block_shape)
        dx_ref[2] = dvacc[pl.ds(ATTN_R, L), :].astype(BF16).reshape(block_shape)

    def col(which):
        return _residue_spec((None,), (which,), S, d)

    act = _residue_spec((), (), S, d)
    out = pl.pallas_call(
        body, name=f"attn_bwd_g{group}",
        grid=(ncol, d),
        in_specs=[col(0), col(1), col(2), act, act, act, pl.BlockSpec((None, 2, 1, ATTN_W), lambda hp, r: (hp, 0, 0, 0))],
        out_specs=_residue_spec((3,), (0,), S, d),
        out_shape=jax.ShapeDtypeStruct((3,) + _residue_shape(S, D, d), BF16),
        scratch_shapes=[pltpu.VMEM((2, L + 2 * ATTN_R, LANES), BF16), pltpu.VMEM((2, L + 2 * ATTN_R, LANES), BF16),
                        pltpu.VMEM((L + 2 * ATTN_R, LANES), F32), pltpu.VMEM((L + 2 * ATTN_R, LANES), F32),
                        pltpu.VMEM((4, 2, ATTN_BQ, ATTN_W), F32)],
        compiler_params=_params(("arbitrary", "arbitrary")),
    )(view, view, view, _residue_view(do, d), _residue_view(o, d), _residue_view(lse, d), slopes)
    return out.reshape(3, S, D)


TILE_ELEMS = 256 * 1024


def _row_tile(R, C):
    if R * C <= TILE_ELEMS or R % 16:
        return R
    return max(t for t in range(16, R + 1, 16) if R % t == 0 and (t * C <= TILE_ELEMS or t == 16))


def pair_sum(core, gs, recvs):
    n = len(gs)
    _, _, R, C = gs[0].shape
    tr = _row_tile(R, C)

    def body(c_ref, *refs):
        for g_ref, r_ref, o_ref in zip(refs[:n], refs[n:2 * n], refs[2 * n:]):
            o_ref[...] = (g_ref[...].astype(F32) + r_ref[...].astype(F32)).astype(BF16)

    blk = pl.BlockSpec((None, tr, C), lambda d, i, c_ref: (d, i, 0))
    mine = pl.BlockSpec((None, None, tr, C), lambda d, i, c_ref: (d, c_ref[0], i, 0))
    return pl.pallas_call(
        body, name="pair_sum",
        grid_spec=pltpu.PrefetchScalarGridSpec(
            num_scalar_prefetch=1, grid=(N_CHIPS, R // tr),
            in_specs=[mine] * n + [blk] * n,
            out_specs=[blk] * n),
        out_shape=[jax.ShapeDtypeStruct((N_CHIPS, R, C), BF16)] * n,
        compiler_params=_params(("parallel", "parallel")),
    )(core, *gs, *recvs)


def chip_sum(chip, owns, recvs):
    n = len(owns)
    _, R, C = owns[0].shape
    tr = _row_tile(R, C)
    slot_of_relation = {2: 0, 1: 1, 3: 2}

    def body(chip_ref, *refs):
        me = chip_ref[0]
        for own_ref, r_ref, o_ref in zip(refs[:n], refs[n:2 * n], refs[2 * n:]):
            mine = own_ref[...].astype(F32)
            theirs = {rel: r_ref[k].astype(F32) for rel, k in slot_of_relation.items()}
            acc = None
            for s in range(N_CHIPS):
                rel = jnp.bitwise_xor(me, s)
                part = jnp.where(rel == 0, mine, jnp.where(rel == 2, theirs[2], jnp.where(rel == 1, theirs[1], theirs[3])))
                acc = part if acc is None else acc + part
            o_ref[...] = acc

    return pl.pallas_call(
        body, name="chip_sum",
        grid_spec=pltpu.PrefetchScalarGridSpec(
            num_scalar_prefetch=1, grid=(R // tr,),
            in_specs=[pl.BlockSpec((None, tr, C), lambda i, chip_ref: (chip_ref[0], i, 0))] * n
            + [pl.BlockSpec((N_CHIPS - 1, tr, C), lambda i, chip_ref: (0, i, 0))] * n,
            out_specs=[pl.BlockSpec((tr, C), lambda i, chip_ref: (i, 0))] * n),
        out_shape=[jax.ShapeDtypeStruct((R, C), F32)] * n,
        compiler_params=_params(("parallel",)),
    )(chip, *owns, *recvs)


def adamw(core, w, g_pairs, m, v):
    L, H, R, C = w.shape
    tr = _row_tile(R, C)

    def body(c_ref, w_ref, *rest):
        g_refs = rest[:2 * L]
        m_ref, v_ref, g_ref, d_ref, nm_ref, nv_ref = rest[2 * L:]
        mine = pl.program_id(1) == c_ref[0]
        g = None
        for l in range(L):
            g_l = jnp.where(mine, g_refs[2 * l][...], g_refs[2 * l + 1][...])
            g = g_l if g is None else jnp.where(pl.program_id(0) == l, g_l, g)
        m = ADAM_B1 * m_ref[...] + (1.0 - ADAM_B1) * g
        v = ADAM_B2 * v_ref[...] + (1.0 - ADAM_B2) * (g * g)
        m_hat = m / (1.0 - ADAM_B1 ** ADAM_STEP)
        v_hat = v / (1.0 - ADAM_B2 ** ADAM_STEP)
        g_ref[...] = g
        d_ref[...] = -ADAM_LR * (m_hat / (jnp.sqrt(v_hat) + ADAM_EPS) + ADAM_WD * w_ref[...])
        nm_ref[...] = m
        nv_ref[...] = v

    blk = pl.BlockSpec((None, None, tr, C), lambda l, h, i, c_ref: (l, h, i, 0))

    def half(layer):
        return pl.BlockSpec((tr, C), lambda l, h, i, c_ref: (jnp.where(l == layer, i, 0), 0))

    shape = jax.ShapeDtypeStruct((L, H, R, C), F32)
    return pl.pallas_call(
        body, name="adamw",
        grid_spec=pltpu.PrefetchScalarGridSpec(
            num_scalar_prefetch=1, grid=(L, H, R // tr),
            in_specs=[blk] + [half(l) for l in range(L) for _ in range(2)] + [blk, blk],
            out_specs=[blk] * 4),
        out_shape=[shape] * 4,
        compiler_params=_params(("parallel", "parallel", "parallel")),
    )(core, w, *[g for pair in g_pairs for g in pair], m, v)


def _place():
    return lax.axis_index("x"), lax.axis_index("y"), lax.axis_index("c")


def _other_chips(x, y):
    return [(2 * (1 - x) + y, (1 - x, y)), (2 * x + (1 - y), (x, 1 - y)), (2 * (1 - x) + (1 - y), (1 - x, 1 - y))]


def all_gather_shards(shards, placed):
    n = len(shards)

    def body(*refs):
        ins, outs = refs[:n], refs[2 * n:3 * n]
        send_sems, recv_sems = refs[3 * n:]
        x, y, c = _place()
        me = 2 * x + y
        sibling = (x, y, 1 - c)
        chips = _other_chips(x, y)

        def copy(a, k, src, dst, to):
            return pltpu.make_async_remote_copy(src_ref=src, dst_ref=dst, send_sem=send_sems.at[a, k], recv_sem=recv_sems.at[a, k],
                                                device_id=to, device_id_type=MESH)

        sends = []
        for a in range(n):
            for k, (_, (px, py)) in enumerate(chips):
                cp = copy(a, k, ins[a].at[c], outs[a].at[me, c], (px, py, c))
                cp.start()
                sends.append(cp)
        for a in range(n):
            for k, (chip, _) in enumerate(chips):
                landed = outs[a].at[chip, c]
                copy(a, k, landed, landed, sibling).wait_recv()
                cp = copy(a, 3 + k, landed, landed, sibling)
                cp.start()
                sends.append(cp)
        for a in range(n):
            for k, (chip, _) in enumerate(chips):
                other = outs[a].at[chip, 1 - c]
                copy(a, 3 + k, other, other, sibling).wait_recv()
        for cp in sends:
            cp.wait_send()

    return pl.pallas_call(
        body, name="all_gather_shards",
        in_specs=[ANY] * (2 * n),
        out_specs=[ANY] * n,
        out_shape=[jax.ShapeDtypeStruct(p.shape, p.dtype) for p in placed],
        scratch_shapes=[pltpu.SemaphoreType.DMA((n, 6)), pltpu.SemaphoreType.DMA((n, 6))],
        input_output_aliases={n + a: a for a in range(n)},
        compiler_params=pltpu.CompilerParams(has_side_effects=True),
    )(*shards, *placed)


def all_gather_shards_async(shards, placed, collective_id, name):
    n = len(shards)
    srcs = [jax.new_ref(s, memory_space=pltpu.MemorySpace.HBM) for s in shards]
    dsts = [jax.new_ref(p, memory_space=pltpu.MemorySpace.HBM) for p in placed]

    @pl.kernel(mesh=plsc.ScalarSubcoreMesh(axis_name="sequencer", num_cores=1), name=name,
               scratch_types=(pltpu.SemaphoreType.DMA((n, 6)), pltpu.SemaphoreType.DMA((n, 6))),
               compiler_params=pltpu.CompilerParams(collective_id=collective_id))
    def launch(send_sems, recv_sems):
        x, y, c = _place()
        me = 2 * x + y
        sibling = (x, y, 1 - c)
        chips = _other_chips(x, y)
        barrier = pltpu.get_barrier_semaphore()
        peers = [sibling] + [(px, py, c) for _, (px, py) in chips]
        for peer in peers:
            pl.semaphore_signal(barrier, inc=1, device_id=peer, device_id_type=MESH)
        pl.semaphore_wait(barrier, len(peers))

        def copy(a, k, src, dst, to):
            return pltpu.make_async_remote_copy(src_ref=src, dst_ref=dst, send_sem=send_sems.at[a, k], recv_sem=recv_sems.at[a, k],
                                                device_id=to, device_id_type=MESH)

        sends = []
        for a in range(n):
            for k, (_, (px, py)) in enumerate(chips):
                cp = copy(a, k, srcs[a].at[c], dsts[a].at[me, c], (px, py, c))
                cp.start()
                sends.append(cp)
        for a in range(n):
            for k, (chip, _) in enumerate(chips):
                landed = dsts[a].at[chip, c]
                copy(a, k, landed, landed, sibling).wait_recv()
                cp = copy(a, 3 + k, landed, landed, sibling)
                cp.start()
                sends.append(cp)
        for a in range(n):
            for k, (chip, _) in enumerate(chips):
                other = dsts[a].at[chip, 1 - c]
                copy(a, 3 + k, other, other, sibling).wait_recv()
        for cp in sends:
            cp.wait_send()

    launch()
    return [d[...] for d in dsts]


def _exchange(body, ins, out_shapes, sem_shapes, name, peers=None, collective_id=None):
    n_in, n_out = len(ins), len(out_shapes)
    sems = [pltpu.SemaphoreType.DMA(shape) for shape in sem_shapes]
    if collective_id is None:
        def tc_body(*refs):
            body(refs[:n_in], refs[n_in:n_in + n_out], *refs[n_in + n_out:])

        return pl.pallas_call(tc_body, name=name, in_specs=[ANY] * n_in, out_specs=[ANY] * n_out, out_shape=out_shapes,
                              scratch_shapes=sems, compiler_params=pltpu.CompilerParams(has_side_effects=True))(*ins)
    srcs = [jax.new_ref(a, memory_space=pltpu.MemorySpace.HBM) for a in ins]
    dsts = [jax.empty_ref(shape, memory_space=pltpu.MemorySpace.HBM) for shape in out_shapes]

    @pl.kernel(mesh=plsc.ScalarSubcoreMesh(axis_name="sequencer", num_cores=1), name=name, scratch_types=tuple(sems),
               compiler_params=pltpu.CompilerParams(collective_id=collective_id))
    def launch(*sem_refs):
        barrier = pltpu.get_barrier_semaphore()
        others = peers(*_place())
        for peer in others:
            pl.semaphore_signal(barrier, inc=1, device_id=peer, device_id_type=MESH)
        pl.semaphore_wait(barrier, len(others))
        body(srcs, dsts, *sem_refs)

    launch()
    return [d[...] for d in dsts]


def _sibling(x, y, c):
    return [(x, y, 1 - c)]


def _same_core_of_other_chips(x, y, c):
    return [(px, py, c) for _, (px, py) in _other_chips(x, y)]


def sibling_exchange_halves(grads, name="sibling_exchange_halves", collective_id=None):
    n = len(grads)

    def body(ins, outs, send_sems, recv_sems):
        x, y, c = _place()
        copies = [pltpu.make_async_remote_copy(src_ref=ins[a].at[:, 1 - c], dst_ref=outs[a], send_sem=send_sems.at[a],
                                               recv_sem=recv_sems.at[a], device_id=(x, y, 1 - c), device_id_type=MESH) for a in range(n)]
        for cp in copies:
            cp.start()
        for cp in copies:
            cp.wait()

    shapes = [jax.ShapeDtypeStruct((N_CHIPS,) + g.shape[2:], g.dtype) for g in grads]
    return _exchange(body, grads, shapes, [(n,), (n,)], name, _sibling, collective_id)


def chip_exchange(sums, name="chip_exchange", collective_id=None):
    n = len(sums)

    def body(ins, outs, send_sems, recv_sems):
        x, y, c = _place()
        copies = []
        for a in range(n):
            for k, (chip, (px, py)) in enumerate(_other_chips(x, y)):
                cp = pltpu.make_async_remote_copy(src_ref=ins[a].at[chip], dst_ref=outs[a].at[k], send_sem=send_sems.at[a, k],
                                                  recv_sem=recv_sems.at[a, k], device_id=(px, py, c), device_id_type=MESH)
                cp.start()
                copies.append(cp)
        for cp in copies:
            cp.wait()

    shapes = [jax.ShapeDtypeStruct((N_CHIPS - 1,) + s.shape[1:], s.dtype) for s in sums]
    return _exchange(body, sums, shapes, [(n, 3), (n, 3)], name, _same_core_of_other_chips, collective_id)


def sibling_share(halves, name="sibling_share", collective_id=None):
    n = len(halves)

    def body(ins, outs, send_sems, recv_sems):
        x, y, c = _place()
        copies = [pltpu.make_async_remote_copy(src_ref=ins[a], dst_ref=outs[a], send_sem=send_sems.at[a], recv_sem=recv_sems.at[a],
                                               device_id=(x, y, 1 - c), device_id_type=MESH) for a in range(n)]
        for cp in copies:
            cp.start()
        for cp in copies:
            cp.wait()

    shapes = [jax.ShapeDtypeStruct(h.shape, h.dtype) for h in halves]
    return _exchange(body, halves, shapes, [(n,), (n,)], name, _sibling, collective_id)


def all_reduce_small(v):
    R, C = v.shape
    n_dev = 8

    def body(v_ref, o_ref, buf, send_sems, recv_sems):
        x, y, c = _place()
        me = 4 * x + 2 * y + c
        buf[me] = v_ref[...]
        copies = []
        for rel in range(1, n_dev):
            fx, fy, fc = rel >> 2, (rel >> 1) & 1, rel & 1
            peer = (x ^ fx, y ^ fy, c ^ fc)
            cp = pltpu.make_async_remote_copy(src_ref=v_ref, dst_ref=buf.at[me], send_sem=send_sems.at[rel - 1],
                                              recv_sem=recv_sems.at[rel - 1], device_id=peer, device_id_type=MESH)
            cp.start()
            copies.append(cp)
        for cp in copies:
            cp.wait()
        acc = buf[0]
        for k in range(1, n_dev):
            acc = acc + buf[k]
        o_ref[...] = acc

    return pl.pallas_call(
        body, name="all_reduce_small",
        in_specs=[pl.BlockSpec(memory_space=pltpu.VMEM)],
        out_specs=pl.BlockSpec(memory_space=pltpu.VMEM),
        out_shape=jax.ShapeDtypeStruct((R, C), F32),
        scratch_shapes=[pltpu.VMEM((n_dev, R, C), F32), pltpu.SemaphoreType.DMA((n_dev - 1,)), pltpu.SemaphoreType.DMA((n_dev - 1,))],
        compiler_params=pltpu.CompilerParams(has_side_effects=True),
    )(v)


WEIGHT_NAMES = ("ffn1_w_gate", "ffn1_w_up", "ffn1_w_down", "ffn2_w_gate", "ffn2_w_up", "ffn2_w_down", "ln_gain", "ln_bias",
                "pool_w_in", "pool_w_group", "pool_scale", "pool_w_out", "attn_w_qkv", "attn_w_out")
MATRIX_NAMES = ("ffn1_w_gate", "ffn1_w_up", "ffn1_w_down", "ffn2_w_gate", "ffn2_w_up", "ffn2_w_down",
                "pool_w_in", "pool_w_group", "pool_w_out", "attn_w_qkv", "attn_w_out")


TRANSPOSED_NAMES = ("ffn1_w_gate", "ffn1_w_up", "ffn2_w_gate", "ffn2_w_up")


def _halves(name, w):
    if name in TRANSPOSED_NAMES:
        w = jnp.swapaxes(w, 1, 2)
    return w.reshape(2, -1, w.shape[-1])


def _unhalves(name, t, shape):
    if name in TRANSPOSED_NAMES:
        return jnp.swapaxes(t.reshape(shape[0], shape[2], shape[1]), 1, 2)
    return t.reshape(shape)


def kernel(x, ffn1_w_gate, ffn1_w_up, ffn1_w_down, ffn2_w_gate, ffn2_w_up, ffn2_w_down, ln_gain, ln_bias, pool_w_in, pool_w_group, pool_scale, pool_w_out, attn_w_qkv, attn_w_out, loss_target, m_ffn1_w_gate, m_ffn1_w_up, m_ffn1_w_down, m_ffn2_w_gate, m_ffn2_w_up, m_ffn2_w_down, m_ln_gain, m_ln_bias, m_pool_w_in, m_pool_w_group, m_pool_scale, m_pool_w_out, m_attn_w_qkv, m_attn_w_out, v_ffn1_w_gate, v_ffn1_w_up, v_ffn1_w_down, v_ffn2_w_gate, v_ffn2_w_up, v_ffn2_w_down, v_ln_gain, v_ln_bias, v_pool_w_in, v_pool_w_group, v_pool_scale, v_pool_w_out, v_attn_w_qkv, v_attn_w_out):
    weights = dict(zip(WEIGHT_NAMES, (ffn1_w_gate, ffn1_w_up, ffn1_w_down, ffn2_w_gate, ffn2_w_up, ffn2_w_down, ln_gain, ln_bias,
                                      pool_w_in, pool_w_group, pool_scale, pool_w_out, attn_w_qkv, attn_w_out)))
    moms = dict(zip(WEIGHT_NAMES, (m_ffn1_w_gate, m_ffn1_w_up, m_ffn1_w_down, m_ffn2_w_gate, m_ffn2_w_up, m_ffn2_w_down, m_ln_gain,
                                   m_ln_bias, m_pool_w_in, m_pool_w_group, m_pool_scale, m_pool_w_out, m_attn_w_qkv, m_attn_w_out)))
    vels = dict(zip(WEIGHT_NAMES, (v_ffn1_w_gate, v_ffn1_w_up, v_ffn1_w_down, v_ffn2_w_gate, v_ffn2_w_up, v_ffn2_w_down, v_ln_gain,
                                   v_ln_bias, v_pool_w_in, v_pool_w_group, v_pool_scale, v_pool_w_out, v_attn_w_qkv, v_attn_w_out)))
    S, D = x.shape[1], x.shape[2]
    FB = ffn1_w_gate.shape[2]
    QKV = attn_w_qkv.shape[2] * N_CHIPS
    G, CB = pool_w_group.shape[1], pool_w_group.shape[2]
    C = pool_w_group.shape[3]
    cx, cy, cc = _place()
    chip = 2 * cx + cy
    xs = x.reshape(S, D)
    target = loss_target.reshape(S, D)

    ln_rows = jnp.concatenate([ln_gain, ln_bias, jnp.zeros((DEPTH, 2, ln_gain.shape[2]), F32)], axis=1)
    shard = {n: _halves(n, weights[n]).astype(BF16) for n in MATRIX_NAMES}
    ffn_layer = lambda f, i: {f"{f}{s}@{i}": shard[f + s][i].reshape(2, FB // 2, D) for s in ("_w_gate", "_w_up", "_w_down")}
    pool = {n: shard[n] for n in ("pool_w_in", "pool_w_group", "pool_w_out")}
    attn = {n: shard[n] for n in ("attn_w_qkv", "attn_w_out")}
    groups = [dict(ffn_layer("ffn1", 0), ln=ln_rows), pool, ffn_layer("ffn2", 0), ffn_layer("ffn1", 1), attn, ffn_layer("ffn2", 1)]
    full = {}

    def launch(k, after=None):
        parts = list(groups[k].values())
        if after is not None:
            *parts, after = lax.optimization_barrier((*parts, after))
        placed = [lax.dynamic_update_slice(lax.empty((N_CHIPS,) + s.shape, s.dtype), s[None], (chip, 0, 0, 0)) for s in parts]
        full.update(zip(groups[k], all_gather_shards_async(parts, placed, GATHER_COLLECTIVE_ID, f"gather_weights_{k}")))
        return after

    for k in range(4):
        launch(k)
    ffn_w = lambda f, i: [full[f"{f}{s}@{i}"].reshape(N_CHIPS, FB, D) for s in ("_w_gate", "_w_up", "_w_down")]
    w_pool_in = full["pool_w_in"].reshape(D, D)
    w_pool_out = full["pool_w_out"].reshape(D, D)
    w_group = full["pool_w_group"].reshape(N_CHIPS, G, CB, C).transpose(1, 0, 2, 3).reshape(G, N_CHIPS * CB, C)
    ln_full = full["ln"].transpose(1, 2, 0, 3).reshape(DEPTH, 8, D)
    gain = lambda i, k: ln_full[i, k].reshape(1, D)
    bias = lambda i, k: ln_full[i, 3 + k].reshape(1, D)

    dils = [dil for _, dil in DIL_CONFIGS]
    moved_dils = [dil for dil in dils if dil > 1]

    saved = []
    y = xs
    for i in range(DEPTH):
        y_in = y
        y, xh, rs, a, u = ffn_fwd(y_in, *ffn_w("ffn1", i), gain(i, 0), bias(i, 0))
        f1 = (y_in, xh, rs, a, u)
        if i == 0:
            y = launch(4, after=y)
        y_mid = y
        if i % 2 == 0:
            pu = mm_nn(y_mid, w_pool_in, F32)[0]
            mixed, pv = pool_mix(pu, w_group, pool_scale)
            y, xh, rs = proj_ln(pv, w_pool_out, y_mid, gain(i, 1), bias(i, 1))
            mix = (y_mid, xh, rs, mixed, pv)
            if i == 0:
                y = launch(5, after=y)
        else:
            qkv_blocks, y_mid = lax.optimization_barrier((full["attn_w_qkv"], y_mid))
            w_qkv = chip_blocks_to_columns(qkv_blocks.reshape(N_CHIPS, D, QKV // N_CHIPS))
            w_attn_out = full["attn_w_out"].reshape(D, D)
            moved = dict(zip(moved_dils, dilate_rows(y_mid, moved_dils, BF16)))
            srcs = [moved.get(dil, y_mid) for dil in dils]
            qkvs = [mm_nn(src, w_qkv, BF16, first_block=3 * g, nb=3) for g, src in enumerate(srcs)]
            parts = [attn_fwd(qkv, g, dil) for g, (qkv, dil) in enumerate(zip(qkvs, dils))]
            ao, lse = attn_combine([p[0] for p in parts], [p[1] for p in parts], dils)
            y, xh, rs = proj_ln(ao, w_attn_out, y_mid, gain(i, 1), bias(i, 1))
            mix = (y_mid, xh, rs, srcs, qkvs, ao, lse, w_qkv, w_attn_out)
        y_in2 = y
        y, xh, rs, a, u = ffn_fwd(y_in2, *ffn_w("ffn2", i), gain(i, 2), bias(i, 2))
        f2 = (y_in2, xh, rs, a, u)
        saved.append((f1, mix, f2))

    dy, loss_part = loss_head(y, target)
    loss = lax.psum(loss_part[0, 0], ("x", "y", "c"))

    core = cc.reshape(1).astype(jnp.int32)
    chip_id = chip.reshape(1).astype(jnp.int32)
    dgain = [[None] * 3 for _ in range(DEPTH)]
    dbias = [[None] * 3 for _ in range(DEPTH)]
    dscale = None
    pieces = []
    own_half, other_half = {}, {}

    def tie(arrays, after):
        *arrays, after = lax.optimization_barrier((*arrays, after))
        return arrays, after

    def by_shape(fn, xs, ys):
        out = [None] * len(xs)
        for shape in dict.fromkeys(x.shape for x in xs):
            idx = [k for k, x in enumerate(xs) if x.shape == shape]
            for k, res in zip(idx, fn([xs[k] for k in idx], [ys[k] for k in idx])):
                out[k] = res
        return out

    def start_piece(keys, arrays):
        blocks = [g.reshape(N_CHIPS, 2, -1, g.shape[-1]) for g in arrays]
        k = len(pieces)
        pieces.append(dict(keys=keys, blocks=blocks, from_sibling=sibling_exchange_halves(
            blocks, name=f"reduce_halves_{k}", collective_id=SIBLING_COLLECTIVE_ID)))

    def pair_sums_and_chip_exchange(k, after):
        piece = pieces[k]
        received, after = tie(piece["from_sibling"], after)
        piece["pair_sums"] = by_shape(lambda gs, rs: pair_sum(core, gs, rs), piece["blocks"], received)
        piece["from_chips"] = chip_exchange(piece["pair_sums"], name=f"reduce_chips_{k}", collective_id=CHIPS_COLLECTIVE_ID)
        return after

    def chip_sums_and_share(k, after):
        piece = pieces[k]
        received, after = tie(piece["from_chips"], after)
        mine = by_shape(lambda ps, rs: chip_sum(chip_id, ps, rs), piece["pair_sums"], received)
        theirs = sibling_share(mine, name=f"reduce_share_{k}", collective_id=SIBLING_COLLECTIVE_ID)
        own_half.update(zip(piece["keys"], mine))
        other_half.update(zip(piece["keys"], theirs))
        return after

    def piece_done(keys, arrays, dy):
        start_piece(keys, arrays)
        k = len(pieces) - 1
        if k >= 1:
            dy = pair_sums_and_chip_exchange(k - 1, dy)
        if k >= 2:
            dy = chip_sums_and_share(k - 2, dy)
        return dy

    def ffn_backward(name, i, dy, state):
        y_in, xh, rs, a, u = state
        k = 0 if name == "ffn1" else 2
        dx, dz, h, da, du, dgain[i][k], dbias[i][k] = ffn_bwd(dy, xh, rs, gain(i, k), a, u, *ffn_w(name, i))
        outs = mm_tn([(da, y_in, 1.0), (du, y_in, 1.0), (h, dz, MACARON_WEIGHT)], nblk=N_CHIPS, out_shape=(N_CHIPS, FB, D),
                     out_block=(None, FB, D), out_index=lambda j: (j, 0, 0), name="ffn_wgrad")
        return piece_done([(name + s, i) for s in ("_w_gate", "_w_up", "_w_down")], outs, dx)

    def square_grad(a, b):
        return mm_tn([(a, b, 1.0)], nblk=1, out_shape=(D, D), out_block=(D, D), out_index=lambda j: (0, 0), name="square_wgrad")[0]

    for i in reversed(range(DEPTH)):
        f1, mix, f2 = saved[i]
        dy = ffn_backward("ffn2", i, dy, f2)
        dz, dgain[i][1], dbias[i][1] = ln_bwd(dy, mix[1], mix[2], gain(i, 1))
        if i % 2 == 0:
            y_mid, _, _, mixed, pv = mix
            g_out = square_grad(pv, dz)
            dv = mm_nt(dz, w_pool_out, None, a_blocked=False)
            du, dwg, dscale = pool_mix_bwd(dv, mixed, w_group, pool_scale)
            g_group = dwg.reshape(G, N_CHIPS, CB, C).transpose(1, 0, 2, 3)
            g_in = square_grad(y_mid, du)
            dy = mm_nt(du, w_pool_in, dz, a_blocked=False)
            dy = piece_done([("pool_w_out", 0), ("pool_w_group", 0), ("pool_w_in", 0)], [g_out, g_group, g_in], dy)
        else:
            y_mid, _, _, srcs, qkvs, ao, lse, w_qkv, w_attn_out = mix
            g_out = square_grad(ao, dz)
            dao = mm_nt(dz, w_attn_out, None, a_blocked=False)
            in_order = [dict(zip(moved_dils, dilate_rows(t, moved_dils, F32))) for t in (dao, ao, lse)]
            dqkvs = [attn_bwd(qkvs[g], *[m.get(dil, t) for m, t in zip(in_order, (dao, ao, lse))], g, dil)
                     for g, dil in enumerate(dils)]
            g_qkv = [mm_tn([(src, dqkv, 1.0)], nblk=3, out_shape=(D, 3 * D), out_block=(D, D), out_index=lambda j: (0, j),
                           name="qkv_wgrad")[0] for src, dqkv in zip(srcs, dqkvs)]
            g_qkv = jnp.concatenate(g_qkv, axis=1).reshape(D, N_CHIPS, QKV // N_CHIPS).transpose(1, 0, 2)
            dy = mm_nt_dilated(dqkvs, dils, w_qkv, dz)
            dy = piece_done([("attn_w_out", 0), ("attn_w_qkv", 0)], [g_out, g_qkv], dy)
        dy = ffn_backward("ffn1", i, dy, f1)
    grad_x = dy.reshape(x.shape)

    last = len(pieces) - 1
    small = jnp.concatenate([jnp.concatenate(dgain[i] + dbias[i], axis=0) for i in range(DEPTH)] + [dscale, jnp.zeros((3, D), F32)], axis=0)
    small = all_reduce_small(small)
    per_layer = small[:6 * DEPTH].reshape(DEPTH, 6, D)
    cols = D // N_CHIPS
    small_grads = {"ln_gain": lax.dynamic_slice_in_dim(per_layer[:, 0:3], chip * cols, cols, axis=2),
                   "ln_bias": lax.dynamic_slice_in_dim(per_layer[:, 3:6], chip * cols, cols, axis=2),
                   "pool_scale": small[6 * DEPTH:6 * DEPTH + 1]}

    grad_w, delta, new_m, new_v = {}, {}, {}, {}

    def update(n, after):
        shape = weights[n].shape
        if n in MATRIX_NAMES:
            layers = DEPTH if (n, 1) in own_half else 1
            as4 = lambda t: _halves(n, t).reshape(layers, 2, -1, shape[-1] if n not in TRANSPOSED_NAMES else shape[1])
            mine, after = tie([own_half[n, l] for l in range(layers)], after)
            outs = adamw(core, as4(weights[n]), [(mine[l], other_half[n, l]) for l in range(layers)], as4(moms[n]), as4(vels[n]))
            grad_w[n], delta[n], new_m[n], new_v[n] = [_unhalves(n, t, shape) for t in outs]
        else:
            as4 = lambda t: t.reshape(1, 1, -1, shape[-1])
            g2 = small_grads[n].reshape(-1, shape[-1])
            outs = adamw(core, as4(weights[n]), [(g2, g2)], as4(moms[n]), as4(vels[n]))
            grad_w[n], delta[n], new_m[n], new_v[n] = [t.reshape(shape) for t in outs]
        return outs[1]

    marker = small
    for n in ("ln_gain", "ln_bias", "pool_scale"):
        marker = update(n, marker)
    marker = pair_sums_and_chip_exchange(last, marker)
    marker = chip_sums_and_share(last - 1, marker)
    for n in MATRIX_NAMES:
        if not n.startswith("ffn1"):
            marker = update(n, marker)
    marker = chip_sums_and_share(last, marker)
    for n in MATRIX_NAMES:
        if n.startswith("ffn1"):
            marker = update(n, marker)

    return (loss, grad_x, *[grad_w[n] for n in WEIGHT_NAMES], *[delta[n] for n in WEIGHT_NAMES],
            *[new_m[n] for n in WEIGHT_NAMES], *[new_v[n] for n in WEIGHT_NAMES])
```

```python
import functools
import math

import numpy as np
import jax
import jax.numpy as jnp
from jax import lax
from jax.experimental import pallas as pl
from jax.experimental.pallas import tpu as pltpu
from jax.experimental.pallas import tpu_sc as plsc

F32 = jnp.float32
BF16 = jnp.bfloat16

DEPTH = 2
ALPHA = (2.0 * DEPTH) ** 0.25
MACARON_WEIGHT = 0.5
LN_EPS = 1e-5
MASK_VALUE = -1e30
POOL_WINDOWS = (2, 4, 8, 16)
POOL_PAD = 16
HEAD_DIM = 64
N_HEADS = 16
DIL_CONFIGS = ((128, 1), (512, 4), (2048, 16))
ATTN_R = 64
ATTN_BQ = 128
ATTN_W = ATTN_BQ + 2 * ATTN_R
FFN_HIDDEN_TILE = 256
ATTN_UNROLL = 8
LANES = 128
ADAM_LR = 0.001
ADAM_B1 = 0.9
ADAM_B2 = 0.999
ADAM_EPS = 1e-08
ADAM_WD = 0.01
ADAM_STEP = 10
N_CHIPS = 4
GATHER_COLLECTIVE_ID = 1
SIBLING_COLLECTIVE_ID = 2
CHIPS_COLLECTIVE_ID = 3
VMEM_LIMIT = 56 * 1024 * 1024
MESH = pl.DeviceIdType.MESH
ANY = pl.BlockSpec(memory_space=pl.ANY)


def _params(sem=None, vmem=VMEM_LIMIT):
    return pltpu.CompilerParams(dimension_semantics=sem, vmem_limit_bytes=vmem)


def _alibi_slopes():
    n = len(DIL_CONFIGS) * N_HEADS
    s = 2.0 ** (-8.0 * np.arange(1, n + 1) / n)
    return s.reshape(len(DIL_CONFIGS), N_HEADS).astype(np.float32)


def _ln_fwd(z, g, b):
    mu = jnp.mean(z, axis=-1, keepdims=True)
    zc = z - mu
    var = jnp.mean(zc * zc, axis=-1, keepdims=True)
    rstd = lax.rsqrt(var + LN_EPS)
    xhat = zc * rstd
    return xhat * g + b, xhat, rstd


def _dot(a, b):
    return jnp.dot(a, b, preferred_element_type=F32)


def _dot_nt(a, b):
    return lax.dot_general(a, b, (((1,), (1,)), ((), ())), preferred_element_type=F32)


def _dot_tn(a, b):
    return lax.dot_general(a, b, (((0,), (0,)), ((), ())), preferred_element_type=F32)


def mm_nn(a, b, out_dtype, first_block=0, nb=None, tm=1024):
    S, K = a.shape
    Nb = K
    nb = b.shape[1] // Nb if nb is None else nb

    def body(a_ref, b_ref, o_ref):
        o_ref[...] = _dot(a_ref[...].astype(BF16), b_ref[...]).astype(out_dtype)

    return pl.pallas_call(
        body, name="mm_nn",
        grid=(S // tm, nb),
        in_specs=[pl.BlockSpec((tm, K), lambda i, j: (i, 0)), pl.BlockSpec((K, Nb), lambda i, j: (0, first_block + j))],
        out_specs=pl.BlockSpec((None, tm, Nb), lambda i, j: (j, i, 0)),
        out_shape=jax.ShapeDtypeStruct((nb, S, Nb), out_dtype),
        compiler_params=_params(("parallel", "arbitrary")),
    )(a, b)


def chip_blocks_to_columns(w):
    nb, K, Nb = w.shape

    def body(w_ref, o_ref):
        o_ref[...] = w_ref[...]

    return pl.pallas_call(
        body, name="chip_blocks_to_columns",
        grid=(nb,),
        in_specs=[pl.BlockSpec((None, K, Nb), lambda b: (b, 0, 0))],
        out_specs=pl.BlockSpec((K, Nb), lambda b: (0, b)),
        out_shape=jax.ShapeDtypeStruct((K, nb * Nb), w.dtype),
        compiler_params=_params(("parallel",)),
    )(w)


def proj_ln(a, w, resid, gain, bias, tm=1024):
    S, K = a.shape
    D = w.shape[1]

    def body(a_ref, w_ref, r_ref, g_ref, b_ref, y_ref, xh_ref, rs_ref):
        z = ALPHA * r_ref[...] + _dot(a_ref[...].astype(BF16), w_ref[...])
        y, xh, rs = _ln_fwd(z, g_ref[...], b_ref[...])
        y_ref[...] = y
        xh_ref[...] = xh
        rs_ref[...] = rs

    row = pl.BlockSpec((tm, D), lambda i: (i, 0))
    vec = pl.BlockSpec((1, D), lambda i: (0, 0))
    return pl.pallas_call(
        body, name="proj_ln",
        grid=(S // tm,),
        in_specs=[pl.BlockSpec((tm, K), lambda i: (i, 0)), pl.BlockSpec((K, D), lambda i: (0, 0)), row, vec, vec],
        out_specs=[row, row, pl.BlockSpec((tm, 1), lambda i: (i, 0))],
        out_shape=[jax.ShapeDtypeStruct((S, D), F32), jax.ShapeDtypeStruct((S, D), F32), jax.ShapeDtypeStruct((S, 1), F32)],
        compiler_params=_params(("parallel",)),
    )(a, w, resid, gain, bias)


def mm_nt(a, w, resid, a_blocked, out_dtype=F32, tm=1024):
    if a_blocked:
        nk, S, Kb = a.shape
        a_spec = pl.BlockSpec((None, tm, Kb), lambda i, n: (n, i, 0))
    else:
        S, Kb = a.shape
        nk = 1
        a_spec = pl.BlockSpec((tm, Kb), lambda i, n: (i, 0))
    M = w.shape[0]
    has_resid = resid is not None

    def body(*refs):
        if has_resid:
            a_ref, w_ref, r_ref, o_ref, acc = refs
        else:
            a_ref, w_ref, o_ref, acc = refs
        n = pl.program_id(1)
        part = _dot_nt(a_ref[...].astype(BF16), w_ref[...])

        @pl.when(n == 0)
        def _():
            acc[...] = part

        @pl.when(n > 0)
        def _():
            acc[...] += part

        @pl.when(n == nk - 1)
        def _():
            out = acc[...]
            if has_resid:
                out = out + ALPHA * r_ref[...]
            o_ref[...] = out.astype(out_dtype)

    row = pl.BlockSpec((tm, M), lambda i, n: (i, 0))
    in_specs = [a_spec, pl.BlockSpec((M, Kb), lambda i, n: (0, n))] + ([row] if has_resid else [])
    args = (a, w) + ((resid,) if has_resid else ())
    return pl.pallas_call(
        body, name="mm_nt",
        grid=(S // tm, nk),
        in_specs=in_specs,
        out_specs=row,
        out_shape=jax.ShapeDtypeStruct((S, M), out_dtype),
        scratch_shapes=[pltpu.VMEM((tm, M), F32)],
        compiler_params=_params(("parallel", "arbitrary")),
    )(*args)


def mm_nt_dilated(parts, dils, w, resid, tm=1024):
    n_groups = len(parts)
    _, S, K = parts[0].shape
    M = w.shape[0]
    nk = 3 * n_groups

    def body(*refs):
        a_refs = refs[:n_groups]
        w_ref, r_ref, o_ref, group_acc, total = refs[n_groups:]
        n = pl.program_id(1)
        for g in range(n_groups):
            for k in range(3):
                @pl.when(n == 3 * g + k)
                def _():
                    part = _dot_nt(a_refs[g][...], w_ref[...])
                    for c in range(M // LANES):
                        lanes = slice(c * LANES, (c + 1) * LANES)
                        if k == 0:
                            group_acc[c] = part[:, lanes]
                        else:
                            group_acc[c] += part[:, lanes]
                        if k == 2:
                            _rows_from_dilated(group_acc.at[c], total.at[c], tm, dils[g], accumulate=g > 0)

        @pl.when(n == nk - 1)
        def _():
            for c in range(M // LANES):
                lanes = slice(c * LANES, (c + 1) * LANES)
                o_ref[:, lanes] = total[c] + ALPHA * r_ref[:, lanes]

    def a_spec(g):
        return pl.BlockSpec((None, tm, K), lambda i, n: (jnp.clip(n - 3 * g, 0, 2), i, 0))

    row = pl.BlockSpec((tm, M), lambda i, n: (i, 0))
    return pl.pallas_call(
        body, name="mm_nt_dilated",
        grid=(S // tm, nk),
        in_specs=[a_spec(g) for g in range(n_groups)] + [pl.BlockSpec((M, K), lambda i, n: (0, n)), row],
        out_specs=row,
        out_shape=jax.ShapeDtypeStruct((S, M), F32),
        scratch_shapes=[pltpu.VMEM((M // LANES, tm, LANES), F32), pltpu.VMEM((M // LANES, tm, LANES), F32)],
        compiler_params=_params(("parallel", "arbitrary")),
    )(*parts, w, resid)


def mm_tn(pairs, *, nblk, out_shape, out_block, out_index, alias=None, tk=1024, name="mm_tn"):
    operands = []
    for a, b, _ in pairs:
        for t in (a, b):
            if not any(t is o for o in operands):
                operands.append(t)
    where = lambda t: next(i for i, o in enumerate(operands) if o is t)
    S = pairs[0][0].shape[-2]
    n_out, n_in = len(pairs), len(operands)
    n_alias = len(alias) if alias is not None else 0

    def spec(t):
        if t.ndim == 3:
            return pl.BlockSpec((None, tk, t.shape[-1]), lambda j, k: (j, k, 0))
        return pl.BlockSpec((tk, t.shape[-1]), lambda j, k: (k, 0))

    def body(*refs):
        refs = refs[n_alias:]
        in_refs, o_refs, accs = refs[:n_in], refs[n_in:n_in + n_out], refs[n_in + n_out:]
        k = pl.program_id(1)
        for (a, b, scale), o_ref, acc in zip(pairs, o_refs, accs):
            part = _dot_tn(in_refs[where(a)][...].astype(BF16), in_refs[where(b)][...].astype(BF16))

            @pl.when(k == 0)
            def _():
                acc[...] = part

            @pl.when(k > 0)
            def _():
                acc[...] += part

            @pl.when(k == S // tk - 1)
            def _():
                o_ref[...] = (scale * acc[...]).astype(BF16)

    out_spec = pl.BlockSpec(out_block, lambda j, k: out_index(j))
    outs = pl.pallas_call(
        body, name=name,
        grid=(nblk, S // tk),
        in_specs=[ANY] * n_alias + [spec(t) for t in operands],
        out_specs=[out_spec] * n_out,
        out_shape=[jax.ShapeDtypeStruct(out_shape, BF16)] * n_out,
        scratch_shapes=[pltpu.VMEM((a.shape[-1], b.shape[-1]), F32) for a, b, _ in pairs],
        input_output_aliases={i: i for i in range(n_alias)},
        compiler_params=_params(("parallel", "arbitrary")),
    )(*(tuple(alias) if alias is not None else ()), *operands)
    return list(outs)


def ffn_fwd(x, wg, wu, wd, gain, bias, tm=1024):
    S, D = x.shape
    nb, FB = wg.shape[0], wg.shape[1]

    def body(x_ref, wg_ref, wu_ref, wd_ref, g_ref, b_ref, y_ref, xh_ref, rs_ref, a_ref, u_ref, acc, xb_ref):
        j = pl.program_id(1)

        @pl.when(j == 0)
        def _():
            xb_ref[...] = x_ref[...].astype(BF16)
            acc[...] = jnp.zeros_like(acc)

        xb = xb_ref[...]
        total = None
        for t0 in range(0, FB, FFN_HIDDEN_TILE):
            cols = pl.ds(t0, min(FFN_HIDDEN_TILE, FB - t0))
            a = _dot_nt(xb, wg_ref[cols, :])
            u = _dot_nt(xb, wu_ref[cols, :])
            a_ref[:, cols] = a.astype(BF16)
            u_ref[:, cols] = u.astype(BF16)
            h = a * jax.nn.sigmoid(a) * u
            part = _dot(h.astype(BF16), wd_ref[cols, :])
            total = part if total is None else total + part
        acc[...] += total

        @pl.when(j == nb - 1)
        def _():
            z = ALPHA * x_ref[...] + MACARON_WEIGHT * acc[...]
            y, xh, rs = _ln_fwd(z, g_ref[...], b_ref[...])
            y_ref[...] = y
            xh_ref[...] = xh
            rs_ref[...] = rs

    row = pl.BlockSpec((tm, D), lambda i, j: (i, 0))
    vec = pl.BlockSpec((1, D), lambda i, j: (0, 0))
    w_out = pl.BlockSpec((None, FB, D), lambda i, j: (j, 0, 0))
    act = pl.BlockSpec((None, tm, FB), lambda i, j: (j, i, 0))
    return pl.pallas_call(
        body, name="ffn_fwd",
        grid=(S // tm, nb),
        in_specs=[row, w_out, w_out, w_out, vec, vec],
        out_specs=[row, row, pl.BlockSpec((tm, 1), lambda i, j: (i, 0)), act, act],
        out_shape=[jax.ShapeDtypeStruct((S, D), F32), jax.ShapeDtypeStruct((S, D), F32), jax.ShapeDtypeStruct((S, 1), F32),
                   jax.ShapeDtypeStruct((nb, S, FB), BF16), jax.ShapeDtypeStruct((nb, S, FB), BF16)],
        scratch_shapes=[pltpu.VMEM((tm, D), F32), pltpu.VMEM((tm, D), BF16)],
        compiler_params=_params(("parallel", "arbitrary")),
    )(x, wg, wu, wd, gain, bias)


def ffn_bwd(dz, a, u, wg, wu, wd, tm=1024):
    S, D = dz.shape
    nb, FB = wg.shape[0], wg.shape[1]

    def body(dz_ref, a_ref, u_ref, wg_ref, wu_ref, wd_ref, dx_ref, h_ref, da_ref, du_ref, acc, dzb_ref):
        j = pl.program_id(1)

        @pl.when(j == 0)
        def _():
            dzb_ref[...] = (MACARON_WEIGHT * dz_ref[...]).astype(BF16)
            acc[...] = jnp.zeros_like(acc)

        dzb = dzb_ref[...]
        total = None
        for t0 in range(0, FB, FFN_HIDDEN_TILE):
            cols = pl.ds(t0, min(FFN_HIDDEN_TILE, FB - t0))
            dh = _dot_nt(dzb, wd_ref[cols, :])
            av = a_ref[:, cols].astype(F32)
            uv = u_ref[:, cols].astype(F32)
            s = jax.nn.sigmoid(av)
            silu = av * s
            h_ref[:, cols] = (silu * uv).astype(BF16)
            da = (dh * uv * (s * (1.0 + av * (1.0 - s)))).astype(BF16)
            du = (dh * silu).astype(BF16)
            da_ref[:, cols] = da
            du_ref[:, cols] = du
            both = jnp.concatenate([da, du], axis=1)
            weights = jnp.concatenate([wg_ref[cols, :], wu_ref[cols, :]], axis=0)
            part = _dot(both, weights)
            total = part if total is None else total + part
        acc[...] += total

        @pl.when(j == nb - 1)
        def _():
            dx_ref[...] = ALPHA * dz_ref[...] + acc[...]

    row = pl.BlockSpec((tm, D), lambda i, j: (i, 0))
    w_out = pl.BlockSpec((None, FB, D), lambda i, j: (j, 0, 0))
    act = pl.BlockSpec((None, tm, FB), lambda i, j: (j, i, 0))
    act_shape = jax.ShapeDtypeStruct((nb, S, FB), BF16)
    return pl.pallas_call(
        body, name="ffn_bwd",
        grid=(S // tm, nb),
        in_specs=[row, act, act, w_out, w_out, w_out],
        out_specs=[row, act, act, act],
        out_shape=[jax.ShapeDtypeStruct((S, D), F32), act_shape, act_shape, act_shape],
        scratch_shapes=[pltpu.VMEM((tm, D), F32), pltpu.VMEM((tm, D), BF16)],
        compiler_params=_params(("parallel", "arbitrary")),
    )(dz, a, u, wg, wu, wd)


def ln_bwd(dy, xhat, rstd, gain, tm=1024):
    S, D = dy.shape

    def body(dy_ref, xh_ref, rs_ref, g_ref, dz_ref, dg_ref, db_ref):
        i = pl.program_id(0)
        dy = dy_ref[...]
        xh = xh_ref[...]
        dxh = dy * g_ref[...]
        m1 = jnp.mean(dxh, axis=-1, keepdims=True)
        m2 = jnp.mean(dxh * xh, axis=-1, keepdims=True)
        dz_ref[...] = rs_ref[...] * (dxh - m1 - xh * m2)
        dg = jnp.sum(dy * xh, axis=0, keepdims=True)
        db = jnp.sum(dy, axis=0, keepdims=True)

        @pl.when(i == 0)
        def _():
            dg_ref[...] = dg
            db_ref[...] = db

        @pl.when(i > 0)
        def _():
            dg_ref[...] += dg
            db_ref[...] += db

    row = pl.BlockSpec((tm, D), lambda i: (i, 0))
    vec = pl.BlockSpec((1, D), lambda i: (0, 0))
    return pl.pallas_call(
        body, name="ln_bwd",
        grid=(S // tm,),
        in_specs=[row, row, pl.BlockSpec((tm, 1), lambda i: (i, 0)), vec],
        out_specs=[row, vec, vec],
        out_shape=[jax.ShapeDtypeStruct((S, D), F32), jax.ShapeDtypeStruct((1, D), F32), jax.ShapeDtypeStruct((1, D), F32)],
        compiler_params=_params(("arbitrary",)),
    )(dy, xhat, rstd, gain)


def loss_head(y, target, tm=1024):
    S, D = y.shape

    def body(y_ref, t_ref, dy_ref, l_ref):
        i = pl.program_id(0)
        e = y_ref[...] - t_ref[...]
        dy_ref[...] = e / D
        part = 0.5 * jnp.sum(jnp.mean(e * e, axis=-1, keepdims=True), axis=0, keepdims=True)

        @pl.when(i == 0)
        def _():
            l_ref[...] = part

        @pl.when(i > 0)
        def _():
            l_ref[...] += part

    row = pl.BlockSpec((tm, D), lambda i: (i, 0))
    return pl.pallas_call(
        body, name="loss_head",
        grid=(S // tm,),
        in_specs=[row, row],
        out_specs=[row, pl.BlockSpec((1, 1), lambda i: (0, 0))],
        out_shape=[jax.ShapeDtypeStruct((S, D), F32), jax.ShapeDtypeStruct((1, 1), F32)],
        compiler_params=_params(("arbitrary",)),
    )(y, target)


def _pool_window(xp, g):
    n = xp.shape[0]
    w = xp + pltpu.roll(xp, 1, 0)
    out = w
    for level, shift in enumerate((1, 2, 4), start=1):
        w = pltpu.roll(w, shift, 0) + pltpu.roll(w, n - shift, 0)
        out = jnp.where(g >= level, w, out)
    return out


def _pool_count(S, C, g):
    half = lax.shift_left(jnp.int32(1), g)
    t = lax.broadcasted_iota(jnp.int32, (S, C), 0)
    return (jnp.minimum(t + half, S) - jnp.maximum(t - half, 0)).astype(F32)


def pool_mix(u, wgrp, scale):
    S, D = u.shape
    G, C = wgrp.shape[0], wgrp.shape[1]

    def body(u_ref, w_ref, s_ref, mix_ref, v_ref, pad):
        g = pl.program_id(0)
        zeros = jnp.zeros((POOL_PAD, C), F32)
        pad[pl.ds(0, POOL_PAD), :] = zeros
        pad[pl.ds(POOL_PAD + S, POOL_PAD), :] = zeros
        pad[pl.ds(POOL_PAD, S), :] = u_ref[...]
        win = _pool_window(pad[...], g)[POOL_PAD:POOL_PAD + S]
        mixed = (win / _pool_count(S, C, g) - u_ref[...]).astype(BF16)
        mix_ref[...] = mixed
        v_ref[...] = _dot(mixed, w_ref[...]) * s_ref[...]

    col = pl.BlockSpec((S, C), lambda g: (0, g))
    return pl.pallas_call(
        body, name="pool_mix",
        grid=(G,),
        in_specs=[col, pl.BlockSpec((None, C, C), lambda g: (g, 0, 0)), pl.BlockSpec((1, C), lambda g: (0, g))],
        out_specs=[col, col],
        out_shape=[jax.ShapeDtypeStruct((S, D), BF16), jax.ShapeDtypeStruct((S, D), F32)],
        scratch_shapes=[pltpu.VMEM((S + 2 * POOL_PAD, C), F32)],
        compiler_params=_params(("arbitrary",)),
    )(u, wgrp, scale)


def pool_mix_bwd(dv, mixed, wgrp, scale):
    S, D = dv.shape
    G, C = wgrp.shape[0], wgrp.shape[1]

    def body(dv_ref, mix_ref, w_ref, s_ref, du_ref, dw_ref, ds_ref, pad):
        g = pl.program_id(0)
        mixed = mix_ref[...]
        dv = dv_ref[...]
        yg = _dot(mixed, w_ref[...])
        ds_ref[...] = jnp.sum(dv * yg, axis=0, keepdims=True)
        dyg = (dv * s_ref[...]).astype(BF16)
        dw_ref[...] = _dot_tn(mixed, dyg).astype(BF16)
        dmix = _dot_nt(dyg, w_ref[...])
        zeros = jnp.zeros((POOL_PAD, C), F32)
        pad[pl.ds(0, POOL_PAD), :] = zeros
        pad[pl.ds(POOL_PAD + S, POOL_PAD), :] = zeros
        pad[pl.ds(POOL_PAD, S), :] = dmix / _pool_count(S, C, g)
        win = _pool_window(pad[...], g)
        win = pltpu.roll(win, win.shape[0] - 1, 0)[POOL_PAD:POOL_PAD + S]
        du_ref[...] = win - dmix

    col = pl.BlockSpec((S, C), lambda g: (0, g))
    return pl.pallas_call(
        body, name="pool_mix_bwd",
        grid=(G,),
        in_specs=[col, col, pl.BlockSpec((None, C, C), lambda g: (g, 0, 0)), pl.BlockSpec((1, C), lambda g: (0, g))],
        out_specs=[col, pl.BlockSpec((None, C, C), lambda g: (g, 0, 0)), pl.BlockSpec((1, C), lambda g: (0, g))],
        out_shape=[jax.ShapeDtypeStruct((S, D), F32), jax.ShapeDtypeStruct((G, C, C), BF16), jax.ShapeDtypeStruct((1, D), F32)],
        scratch_shapes=[pltpu.VMEM((S + 2 * POOL_PAD, C), F32)],
        compiler_params=_params(("arbitrary",)),
    )(dv, mixed, wgrp, scale)


PERM_BLOCK = 256


def _dilated_runs(rows, d):
    n = PERM_BLOCK // d
    return [(c * PERM_BLOCK, r, n) for c in range(rows // PERM_BLOCK) for r in range(d)]


def _rows_to_dilated(src_ref, dst_ref, rows, d):
    for base, r, n in _dilated_runs(rows, d):
        dst_ref[pl.ds(base + r * n, n), :] = src_ref[pl.ds(base + r, n, stride=d), :].astype(dst_ref.dtype)


def _rows_from_dilated(src_ref, dst_ref, rows, d, accumulate=False):
    for base, r, n in _dilated_runs(rows, d):
        at = pl.ds(base + r, n, stride=d)
        v = src_ref[pl.ds(base + r * n, n), :]
        dst_ref[at, :] = dst_ref[at, :] + v if accumulate else v


def dilate_rows(x, dils, out_dtype, tm=2048):
    S, D = x.shape

    def body(x_ref, *o_refs):
        for d, o_ref in zip(dils, o_refs):
            _rows_to_dilated(x_ref, o_ref, tm, d)

    tile = pl.BlockSpec((tm, LANES), lambda i, j: (i, j))
    return pl.pallas_call(
        body, name="dilate_rows",
        grid=(S // tm, D // LANES),
        in_specs=[tile],
        out_specs=[tile] * len(dils),
        out_shape=[jax.ShapeDtypeStruct((S, D), out_dtype)] * len(dils),
        compiler_params=_params(("parallel", "parallel")),
    )(x)


def _slope_table(group, dilation):
    s = _alibi_slopes()[group].reshape(N_HEADS // 2, 2, 1, 1) * float(dilation)
    return jnp.asarray(np.broadcast_to(s, (N_HEADS // 2, 2, 1, ATTN_W)).copy())


def _residue_shape(S, D, d):
    return (S, D) if d == 1 else (S // PERM_BLOCK, d, PERM_BLOCK // d, D)


def _residue_view(x, d):
    return x.reshape(x.shape[:-2] + _residue_shape(x.shape[-2], x.shape[-1], d))


def _residue_spec(lead_block, lead_index, S, d):
    if d == 1:
        return pl.BlockSpec(lead_block + (S, LANES), lambda hp, r: lead_index + (0, hp))
    return pl.BlockSpec(lead_block + (S // PERM_BLOCK, None, PERM_BLOCK // d, LANES), lambda hp, r: lead_index + (0, r, 0, hp))


def _whole(ref, lead, L):
    return ref[lead + (slice(None),) * (len(ref.shape) - len(lead))].reshape(L, LANES)


def _query_rows(lead, i, d):
    if d == 1:
        return lead + (pl.ds(pl.multiple_of(i * ATTN_BQ, ATTN_BQ), ATTN_BQ), slice(None)), (ATTN_BQ, LANES)
    n = PERM_BLOCK // d
    return lead + (pl.ds(i * (ATTN_BQ // n), ATTN_BQ // n), slice(None), slice(None)), (ATTN_BQ // n, n, LANES)


def _load_query_rows(ref, lead, i, d):
    at, _ = _query_rows(lead, i, d)
    return ref[at].reshape(ATTN_BQ, LANES)


def _store_query_rows(ref, lead, i, d, value):
    at, shape = _query_rows(lead, i, d)
    ref[at] = value.reshape(shape)


def _stage_keys(dst, src_ref, L):
    rows = _whole(src_ref, (), L)
    lane = lax.broadcasted_iota(jnp.int32, (L, LANES), 1)
    zeros = jnp.zeros((ATTN_R, LANES), dst.dtype)
    for h in range(2):
        mine = (lane < HEAD_DIM) if h == 0 else (lane >= HEAD_DIM)
        dst[h, pl.ds(0, ATTN_R), :] = zeros
        dst[h, pl.ds(ATTN_R + L, ATTN_R), :] = zeros
        dst[h, pl.ds(ATTN_R, L), :] = jnp.where(mine, rows, jnp.zeros_like(rows))


def _fill_bias(bias, sl_ref):
    a = lax.broadcasted_iota(jnp.int32, (ATTN_BQ, ATTN_W), 0)
    c = lax.broadcasted_iota(jnp.int32, (ATTN_BQ, ATTN_W), 1)
    rel = jnp.abs(c - ATTN_R - a)
    band = rel <= ATTN_R
    after_start = c >= ATTN_R
    before_end = c < ATTN_BQ + ATTN_R
    for h in range(2):
        base = -(sl_ref[h] * rel.astype(F32))
        for variant in range(4):
            ok = band
            if variant & 1:
                ok = ok & after_start
            if variant & 2:
                ok = ok & before_end
            bias[variant, h] = jnp.where(ok, base, MASK_VALUE)


def _bias_variant(i, nq):
    return jnp.where(i == 0, 1, 0) + jnp.where(i == nq - 1, 2, 0)


def attn_fwd(qkv, group, dilation):
    _, S, D = qkv.shape
    d = dilation
    L = S // d
    nq = L // ATTN_BQ
    ncol = D // LANES
    view = _residue_view(qkv, d)
    slopes = _slope_table(group, d)
    scale = HEAD_DIM ** -0.5

    def body(q_ref, k_ref, v_ref, sl_ref, o_ref, lse_ref, k2, v2, bias):
        @pl.when(pl.program_id(1) == 0)
        def _():
            _fill_bias(bias, sl_ref)

        _stage_keys(k2, k_ref, L)
        _stage_keys(v2, v_ref, L)
        head0 = lax.broadcasted_iota(jnp.int32, (ATTN_BQ, LANES), 1) < HEAD_DIM

        def block(i):
            variant = _bias_variant(i, nq)
            win = pl.ds(pl.multiple_of(i * ATTN_BQ, ATTN_BQ), ATTN_W)
            qs = _load_query_rows(q_ref, (), i, d) * jnp.asarray(scale, BF16)
            es, ms, ls = [], [], []
            for h in range(2):
                s = _dot_nt(qs, k2[h, win, :]) + bias[variant, h]
                m = jnp.max(s, axis=-1, keepdims=True)
                e = jnp.exp(s - m)
                ls.append(jnp.sum(e, axis=-1, keepdims=True))
                ms.append(m)
                es.append(e.astype(BF16))
            acc = _dot(jnp.concatenate(es, axis=1), jnp.concatenate([v2[0, win, :], v2[1, win, :]], axis=0))
            out = acc * jnp.where(head0, 1.0 / ls[0], 1.0 / ls[1])
            lse = jnp.where(head0, ms[0] + jnp.log(ls[0]), ms[1] + jnp.log(ls[1]))
            return out, lse

        def step(t, carry):
            results = [block(t * group + b) for b in range(group)]
            for b, (out, lse) in enumerate(results):
                _store_query_rows(o_ref, (), t * group + b, d, out)
                _store_query_rows(lse_ref, (), t * group + b, d, lse)
            return carry

        group = min(ATTN_UNROLL, nq)
        lax.fori_loop(0, nq // group, step, 0)

    def col(which):
        return _residue_spec((None,), (which,), S, d)

    out = _residue_spec((), (), S, d)
    o, lse = pl.pallas_call(
        body, name=f"attn_fwd_g{group}",
        grid=(ncol, d),
        in_specs=[col(0), col(1), col(2), pl.BlockSpec((None, 2, 1, ATTN_W), lambda hp, r: (hp, 0, 0, 0))],
        out_specs=[out, out],
        out_shape=[jax.ShapeDtypeStruct(_residue_shape(S, D, d), F32)] * 2,
        scratch_shapes=[pltpu.VMEM((2, L + 2 * ATTN_R, LANES), BF16), pltpu.VMEM((2, L + 2 * ATTN_R, LANES), BF16),
                        pltpu.VMEM((4, 2, ATTN_BQ, ATTN_W), F32)],
        compiler_params=_params(("arbitrary", "arbitrary")),
    )(view, view, view, slopes)
    return o.reshape(S, D), lse.reshape(S, D)


def attn_combine(os, lses, dils, tm=2048):
    S, D = os[0].shape
    n = len(os)
    n_moved = sum(d > 1 for d in dils)

    def body(*refs):
        o_refs, l_refs, out_ref, lse_ref = list(refs[:n]), list(refs[n:2 * n]), refs[2 * n], refs[2 * n + 1]
        spare = list(refs[2 * n + 2:])
        for g, d in enumerate(dils):
            if d > 1:
                for which in (o_refs, l_refs):
                    token_order = spare.pop()
                    _rows_from_dilated(which[g], token_order, tm, d)
                    which[g] = token_order
        ls = [r[...] for r in l_refs]
        m = functools.reduce(jnp.maximum, ls)
        es = [jnp.exp(l - m) for l in ls]
        tot = functools.reduce(lambda x, y: x + y, es)
        inv = 1.0 / tot
        out_ref[...] = functools.reduce(lambda x, y: x + y, [(e * inv) * r[...] for e, r in zip(es, o_refs)])
        lse_ref[...] = m + jnp.log(tot)

    tile = pl.BlockSpec((tm, LANES), lambda i, j: (i, j))
    return pl.pallas_call(
        body, name="attn_combine",
        grid=(S // tm, D // LANES),
        in_specs=[tile] * (2 * n),
        out_specs=[tile, tile],
        out_shape=[jax.ShapeDtypeStruct((S, D), F32), jax.ShapeDtypeStruct((S, D), F32)],
        scratch_shapes=[pltpu.VMEM((tm, LANES), F32)] * (2 * n_moved),
        compiler_params=_params(("parallel", "parallel")),
    )(*os, *lses)


def attn_bwd(qkv, do, o, lse, group, dilation):
    _, S, D = qkv.shape
    d = dilation
    L = S // d
    nq = L // ATTN_BQ
    ncol = D // LANES
    view = _residue_view(qkv, d)
    slopes = _slope_table(group, d)
    scale = HEAD_DIM ** -0.5

    def body(q_ref, k_ref, v_ref, do_ref, o_ref, lse_ref, sl_ref, dx_ref, k2, v2, dkacc, dvacc, bias):
        @pl.when(pl.program_id(1) == 0)
        def _():
            _fill_bias(bias, sl_ref)

        _stage_keys(k2, k_ref, L)
        _stage_keys(v2, v_ref, L)
        dkacc[...] = jnp.zeros_like(dkacc)
        dvacc[...] = jnp.zeros_like(dvacc)
        lane = lax.broadcasted_iota(jnp.int32, (ATTN_BQ, LANES), 1)
        heads = (lane < HEAD_DIM, lane >= HEAD_DIM)
        key_head0 = lax.broadcasted_iota(jnp.int32, (ATTN_W, LANES), 1) < HEAD_DIM

        def step(i, carry):
            variant = _bias_variant(i, nq)
            win = pl.ds(pl.multiple_of(i * ATTN_BQ, ATTN_BQ), ATTN_W)
            q = _load_query_rows(q_ref, (), i, d)
            qs = q * jnp.asarray(scale, BF16)
            dov = _load_query_rows(do_ref, (), i, d)
            prod = dov * _load_query_rows(o_ref, (), i, d)
            lse_v = _load_query_rows(lse_ref, (), i, d)
            dob = dov.astype(BF16)
            dss, dks, dvs = [], [], []
            for h in range(2):
                s = _dot_nt(qs, k2[h, win, :]) + bias[variant, h]
                lse_h = jnp.max(jnp.where(heads[h], lse_v, -jnp.inf), axis=-1, keepdims=True)
                dterm = jnp.sum(jnp.where(heads[h], prod, 0.0), axis=-1, keepdims=True)
                p = jnp.exp(s - lse_h)
                dp = _dot_nt(dob, v2[h, win, :])
                ds = (p * (dp - dterm) * scale).astype(BF16)
                dvs.append(_dot_tn(p.astype(BF16), dob))
                dks.append(_dot_tn(ds, q))
                dss.append(ds)
            dq = _dot(jnp.concatenate(dss, axis=1), jnp.concatenate([k2[0, win, :], k2[1, win, :]], axis=0))
            dvacc[win, :] += jnp.where(key_head0, dvs[0], dvs[1])
            dkacc[win, :] += jnp.where(key_head0, dks[0], dks[1])
            _store_query_rows(dx_ref, (0,), i, d, dq.astype(BF16))
            return carry

        lax.fori_loop(0, nq, step, 0, unroll=min(ATTN_UNROLL, nq))
        block_shape = dx_ref.shape[1:]
        dx_ref[1] = dkacc[pl.ds(ATTN_R, L), :].astype(BF16).reshape(block_shape)
        dx_ref[2] = dvacc[pl.ds(ATTN_R, L), :].astype(BF16).reshape(block_shape)

    def col(which):
        return _residue_spec((None,), (which,), S, d)

    act = _residue_spec((), (), S, d)
    out = pl.pallas_call(
        body, name=f"attn_bwd_g{group}",
        grid=(ncol, d),
        in_specs=[col(0), col(1), col(2), act, act, act, pl.BlockSpec((None, 2, 1, ATTN_W), lambda hp, r: (hp, 0, 0, 0))],
        out_specs=_residue_spec((3,), (0,), S, d),
        out_shape=jax.ShapeDtypeStruct((3,) + _residue_shape(S, D, d), BF16),
        scratch_shapes=[pltpu.VMEM((2, L + 2 * ATTN_R, LANES), BF16), pltpu.VMEM((2, L + 2 * ATTN_R, LANES), BF16),
                        pltpu.VMEM((L + 2 * ATTN_R, LANES), F32), pltpu.VMEM((L + 2 * ATTN_R, LANES), F32),
                        pltpu.VMEM((4, 2, ATTN_BQ, ATTN_W), F32)],
        compiler_params=_params(("arbitrary", "arbitrary")),
    )(view, view, view, _residue_view(do, d), _residue_view(o, d), _residue_view(lse, d), slopes)
    return out.reshape(3, S, D)


TILE_ELEMS = 256 * 1024


def _row_tile(R, C):
    if R * C <= TILE_ELEMS or R % 16:
        return R
    return max(t for t in range(16, R + 1, 16) if R % t == 0 and (t * C <= TILE_ELEMS or t == 16))


def pair_sum(core, gs, recvs):
    n = len(gs)
    _, _, R, C = gs[0].shape
    tr = _row_tile(R, C)

    def body(c_ref, *refs):
        for g_ref, r_ref, o_ref in zip(refs[:n], refs[n:2 * n], refs[2 * n:]):
            o_ref[...] = (g_ref[...].astype(F32) + r_ref[...].astype(F32)).astype(BF16)

    blk = pl.BlockSpec((None, tr, C), lambda d, i, c_ref: (d, i, 0))
    mine = pl.BlockSpec((None, None, tr, C), lambda d, i, c_ref: (d, c_ref[0], i, 0))
    return pl.pallas_call(
        body, name="pair_sum",
        grid_spec=pltpu.PrefetchScalarGridSpec(
            num_scalar_prefetch=1, grid=(N_CHIPS, R // tr),
            in_specs=[mine] * n + [blk] * n,
            out_specs=[blk] * n),
        out_shape=[jax.ShapeDtypeStruct((N_CHIPS, R, C), BF16)] * n,
        compiler_params=_params(("parallel", "parallel")),
    )(core, *gs, *recvs)


def chip_sum(chip, owns, recvs):
    n = len(owns)
    _, R, C = owns[0].shape
    tr = _row_tile(R, C)
    slot_of_relation = {2: 0, 1: 1, 3: 2}

    def body(chip_ref, *refs):
        me = chip_ref[0]
        for own_ref, r_ref, o_ref in zip(refs[:n], refs[n:2 * n], refs[2 * n:]):
            mine = own_ref[...].astype(F32)
            theirs = {rel: r_ref[k].astype(F32) for rel, k in slot_of_relation.items()}
            acc = None
            for s in range(N_CHIPS):
                rel = jnp.bitwise_xor(me, s)
                part = jnp.where(rel == 0, mine, jnp.where(rel == 2, theirs[2], jnp.where(rel == 1, theirs[1], theirs[3])))
                acc = part if acc is None else acc + part
            o_ref[...] = acc

    return pl.pallas_call(
        body, name="chip_sum",
        grid_spec=pltpu.PrefetchScalarGridSpec(
            num_scalar_prefetch=1, grid=(R // tr,),
            in_specs=[pl.BlockSpec((None, tr, C), lambda i, chip_ref: (chip_ref[0], i, 0))] * n
            + [pl.BlockSpec((N_CHIPS - 1, tr, C), lambda i, chip_ref: (0, i, 0))] * n,
            out_specs=[pl.BlockSpec((tr, C), lambda i, chip_ref: (i, 0))] * n),
        out_shape=[jax.ShapeDtypeStruct((R, C), F32)] * n,
        compiler_params=_params(("parallel",)),
    )(chip, *owns, *recvs)


def adamw(core, w, g_pairs, m, v):
    L, H, R, C = w.shape
    tr = _row_tile(R, C)

    def body(c_ref, w_ref, *rest):
        g_refs = rest[:2 * L]
        m_ref, v_ref, g_ref, d_ref, nm_ref, nv_ref = rest[2 * L:]
        mine = pl.program_id(1) == c_ref[0]
        g = None
        for l in range(L):
            g_l = jnp.where(mine, g_refs[2 * l][...], g_refs[2 * l + 1][...])
            g = g_l if g is None else jnp.where(pl.program_id(0) == l, g_l, g)
        m = ADAM_B1 * m_ref[...] + (1.0 - ADAM_B1) * g
        v = ADAM_B2 * v_ref[...] + (1.0 - ADAM_B2) * (g * g)
        m_hat = m / (1.0 - ADAM_B1 ** ADAM_STEP)
        v_hat = v / (1.0 - ADAM_B2 ** ADAM_STEP)
        g_ref[...] = g
        d_ref[...] = -ADAM_LR * (m_hat / (jnp.sqrt(v_hat) + ADAM_EPS) + ADAM_WD * w_ref[...])
        nm_ref[...] = m
        nv_ref[...] = v

    blk = pl.BlockSpec((None, None, tr, C), lambda l, h, i, c_ref: (l, h, i, 0))

    def half(layer):
        return pl.BlockSpec((tr, C), lambda l, h, i, c_ref: (jnp.where(l == layer, i, 0), 0))

    shape = jax.ShapeDtypeStruct((L, H, R, C), F32)
    return pl.pallas_call(
        body, name="adamw",
        grid_spec=pltpu.PrefetchScalarGridSpec(
            num_scalar_prefetch=1, grid=(L, H, R // tr),
            in_specs=[blk] + [half(l) for l in range(L) for _ in range(2)] + [blk, blk],
            out_specs=[blk] * 4),
        out_shape=[shape] * 4,
        compiler_params=_params(("parallel", "parallel", "parallel")),
    )(core, w, *[g for pair in g_pairs for g in pair], m, v)


def _place():
    return lax.axis_index("x"), lax.axis_index("y"), lax.axis_index("c")


def _other_chips(x, y):
    return [(2 * (1 - x) + y, (1 - x, y)), (2 * x + (1 - y), (x, 1 - y)), (2 * (1 - x) + (1 - y), (1 - x, 1 - y))]


def all_gather_shards(shards, placed):
    n = len(shards)

    def body(*refs):
        ins, outs = refs[:n], refs[2 * n:3 * n]
        send_sems, recv_sems = refs[3 * n:]
        x, y, c = _place()
        me = 2 * x + y
        sibling = (x, y, 1 - c)
        chips = _other_chips(x, y)

        def copy(a, k, src, dst, to):
            return pltpu.make_async_remote_copy(src_ref=src, dst_ref=dst, send_sem=send_sems.at[a, k], recv_sem=recv_sems.at[a, k],
                                                device_id=to, device_id_type=MESH)

        sends = []
        for a in range(n):
            for k, (_, (px, py)) in enumerate(chips):
                cp = copy(a, k, ins[a].at[c], outs[a].at[me, c], (px, py, c))
                cp.start()
                sends.append(cp)
        for a in range(n):
            for k, (chip, _) in enumerate(chips):
                landed = outs[a].at[chip, c]
                copy(a, k, landed, landed, sibling).wait_recv()
                cp = copy(a, 3 + k, landed, landed, sibling)
                cp.start()
                sends.append(cp)
        for a in range(n):
            for k, (chip, _) in enumerate(chips):
                other = outs[a].at[chip, 1 - c]
                copy(a, 3 + k, other, other, sibling).wait_recv()
        for cp in sends:
            cp.wait_send()

    return pl.pallas_call(
        body, name="all_gather_shards",
        in_specs=[ANY] * (2 * n),
        out_specs=[ANY] * n,
        out_shape=[jax.ShapeDtypeStruct(p.shape, p.dtype) for p in placed],
        scratch_shapes=[pltpu.SemaphoreType.DMA((n, 6)), pltpu.SemaphoreType.DMA((n, 6))],
        input_output_aliases={n + a: a for a in range(n)},
        compiler_params=pltpu.CompilerParams(has_side_effects=True),
    )(*shards, *placed)


def all_gather_shards_async(shards, placed, collective_id, name):
    n = len(shards)
    srcs = [jax.new_ref(s, memory_space=pltpu.MemorySpace.HBM) for s in shards]
    dsts = [jax.new_ref(p, memory_space=pltpu.MemorySpace.HBM) for p in placed]

    @pl.kernel(mesh=plsc.ScalarSubcoreMesh(axis_name="sequencer", num_cores=1), name=name,
               scratch_types=(pltpu.SemaphoreType.DMA((n, 6)), pltpu.SemaphoreType.DMA((n, 6))),
               compiler_params=pltpu.CompilerParams(collective_id=collective_id))
    def launch(send_sems, recv_sems):
        x, y, c = _place()
        me = 2 * x + y
        sibling = (x, y, 1 - c)
        chips = _other_chips(x, y)
        barrier = pltpu.get_barrier_semaphore()
        peers = [sibling] + [(px, py, c) for _, (px, py) in chips]
        for peer in peers:
            pl.semaphore_signal(barrier, inc=1, device_id=peer, device_id_type=MESH)
        pl.semaphore_wait(barrier, len(peers))

        def copy(a, k, src, dst, to):
            return pltpu.make_async_remote_copy(src_ref=src, dst_ref=dst, send_sem=send_sems.at[a, k], recv_sem=recv_sems.at[a, k],
                                                device_id=to, device_id_type=MESH)

        sends = []
        for a in range(n):
            for k, (_, (px, py)) in enumerate(chips):
                cp = copy(a, k, srcs[a].at[c], dsts[a].at[me, c], (px, py, c))
                cp.start()
                sends.append(cp)
        for a in range(n):
            for k, (chip, _) in enumerate(chips):
                landed = dsts[a].at[chip, c]
                copy(a, k, landed, landed, sibling).wait_recv()
                cp = copy(a, 3 + k, landed, landed, sibling)
                cp.start()
                sends.append(cp)
        for a in range(n):
            for k, (chip, _) in enumerate(chips):
                other = dsts[a].at[chip, 1 - c]
                copy(a, 3 + k, other, other, sibling).wait_recv()
        for cp in sends:
            cp.wait_send()

    launch()
    return [d[...] for d in dsts]


def _exchange(body, ins, out_shapes, sem_shapes, name, peers=None, collective_id=None):
    n_in, n_out = len(ins), len(out_shapes)
    sems = [pltpu.SemaphoreType.DMA(shape) for shape in sem_shapes]
    if collective_id is None:
        def tc_body(*refs):
            body(refs[:n_in], refs[n_in:n_in + n_out], *refs[n_in + n_out:])

        return pl.pallas_call(tc_body, name=name, in_specs=[ANY] * n_in, out_specs=[ANY] * n_out, out_shape=out_shapes,
                              scratch_shapes=sems, compiler_params=pltpu.CompilerParams(has_side_effects=True))(*ins)
    srcs = [jax.new_ref(a, memory_space=pltpu.MemorySpace.HBM) for a in ins]
    dsts = [jax.empty_ref(shape, memory_space=pltpu.MemorySpace.HBM) for shape in out_shapes]

    @pl.kernel(mesh=plsc.ScalarSubcoreMesh(axis_name="sequencer", num_cores=1), name=name, scratch_types=tuple(sems),
               compiler_params=pltpu.CompilerParams(collective_id=collective_id))
    def launch(*sem_refs):
        barrier = pltpu.get_barrier_semaphore()
        others = peers(*_place())
        for peer in others:
            pl.semaphore_signal(barrier, inc=1, device_id=peer, device_id_type=MESH)
        pl.semaphore_wait(barrier, len(others))
        body(srcs, dsts, *sem_refs)

    launch()
    return [d[...] for d in dsts]


def _sibling(x, y, c):
    return [(x, y, 1 - c)]


def _same_core_of_other_chips(x, y, c):
    return [(px, py, c) for _, (px, py) in _other_chips(x, y)]


def sibling_exchange_halves(grads, name="sibling_exchange_halves", collective_id=None):
    n = len(grads)

    def body(ins, outs, send_sems, recv_sems):
        x, y, c = _place()
        copies = [pltpu.make_async_remote_copy(src_ref=ins[a].at[:, 1 - c], dst_ref=outs[a], send_sem=send_sems.at[a],
                                               recv_sem=recv_sems.at[a], device_id=(x, y, 1 - c), device_id_type=MESH) for a in range(n)]
        for cp in copies:
            cp.start()
        for cp in copies:
            cp.wait()

    shapes = [jax.ShapeDtypeStruct((N_CHIPS,) + g.shape[2:], g.dtype) for g in grads]
    return _exchange(body, grads, shapes, [(n,), (n,)], name, _sibling, collective_id)


def chip_exchange(sums, name="chip_exchange", collective_id=None):
    n = len(sums)

    def body(ins, outs, send_sems, recv_sems):
        x, y, c = _place()
        copies = []
        for a in range(n):
            for k, (chip, (px, py)) in enumerate(_other_chips(x, y)):
                cp = pltpu.make_async_remote_copy(src_ref=ins[a].at[chip], dst_ref=outs[a].at[k], send_sem=send_sems.at[a, k],
                                                  recv_sem=recv_sems.at[a, k], device_id=(px, py, c), device_id_type=MESH)
                cp.start()
                copies.append(cp)
        for cp in copies:
            cp.wait()

    shapes = [jax.ShapeDtypeStruct((N_CHIPS - 1,) + s.shape[1:], s.dtype) for s in sums]
    return _exchange(body, sums, shapes, [(n, 3), (n, 3)], name, _same_core_of_other_chips, collective_id)


def sibling_share(halves, name="sibling_share", collective_id=None):
    n = len(halves)

    def body(ins, outs, send_sems, recv_sems):
        x, y, c = _place()
        copies = [pltpu.make_async_remote_copy(src_ref=ins[a], dst_ref=outs[a], send_sem=send_sems.at[a], recv_sem=recv_sems.at[a],
                                               device_id=(x, y, 1 - c), device_id_type=MESH) for a in range(n)]
        for cp in copies:
            cp.start()
        for cp in copies:
            cp.wait()

    shapes = [jax.ShapeDtypeStruct(h.shape, h.dtype) for h in halves]
    return _exchange(body, halves, shapes, [(n,), (n,)], name, _sibling, collective_id)


def all_reduce_small(v):
    R, C = v.shape
    n_dev = 8

    def body(v_ref, o_ref, buf, send_sems, recv_sems):
        x, y, c = _place()
        me = 4 * x + 2 * y + c
        buf[me] = v_ref[...]
        copies = []
        for rel in range(1, n_dev):
            fx, fy, fc = rel >> 2, (rel >> 1) & 1, rel & 1
            peer = (x ^ fx, y ^ fy, c ^ fc)
            cp = pltpu.make_async_remote_copy(src_ref=v_ref, dst_ref=buf.at[me], send_sem=send_sems.at[rel - 1],
                                              recv_sem=recv_sems.at[rel - 1], device_id=peer, device_id_type=MESH)
            cp.start()
            copies.append(cp)
        for cp in copies:
            cp.wait()
        acc = buf[0]
        for k in range(1, n_dev):
            acc = acc + buf[k]
        o_ref[...] = acc

    return pl.pallas_call(
        body, name="all_reduce_small",
        in_specs=[pl.BlockSpec(memory_space=pltpu.VMEM)],
        out_specs=pl.BlockSpec(memory_space=pltpu.VMEM),
        out_shape=jax.ShapeDtypeStruct((R, C), F32),
        scratch_shapes=[pltpu.VMEM((n_dev, R, C), F32), pltpu.SemaphoreType.DMA((n_dev - 1,)), pltpu.SemaphoreType.DMA((n_dev - 1,))],
        compiler_params=pltpu.CompilerParams(has_side_effects=True),
    )(v)


WEIGHT_NAMES = ("ffn1_w_gate", "ffn1_w_up", "ffn1_w_down", "ffn2_w_gate", "ffn2_w_up", "ffn2_w_down", "ln_gain", "ln_bias",
                "pool_w_in", "pool_w_group", "pool_scale", "pool_w_out", "attn_w_qkv", "attn_w_out")
MATRIX_NAMES = ("ffn1_w_gate", "ffn1_w_up", "ffn1_w_down", "ffn2_w_gate", "ffn2_w_up", "ffn2_w_down",
                "pool_w_in", "pool_w_group", "pool_w_out", "attn_w_qkv", "attn_w_out")


TRANSPOSED_NAMES = ("ffn1_w_gate", "ffn1_w_up", "ffn2_w_gate", "ffn2_w_up")


def _halves(name, w):
    if name in TRANSPOSED_NAMES:
        w = jnp.swapaxes(w, 1, 2)
    return w.reshape(2, -1, w.shape[-1])


def _unhalves(name, t, shape):
    if name in TRANSPOSED_NAMES:
        return jnp.swapaxes(t.reshape(shape[0], shape[2], shape[1]), 1, 2)
    return t.reshape(shape)


def kernel(x, ffn1_w_gate, ffn1_w_up, ffn1_w_down, ffn2_w_gate, ffn2_w_up, ffn2_w_down, ln_gain, ln_bias, pool_w_in, pool_w_group, pool_scale, pool_w_out, attn_w_qkv, attn_w_out, loss_target, m_ffn1_w_gate, m_ffn1_w_up, m_ffn1_w_down, m_ffn2_w_gate, m_ffn2_w_up, m_ffn2_w_down, m_ln_gain, m_ln_bias, m_pool_w_in, m_pool_w_group, m_pool_scale, m_pool_w_out, m_attn_w_qkv, m_attn_w_out, v_ffn1_w_gate, v_ffn1_w_up, v_ffn1_w_down, v_ffn2_w_gate, v_ffn2_w_up, v_ffn2_w_down, v_ln_gain, v_ln_bias, v_pool_w_in, v_pool_w_group, v_pool_scale, v_pool_w_out, v_attn_w_qkv, v_attn_w_out):
    weights = dict(zip(WEIGHT_NAMES, (ffn1_w_gate, ffn1_w_up, ffn1_w_down, ffn2_w_gate, ffn2_w_up, ffn2_w_down, ln_gain, ln_bias,
                                      pool_w_in, pool_w_group, pool_scale, pool_w_out, attn_w_qkv, attn_w_out)))
    moms = dict(zip(WEIGHT_NAMES, (m_ffn1_w_gate, m_ffn1_w_up, m_ffn1_w_down, m_ffn2_w_gate, m_ffn2_w_up, m_ffn2_w_down, m_ln_gain,
                                   m_ln_bias, m_pool_w_in, m_pool_w_group, m_pool_scale, m_pool_w_out, m_attn_w_qkv, m_attn_w_out)))
    vels = dict(zip(WEIGHT_NAMES, (v_ffn1_w_gate, v_ffn1_w_up, v_ffn1_w_down, v_ffn2_w_gate, v_ffn2_w_up, v_ffn2_w_down, v_ln_gain,
                                   v_ln_bias, v_pool_w_in, v_pool_w_group, v_pool_scale, v_pool_w_out, v_attn_w_qkv, v_attn_w_out)))
    S, D = x.shape[1], x.shape[2]
    FB = ffn1_w_gate.shape[2]
    QKV = attn_w_qkv.shape[2] * N_CHIPS
    G, CB = pool_w_group.shape[1], pool_w_group.shape[2]
    C = pool_w_group.shape[3]
    cx, cy, cc = _place()
    chip = 2 * cx + cy
    xs = x.reshape(S, D)
    target = loss_target.reshape(S, D)

    ln_rows = jnp.concatenate([ln_gain, ln_bias, jnp.zeros((DEPTH, 2, ln_gain.shape[2]), F32)], axis=1)
    shard = {n: _halves(n, weights[n]).astype(BF16) for n in MATRIX_NAMES}
    ffn_layer = lambda f, i: {f"{f}{s}@{i}": shard[f + s][i].reshape(2, FB // 2, D) for s in ("_w_gate", "_w_up", "_w_down")}
    pool = {n: shard[n] for n in ("pool_w_in", "pool_w_group", "pool_w_out")}
    attn = {n: shard[n] for n in ("attn_w_qkv", "attn_w_out")}
    groups = [dict(ffn_layer("ffn1", 0), ln=ln_rows), pool, ffn_layer("ffn2", 0), ffn_layer("ffn1", 1), attn, ffn_layer("ffn2", 1)]
    full = {}

    def launch(k, after=None):
        parts = list(groups[k].values())
        if after is not None:
            *parts, after = lax.optimization_barrier((*parts, after))
        placed = [lax.dynamic_update_slice(lax.empty((N_CHIPS,) + s.shape, s.dtype), s[None], (chip, 0, 0, 0)) for s in parts]
        full.update(zip(groups[k], all_gather_shards_async(parts, placed, GATHER_COLLECTIVE_ID, f"gather_weights_{k}")))
        return after

    for k in range(4):
        launch(k)
    ffn_w = lambda f, i: [full[f"{f}{s}@{i}"].reshape(N_CHIPS, FB, D) for s in ("_w_gate", "_w_up", "_w_down")]
    w_pool_in = full["pool_w_in"].reshape(D, D)
    w_pool_out = full["pool_w_out"].reshape(D, D)
    w_group = full["pool_w_group"].reshape(N_CHIPS, G, CB, C).transpose(1, 0, 2, 3).reshape(G, N_CHIPS * CB, C)
    ln_full = full["ln"].transpose(1, 2, 0, 3).reshape(DEPTH, 8, D)
    gain = lambda i, k: ln_full[i, k].reshape(1, D)
    bias = lambda i, k: ln_full[i, 3 + k].reshape(1, D)

    dils = [dil for _, dil in DIL_CONFIGS]
    moved_dils = [dil for dil in dils if dil > 1]

    saved = []
    y = xs
    for i in range(DEPTH):
        y_in = y
        y, xh, rs, a, u = ffn_fwd(y_in, *ffn_w("ffn1", i), gain(i, 0), bias(i, 0))
        f1 = (y_in, xh, rs, a, u)
        if i == 0:
            y = launch(4, after=y)
        y_mid = y
        if i % 2 == 0:
            pu = mm_nn(y_mid, w_pool_in, F32)[0]
            mixed, pv = pool_mix(pu, w_group, pool_scale)
            y, xh, rs = proj_ln(pv, w_pool_out, y_mid, gain(i, 1), bias(i, 1))
            mix = (y_mid, xh, rs, mixed, pv)
            if i == 0:
                y = launch(5, after=y)
        else:
            qkv_blocks, y_mid = lax.optimization_barrier((full["attn_w_qkv"], y_mid))
            w_qkv = chip_blocks_to_columns(qkv_blocks.reshape(N_CHIPS, D, QKV // N_CHIPS))
            w_attn_out = full["attn_w_out"].reshape(D, D)
            moved = dict(zip(moved_dils, dilate_rows(y_mid, moved_dils, BF16)))
            srcs = [moved.get(dil, y_mid) for dil in dils]
            qkvs = [mm_nn(src, w_qkv, BF16, first_block=3 * g, nb=3) for g, src in enumerate(srcs)]
            parts = [attn_fwd(qkv, g, dil) for g, (qkv, dil) in enumerate(zip(qkvs, dils))]
            ao, lse = attn_combine([p[0] for p in parts], [p[1] for p in parts], dils)
            y, xh, rs = proj_ln(ao, w_attn_out, y_mid, gain(i, 1), bias(i, 1))
            mix = (y_mid, xh, rs, srcs, qkvs, ao, lse, w_qkv, w_attn_out)
        y_in2 = y
        y, xh, rs, a, u = ffn_fwd(y_in2, *ffn_w("ffn2", i), gain(i, 2), bias(i, 2))
        f2 = (y_in2, xh, rs, a, u)
        saved.append((f1, mix, f2))

    dy, loss_part = loss_head(y, target)
    loss = lax.psum(loss_part[0, 0], ("x", "y", "c"))

    core = cc.reshape(1).astype(jnp.int32)
    chip_id = chip.reshape(1).astype(jnp.int32)
    dgain = [[None] * 3 for _ in range(DEPTH)]
    dbias = [[None] * 3 for _ in range(DEPTH)]
    dscale = None
    pieces = []
    own_half, other_half = {}, {}

    def tie(arrays, after):
        *arrays, after = lax.optimization_barrier((*arrays, after))
        return arrays, after

    def by_shape(fn, xs, ys):
        out = [None] * len(xs)
        for shape in dict.fromkeys(x.shape for x in xs):
            idx = [k for k, x in enumerate(xs) if x.shape == shape]
            for k, res in zip(idx, fn([xs[k] for k in idx], [ys[k] for k in idx])):
                out[k] = res
        return out

    def start_piece(keys, arrays):
        blocks = [g.reshape(N_CHIPS, 2, -1, g.shape[-1]) for g in arrays]
        k = len(pieces)
        pieces.append(dict(keys=keys, blocks=blocks, from_sibling=sibling_exchange_halves(
            blocks, name=f"reduce_halves_{k}", collective_id=SIBLING_COLLECTIVE_ID)))

    def pair_sums_and_chip_exchange(k, after):
        piece = pieces[k]
        received, after = tie(piece["from_sibling"], after)
        piece["pair_sums"] = by_shape(lambda gs, rs: pair_sum(core, gs, rs), piece["blocks"], received)
        piece["from_chips"] = chip_exchange(piece["pair_sums"], name=f"reduce_chips_{k}", collective_id=CHIPS_COLLECTIVE_ID)
        return after

    def chip_sums_and_share(k, after):
        piece = pieces[k]
        received, after = tie(piece["from_chips"], after)
        mine = by_shape(lambda ps, rs: chip_sum(chip_id, ps, rs), piece["pair_sums"], received)
        theirs = sibling_share(mine, name=f"reduce_share_{k}", collective_id=SIBLING_COLLECTIVE_ID)
        own_half.update(zip(piece["keys"], mine))
        other_half.update(zip(piece["keys"], theirs))
        return after

    def piece_done(keys, arrays, dy):
        start_piece(keys, arrays)
        k = len(pieces) - 1
        if k >= 1:
            dy = pair_sums_and_chip_exchange(k - 1, dy)
        if k >= 2:
            dy = chip_sums_and_share(k - 2, dy)
        return dy

    def ffn_backward(name, i, dy, state):
        y_in, xh, rs, a, u = state
        k = 0 if name == "ffn1" else 2
        dz, dgain[i][k], dbias[i][k] = ln_bwd(dy, xh, rs, gain(i, k))
        dx, h, da, du = ffn_bwd(dz, a, u, *ffn_w(name, i))
        outs = mm_tn([(da, y_in, 1.0), (du, y_in, 1.0), (h, dz, MACARON_WEIGHT)], nblk=N_CHIPS, out_shape=(N_CHIPS, FB, D),
                     out_block=(None, FB, D), out_index=lambda j: (j, 0, 0), name="ffn_wgrad")
        return piece_done([(name + s, i) for s in ("_w_gate", "_w_up", "_w_down")], outs, dx)

    def square_grad(a, b):
        return mm_tn([(a, b, 1.0)], nblk=1, out_shape=(D, D), out_block=(D, D), out_index=lambda j: (0, 0), name="square_wgrad")[0]

    for i in reversed(range(DEPTH)):
        f1, mix, f2 = saved[i]
        dy = ffn_backward("ffn2", i, dy, f2)
        dz, dgain[i][1], dbias[i][1] = ln_bwd(dy, mix[1], mix[2], gain(i, 1))
        if i % 2 == 0:
            y_mid, _, _, mixed, pv = mix
            g_out = square_grad(pv, dz)
            dv = mm_nt(dz, w_pool_out, None, a_blocked=False)
            du, dwg, dscale = pool_mix_bwd(dv, mixed, w_group, pool_scale)
            g_group = dwg.reshape(G, N_CHIPS, CB, C).transpose(1, 0, 2, 3)
            g_in = square_grad(y_mid, du)
            dy = mm_nt(du, w_pool_in, dz, a_blocked=False)
            dy = piece_done([("pool_w_out", 0), ("pool_w_group", 0), ("pool_w_in", 0)], [g_out, g_group, g_in], dy)
        else:
            y_mid, _, _, srcs, qkvs, ao, lse, w_qkv, w_attn_out = mix
            g_out = square_grad(ao, dz)
            dao = mm_nt(dz, w_attn_out, None, a_blocked=False)
            in_order = [dict(zip(moved_dils, dilate_rows(t, moved_dils, F32))) for t in (dao, ao, lse)]
            dqkvs = [attn_bwd(qkvs[g], *[m.get(dil, t) for m, t in zip(in_order, (dao, ao, lse))], g, dil)
                     for g, dil in enumerate(dils)]
            g_qkv = [mm_tn([(src, dqkv, 1.0)], nblk=3, out_shape=(D, 3 * D), out_block=(D, D), out_index=lambda j: (0, j),
                           name="qkv_wgrad")[0] for src, dqkv in zip(srcs, dqkvs)]
            g_qkv = jnp.concatenate(g_qkv, axis=1).reshape(D, N_CHIPS, QKV // N_CHIPS).transpose(1, 0, 2)
            dy = mm_nt_dilated(dqkvs, dils, w_qkv, dz)
            dy = piece_done([("attn_w_out", 0), ("attn_w_qkv", 0)], [g_out, g_qkv], dy)
        dy = ffn_backward("ffn1", i, dy, f1)
    grad_x = dy.reshape(x.shape)

    last = len(pieces) - 1
    small = jnp.concatenate([jnp.concatenate(dgain[i] + dbias[i], axis=0) for i in range(DEPTH)] + [dscale, jnp.zeros((3, D), F32)], axis=0)
    small = all_reduce_small(small)
    per_layer = small[:6 * DEPTH].reshape(DEPTH, 6, D)
    cols = D // N_CHIPS
    small_grads = {"ln_gain": lax.dynamic_slice_in_dim(per_layer[:, 0:3], chip * cols, cols, axis=2),
                   "ln_bias": lax.dynamic_slice_in_dim(per_layer[:, 3:6], chip * cols, cols, axis=2),
                   "pool_scale": small[6 * DEPTH:6 * DEPTH + 1]}

    grad_w, delta, new_m, new_v = {}, {}, {}, {}

    def update(n, after):
        shape = weights[n].shape
        if n in MATRIX_NAMES:
            layers = DEPTH if (n, 1) in own_half else 1
            as4 = lambda t: _halves(n, t).reshape(layers, 2, -1, shape[-1] if n not in TRANSPOSED_NAMES else shape[1])
            mine, after = tie([own_half[n, l] for l in range(layers)], after)
            outs = adamw(core, as4(weights[n]), [(mine[l], other_half[n, l]) for l in range(layers)], as4(moms[n]), as4(vels[n]))
            grad_w[n], delta[n], new_m[n], new_v[n] = [_unhalves(n, t, shape) for t in outs]
        else:
            as4 = lambda t: t.reshape(1, 1, -1, shape[-1])
            g2 = small_grads[n].reshape(-1, shape[-1])
            outs = adamw(core, as4(weights[n]), [(g2, g2)], as4(moms[n]), as4(vels[n]))
            grad_w[n], delta[n], new_m[n], new_v[n] = [t.reshape(shape) for t in outs]
        return outs[1]

    marker = small
    for n in ("ln_gain", "ln_bias", "pool_scale"):
        marker = update(n, marker)
    marker = pair_sums_and_chip_exchange(last, marker)
    marker = chip_sums_and_share(last - 1, marker)
    for n in MATRIX_NAMES:
        if not n.startswith("ffn1"):
            marker = update(n, marker)
    marker = chip_sums_and_share(last, marker)
    for n in MATRIX_NAMES:
        if n.startswith("ffn1"):
            marker = update(n, marker)

    return (loss, grad_x, *[grad_w[n] for n in WEIGHT_NAMES], *[delta[n] for n in WEIGHT_NAMES],
            *[new_m[n] for n in WEIGHT_NAMES], *[new_v[n] for n in WEIGHT_NAMES])
```

```python
import functools
import math

import numpy as np
import jax
import jax.numpy as jnp
from jax import lax
from jax.experimental import pallas as pl
from jax.experimental.pallas import tpu as pltpu
from jax.experimental.pallas import tpu_sc as plsc

F32 = jnp.float32
BF16 = jnp.bfloat16

DEPTH = 2
ALPHA = (2.0 * DEPTH) ** 0.25
MACARON_WEIGHT = 0.5
LN_EPS = 1e-5
MASK_VALUE = -1e30
POOL_WINDOWS = (2, 4, 8, 16)
POOL_PAD = 16
HEAD_DIM = 64
N_HEADS = 16
DIL_CONFIGS = ((128, 1), (512, 4), (2048, 16))
ATTN_R = 64
ATTN_BQ = 128
ATTN_W = ATTN_BQ + 2 * ATTN_R
FFN_HIDDEN_TILE = 256
ATTN_UNROLL = 8
LANES = 128
ADAM_LR = 0.001
ADAM_B1 = 0.9
ADAM_B2 = 0.999
ADAM_EPS = 1e-08
ADAM_WD = 0.01
ADAM_STEP = 10
N_CHIPS = 4
GATHER_COLLECTIVE_ID = 1
SIBLING_COLLECTIVE_ID = 2
CHIPS_COLLECTIVE_ID = 3
VMEM_LIMIT = 56 * 1024 * 1024
MESH = pl.DeviceIdType.MESH
ANY = pl.BlockSpec(memory_space=pl.ANY)


def _params(sem=None, vmem=VMEM_LIMIT):
    return pltpu.CompilerParams(dimension_semantics=sem, vmem_limit_bytes=vmem)


def _alibi_slopes():
    n = len(DIL_CONFIGS) * N_HEADS
    s = 2.0 ** (-8.0 * np.arange(1, n + 1) / n)
    return s.reshape(len(DIL_CONFIGS), N_HEADS).astype(np.float32)


def _ln_fwd(z, g, b):
    mu = jnp.mean(z, axis=-1, keepdims=True)
    zc = z - mu
    var = jnp.mean(zc * zc, axis=-1, keepdims=True)
    rstd = lax.rsqrt(var + LN_EPS)
    xhat = zc * rstd
    return xhat * g + b, xhat, rstd


def _dot(a, b):
    return jnp.dot(a, b, preferred_element_type=F32)


def _dot_nt(a, b):
    return lax.dot_general(a, b, (((1,), (1,)), ((), ())), preferred_element_type=F32)


def _dot_tn(a, b):
    return lax.dot_general(a, b, (((0,), (0,)), ((), ())), preferred_element_type=F32)


def mm_nn(a, b, out_dtype, first_block=0, nb=None, tm=1024):
    S, K = a.shape
    Nb = K
    nb = b.shape[1] // Nb if nb is None else nb

    def body(a_ref, b_ref, o_ref):
        o_ref[...] = _dot(a_ref[...].astype(BF16), b_ref[...]).astype(out_dtype)

    return pl.pallas_call(
        body, name="mm_nn",
        grid=(S // tm, nb),
        in_specs=[pl.BlockSpec((tm, K), lambda i, j: (i, 0)), pl.BlockSpec((K, Nb), lambda i, j: (0, first_block + j))],
        out_specs=pl.BlockSpec((None, tm, Nb), lambda i, j: (j, i, 0)),
        out_shape=jax.ShapeDtypeStruct((nb, S, Nb), out_dtype),
        compiler_params=_params(("parallel", "arbitrary")),
    )(a, b)


def chip_blocks_to_columns(w):
    nb, K, Nb = w.shape

    def body(w_ref, o_ref):
        o_ref[...] = w_ref[...]

    return pl.pallas_call(
        body, name="chip_blocks_to_columns",
        grid=(nb,),
        in_specs=[pl.BlockSpec((None, K, Nb), lambda b: (b, 0, 0))],
        out_specs=pl.BlockSpec((K, Nb), lambda b: (0, b)),
        out_shape=jax.ShapeDtypeStruct((K, nb * Nb), w.dtype),
        compiler_params=_params(("parallel",)),
    )(w)


def proj_ln(a, w, resid, gain, bias, tm=1024):
    S, K = a.shape
    D = w.shape[1]

    def body(a_ref, w_ref, r_ref, g_ref, b_ref, y_ref, xh_ref, rs_ref):
        z = ALPHA * r_ref[...] + _dot(a_ref[...].astype(BF16), w_ref[...])
        y, xh, rs = _ln_fwd(z, g_ref[...], b_ref[...])
        y_ref[...] = y
        xh_ref[...] = xh
        rs_ref[...] = rs

    row = pl.BlockSpec((tm, D), lambda i: (i, 0))
    vec = pl.BlockSpec((1, D), lambda i: (0, 0))
    return pl.pallas_call(
        body, name="proj_ln",
        grid=(S // tm,),
        in_specs=[pl.BlockSpec((tm, K), lambda i: (i, 0)), pl.BlockSpec((K, D), lambda i: (0, 0)), row, vec, vec],
        out_specs=[row, row, pl.BlockSpec((tm, 1), lambda i: (i, 0))],
        out_shape=[jax.ShapeDtypeStruct((S, D), F32), jax.ShapeDtypeStruct((S, D), F32), jax.ShapeDtypeStruct((S, 1), F32)],
        compiler_params=_params(("parallel",)),
    )(a, w, resid, gain, bias)


def mm_nt(a, w, resid, a_blocked, out_dtype=F32, tm=1024):
    if a_blocked:
        nk, S, Kb = a.shape
        a_spec = pl.BlockSpec((None, tm, Kb), lambda i, n: (n, i, 0))
    else:
        S, Kb = a.shape
        nk = 1
        a_spec = pl.BlockSpec((tm, Kb), lambda i, n: (i, 0))
    M = w.shape[0]
    has_resid = resid is not None

    def body(*refs):
        if has_resid:
            a_ref, w_ref, r_ref, o_ref, acc = refs
        else:
            a_ref, w_ref, o_ref, acc = refs
        n = pl.program_id(1)
        part = _dot_nt(a_ref[...].astype(BF16), w_ref[...])

        @pl.when(n == 0)
        def _():
            acc[...] = part

        @pl.when(n > 0)
        def _():
            acc[...] += part

        @pl.when(n == nk - 1)
        def _():
            out = acc[...]
            if has_resid:
                out = out + ALPHA * r_ref[...]
            o_ref[...] = out.astype(out_dtype)

    row = pl.BlockSpec((tm, M), lambda i, n: (i, 0))
    in_specs = [a_spec, pl.BlockSpec((M, Kb), lambda i, n: (0, n))] + ([row] if has_resid else [])
    args = (a, w) + ((resid,) if has_resid else ())
    return pl.pallas_call(
        body, name="mm_nt",
        grid=(S // tm, nk),
        in_specs=in_specs,
        out_specs=row,
        out_shape=jax.ShapeDtypeStruct((S, M), out_dtype),
        scratch_shapes=[pltpu.VMEM((tm, M), F32)],
        compiler_params=_params(("parallel", "arbitrary")),
    )(*args)


def mm_nt_dilated(parts, dils, w, resid, tm=1024):
    n_groups = len(parts)
    _, S, K = parts[0].shape
    M = w.shape[0]
    nk = 3 * n_groups

    def body(*refs):
        a_refs = refs[:n_groups]
        w_ref, r_ref, o_ref, group_acc, total = refs[n_groups:]
        n = pl.program_id(1)
        for g in range(n_groups):
            for k in range(3):
                @pl.when(n == 3 * g + k)
                def _():
                    part = _dot_nt(a_refs[g][...], w_ref[...])
                    for c in range(M // LANES):
                        lanes = slice(c * LANES, (c + 1) * LANES)
                        if k == 0:
                            group_acc[c] = part[:, lanes]
                        else:
                            group_acc[c] += part[:, lanes]
                        if k == 2:
                            _rows_from_dilated(group_acc.at[c], total.at[c], tm, dils[g], accumulate=g > 0)

        @pl.when(n == nk - 1)
        def _():
            for c in range(M // LANES):
                lanes = slice(c * LANES, (c + 1) * LANES)
                o_ref[:, lanes] = total[c] + ALPHA * r_ref[:, lanes]

    def a_spec(g):
        return pl.BlockSpec((None, tm, K), lambda i, n: (jnp.clip(n - 3 * g, 0, 2), i, 0))

    row = pl.BlockSpec((tm, M), lambda i, n: (i, 0))
    return pl.pallas_call(
        body, name="mm_nt_dilated",
        grid=(S // tm, nk),
        in_specs=[a_spec(g) for g in range(n_groups)] + [pl.BlockSpec((M, K), lambda i, n: (0, n)), row],
        out_specs=row,
        out_shape=jax.ShapeDtypeStruct((S, M), F32),
        scratch_shapes=[pltpu.VMEM((M // LANES, tm, LANES), F32), pltpu.VMEM((M // LANES, tm, LANES), F32)],
        compiler_params=_params(("parallel", "arbitrary")),
    )(*parts, w, resid)


def mm_tn(pairs, *, nblk, out_shape, out_block, out_index, alias=None, tk=1024, name="mm_tn"):
    operands = []
    for a, b, _ in pairs:
        for t in (a, b):
            if not any(t is o for o in operands):
                operands.append(t)
    where = lambda t: next(i for i, o in enumerate(operands) if o is t)
    S = pairs[0][0].shape[-2]
    n_out, n_in = len(pairs), len(operands)
    n_alias = len(alias) if alias is not None else 0

    def spec(t):
        if t.ndim == 3:
            return pl.BlockSpec((None, tk, t.shape[-1]), lambda j, k: (j, k, 0))
        return pl.BlockSpec((tk, t.shape[-1]), lambda j, k: (k, 0))

    def body(*refs):
        refs = refs[n_alias:]
        in_refs, o_refs, accs = refs[:n_in], refs[n_in:n_in + n_out], refs[n_in + n_out:]
        k = pl.program_id(1)
        for (a, b, scale), o_ref, acc in zip(pairs, o_refs, accs):
            part = _dot_tn(in_refs[where(a)][...].astype(BF16), in_refs[where(b)][...].astype(BF16))

            @pl.when(k == 0)
            def _():
                acc[...] = part

            @pl.when(k > 0)
            def _():
                acc[...] += part

            @pl.when(k == S // tk - 1)
            def _():
                o_ref[...] = (scale * acc[...]).astype(BF16)

    out_spec = pl.BlockSpec(out_block, lambda j, k: out_index(j))
    outs = pl.pallas_call(
        body, name=name,
        grid=(nblk, S // tk),
        in_specs=[ANY] * n_alias + [spec(t) for t in operands],
        out_specs=[out_spec] * n_out,
        out_shape=[jax.ShapeDtypeStruct(out_shape, BF16)] * n_out,
        scratch_shapes=[pltpu.VMEM((a.shape[-1], b.shape[-1]), F32) for a, b, _ in pairs],
        input_output_aliases={i: i for i in range(n_alias)},
        compiler_params=_params(("parallel", "arbitrary")),
    )(*(tuple(alias) if alias is not None else ()), *operands)
    return list(outs)


def ffn_fwd(x, wg, wu, wd, gain, bias, tm=1024):
    S, D = x.shape
    nb, FB = wg.shape[0], wg.shape[1]

    def body(x_ref, wg_ref, wu_ref, wd_ref, g_ref, b_ref, y_ref, xh_ref, rs_ref, a_ref, u_ref, acc, xb_ref):
        j = pl.program_id(1)

        @pl.when(j == 0)
        def _():
            xb_ref[...] = x_ref[...].astype(BF16)
            acc[...] = jnp.zeros_like(acc)

        xb = xb_ref[...]
        total = None
        for t0 in range(0, FB, FFN_HIDDEN_TILE):
            cols = pl.ds(t0, min(FFN_HIDDEN_TILE, FB - t0))
            a = _dot_nt(xb, wg_ref[cols, :])
            u = _dot_nt(xb, wu_ref[cols, :])
            a_ref[:, cols] = a.astype(BF16)
            u_ref[:, cols] = u.astype(BF16)
            h = a * jax.nn.sigmoid(a) * u
            part = _dot(h.astype(BF16), wd_ref[cols, :])
            total = part if total is None else total + part
        acc[...] += total

        @pl.when(j == nb - 1)
        def _():
            z = ALPHA * x_ref[...] + MACARON_WEIGHT * acc[...]
            y, xh, rs = _ln_fwd(z, g_ref[...], b_ref[...])
            y_ref[...] = y
            xh_ref[...] = xh
            rs_ref[...] = rs

    row = pl.BlockSpec((tm, D), lambda i, j: (i, 0))
    vec = pl.BlockSpec((1, D), lambda i, j: (0, 0))
    w_out = pl.BlockSpec((None, FB, D), lambda i, j: (j, 0, 0))
    act = pl.BlockSpec((None, tm, FB), lambda i, j: (j, i, 0))
    return pl.pallas_call(
        body, name="ffn_fwd",
        grid=(S // tm, nb),
        in_specs=[row, w_out, w_out, w_out, vec, vec],
        out_specs=[row, row, pl.BlockSpec((tm, 1), lambda i, j: (i, 0)), act, act],
        out_shape=[jax.ShapeDtypeStruct((S, D), F32), jax.ShapeDtypeStruct((S, D), F32), jax.ShapeDtypeStruct((S, 1), F32),
                   jax.ShapeDtypeStruct((nb, S, FB), BF16), jax.ShapeDtypeStruct((nb, S, FB), BF16)],
        scratch_shapes=[pltpu.VMEM((tm, D), F32), pltpu.VMEM((tm, D), BF16)],
        compiler_params=_params(("parallel", "arbitrary")),
    )(x, wg, wu, wd, gain, bias)


def ffn_bwd(dz, a, u, wg, wu, wd, tm=1024):
    S, D = dz.shape
    nb, FB = wg.shape[0], wg.shape[1]

    def body(dz_ref, a_ref, u_ref, wg_ref, wu_ref, wd_ref, dx_ref, h_ref, da_ref, du_ref, acc, dzb_ref):
        j = pl.program_id(1)

        @pl.when(j == 0)
        def _():
            dzb_ref[...] = (MACARON_WEIGHT * dz_ref[...]).astype(BF16)
            acc[...] = jnp.zeros_like(acc)

        dzb = dzb_ref[...]
        total = None
        for t0 in range(0, FB, FFN_HIDDEN_TILE):
            cols = pl.ds(t0, min(FFN_HIDDEN_TILE, FB - t0))
            dh = _dot_nt(dzb, wd_ref[cols, :])
            av = a_ref[:, cols].astype(F32)
            uv = u_ref[:, cols].astype(F32)
            s = jax.nn.sigmoid(av)
            silu = av * s
            h_ref[:, cols] = (silu * uv).astype(BF16)
            da = (dh * uv * (s * (1.0 + av * (1.0 - s)))).astype(BF16)
            du = (dh * silu).astype(BF16)
            da_ref[:, cols] = da
            du_ref[:, cols] = du
            both = jnp.concatenate([da, du], axis=1)
            weights = jnp.concatenate([wg_ref[cols, :], wu_ref[cols, :]], axis=0)
            part = _dot(both, weights)
            total = part if total is None else total + part
        acc[...] += total

        @pl.when(j == nb - 1)
        def _():
            dx_ref[...] = ALPHA * dz_ref[...] + acc[...]

    row = pl.BlockSpec((tm, D), lambda i, j: (i, 0))
    w_out = pl.BlockSpec((None, FB, D), lambda i, j: (j, 0, 0))
    act = pl.BlockSpec((None, tm, FB), lambda i, j: (j, i, 0))
    act_shape = jax.ShapeDtypeStruct((nb, S, FB), BF16)
    return pl.pallas_call(
        body, name="ffn_bwd",
        grid=(S // tm, nb),
        in_specs=[row, act, act, w_out, w_out, w_out],
        out_specs=[row, act, act, act],
        out_shape=[jax.ShapeDtypeStruct((S, D), F32), act_shape, act_shape, act_shape],
        scratch_shapes=[pltpu.VMEM((tm, D), F32), pltpu.VMEM((tm, D), BF16)],
        compiler_params=_params(("parallel", "arbitrary")),
    )(dz, a, u, wg, wu, wd)


def ln_bwd(dy, xhat, rstd, gain, tm=1024):
    S, D = dy.shape

    def body(dy_ref, xh_ref, rs_ref, g_ref, dz_ref, dg_ref, db_ref):
        i = pl.program_id(0)
        dy = dy_ref[...]
        xh = xh_ref[...]
        dxh = dy * g_ref[...]
        m1 = jnp.mean(dxh, axis=-1, keepdims=True)
        m2 = jnp.mean(dxh * xh, axis=-1, keepdims=True)
        dz_ref[...] = rs_ref[...] * (dxh - m1 - xh * m2)
        dg = jnp.sum(dy * xh, axis=0, keepdims=True)
        db = jnp.sum(dy, axis=0, keepdims=True)

        @pl.when(i == 0)
        def _():
            dg_ref[...] = dg
            db_ref[...] = db

        @pl.when(i > 0)
        def _():
            dg_ref[...] += dg
            db_ref[...] += db

    row = pl.BlockSpec((tm, D), lambda i: (i, 0))
    vec = pl.BlockSpec((1, D), lambda i: (0, 0))
    return pl.pallas_call(
        body, name="ln_bwd",
        grid=(S // tm,),
        in_specs=[row, row, pl.BlockSpec((tm, 1), lambda i: (i, 0)), vec],
        out_specs=[row, vec, vec],
        out_shape=[jax.ShapeDtypeStruct((S, D), F32), jax.ShapeDtypeStruct((1, D), F32), jax.ShapeDtypeStruct((1, D), F32)],
        compiler_params=_params(("arbitrary",)),
    )(dy, xhat, rstd, gain)


def loss_head(y, target, tm=1024):
    S, D = y.shape

    def body(y_ref, t_ref, dy_ref, l_ref):
        i = pl.program_id(0)
        e = y_ref[...] - t_ref[...]
        dy_ref[...] = e / D
        part = 0.5 * jnp.sum(jnp.mean(e * e, axis=-1, keepdims=True), axis=0, keepdims=True)

        @pl.when(i == 0)
        def _():
            l_ref[...] = part

        @pl.when(i > 0)
        def _():
            l_ref[...] += part

    row = pl.BlockSpec((tm, D), lambda i: (i, 0))
    return pl.pallas_call(
        body, name="loss_head",
        grid=(S // tm,),
        in_specs=[row, row],
        out_specs=[row, pl.BlockSpec((1, 1), lambda i: (0, 0))],
        out_shape=[jax.ShapeDtypeStruct((S, D), F32), jax.ShapeDtypeStruct((1, 1), F32)],
        compiler_params=_params(("arbitrary",)),
    )(y, target)


def _pool_window(xp, g):
    n = xp.shape[0]
    w = xp + pltpu.roll(xp, 1, 0)
    out = w
    for level, shift in enumerate((1, 2, 4), start=1):
        w = pltpu.roll(w, shift, 0) + pltpu.roll(w, n - shift, 0)
        out = jnp.where(g >= level, w, out)
    return out


def _pool_count(S, C, g):
    half = lax.shift_left(jnp.int32(1), g)
    t = lax.broadcasted_iota(jnp.int32, (S, C), 0)
    return (jnp.minimum(t + half, S) - jnp.maximum(t - half, 0)).astype(F32)


def pool_mix(u, wgrp, scale):
    S, D = u.shape
    G, C = wgrp.shape[0], wgrp.shape[1]

    def body(u_ref, w_ref, s_ref, mix_ref, v_ref, pad):
        g = pl.program_id(0)
        zeros = jnp.zeros((POOL_PAD, C), F32)
        pad[pl.ds(0, POOL_PAD), :] = zeros
        pad[pl.ds(POOL_PAD + S, POOL_PAD), :] = zeros
        pad[pl.ds(POOL_PAD, S), :] = u_ref[...]
        win = _pool_window(pad[...], g)[POOL_PAD:POOL_PAD + S]
        mixed = (win / _pool_count(S, C, g) - u_ref[...]).astype(BF16)
        mix_ref[...] = mixed
        v_ref[...] = _dot(mixed, w_ref[...]) * s_ref[...]

    col = pl.BlockSpec((S, C), lambda g: (0, g))
    return pl.pallas_call(
        body, name="pool_mix",
        grid=(G,),
        in_specs=[col, pl.BlockSpec((None, C, C), lambda g: (g, 0, 0)), pl.BlockSpec((1, C), lambda g: (0, g))],
        out_specs=[col, col],
        out_shape=[jax.ShapeDtypeStruct((S, D), BF16), jax.ShapeDtypeStruct((S, D), F32)],
        scratch_shapes=[pltpu.VMEM((S + 2 * POOL_PAD, C), F32)],
        compiler_params=_params(("arbitrary",)),
    )(u, wgrp, scale)


def pool_mix_bwd(dv, mixed, wgrp, scale):
    S, D = dv.shape
    G, C = wgrp.shape[0], wgrp.shape[1]

    def body(dv_ref, mix_ref, w_ref, s_ref, du_ref, dw_ref, ds_ref, pad):
        g = pl.program_id(0)
        mixed = mix_ref[...]
        dv = dv_ref[...]
        yg = _dot(mixed, w_ref[...])
        ds_ref[...] = jnp.sum(dv * yg, axis=0, keepdims=True)
        dyg = (dv * s_ref[...]).astype(BF16)
        dw_ref[...] = _dot_tn(mixed, dyg).astype(BF16)
        dmix = _dot_nt(dyg, w_ref[...])
        zeros = jnp.zeros((POOL_PAD, C), F32)
        pad[pl.ds(0, POOL_PAD), :] = zeros
        pad[pl.ds(POOL_PAD + S, POOL_PAD), :] = zeros
        pad[pl.ds(POOL_PAD, S), :] = dmix / _pool_count(S, C, g)
        win = _pool_window(pad[...], g)
        win = pltpu.roll(win, win.shape[0] - 1, 0)[POOL_PAD:POOL_PAD + S]
        du_ref[...] = win - dmix

    col = pl.BlockSpec((S, C), lambda g: (0, g))
    return pl.pallas_call(
        body, name="pool_mix_bwd",
        grid=(G,),
        in_specs=[col, col, pl.BlockSpec((None, C, C), lambda g: (g, 0, 0)), pl.BlockSpec((1, C), lambda g: (0, g))],
        out_specs=[col, pl.BlockSpec((None, C, C), lambda g: (g, 0, 0)), pl.BlockSpec((1, C), lambda g: (0, g))],
        out_shape=[jax.ShapeDtypeStruct((S, D), F32), jax.ShapeDtypeStruct((G, C, C), BF16), jax.ShapeDtypeStruct((1, D), F32)],
        scratch_shapes=[pltpu.VMEM((S + 2 * POOL_PAD, C), F32)],
        compiler_params=_params(("arbitrary",)),
    )(dv, mixed, wgrp, scale)


PERM_BLOCK = 256


def _dilated_runs(rows, d):
    n = PERM_BLOCK // d
    return [(c * PERM_BLOCK, r, n) for c in range(rows // PERM_BLOCK) for r in range(d)]


def _rows_to_dilated(src_ref, dst_ref, rows, d):
    for base, r, n in _dilated_runs(rows, d):
        dst_ref[pl.ds(base + r * n, n), :] = src_ref[pl.ds(base + r, n, stride=d), :].astype(dst_ref.dtype)


def _rows_from_dilated(src_ref, dst_ref, rows, d, accumulate=False):
    for base, r, n in _dilated_runs(rows, d):
        at = pl.ds(base + r, n, stride=d)
        v = src_ref[pl.ds(base + r * n, n), :]
        dst_ref[at, :] = dst_ref[at, :] + v if accumulate else v


def dilate_rows(x, dils, out_dtype, tm=2048):
    S, D = x.shape

    def body(x_ref, *o_refs):
        for d, o_ref in zip(dils, o_refs):
            _rows_to_dilated(x_ref, o_ref, tm, d)

    tile = pl.BlockSpec((tm, LANES), lambda i, j: (i, j))
    return pl.pallas_call(
        body, name="dilate_rows",
        grid=(S // tm, D // LANES),
        in_specs=[tile],
        out_specs=[tile] * len(dils),
        out_shape=[jax.ShapeDtypeStruct((S, D), out_dtype)] * len(dils),
        compiler_params=_params(("parallel", "parallel")),
    )(x)


def _slope_table(group, dilation):
    s = _alibi_slopes()[group].reshape(N_HEADS // 2, 2, 1, 1) * float(dilation)
    return jnp.asarray(np.broadcast_to(s, (N_HEADS // 2, 2, 1, ATTN_W)).copy())


def _residue_shape(S, D, d):
    return (S, D) if d == 1 else (S // PERM_BLOCK, d, PERM_BLOCK // d, D)


def _residue_view(x, d):
    return x.reshape(x.shape[:-2] + _residue_shape(x.shape[-2], x.shape[-1], d))


def _residue_spec(lead_block, lead_index, S, d):
    if d == 1:
        return pl.BlockSpec(lead_block + (S, LANES), lambda hp, r: lead_index + (0, hp))
    return pl.BlockSpec(lead_block + (S // PERM_BLOCK, None, PERM_BLOCK // d, LANES), lambda hp, r: lead_index + (0, r, 0, hp))


def _whole(ref, lead, L):
    return ref[lead + (slice(None),) * (len(ref.shape) - len(lead))].reshape(L, LANES)


def _query_rows(lead, i, d):
    if d == 1:
        return lead + (pl.ds(pl.multiple_of(i * ATTN_BQ, ATTN_BQ), ATTN_BQ), slice(None)), (ATTN_BQ, LANES)
    n = PERM_BLOCK // d
    return lead + (pl.ds(i * (ATTN_BQ // n), ATTN_BQ // n), slice(None), slice(None)), (ATTN_BQ // n, n, LANES)


def _load_query_rows(ref, lead, i, d):
    at, _ = _query_rows(lead, i, d)
    return ref[at].reshape(ATTN_BQ, LANES)


def _store_query_rows(ref, lead, i, d, value):
    at, shape = _query_rows(lead, i, d)
    ref[at] = value.reshape(shape)


def _stage_keys(dst, src_ref, L):
    rows = _whole(src_ref, (), L)
    lane = lax.broadcasted_iota(jnp.int32, (L, LANES), 1)
    zeros = jnp.zeros((ATTN_R, LANES), dst.dtype)
    for h in range(2):
        mine = (lane < HEAD_DIM) if h == 0 else (lane >= HEAD_DIM)
        dst[h, pl.ds(0, ATTN_R), :] = zeros
        dst[h, pl.ds(ATTN_R + L, ATTN_R), :] = zeros
        dst[h, pl.ds(ATTN_R, L), :] = jnp.where(mine, rows, jnp.zeros_like(rows))


def _fill_bias(bias, sl_ref):
    a = lax.broadcasted_iota(jnp.int32, (ATTN_BQ, ATTN_W), 0)
    c = lax.broadcasted_iota(jnp.int32, (ATTN_BQ, ATTN_W), 1)
    rel = jnp.abs(c - ATTN_R - a)
    band = rel <= ATTN_R
    after_start = c >= ATTN_R
    before_end = c < ATTN_BQ + ATTN_R
    for h in range(2):
        base = -(sl_ref[h] * rel.astype(F32))
        for variant in range(4):
            ok = band
            if variant & 1:
                ok = ok & after_start
            if variant & 2:
                ok = ok & before_end
            bias[variant, h] = jnp.where(ok, base, MASK_VALUE)


def _bias_variant(i, nq):
    return jnp.where(i == 0, 1, 0) + jnp.where(i == nq - 1, 2, 0)


def attn_fwd(qkv, group, dilation):
    _, S, D = qkv.shape
    d = dilation
    L = S // d
    nq = L // ATTN_BQ
    ncol = D // LANES
    view = _residue_view(qkv, d)
    slopes = _slope_table(group, d)
    scale = HEAD_DIM ** -0.5

    def body(q_ref, k_ref, v_ref, sl_ref, o_ref, lse_ref, k2, v2, bias):
        @pl.when(pl.program_id(1) == 0)
        def _():
            _fill_bias(bias, sl_ref)

        _stage_keys(k2, k_ref, L)
        _stage_keys(v2, v_ref, L)
        head0 = lax.broadcasted_iota(jnp.int32, (ATTN_BQ, LANES), 1) < HEAD_DIM

        def block(i):
            variant = _bias_variant(i, nq)
            win = pl.ds(pl.multiple_of(i * ATTN_BQ, ATTN_BQ), ATTN_W)
            qs = _load_query_rows(q_ref, (), i, d) * jnp.asarray(scale, BF16)
            es, ms, ls = [], [], []
            for h in range(2):
                s = _dot_nt(qs, k2[h, win, :]) + bias[variant, h]
                m = jnp.max(s, axis=-1, keepdims=True)
                e = jnp.exp(s - m)
                ls.append(jnp.sum(e, axis=-1, keepdims=True))
                ms.append(m)
                es.append(e.astype(BF16))
            acc = _dot(jnp.concatenate(es, axis=1), jnp.concatenate([v2[0, win, :], v2[1, win, :]], axis=0))
            out = acc * jnp.where(head0, 1.0 / ls[0], 1.0 / ls[1])
            lse = jnp.where(head0, ms[0] + jnp.log(ls[0]), ms[1] + jnp.log(ls[1]))
            return out, lse

        def step(t, carry):
            results = [block(t * group + b) for b in range(group)]
            for b, (out, lse) in enumerate(results):
                _store_query_rows(o_ref, (), t * group + b, d, out)
                _store_query_rows(lse_ref, (), t * group + b, d, lse)
            return carry

        group = min(ATTN_UNROLL, nq)
        lax.fori_loop(0, nq // group, step, 0)

    def col(which):
        return _residue_spec((None,), (which,), S, d)

    out = _residue_spec((), (), S, d)
    o, lse = pl.pallas_call(
        body, name=f"attn_fwd_g{group}",
        grid=(ncol, d),
        in_specs=[col(0), col(1), col(2), pl.BlockSpec((None, 2, 1, ATTN_W), lambda hp, r: (hp, 0, 0, 0))],
        out_specs=[out, out],
        out_shape=[jax.ShapeDtypeStruct(_residue_shape(S, D, d), F32)] * 2,
        scratch_shapes=[pltpu.VMEM((2, L + 2 * ATTN_R, LANES), BF16), pltpu.VMEM((2, L + 2 * ATTN_R, LANES), BF16),
                        pltpu.VMEM((4, 2, ATTN_BQ, ATTN_W), F32)],
        compiler_params=_params(("arbitrary", "arbitrary")),
    )(view, view, view, slopes)
    return o.reshape(S, D), lse.reshape(S, D)


def attn_combine(os, lses, dils, tm=2048):
    S, D = os[0].shape
    n = len(os)
    n_moved = sum(d > 1 for d in dils)

    def body(*refs):
        o_refs, l_refs, out_ref, lse_ref = list(refs[:n]), list(refs[n:2 * n]), refs[2 * n], refs[2 * n + 1]
        spare = list(refs[2 * n + 2:])
        for g, d in enumerate(dils):
            if d > 1:
                for which in (o_refs, l_refs):
                    token_order = spare.pop()
                    _rows_from_dilated(which[g], token_order, tm, d)
                    which[g] = token_order
        ls = [r[...] for r in l_refs]
        m = functools.reduce(jnp.maximum, ls)
        es = [jnp.exp(l - m) for l in ls]
        tot = functools.reduce(lambda x, y: x + y, es)
        inv = 1.0 / tot
        out_ref[...] = functools.reduce(lambda x, y: x + y, [(e * inv) * r[...] for e, r in zip(es, o_refs)])
        lse_ref[...] = m + jnp.log(tot)

    tile = pl.BlockSpec((tm, LANES), lambda i, j: (i, j))
    return pl.pallas_call(
        body, name="attn_combine",
        grid=(S // tm, D // LANES),
        in_specs=[tile] * (2 * n),
        out_specs=[tile, tile],
        out_shape=[jax.ShapeDtypeStruct((S, D), F32), jax.ShapeDtypeStruct((S, D), F32)],
        scratch_shapes=[pltpu.VMEM((tm, LANES), F32)] * (2 * n_moved),
        compiler_params=_params(("parallel", "parallel")),
    )(*os, *lses)


def attn_bwd(qkv, do, o, lse, group, dilation):
    _, S, D = qkv.shape
    d = dilation
    L = S // d
    nq = L // ATTN_BQ
    ncol = D // LANES
    view = _residue_view(qkv, d)
    slopes = _slope_table(group, d)
    scale = HEAD_DIM ** -0.5

    def body(q_ref, k_ref, v_ref, do_ref, o_ref, lse_ref, sl_ref, dx_ref, k2, v2, dkacc, dvacc, bias):
        @pl.when(pl.program_id(1) == 0)
        def _():
            _fill_bias(bias, sl_ref)

        _stage_keys(k2, k_ref, L)
        _stage_keys(v2, v_ref, L)
        dkacc[...] = jnp.zeros_like(dkacc)
        dvacc[...] = jnp.zeros_like(dvacc)
        lane = lax.broadcasted_iota(jnp.int32, (ATTN_BQ, LANES), 1)
        heads = (lane < HEAD_DIM, lane >= HEAD_DIM)
        key_head0 = lax.broadcasted_iota(jnp.int32, (ATTN_W, LANES), 1) < HEAD_DIM

        def step(i, carry):
            variant = _bias_variant(i, nq)
            win = pl.ds(pl.multiple_of(i * ATTN_BQ, ATTN_BQ), ATTN_W)
            q = _load_query_rows(q_ref, (), i, d)
            qs = q * jnp.asarray(scale, BF16)
            dov = _load_query_rows(do_ref, (), i, d)
            prod = dov * _load_query_rows(o_ref, (), i, d)
            lse_v = _load_query_rows(lse_ref, (), i, d)
            dob = dov.astype(BF16)
            dss, dks, dvs = [], [], []
            for h in range(2):
                s = _dot_nt(qs, k2[h, win, :]) + bias[variant, h]
                lse_h = jnp.max(jnp.where(heads[h], lse_v, -jnp.inf), axis=-1, keepdims=True)
                dterm = jnp.sum(jnp.where(heads[h], prod, 0.0), axis=-1, keepdims=True)
                p = jnp.exp(s - lse_h)
                dp = _dot_nt(dob, v2[h, win, :])
                ds = (p * (dp - dterm) * scale).astype(BF16)
                dvs.append(_dot_tn(p.astype(BF16), dob))
                dks.append(_dot_tn(ds, q))
                dss.append(ds)
            dq = _dot(jnp.concatenate(dss, axis=1), jnp.concatenate([k2[0, win, :], k2[1, win, :]], axis=0))
            dvacc[win, :] += jnp.where(key_head0, dvs[0], dvs[1])
            dkacc[win, :] += jnp.where(key_head0, dks[0], dks[1])
            _store_query_rows(dx_ref, (0,), i, d, dq.astype(BF16))
            return carry

        lax.fori_loop(0, nq, step, 0, unroll=min(ATTN_UNROLL, nq))
        block_shape = dx_ref.shape[1:]
        dx_ref[1] = dkacc[pl.ds(ATTN_R, L), :].astype(BF16).reshape(block_shape)
        dx_ref[2] = dvacc[pl.ds(ATTN_R, L), :].astype(BF16).reshape(block_shape)

    def col(which):
        return _residue_spec((None,), (which,), S, d)

    act = _residue_spec((), (), S, d)
    out = pl.pallas_call(
        body, name=f"attn_bwd_g{group}",
        grid=(ncol, d),
        in_specs=[col(0), col(1), col(2), act, act, act, pl.BlockSpec((None, 2, 1, ATTN_W), lambda hp, r: (hp, 0, 0, 0))],
        out_specs=_residue_spec((3,), (0,), S, d),
        out_shape=jax.ShapeDtypeStruct((3,) + _residue_shape(S, D, d), BF16),
        scratch_shapes=[pltpu.VMEM((2, L + 2 * ATTN_R, LANES), BF16), pltpu.VMEM((2, L + 2 * ATTN_R, LANES), BF16),
                        pltpu.VMEM((L + 2 * ATTN_R, LANES), F32), pltpu.VMEM((L + 2 * ATTN_R, LANES), F32),
                        pltpu.VMEM((4, 2, ATTN_BQ, ATTN_W), F32)],
        compiler_params=_params(("arbitrary", "arbitrary")),
    )(view, view, view, _residue_view(do, d), _residue_view(o, d), _residue_view(lse, d), slopes)
    return out.reshape(3, S, D)


TILE_ELEMS = 512 * 1024


def _row_tile(R, C):
    if R * C <= TILE_ELEMS or R % 16:
        return R
    return max(t for t in range(16, R + 1, 16) if R % t == 0 and (t * C <= TILE_ELEMS or t == 16))


def pair_sum(core, gs, recvs):
    n = len(gs)
    _, _, R, C = gs[0].shape
    tr = _row_tile(R, C)

    def body(c_ref, *refs):
        for g_ref, r_ref, o_ref in zip(refs[:n], refs[n:2 * n], refs[2 * n:]):
            o_ref[...] = (g_ref[...].astype(F32) + r_ref[...].astype(F32)).astype(BF16)

    blk = pl.BlockSpec((None, tr, C), lambda d, i, c_ref: (d, i, 0))
    mine = pl.BlockSpec((None, None, tr, C), lambda d, i, c_ref: (d, c_ref[0], i, 0))
    return pl.pallas_call(
        body, name="pair_sum",
        grid_spec=pltpu.PrefetchScalarGridSpec(
            num_scalar_prefetch=1, grid=(N_CHIPS, R // tr),
            in_specs=[mine] * n + [blk] * n,
            out_specs=[blk] * n),
        out_shape=[jax.ShapeDtypeStruct((N_CHIPS, R, C), BF16)] * n,
        compiler_params=_params(("parallel", "parallel")),
    )(core, *gs, *recvs)


def chip_sum(chip, owns, recvs):
    n = len(owns)
    _, R, C = owns[0].shape
    tr = _row_tile(R, C)
    slot_of_relation = {2: 0, 1: 1, 3: 2}

    def body(chip_ref, *refs):
        me = chip_ref[0]
        for own_ref, r_ref, o_ref in zip(refs[:n], refs[n:2 * n], refs[2 * n:]):
            mine = own_ref[...].astype(F32)
            theirs = {rel: r_ref[k].astype(F32) for rel, k in slot_of_relation.items()}
            acc = None
            for s in range(N_CHIPS):
                rel = jnp.bitwise_xor(me, s)
                part = jnp.where(rel == 0, mine, jnp.where(rel == 2, theirs[2], jnp.where(rel == 1, theirs[1], theirs[3])))
                acc = part if acc is None else acc + part
            o_ref[...] = acc

    return pl.pallas_call(
        body, name="chip_sum",
        grid_spec=pltpu.PrefetchScalarGridSpec(
            num_scalar_prefetch=1, grid=(R // tr,),
            in_specs=[pl.BlockSpec((None, tr, C), lambda i, chip_ref: (chip_ref[0], i, 0))] * n
            + [pl.BlockSpec((N_CHIPS - 1, tr, C), lambda i, chip_ref: (0, i, 0))] * n,
            out_specs=[pl.BlockSpec((tr, C), lambda i, chip_ref: (i, 0))] * n),
        out_shape=[jax.ShapeDtypeStruct((R, C), F32)] * n,
        compiler_params=_params(("parallel",)),
    )(chip, *owns, *recvs)


def adamw(core, w, g_pairs, m, v):
    L, H, R, C = w.shape
    tr = _row_tile(R, C)

    def body(c_ref, w_ref, *rest):
        g_refs = rest[:2 * L]
        m_ref, v_ref, g_ref, d_ref, nm_ref, nv_ref = rest[2 * L:]
        mine = pl.program_id(1) == c_ref[0]
        g = None
        for l in range(L):
            g_l = jnp.where(mine, g_refs[2 * l][...], g_refs[2 * l + 1][...])
            g = g_l if g is None else jnp.where(pl.program_id(0) == l, g_l, g)
        m = ADAM_B1 * m_ref[...] + (1.0 - ADAM_B1) * g
        v = ADAM_B2 * v_ref[...] + (1.0 - ADAM_B2) * (g * g)
        m_hat = m / (1.0 - ADAM_B1 ** ADAM_STEP)
        v_hat = v / (1.0 - ADAM_B2 ** ADAM_STEP)
        g_ref[...] = g
        d_ref[...] = -ADAM_LR * (m_hat / (jnp.sqrt(v_hat) + ADAM_EPS) + ADAM_WD * w_ref[...])
        nm_ref[...] = m
        nv_ref[...] = v

    blk = pl.BlockSpec((None, None, tr, C), lambda l, h, i, c_ref: (l, h, i, 0))

    def half(layer):
        return pl.BlockSpec((tr, C), lambda l, h, i, c_ref: (jnp.where(l == layer, i, 0), 0))

    shape = jax.ShapeDtypeStruct((L, H, R, C), F32)
    return pl.pallas_call(
        body, name="adamw",
        grid_spec=pltpu.PrefetchScalarGridSpec(
            num_scalar_prefetch=1, grid=(L, H, R // tr),
            in_specs=[blk] + [half(l) for l in range(L) for _ in range(2)] + [blk, blk],
            out_specs=[blk] * 4),
        out_shape=[shape] * 4,
        compiler_params=_params(("parallel", "parallel", "parallel")),
    )(core, w, *[g for pair in g_pairs for g in pair], m, v)


def _place():
    return lax.axis_index("x"), lax.axis_index("y"), lax.axis_index("c")


def _other_chips(x, y):
    return [(2 * (1 - x) + y, (1 - x, y)), (2 * x + (1 - y), (x, 1 - y)), (2 * (1 - x) + (1 - y), (1 - x, 1 - y))]


def all_gather_shards(shards, placed):
    n = len(shards)

    def body(*refs):
        ins, outs = refs[:n], refs[2 * n:3 * n]
        send_sems, recv_sems = refs[3 * n:]
        x, y, c = _place()
        me = 2 * x + y
        sibling = (x, y, 1 - c)
        chips = _other_chips(x, y)

        def copy(a, k, src, dst, to):
            return pltpu.make_async_remote_copy(src_ref=src, dst_ref=dst, send_sem=send_sems.at[a, k], recv_sem=recv_sems.at[a, k],
                                                device_id=to, device_id_type=MESH)

        sends = []
        for a in range(n):
            for k, (_, (px, py)) in enumerate(chips):
                cp = copy(a, k, ins[a].at[c], outs[a].at[me, c], (px, py, c))
                cp.start()
                sends.append(cp)
        for a in range(n):
            for k, (chip, _) in enumerate(chips):
                landed = outs[a].at[chip, c]
                copy(a, k, landed, landed, sibling).wait_recv()
                cp = copy(a, 3 + k, landed, landed, sibling)
                cp.start()
                sends.append(cp)
        for a in range(n):
            for k, (chip, _) in enumerate(chips):
                other = outs[a].at[chip, 1 - c]
                copy(a, 3 + k, other, other, sibling).wait_recv()
        for cp in sends:
            cp.wait_send()

    return pl.pallas_call(
        body, name="all_gather_shards",
        in_specs=[ANY] * (2 * n),
        out_specs=[ANY] * n,
        out_shape=[jax.ShapeDtypeStruct(p.shape, p.dtype) for p in placed],
        scratch_shapes=[pltpu.SemaphoreType.DMA((n, 6)), pltpu.SemaphoreType.DMA((n, 6))],
        input_output_aliases={n + a: a for a in range(n)},
        compiler_params=pltpu.CompilerParams(has_side_effects=True),
    )(*shards, *placed)


def all_gather_shards_async(shards, placed, collective_id, name):
    n = len(shards)
    srcs = [jax.new_ref(s, memory_space=pltpu.MemorySpace.HBM) for s in shards]
    dsts = [jax.new_ref(p, memory_space=pltpu.MemorySpace.HBM) for p in placed]

    @pl.kernel(mesh=plsc.ScalarSubcoreMesh(axis_name="sequencer", num_cores=1), name=name,
               scratch_types=(pltpu.SemaphoreType.DMA((n, 6)), pltpu.SemaphoreType.DMA((n, 6))),
               compiler_params=pltpu.CompilerParams(collective_id=collective_id))
    def launch(send_sems, recv_sems):
        x, y, c = _place()
        me = 2 * x + y
        sibling = (x, y, 1 - c)
        chips = _other_chips(x, y)
        barrier = pltpu.get_barrier_semaphore()
        peers = [sibling] + [(px, py, c) for _, (px, py) in chips]
        for peer in peers:
            pl.semaphore_signal(barrier, inc=1, device_id=peer, device_id_type=MESH)
        pl.semaphore_wait(barrier, len(peers))

        def copy(a, k, src, dst, to):
            return pltpu.make_async_remote_copy(src_ref=src, dst_ref=dst, send_sem=send_sems.at[a, k], recv_sem=recv_sems.at[a, k],
                                                device_id=to, device_id_type=MESH)

        sends = []
        for a in range(n):
            for k, (_, (px, py)) in enumerate(chips):
                cp = copy(a, k, srcs[a].at[c], dsts[a].at[me, c], (px, py, c))
                cp.start()
                sends.append(cp)
        for a in range(n):
            for k, (chip, _) in enumerate(chips):
                landed = dsts[a].at[chip, c]
                copy(a, k, landed, landed, sibling).wait_recv()
                cp = copy(a, 3 + k, landed, landed, sibling)
                cp.start()
                sends.append(cp)
        for a in range(n):
            for k, (chip, _) in enumerate(chips):
                other = dsts[a].at[chip, 1 - c]
                copy(a, 3 + k, other, other, sibling).wait_recv()
        for cp in sends:
            cp.wait_send()

    launch()
    return [d[...] for d in dsts]


def _exchange(body, ins, out_shapes, sem_shapes, name, peers=None, collective_id=None):
    n_in, n_out = len(ins), len(out_shapes)
    sems = [pltpu.SemaphoreType.DMA(shape) for shape in sem_shapes]
    if collective_id is None:
        def tc_body(*refs):
            body(refs[:n_in], refs[n_in:n_in + n_out], *refs[n_in + n_out:])

        return pl.pallas_call(tc_body, name=name, in_specs=[ANY] * n_in, out_specs=[ANY] * n_out, out_shape=out_shapes,
                              scratch_shapes=sems, compiler_params=pltpu.CompilerParams(has_side_effects=True))(*ins)
    srcs = [jax.new_ref(a, memory_space=pltpu.MemorySpace.HBM) for a in ins]
    dsts = [jax.empty_ref(shape, memory_space=pltpu.MemorySpace.HBM) for shape in out_shapes]

    @pl.kernel(mesh=plsc.ScalarSubcoreMesh(axis_name="sequencer", num_cores=1), name=name, scratch_types=tuple(sems),
               compiler_params=pltpu.CompilerParams(collective_id=collective_id))
    def launch(*sem_refs):
        barrier = pltpu.get_barrier_semaphore()
        others = peers(*_place())
        for peer in others:
            pl.semaphore_signal(barrier, inc=1, device_id=peer, device_id_type=MESH)
        pl.semaphore_wait(barrier, len(others))
        body(srcs, dsts, *sem_refs)

    launch()
    return [d[...] for d in dsts]


def _sibling(x, y, c):
    return [(x, y, 1 - c)]


def _same_core_of_other_chips(x, y, c):
    return [(px, py, c) for _, (px, py) in _other_chips(x, y)]


def sibling_exchange_halves(grads, name="sibling_exchange_halves", collective_id=None):
    n = len(grads)

    def body(ins, outs, send_sems, recv_sems):
        x, y, c = _place()
        copies = [pltpu.make_async_remote_copy(src_ref=ins[a].at[:, 1 - c], dst_ref=outs[a], send_sem=send_sems.at[a],
                                               recv_sem=recv_sems.at[a], device_id=(x, y, 1 - c), device_id_type=MESH) for a in range(n)]
        for cp in copies:
            cp.start()
        for cp in copies:
            cp.wait()

    shapes = [jax.ShapeDtypeStruct((N_CHIPS,) + g.shape[2:], g.dtype) for g in grads]
    return _exchange(body, grads, shapes, [(n,), (n,)], name, _sibling, collective_id)


def chip_exchange(sums, name="chip_exchange", collective_id=None):
    n = len(sums)

    def body(ins, outs, send_sems, recv_sems):
        x, y, c = _place()
        copies = []
        for a in range(n):
            for k, (chip, (px, py)) in enumerate(_other_chips(x, y)):
                cp = pltpu.make_async_remote_copy(src_ref=ins[a].at[chip], dst_ref=outs[a].at[k], send_sem=send_sems.at[a, k],
                                                  recv_sem=recv_sems.at[a, k], device_id=(px, py, c), device_id_type=MESH)
                cp.start()
                copies.append(cp)
        for cp in copies:
            cp.wait()

    shapes = [jax.ShapeDtypeStruct((N_CHIPS - 1,) + s.shape[1:], s.dtype) for s in sums]
    return _exchange(body, sums, shapes, [(n, 3), (n, 3)], name, _same_core_of_other_chips, collective_id)


def sibling_share(halves, name="sibling_share", collective_id=None):
    n = len(halves)

    def body(ins, outs, send_sems, recv_sems):
        x, y, c = _place()
        copies = [pltpu.make_async_remote_copy(src_ref=ins[a], dst_ref=outs[a], send_sem=send_sems.at[a], recv_sem=recv_sems.at[a],
                                               device_id=(x, y, 1 - c), device_id_type=MESH) for a in range(n)]
        for cp in copies:
            cp.start()
        for cp in copies:
            cp.wait()

    shapes = [jax.ShapeDtypeStruct(h.shape, h.dtype) for h in halves]
    return _exchange(body, halves, shapes, [(n,), (n,)], name, _sibling, collective_id)


def all_reduce_small(v):
    R, C = v.shape
    n_dev = 8

    def body(v_ref, o_ref, buf, send_sems, recv_sems):
        x, y, c = _place()
        me = 4 * x + 2 * y + c
        buf[me] = v_ref[...]
        copies = []
        for rel in range(1, n_dev):
            fx, fy, fc = rel >> 2, (rel >> 1) & 1, rel & 1
            peer = (x ^ fx, y ^ fy, c ^ fc)
            cp = pltpu.make_async_remote_copy(src_ref=v_ref, dst_ref=buf.at[me], send_sem=send_sems.at[rel - 1],
                                              recv_sem=recv_sems.at[rel - 1], device_id=peer, device_id_type=MESH)
            cp.start()
            copies.append(cp)
        for cp in copies:
            cp.wait()
        acc = buf[0]
        for k in range(1, n_dev):
            acc = acc + buf[k]
        o_ref[...] = acc

    return pl.pallas_call(
        body, name="all_reduce_small",
        in_specs=[pl.BlockSpec(memory_space=pltpu.VMEM)],
        out_specs=pl.BlockSpec(memory_space=pltpu.VMEM),
        out_shape=jax.ShapeDtypeStruct((R, C), F32),
        scratch_shapes=[pltpu.VMEM((n_dev, R, C), F32), pltpu.SemaphoreType.DMA((n_dev - 1,)), pltpu.SemaphoreType.DMA((n_dev - 1,))],
        compiler_params=pltpu.CompilerParams(has_side_effects=True),
    )(v)


WEIGHT_NAMES = ("ffn1_w_gate", "ffn1_w_up", "ffn1_w_down", "ffn2_w_gate", "ffn2_w_up", "ffn2_w_down", "ln_gain", "ln_bias",
                "pool_w_in", "pool_w_group", "pool_scale", "pool_w_out", "attn_w_qkv", "attn_w_out")
MATRIX_NAMES = ("ffn1_w_gate", "ffn1_w_up", "ffn1_w_down", "ffn2_w_gate", "ffn2_w_up", "ffn2_w_down",
                "pool_w_in", "pool_w_group", "pool_w_out", "attn_w_qkv", "attn_w_out")


TRANSPOSED_NAMES = ("ffn1_w_gate", "ffn1_w_up", "ffn2_w_gate", "ffn2_w_up")


def _halves(name, w):
    if name in TRANSPOSED_NAMES:
        w = jnp.swapaxes(w, 1, 2)
    return w.reshape(2, -1, w.shape[-1])


def _unhalves(name, t, shape):
    if name in TRANSPOSED_NAMES:
        return jnp.swapaxes(t.reshape(shape[0], shape[2], shape[1]), 1, 2)
    return t.reshape(shape)


def kernel(x, ffn1_w_gate, ffn1_w_up, ffn1_w_down, ffn2_w_gate, ffn2_w_up, ffn2_w_down, ln_gain, ln_bias, pool_w_in, pool_w_group, pool_scale, pool_w_out, attn_w_qkv, attn_w_out, loss_target, m_ffn1_w_gate, m_ffn1_w_up, m_ffn1_w_down, m_ffn2_w_gate, m_ffn2_w_up, m_ffn2_w_down, m_ln_gain, m_ln_bias, m_pool_w_in, m_pool_w_group, m_pool_scale, m_pool_w_out, m_attn_w_qkv, m_attn_w_out, v_ffn1_w_gate, v_ffn1_w_up, v_ffn1_w_down, v_ffn2_w_gate, v_ffn2_w_up, v_ffn2_w_down, v_ln_gain, v_ln_bias, v_pool_w_in, v_pool_w_group, v_pool_scale, v_pool_w_out, v_attn_w_qkv, v_attn_w_out):
    weights = dict(zip(WEIGHT_NAMES, (ffn1_w_gate, ffn1_w_up, ffn1_w_down, ffn2_w_gate, ffn2_w_up, ffn2_w_down, ln_gain, ln_bias,
                                      pool_w_in, pool_w_group, pool_scale, pool_w_out, attn_w_qkv, attn_w_out)))
    moms = dict(zip(WEIGHT_NAMES, (m_ffn1_w_gate, m_ffn1_w_up, m_ffn1_w_down, m_ffn2_w_gate, m_ffn2_w_up, m_ffn2_w_down, m_ln_gain,
                                   m_ln_bias, m_pool_w_in, m_pool_w_group, m_pool_scale, m_pool_w_out, m_attn_w_qkv, m_attn_w_out)))
    vels = dict(zip(WEIGHT_NAMES, (v_ffn1_w_gate, v_ffn1_w_up, v_ffn1_w_down, v_ffn2_w_gate, v_ffn2_w_up, v_ffn2_w_down, v_ln_gain,
                                   v_ln_bias, v_pool_w_in, v_pool_w_group, v_pool_scale, v_pool_w_out, v_attn_w_qkv, v_attn_w_out)))
    S, D = x.shape[1], x.shape[2]
    FB = ffn1_w_gate.shape[2]
    QKV = attn_w_qkv.shape[2] * N_CHIPS
    G, CB = pool_w_group.shape[1], pool_w_group.shape[2]
    C = pool_w_group.shape[3]
    cx, cy, cc = _place()
    chip = 2 * cx + cy
    xs = x.reshape(S, D)
    target = loss_target.reshape(S, D)

    ln_rows = jnp.concatenate([ln_gain, ln_bias, jnp.zeros((DEPTH, 2, ln_gain.shape[2]), F32)], axis=1)
    shard = {n: _halves(n, weights[n]).astype(BF16) for n in MATRIX_NAMES}
    ffn_layer = lambda f, i: {f"{f}{s}@{i}": shard[f + s][i].reshape(2, FB // 2, D) for s in ("_w_gate", "_w_up", "_w_down")}
    pool = {n: shard[n] for n in ("pool_w_in", "pool_w_group", "pool_w_out")}
    attn = {n: shard[n] for n in ("attn_w_qkv", "attn_w_out")}
    groups = [dict(ffn_layer("ffn1", 0), ln=ln_rows), pool, ffn_layer("ffn2", 0), ffn_layer("ffn1", 1), attn, ffn_layer("ffn2", 1)]
    full = {}

    def launch(k, after=None):
        parts = list(groups[k].values())
        if after is not None:
            *parts, after = lax.optimization_barrier((*parts, after))
        placed = [lax.dynamic_update_slice(lax.empty((N_CHIPS,) + s.shape, s.dtype), s[None], (chip, 0, 0, 0)) for s in parts]
        full.update(zip(groups[k], all_gather_shards_async(parts, placed, GATHER_COLLECTIVE_ID, f"gather_weights_{k}")))
        return after

    for k in range(4):
        launch(k)
    ffn_w = lambda f, i: [full[f"{f}{s}@{i}"].reshape(N_CHIPS, FB, D) for s in ("_w_gate", "_w_up", "_w_down")]
    w_pool_in = full["pool_w_in"].reshape(D, D)
    w_pool_out = full["pool_w_out"].reshape(D, D)
    w_group = full["pool_w_group"].reshape(N_CHIPS, G, CB, C).transpose(1, 0, 2, 3).reshape(G, N_CHIPS * CB, C)
    ln_full = full["ln"].transpose(1, 2, 0, 3).reshape(DEPTH, 8, D)
    gain = lambda i, k: ln_full[i, k].reshape(1, D)
    bias = lambda i, k: ln_full[i, 3 + k].reshape(1, D)

    dils = [dil for _, dil in DIL_CONFIGS]
    moved_dils = [dil for dil in dils if dil > 1]

    saved = []
    y = xs
    for i in range(DEPTH):
        y_in = y
        y, xh, rs, a, u = ffn_fwd(y_in, *ffn_w("ffn1", i), gain(i, 0), bias(i, 0))
        f1 = (y_in, xh, rs, a, u)
        if i == 0:
            y = launch(4, after=y)
        y_mid = y
        if i % 2 == 0:
            pu = mm_nn(y_mid, w_pool_in, F32)[0]
            mixed, pv = pool_mix(pu, w_group, pool_scale)
            y, xh, rs = proj_ln(pv, w_pool_out, y_mid, gain(i, 1), bias(i, 1))
            mix = (y_mid, xh, rs, mixed, pv)
            if i == 0:
                y = launch(5, after=y)
        else:
            qkv_blocks, y_mid = lax.optimization_barrier((full["attn_w_qkv"], y_mid))
            w_qkv = chip_blocks_to_columns(qkv_blocks.reshape(N_CHIPS, D, QKV // N_CHIPS))
            w_attn_out = full["attn_w_out"].reshape(D, D)
            moved = dict(zip(moved_dils, dilate_rows(y_mid, moved_dils, BF16)))
            srcs = [moved.get(dil, y_mid) for dil in dils]
            qkvs = [mm_nn(src, w_qkv, BF16, first_block=3 * g, nb=3) for g, src in enumerate(srcs)]
            parts = [attn_fwd(qkv, g, dil) for g, (qkv, dil) in enumerate(zip(qkvs, dils))]
            ao, lse = attn_combine([p[0] for p in parts], [p[1] for p in parts], dils)
            y, xh, rs = proj_ln(ao, w_attn_out, y_mid, gain(i, 1), bias(i, 1))
            mix = (y_mid, xh, rs, srcs, qkvs, ao, lse, w_qkv, w_attn_out)
        y_in2 = y
        y, xh, rs, a, u = ffn_fwd(y_in2, *ffn_w("ffn2", i), gain(i, 2), bias(i, 2))
        f2 = (y_in2, xh, rs, a, u)
        saved.append((f1, mix, f2))

    dy, loss_part = loss_head(y, target)
    loss = lax.psum(loss_part[0, 0], ("x", "y", "c"))

    core = cc.reshape(1).astype(jnp.int32)
    chip_id = chip.reshape(1).astype(jnp.int32)
    dgain = [[None] * 3 for _ in range(DEPTH)]
    dbias = [[None] * 3 for _ in range(DEPTH)]
    dscale = None
    pieces = []
    own_half, other_half = {}, {}

    def tie(arrays, after):
        *arrays, after = lax.optimization_barrier((*arrays, after))
        return arrays, after

    def by_shape(fn, xs, ys):
        out = [None] * len(xs)
        for shape in dict.fromkeys(x.shape for x in xs):
            idx = [k for k, x in enumerate(xs) if x.shape == shape]
            for k, res in zip(idx, fn([xs[k] for k in idx], [ys[k] for k in idx])):
                out[k] = res
        return out

    def start_piece(keys, arrays):
        blocks = [g.reshape(N_CHIPS, 2, -1, g.shape[-1]) for g in arrays]
        k = len(pieces)
        pieces.append(dict(keys=keys, blocks=blocks, from_sibling=sibling_exchange_halves(
            blocks, name=f"reduce_halves_{k}", collective_id=SIBLING_COLLECTIVE_ID)))

    def pair_sums_and_chip_exchange(k, after):
        piece = pieces[k]
        received, after = tie(piece["from_sibling"], after)
        piece["pair_sums"] = by_shape(lambda gs, rs: pair_sum(core, gs, rs), piece["blocks"], received)
        piece["from_chips"] = chip_exchange(piece["pair_sums"], name=f"reduce_chips_{k}", collective_id=CHIPS_COLLECTIVE_ID)
        return after

    def chip_sums_and_share(k, after):
        piece = pieces[k]
        received, after = tie(piece["from_chips"], after)
        mine = by_shape(lambda ps, rs: chip_sum(chip_id, ps, rs), piece["pair_sums"], received)
        theirs = sibling_share(mine, name=f"reduce_share_{k}", collective_id=SIBLING_COLLECTIVE_ID)
        own_half.update(zip(piece["keys"], mine))
        other_half.update(zip(piece["keys"], theirs))
        return after

    def piece_done(keys, arrays, dy):
        start_piece(keys, arrays)
        k = len(pieces) - 1
        if k >= 1:
            dy = pair_sums_and_chip_exchange(k - 1, dy)
        if k >= 2:
            dy = chip_sums_and_share(k - 2, dy)
        return dy

    def ffn_backward(name, i, dy, state):
        y_in, xh, rs, a, u = state
        k = 0 if name == "ffn1" else 2
        dz, dgain[i][k], dbias[i][k] = ln_bwd(dy, xh, rs, gain(i, k))
        dx, h, da, du = ffn_bwd(dz, a, u, *ffn_w(name, i))
        outs = mm_tn([(da, y_in, 1.0), (du, y_in, 1.0), (h, dz, MACARON_WEIGHT)], nblk=N_CHIPS, out_shape=(N_CHIPS, FB, D),
                     out_block=(None, FB, D), out_index=lambda j: (j, 0, 0), name="ffn_wgrad")
        return piece_done([(name + s, i) for s in ("_w_gate", "_w_up", "_w_down")], outs, dx)

    def square_grad(a, b):
        return mm_tn([(a, b, 1.0)], nblk=1, out_shape=(D, D), out_block=(D, D), out_index=lambda j: (0, 0), name="square_wgrad")[0]

    for i in reversed(range(DEPTH)):
        f1, mix, f2 = saved[i]
        dy = ffn_backward("ffn2", i, dy, f2)
        dz, dgain[i][1], dbias[i][1] = ln_bwd(dy, mix[1], mix[2], gain(i, 1))
        if i % 2 == 0:
            y_mid, _, _, mixed, pv = mix
            g_out = square_grad(pv, dz)
            dv = mm_nt(dz, w_pool_out, None, a_blocked=False)
            du, dwg, dscale = pool_mix_bwd(dv, mixed, w_group, pool_scale)
            g_group = dwg.reshape(G, N_CHIPS, CB, C).transpose(1, 0, 2, 3)
            g_in = square_grad(y_mid, du)
            dy = mm_nt(du, w_pool_in, dz, a_blocked=False)
            dy = piece_done([("pool_w_out", 0), ("pool_w_group", 0), ("pool_w_in", 0)], [g_out, g_group, g_in], dy)
        else:
            y_mid, _, _, srcs, qkvs, ao, lse, w_qkv, w_attn_out = mix
            g_out = square_grad(ao, dz)
            dao = mm_nt(dz, w_attn_out, None, a_blocked=False)
            in_order = [dict(zip(moved_dils, dilate_rows(t, moved_dils, F32))) for t in (dao, ao, lse)]
            dqkvs = [attn_bwd(qkvs[g], *[m.get(dil, t) for m, t in zip(in_order, (dao, ao, lse))], g, dil)
                     for g, dil in enumerate(dils)]
            g_qkv = [mm_tn([(src, dqkv, 1.0)], nblk=3, out_shape=(D, 3 * D), out_block=(D, D), out_index=lambda j: (0, j),
                           name="qkv_wgrad")[0] for src, dqkv in zip(srcs, dqkvs)]
            g_qkv = jnp.concatenate(g_qkv, axis=1).reshape(D, N_CHIPS, QKV // N_CHIPS).transpose(1, 0, 2)
            dy = mm_nt_dilated(dqkvs, dils, w_qkv, dz)
            dy = piece_done([("attn_w_out", 0), ("attn_w_qkv", 0)], [g_out, g_qkv], dy)
        dy = ffn_backward("ffn1", i, dy, f1)
    grad_x = dy.reshape(x.shape)

    last = len(pieces) - 1
    small = jnp.concatenate([jnp.concatenate(dgain[i] + dbias[i], axis=0) for i in range(DEPTH)] + [dscale, jnp.zeros((3, D), F32)], axis=0)
    small = all_reduce_small(small)
    per_layer = small[:6 * DEPTH].reshape(DEPTH, 6, D)
    cols = D // N_CHIPS
    small_grads = {"ln_gain": lax.dynamic_slice_in_dim(per_layer[:, 0:3], chip * cols, cols, axis=2),
                   "ln_bias": lax.dynamic_slice_in_dim(per_layer[:, 3:6], chip * cols, cols, axis=2),
                   "pool_scale": small[6 * DEPTH:6 * DEPTH + 1]}

    grad_w, delta, new_m, new_v = {}, {}, {}, {}

    def update(n, after):
        shape = weights[n].shape
        if n in MATRIX_NAMES:
            layers = DEPTH if (n, 1) in own_half else 1
            as4 = lambda t: _halves(n, t).reshape(layers, 2, -1, shape[-1] if n not in TRANSPOSED_NAMES else shape[1])
            mine, after = tie([own_half[n, l] for l in range(layers)], after)
            outs = adamw(core, as4(weights[n]), [(mine[l], other_half[n, l]) for l in range(layers)], as4(moms[n]), as4(vels[n]))
            grad_w[n], delta[n], new_m[n], new_v[n] = [_unhalves(n, t, shape) for t in outs]
        else:
            as4 = lambda t: t.reshape(1, 1, -1, shape[-1])
            g2 = small_grads[n].reshape(-1, shape[-1])
            outs = adamw(core, as4(weights[n]), [(g2, g2)], as4(moms[n]), as4(vels[n]))
            grad_w[n], delta[n], new_m[n], new_v[n] = [t.reshape(shape) for t in outs]
        return outs[1]

    marker = small
    for n in ("ln_gain", "ln_bias", "pool_scale"):
        marker = update(n, marker)
    marker = pair_sums_and_chip_exchange(last, marker)
    marker = chip_sums_and_share(last - 1, marker)
    for n in MATRIX_NAMES:
        if not n.startswith("ffn1"):
            marker = update(n, marker)
    marker = chip_sums_and_share(last, marker)
    for n in MATRIX_NAMES:
        if n.startswith("ffn1"):
            marker = update(n, marker)

    return (loss, grad_x, *[grad_w[n] for n in WEIGHT_NAMES], *[delta[n] for n in WEIGHT_NAMES],
            *[new_m[n] for n in WEIGHT_NAMES], *[new_v[n] for n in WEIGHT_NAMES])
```

```python
import functools
import math

import numpy as np
import jax
import jax.numpy as jnp
from jax import lax
from jax.experimental import pallas as pl
from jax.experimental.pallas import tpu as pltpu
from jax.experimental.pallas import tpu_sc as plsc

F32 = jnp.float32
BF16 = jnp.bfloat16

DEPTH = 2
ALPHA = (2.0 * DEPTH) ** 0.25
MACARON_WEIGHT = 0.5
LN_EPS = 1e-5
MASK_VALUE = -1e30
POOL_WINDOWS = (2, 4, 8, 16)
POOL_PAD = 16
HEAD_DIM = 64
N_HEADS = 16
DIL_CONFIGS = ((128, 1), (512, 4), (2048, 16))
ATTN_R = 64
ATTN_BQ = 128
ATTN_W = ATTN_BQ + 2 * ATTN_R
FFN_HIDDEN_TILE = 256
ATTN_UNROLL = 8
LANES = 128
ADAM_LR = 0.001
ADAM_B1 = 0.9
ADAM_B2 = 0.999
ADAM_EPS = 1e-08
ADAM_WD = 0.01
ADAM_STEP = 10
N_CHIPS = 4
GATHER_COLLECTIVE_ID = 1
SIBLING_COLLECTIVE_ID = 2
CHIPS_COLLECTIVE_ID = 3
VMEM_LIMIT = 56 * 1024 * 1024
MESH = pl.DeviceIdType.MESH
ANY = pl.BlockSpec(memory_space=pl.ANY)


def _params(sem=None, vmem=VMEM_LIMIT):
    return pltpu.CompilerParams(dimension_semantics=sem, vmem_limit_bytes=vmem)


def _alibi_slopes():
    n = len(DIL_CONFIGS) * N_HEADS
    s = 2.0 ** (-8.0 * np.arange(1, n + 1) / n)
    return s.reshape(len(DIL_CONFIGS), N_HEADS).astype(np.float32)


def _ln_fwd(z, g, b):
    mu = jnp.mean(z, axis=-1, keepdims=True)
    zc = z - mu
    var = jnp.mean(zc * zc, axis=-1, keepdims=True)
    rstd = lax.rsqrt(var + LN_EPS)
    xhat = zc * rstd
    return xhat * g + b, xhat, rstd


def _dot(a, b):
    return jnp.dot(a, b, preferred_element_type=F32)


def _dot_nt(a, b):
    return lax.dot_general(a, b, (((1,), (1,)), ((), ())), preferred_element_type=F32)


def _dot_tn(a, b):
    return lax.dot_general(a, b, (((0,), (0,)), ((), ())), preferred_element_type=F32)


def mm_nn(a, b, out_dtype, first_block=0, nb=None, tm=1024):
    S, K = a.shape
    Nb = K
    nb = b.shape[1] // Nb if nb is None else nb

    def body(a_ref, b_ref, o_ref):
        o_ref[...] = _dot(a_ref[...].astype(BF16), b_ref[...]).astype(out_dtype)

    return pl.pallas_call(
        body, name="mm_nn",
        grid=(S // tm, nb),
        in_specs=[pl.BlockSpec((tm, K), lambda i, j: (i, 0)), pl.BlockSpec((K, Nb), lambda i, j: (0, first_block + j))],
        out_specs=pl.BlockSpec((None, tm, Nb), lambda i, j: (j, i, 0)),
        out_shape=jax.ShapeDtypeStruct((nb, S, Nb), out_dtype),
        compiler_params=_params(("parallel", "arbitrary")),
    )(a, b)


def chip_blocks_to_columns(w):
    nb, K, Nb = w.shape

    def body(w_ref, o_ref):
        o_ref[...] = w_ref[...]

    return pl.pallas_call(
        body, name="chip_blocks_to_columns",
        grid=(nb,),
        in_specs=[pl.BlockSpec((None, K, Nb), lambda b: (b, 0, 0))],
        out_specs=pl.BlockSpec((K, Nb), lambda b: (0, b)),
        out_shape=jax.ShapeDtypeStruct((K, nb * Nb), w.dtype),
        compiler_params=_params(("parallel",)),
    )(w)


def proj_ln(a, w, resid, gain, bias, tm=1024):
    S, K = a.shape
    D = w.shape[1]

    def body(a_ref, w_ref, r_ref, g_ref, b_ref, y_ref, xh_ref, rs_ref):
        z = ALPHA * r_ref[...] + _dot(a_ref[...].astype(BF16), w_ref[...])
        y, xh, rs = _ln_fwd(z, g_ref[...], b_ref[...])
        y_ref[...] = y
        xh_ref[...] = xh
        rs_ref[...] = rs

    row = pl.BlockSpec((tm, D), lambda i: (i, 0))
    vec = pl.BlockSpec((1, D), lambda i: (0, 0))
    return pl.pallas_call(
        body, name="proj_ln",
        grid=(S // tm,),
        in_specs=[pl.BlockSpec((tm, K), lambda i: (i, 0)), pl.BlockSpec((K, D), lambda i: (0, 0)), row, vec, vec],
        out_specs=[row, row, pl.BlockSpec((tm, 1), lambda i: (i, 0))],
        out_shape=[jax.ShapeDtypeStruct((S, D), F32), jax.ShapeDtypeStruct((S, D), F32), jax.ShapeDtypeStruct((S, 1), F32)],
        compiler_params=_params(("parallel",)),
    )(a, w, resid, gain, bias)


def mm_nt(a, w, resid, a_blocked, out_dtype=F32, tm=1024):
    if a_blocked:
        nk, S, Kb = a.shape
        a_spec = pl.BlockSpec((None, tm, Kb), lambda i, n: (n, i, 0))
    else:
        S, Kb = a.shape
        nk = 1
        a_spec = pl.BlockSpec((tm, Kb), lambda i, n: (i, 0))
    M = w.shape[0]
    has_resid = resid is not None

    def body(*refs):
        if has_resid:
            a_ref, w_ref, r_ref, o_ref, acc = refs
        else:
            a_ref, w_ref, o_ref, acc = refs
        n = pl.program_id(1)
        part = _dot_nt(a_ref[...].astype(BF16), w_ref[...])

        @pl.when(n == 0)
        def _():
            acc[...] = part

        @pl.when(n > 0)
        def _():
            acc[...] += part

        @pl.when(n == nk - 1)
        def _():
            out = acc[...]
            if has_resid:
                out = out + ALPHA * r_ref[...]
            o_ref[...] = out.astype(out_dtype)

    row = pl.BlockSpec((tm, M), lambda i, n: (i, 0))
    in_specs = [a_spec, pl.BlockSpec((M, Kb), lambda i, n: (0, n))] + ([row] if has_resid else [])
    args = (a, w) + ((resid,) if has_resid else ())
    return pl.pallas_call(
        body, name="mm_nt",
        grid=(S // tm, nk),
        in_specs=in_specs,
        out_specs=row,
        out_shape=jax.ShapeDtypeStruct((S, M), out_dtype),
        scratch_shapes=[pltpu.VMEM((tm, M), F32)],
        compiler_params=_params(("parallel", "arbitrary")),
    )(*args)


def mm_nt_dilated(parts, dils, w, resid, tm=1024):
    n_groups = len(parts)
    _, S, K = parts[0].shape
    M = w.shape[0]
    nk = 3 * n_groups

    def body(*refs):
        a_refs = refs[:n_groups]
        w_ref, r_ref, o_ref, group_acc, total = refs[n_groups:]
        n = pl.program_id(1)
        for g in range(n_groups):
            for k in range(3):
                @pl.when(n == 3 * g + k)
                def _():
                    part = _dot_nt(a_refs[g][...], w_ref[...])
                    for c in range(M // LANES):
                        lanes = slice(c * LANES, (c + 1) * LANES)
                        if k == 0:
                            group_acc[c] = part[:, lanes]
                        else:
                            group_acc[c] += part[:, lanes]
                        if k == 2:
                            _rows_from_dilated(group_acc.at[c], total.at[c], tm, dils[g], accumulate=g > 0)

        @pl.when(n == nk - 1)
        def _():
            for c in range(M // LANES):
                lanes = slice(c * LANES, (c + 1) * LANES)
                o_ref[:, lanes] = total[c] + ALPHA * r_ref[:, lanes]

    def a_spec(g):
        return pl.BlockSpec((None, tm, K), lambda i, n: (jnp.clip(n - 3 * g, 0, 2), i, 0))

    row = pl.BlockSpec((tm, M), lambda i, n: (i, 0))
    return pl.pallas_call(
        body, name="mm_nt_dilated",
        grid=(S // tm, nk),
        in_specs=[a_spec(g) for g in range(n_groups)] + [pl.BlockSpec((M, K), lambda i, n: (0, n)), row],
        out_specs=row,
        out_shape=jax.ShapeDtypeStruct((S, M), F32),
        scratch_shapes=[pltpu.VMEM((M // LANES, tm, LANES), F32), pltpu.VMEM((M // LANES, tm, LANES), F32)],
        compiler_params=_params(("parallel", "arbitrary")),
    )(*parts, w, resid)


def mm_tn(pairs, *, nblk, out_shape, out_block, out_index, alias=None, tk=1024, name="mm_tn"):
    operands = []
    for a, b, _ in pairs:
        for t in (a, b):
            if not any(t is o for o in operands):
                operands.append(t)
    where = lambda t: next(i for i, o in enumerate(operands) if o is t)
    S = pairs[0][0].shape[-2]
    n_out, n_in = len(pairs), len(operands)
    n_alias = len(alias) if alias is not None else 0

    def spec(t):
        if t.ndim == 3:
            return pl.BlockSpec((None, tk, t.shape[-1]), lambda j, k: (j, k, 0))
        return pl.BlockSpec((tk, t.shape[-1]), lambda j, k: (k, 0))

    def body(*refs):
        refs = refs[n_alias:]
        in_refs, o_refs, accs = refs[:n_in], refs[n_in:n_in + n_out], refs[n_in + n_out:]
        k = pl.program_id(1)
        for (a, b, scale), o_ref, acc in zip(pairs, o_refs, accs):
            part = _dot_tn(in_refs[where(a)][...].astype(BF16), in_refs[where(b)][...].astype(BF16))

            @pl.when(k == 0)
            def _():
                acc[...] = part

            @pl.when(k > 0)
            def _():
                acc[...] += part

            @pl.when(k == S // tk - 1)
            def _():
                o_ref[...] = (scale * acc[...]).astype(BF16)

    out_spec = pl.BlockSpec(out_block, lambda j, k: out_index(j))
    outs = pl.pallas_call(
        body, name=name,
        grid=(nblk, S // tk),
        in_specs=[ANY] * n_alias + [spec(t) for t in operands],
        out_specs=[out_spec] * n_out,
        out_shape=[jax.ShapeDtypeStruct(out_shape, BF16)] * n_out,
        scratch_shapes=[pltpu.VMEM((a.shape[-1], b.shape[-1]), F32) for a, b, _ in pairs],
        input_output_aliases={i: i for i in range(n_alias)},
        compiler_params=_params(("parallel", "arbitrary")),
    )(*(tuple(alias) if alias is not None else ()), *operands)
    return list(outs)


def ffn_fwd(x, wg, wu, wd, gain, bias, tm=1024):
    S, D = x.shape
    nb, FB = wg.shape[0], wg.shape[1]

    def body(x_ref, wg_ref, wu_ref, wd_ref, g_ref, b_ref, y_ref, xh_ref, rs_ref, a_ref, u_ref, acc, xb_ref):
        j = pl.program_id(1)

        @pl.when(j == 0)
        def _():
            xb_ref[...] = x_ref[...].astype(BF16)
            acc[...] = jnp.zeros_like(acc)

        xb = xb_ref[...]
        total = None
        for t0 in range(0, FB, FFN_HIDDEN_TILE):
            cols = pl.ds(t0, min(FFN_HIDDEN_TILE, FB - t0))
            a = _dot_nt(xb, wg_ref[cols, :])
            u = _dot_nt(xb, wu_ref[cols, :])
            a_ref[:, cols] = a.astype(BF16)
            u_ref[:, cols] = u.astype(BF16)
            h = a * jax.nn.sigmoid(a) * u
            part = _dot(h.astype(BF16), wd_ref[cols, :])
            total = part if total is None else total + part
        acc[...] += total

        @pl.when(j == nb - 1)
        def _():
            z = ALPHA * x_ref[...] + MACARON_WEIGHT * acc[...]
            y, xh, rs = _ln_fwd(z, g_ref[...], b_ref[...])
            y_ref[...] = y
            xh_ref[...] = xh
            rs_ref[...] = rs

    row = pl.BlockSpec((tm, D), lambda i, j: (i, 0))
    vec = pl.BlockSpec((1, D), lambda i, j: (0, 0))
    w_out = pl.BlockSpec((None, FB, D), lambda i, j: (j, 0, 0))
    act = pl.BlockSpec((None, tm, FB), lambda i, j: (j, i, 0))
    return pl.pallas_call(
        body, name="ffn_fwd",
        grid=(S // tm, nb),
        in_specs=[row, w_out, w_out, w_out, vec, vec],
        out_specs=[row, row, pl.BlockSpec((tm, 1), lambda i, j: (i, 0)), act, act],
        out_shape=[jax.ShapeDtypeStruct((S, D), F32), jax.ShapeDtypeStruct((S, D), F32), jax.ShapeDtypeStruct((S, 1), F32),
                   jax.ShapeDtypeStruct((nb, S, FB), BF16), jax.ShapeDtypeStruct((nb, S, FB), BF16)],
        scratch_shapes=[pltpu.VMEM((tm, D), F32), pltpu.VMEM((tm, D), BF16)],
        compiler_params=_params(("parallel", "arbitrary")),
    )(x, wg, wu, wd, gain, bias)


def ffn_bwd(dz, a, u, wg, wu, wd, tm=1024):
    S, D = dz.shape
    nb, FB = wg.shape[0], wg.shape[1]

    def body(dz_ref, a_ref, u_ref, wg_ref, wu_ref, wd_ref, dx_ref, h_ref, da_ref, du_ref, acc, dzb_ref):
        j = pl.program_id(1)

        @pl.when(j == 0)
        def _():
            dzb_ref[...] = (MACARON_WEIGHT * dz_ref[...]).astype(BF16)
            acc[...] = jnp.zeros_like(acc)

        dzb = dzb_ref[...]
        total = None
        for t0 in range(0, FB, FFN_HIDDEN_TILE):
            cols = pl.ds(t0, min(FFN_HIDDEN_TILE, FB - t0))
            dh = _dot_nt(dzb, wd_ref[cols, :])
            av = a_ref[:, cols].astype(F32)
            uv = u_ref[:, cols].astype(F32)
            s = jax.nn.sigmoid(av)
            silu = av * s
            h_ref[:, cols] = (silu * uv).astype(BF16)
            da = (dh * uv * (s * (1.0 + av * (1.0 - s)))).astype(BF16)
            du = (dh * silu).astype(BF16)
            da_ref[:, cols] = da
            du_ref[:, cols] = du
            both = jnp.concatenate([da, du], axis=1)
            weights = jnp.concatenate([wg_ref[cols, :], wu_ref[cols, :]], axis=0)
            part = _dot(both, weights)
            total = part if total is None else total + part
        acc[...] += total

        @pl.when(j == nb - 1)
        def _():
            dx_ref[...] = ALPHA * dz_ref[...] + acc[...]

    row = pl.BlockSpec((tm, D), lambda i, j: (i, 0))
    w_out = pl.BlockSpec((None, FB, D), lambda i, j: (j, 0, 0))
    act = pl.BlockSpec((None, tm, FB), lambda i, j: (j, i, 0))
    act_shape = jax.ShapeDtypeStruct((nb, S, FB), BF16)
    return pl.pallas_call(
        body, name="ffn_bwd",
        grid=(S // tm, nb),
        in_specs=[row, act, act, w_out, w_out, w_out],
        out_specs=[row, act, act, act],
        out_shape=[jax.ShapeDtypeStruct((S, D), F32), act_shape, act_shape, act_shape],
        scratch_shapes=[pltpu.VMEM((tm, D), F32), pltpu.VMEM((tm, D), BF16)],
        compiler_params=_params(("parallel", "arbitrary")),
    )(dz, a, u, wg, wu, wd)


def ln_bwd(dy, xhat, rstd, gain, tm=1024):
    S, D = dy.shape

    def body(dy_ref, xh_ref, rs_ref, g_ref, dz_ref, dg_ref, db_ref):
        i = pl.program_id(0)
        dy = dy_ref[...]
        xh = xh_ref[...]
        dxh = dy * g_ref[...]
        m1 = jnp.mean(dxh, axis=-1, keepdims=True)
        m2 = jnp.mean(dxh * xh, axis=-1, keepdims=True)
        dz_ref[...] = rs_ref[...] * (dxh - m1 - xh * m2)
        dg = jnp.sum(dy * xh, axis=0, keepdims=True)
        db = jnp.sum(dy, axis=0, keepdims=True)

        @pl.when(i == 0)
        def _():
            dg_ref[...] = dg
            db_ref[...] = db

        @pl.when(i > 0)
        def _():
            dg_ref[...] += dg
            db_ref[...] += db

    row = pl.BlockSpec((tm, D), lambda i: (i, 0))
    vec = pl.BlockSpec((1, D), lambda i: (0, 0))
    return pl.pallas_call(
        body, name="ln_bwd",
        grid=(S // tm,),
        in_specs=[row, row, pl.BlockSpec((tm, 1), lambda i: (i, 0)), vec],
        out_specs=[row, vec, vec],
        out_shape=[jax.ShapeDtypeStruct((S, D), F32), jax.ShapeDtypeStruct((1, D), F32), jax.ShapeDtypeStruct((1, D), F32)],
        compiler_params=_params(("arbitrary",)),
    )(dy, xhat, rstd, gain)


def loss_head(y, target, tm=1024):
    S, D = y.shape

    def body(y_ref, t_ref, dy_ref, l_ref):
        i = pl.program_id(0)
        e = y_ref[...] - t_ref[...]
        dy_ref[...] = e / D
        part = 0.5 * jnp.sum(jnp.mean(e * e, axis=-1, keepdims=True), axis=0, keepdims=True)

        @pl.when(i == 0)
        def _():
            l_ref[...] = part

        @pl.when(i > 0)
        def _():
            l_ref[...] += part

    row = pl.BlockSpec((tm, D), lambda i: (i, 0))
    return pl.pallas_call(
        body, name="loss_head",
        grid=(S // tm,),
        in_specs=[row, row],
        out_specs=[row, pl.BlockSpec((1, 1), lambda i: (0, 0))],
        out_shape=[jax.ShapeDtypeStruct((S, D), F32), jax.ShapeDtypeStruct((1, 1), F32)],
        compiler_params=_params(("arbitrary",)),
    )(y, target)


def _pool_window(xp, g):
    n = xp.shape[0]
    w = xp + pltpu.roll(xp, 1, 0)
    out = w
    for level, shift in enumerate((1, 2, 4), start=1):
        w = pltpu.roll(w, shift, 0) + pltpu.roll(w, n - shift, 0)
        out = jnp.where(g >= level, w, out)
    return out


def _pool_count(S, C, g):
    half = lax.shift_left(jnp.int32(1), g)
    t = lax.broadcasted_iota(jnp.int32, (S, C), 0)
    return (jnp.minimum(t + half, S) - jnp.maximum(t - half, 0)).astype(F32)


def pool_mix(u, wgrp, scale):
    S, D = u.shape
    G, C = wgrp.shape[0], wgrp.shape[1]

    def body(u_ref, w_ref, s_ref, mix_ref, v_ref, pad):
        g = pl.program_id(0)
        zeros = jnp.zeros((POOL_PAD, C), F32)
        pad[pl.ds(0, POOL_PAD), :] = zeros
        pad[pl.ds(POOL_PAD + S, POOL_PAD), :] = zeros
        pad[pl.ds(POOL_PAD, S), :] = u_ref[...]
        win = _pool_window(pad[...], g)[POOL_PAD:POOL_PAD + S]
        mixed = (win / _pool_count(S, C, g) - u_ref[...]).astype(BF16)
        mix_ref[...] = mixed
        v_ref[...] = _dot(mixed, w_ref[...]) * s_ref[...]

    col = pl.BlockSpec((S, C), lambda g: (0, g))
    return pl.pallas_call(
        body, name="pool_mix",
        grid=(G,),
        in_specs=[col, pl.BlockSpec((None, C, C), lambda g: (g, 0, 0)), pl.BlockSpec((1, C), lambda g: (0, g))],
        out_specs=[col, col],
        out_shape=[jax.ShapeDtypeStruct((S, D), BF16), jax.ShapeDtypeStruct((S, D), F32)],
        scratch_shapes=[pltpu.VMEM((S + 2 * POOL_PAD, C), F32)],
        compiler_params=_params(("arbitrary",)),
    )(u, wgrp, scale)


def pool_mix_bwd(dv, mixed, wgrp, scale):
    S, D = dv.shape
    G, C = wgrp.shape[0], wgrp.shape[1]

    def body(dv_ref, mix_ref, w_ref, s_ref, du_ref, dw_ref, ds_ref, pad):
        g = pl.program_id(0)
        mixed = mix_ref[...]
        dv = dv_ref[...]
        yg = _dot(mixed, w_ref[...])
        ds_ref[...] = jnp.sum(dv * yg, axis=0, keepdims=True)
        dyg = (dv * s_ref[...]).astype(BF16)
        dw_ref[...] = _dot_tn(mixed, dyg).astype(BF16)
        dmix = _dot_nt(dyg, w_ref[...])
        zeros = jnp.zeros((POOL_PAD, C), F32)
        pad[pl.ds(0, POOL_PAD), :] = zeros
        pad[pl.ds(POOL_PAD + S, POOL_PAD), :] = zeros
        pad[pl.ds(POOL_PAD, S), :] = dmix / _pool_count(S, C, g)
        win = _pool_window(pad[...], g)
        win = pltpu.roll(win, win.shape[0] - 1, 0)[POOL_PAD:POOL_PAD + S]
        du_ref[...] = win - dmix

    col = pl.BlockSpec((S, C), lambda g: (0, g))
    return pl.pallas_call(
        body, name="pool_mix_bwd",
        grid=(G,),
        in_specs=[col, col, pl.BlockSpec((None, C, C), lambda g: (g, 0, 0)), pl.BlockSpec((1, C), lambda g: (0, g))],
        out_specs=[col, pl.BlockSpec((None, C, C), lambda g: (g, 0, 0)), pl.BlockSpec((1, C), lambda g: (0, g))],
        out_shape=[jax.ShapeDtypeStruct((S, D), F32), jax.ShapeDtypeStruct((G, C, C), BF16), jax.ShapeDtypeStruct((1, D), F32)],
        scratch_shapes=[pltpu.VMEM((S + 2 * POOL_PAD, C), F32)],
        compiler_params=_params(("arbitrary",)),
    )(dv, mixed, wgrp, scale)


PERM_BLOCK = 256


def _dilated_runs(rows, d):
    n = PERM_BLOCK // d
    return [(c * PERM_BLOCK, r, n) for c in range(rows // PERM_BLOCK) for r in range(d)]


def _rows_to_dilated(src_ref, dst_ref, rows, d):
    for base, r, n in _dilated_runs(rows, d):
        dst_ref[pl.ds(base + r * n, n), :] = src_ref[pl.ds(base + r, n, stride=d), :].astype(dst_ref.dtype)


def _rows_from_dilated(src_ref, dst_ref, rows, d, accumulate=False):
    for base, r, n in _dilated_runs(rows, d):
        at = pl.ds(base + r, n, stride=d)
        v = src_ref[pl.ds(base + r * n, n), :]
        dst_ref[at, :] = dst_ref[at, :] + v if accumulate else v


def dilate_rows(x, dils, out_dtype, tm=4096):
    S, D = x.shape

    def body(x_ref, *o_refs):
        for d, o_ref in zip(dils, o_refs):
            _rows_to_dilated(x_ref, o_ref, tm, d)

    tile = pl.BlockSpec((tm, LANES), lambda i, j: (i, j))
    return pl.pallas_call(
        body, name="dilate_rows",
        grid=(S // tm, D // LANES),
        in_specs=[tile],
        out_specs=[tile] * len(dils),
        out_shape=[jax.ShapeDtypeStruct((S, D), out_dtype)] * len(dils),
        compiler_params=_params(("parallel", "parallel")),
    )(x)


def _slope_table(group, dilation):
    s = _alibi_slopes()[group].reshape(N_HEADS // 2, 2, 1, 1) * float(dilation)
    return jnp.asarray(np.broadcast_to(s, (N_HEADS // 2, 2, 1, ATTN_W)).copy())


def _residue_shape(S, D, d):
    return (S, D) if d == 1 else (S // PERM_BLOCK, d, PERM_BLOCK // d, D)


def _residue_view(x, d):
    return x.reshape(x.shape[:-2] + _residue_shape(x.shape[-2], x.shape[-1], d))


def _residue_spec(lead_block, lead_index, S, d):
    if d == 1:
        return pl.BlockSpec(lead_block + (S, LANES), lambda hp, r: lead_index + (0, hp))
    return pl.BlockSpec(lead_block + (S // PERM_BLOCK, None, PERM_BLOCK // d, LANES), lambda hp, r: lead_index + (0, r, 0, hp))


def _whole(ref, lead, L):
    return ref[lead + (slice(None),) * (len(ref.shape) - len(lead))].reshape(L, LANES)


def _query_rows(lead, i, d):
    if d == 1:
        return lead + (pl.ds(pl.multiple_of(i * ATTN_BQ, ATTN_BQ), ATTN_BQ), slice(None)), (ATTN_BQ, LANES)
    n = PERM_BLOCK // d
    return lead + (pl.ds(i * (ATTN_BQ // n), ATTN_BQ // n), slice(None), slice(None)), (ATTN_BQ // n, n, LANES)


def _load_query_rows(ref, lead, i, d):
    at, _ = _query_rows(lead, i, d)
    return ref[at].reshape(ATTN_BQ, LANES)


def _store_query_rows(ref, lead, i, d, value):
    at, shape = _query_rows(lead, i, d)
    ref[at] = value.reshape(shape)


def _stage_keys(dst, src_ref, L):
    rows = _whole(src_ref, (), L)
    lane = lax.broadcasted_iota(jnp.int32, (L, LANES), 1)
    zeros = jnp.zeros((ATTN_R, LANES), dst.dtype)
    for h in range(2):
        mine = (lane < HEAD_DIM) if h == 0 else (lane >= HEAD_DIM)
        dst[h, pl.ds(0, ATTN_R), :] = zeros
        dst[h, pl.ds(ATTN_R + L, ATTN_R), :] = zeros
        dst[h, pl.ds(ATTN_R, L), :] = jnp.where(mine, rows, jnp.zeros_like(rows))


def _fill_bias(bias, sl_ref):
    a = lax.broadcasted_iota(jnp.int32, (ATTN_BQ, ATTN_W), 0)
    c = lax.broadcasted_iota(jnp.int32, (ATTN_BQ, ATTN_W), 1)
    rel = jnp.abs(c - ATTN_R - a)
    band = rel <= ATTN_R
    after_start = c >= ATTN_R
    before_end = c < ATTN_BQ + ATTN_R
    for h in range(2):
        base = -(sl_ref[h] * rel.astype(F32))
        for variant in range(4):
            ok = band
            if variant & 1:
                ok = ok & after_start
            if variant & 2:
                ok = ok & before_end
            bias[variant, h] = jnp.where(ok, base, MASK_VALUE)


def _bias_variant(i, nq):
    return jnp.where(i == 0, 1, 0) + jnp.where(i == nq - 1, 2, 0)


def attn_fwd(qkv, group, dilation):
    _, S, D = qkv.shape
    d = dilation
    L = S // d
    nq = L // ATTN_BQ
    ncol = D // LANES
    view = _residue_view(qkv, d)
    slopes = _slope_table(group, d)
    scale = HEAD_DIM ** -0.5

    def body(q_ref, k_ref, v_ref, sl_ref, o_ref, lse_ref, k2, v2, bias):
        @pl.when(pl.program_id(1) == 0)
        def _():
            _fill_bias(bias, sl_ref)

        _stage_keys(k2, k_ref, L)
        _stage_keys(v2, v_ref, L)
        head0 = lax.broadcasted_iota(jnp.int32, (ATTN_BQ, LANES), 1) < HEAD_DIM

        def block(i):
            variant = _bias_variant(i, nq)
            win = pl.ds(pl.multiple_of(i * ATTN_BQ, ATTN_BQ), ATTN_W)
            qs = _load_query_rows(q_ref, (), i, d) * jnp.asarray(scale, BF16)
            es, ms, ls = [], [], []
            for h in range(2):
                s = _dot_nt(qs, k2[h, win, :]) + bias[variant, h]
                m = jnp.max(s, axis=-1, keepdims=True)
                e = jnp.exp(s - m)
                ls.append(jnp.sum(e, axis=-1, keepdims=True))
                ms.append(m)
                es.append(e.astype(BF16))
            acc = _dot(jnp.concatenate(es, axis=1), jnp.concatenate([v2[0, win, :], v2[1, win, :]], axis=0))
            out = acc * jnp.where(head0, 1.0 / ls[0], 1.0 / ls[1])
            lse = jnp.where(head0, ms[0] + jnp.log(ls[0]), ms[1] + jnp.log(ls[1]))
            return out, lse

        def step(t, carry):
            results = [block(t * group + b) for b in range(group)]
            for b, (out, lse) in enumerate(results):
                _store_query_rows(o_ref, (), t * group + b, d, out)
                _store_query_rows(lse_ref, (), t * group + b, d, lse)
            return carry

        group = min(ATTN_UNROLL, nq)
        lax.fori_loop(0, nq // group, step, 0)

    def col(which):
        return _residue_spec((None,), (which,), S, d)

    out = _residue_spec((), (), S, d)
    o, lse = pl.pallas_call(
        body, name=f"attn_fwd_g{group}",
        grid=(ncol, d),
        in_specs=[col(0), col(1), col(2), pl.BlockSpec((None, 2, 1, ATTN_W), lambda hp, r: (hp, 0, 0, 0))],
        out_specs=[out, out],
        out_shape=[jax.ShapeDtypeStruct(_residue_shape(S, D, d), F32)] * 2,
        scratch_shapes=[pltpu.VMEM((2, L + 2 * ATTN_R, LANES), BF16), pltpu.VMEM((2, L + 2 * ATTN_R, LANES), BF16),
                        pltpu.VMEM((4, 2, ATTN_BQ, ATTN_W), F32)],
        compiler_params=_params(("arbitrary", "arbitrary")),
    )(view, view, view, slopes)
    return o.reshape(S, D), lse.reshape(S, D)


def attn_combine(os, lses, dils, tm=2048):
    S, D = os[0].shape
    n = len(os)
    n_moved = sum(d > 1 for d in dils)

    def body(*refs):
        o_refs, l_refs, out_ref, lse_ref = list(refs[:n]), list(refs[n:2 * n]), refs[2 * n], refs[2 * n + 1]
        spare = list(refs[2 * n + 2:])
        for g, d in enumerate(dils):
            if d > 1:
                for which in (o_refs, l_refs):
                    token_order = spare.pop()
                    _rows_from_dilated(which[g], token_order, tm, d)
                    which[g] = token_order
        ls = [r[...] for r in l_refs]
        m = functools.reduce(jnp.maximum, ls)
        es = [jnp.exp(l - m) for l in ls]
        tot = functools.reduce(lambda x, y: x + y, es)
        inv = 1.0 / tot
        out_ref[...] = functools.reduce(lambda x, y: x + y, [(e * inv) * r[...] for e, r in zip(es, o_refs)])
        lse_ref[...] = m + jnp.log(tot)

    tile = pl.BlockSpec((tm, LANES), lambda i, j: (i, j))
    return pl.pallas_call(
        body, name="attn_combine",
        grid=(S // tm, D // LANES),
        in_specs=[tile] * (2 * n),
        out_specs=[tile, tile],
        out_shape=[jax.ShapeDtypeStruct((S, D), F32), jax.ShapeDtypeStruct((S, D), F32)],
        scratch_shapes=[pltpu.VMEM((tm, LANES), F32)] * (2 * n_moved),
        compiler_params=_params(("parallel", "parallel")),
    )(*os, *lses)


def attn_bwd(qkv, do, o, lse, group, dilation):
    _, S, D = qkv.shape
    d = dilation
    L = S // d
    nq = L // ATTN_BQ
    ncol = D // LANES
    view = _residue_view(qkv, d)
    slopes = _slope_table(group, d)
    scale = HEAD_DIM ** -0.5

    def body(q_ref, k_ref, v_ref, do_ref, o_ref, lse_ref, sl_ref, dx_ref, k2, v2, dkacc, dvacc, bias):
        @pl.when(pl.program_id(1) == 0)
        def _():
            _fill_bias(bias, sl_ref)

        _stage_keys(k2, k_ref, L)
        _stage_keys(v2, v_ref, L)
        dkacc[...] = jnp.zeros_like(dkacc)
        dvacc[...] = jnp.zeros_like(dvacc)
        lane = lax.broadcasted_iota(jnp.int32, (ATTN_BQ, LANES), 1)
        heads = (lane < HEAD_DIM, lane >= HEAD_DIM)
        key_head0 = lax.broadcasted_iota(jnp.int32, (ATTN_W, LANES), 1) < HEAD_DIM

        def step(i, carry):
            variant = _bias_variant(i, nq)
            win = pl.ds(pl.multiple_of(i * ATTN_BQ, ATTN_BQ), ATTN_W)
            q = _load_query_rows(q_ref, (), i, d)
            qs = q * jnp.asarray(scale, BF16)
            dov = _load_query_rows(do_ref, (), i, d)
            prod = dov * _load_query_rows(o_ref, (), i, d)
            lse_v = _load_query_rows(lse_ref, (), i, d)
            dob = dov.astype(BF16)
            dss, dks, dvs = [], [], []
            for h in range(2):
                s = _dot_nt(qs, k2[h, win, :]) + bias[variant, h]
                lse_h = jnp.max(jnp.where(heads[h], lse_v, -jnp.inf), axis=-1, keepdims=True)
                dterm = jnp.sum(jnp.where(heads[h], prod, 0.0), axis=-1, keepdims=True)
                p = jnp.exp(s - lse_h)
                dp = _dot_nt(dob, v2[h, win, :])
                ds = (p * (dp - dterm) * scale).astype(BF16)
                dvs.append(_dot_tn(p.astype(BF16), dob))
                dks.append(_dot_tn(ds, q))
                dss.append(ds)
            dq = _dot(jnp.concatenate(dss, axis=1), jnp.concatenate([k2[0, win, :], k2[1, win, :]], axis=0))
            dvacc[win, :] += jnp.where(key_head0, dvs[0], dvs[1])
            dkacc[win, :] += jnp.where(key_head0, dks[0], dks[1])
            _store_query_rows(dx_ref, (0,), i, d, dq.astype(BF16))
            return carry

        lax.fori_loop(0, nq, step, 0, unroll=min(ATTN_UNROLL, nq))
        block_shape = dx_ref.shape[1:]
        dx_ref[1] = dkacc[pl.ds(ATTN_R, L), :].astype(BF16).reshape(block_shape)
        dx_ref[2] = dvacc[pl.ds(ATTN_R, L), :].astype(BF16).reshape(block_shape)

    def col(which):
        return _residue_spec((None,), (which,), S, d)

    act = _residue_spec((), (), S, d)
    out = pl.pallas_call(
        body, name=f"attn_bwd_g{group}",
        grid=(ncol, d),
        in_specs=[col(0), col(1), col(2), act, act, act, pl.BlockSpec((None, 2, 1, ATTN_W), lambda hp, r: (hp, 0, 0, 0))],
        out_specs=_residue_spec((3,), (0,), S, d),
        out_shape=jax.ShapeDtypeStruct((3,) + _residue_shape(S, D, d), BF16),
        scratch_shapes=[pltpu.VMEM((2, L + 2 * ATTN_R, LANES), BF16), pltpu.VMEM((2, L + 2 * ATTN_R, LANES), BF16),
                        pltpu.VMEM((L + 2 * ATTN_R, LANES), F32), pltpu.VMEM((L + 2 * ATTN_R, LANES), F32),
                        pltpu.VMEM((4, 2, ATTN_BQ, ATTN_W), F32)],
        compiler_params=_params(("arbitrary", "arbitrary")),
    )(view, view, view, _residue_view(do, d), _residue_view(o, d), _residue_view(lse, d), slopes)
    return out.reshape(3, S, D)


TILE_ELEMS = 512 * 1024


def _row_tile(R, C):
    if R * C <= TILE_ELEMS or R % 16:
        return R
    return max(t for t in range(16, R + 1, 16) if R % t == 0 and (t * C <= TILE_ELEMS or t == 16))


def pair_sum(core, gs, recvs):
    n = len(gs)
    _, _, R, C = gs[0].shape
    tr = _row_tile(R, C)

    def body(c_ref, *refs):
        for g_ref, r_ref, o_ref in zip(refs[:n], refs[n:2 * n], refs[2 * n:]):
            o_ref[...] = (g_ref[...].astype(F32) + r_ref[...].astype(F32)).astype(BF16)

    blk = pl.BlockSpec((None, tr, C), lambda d, i, c_ref: (d, i, 0))
    mine = pl.BlockSpec((None, None, tr, C), lambda d, i, c_ref: (d, c_ref[0], i, 0))
    return pl.pallas_call(
        body, name="pair_sum",
        grid_spec=pltpu.PrefetchScalarGridSpec(
            num_scalar_prefetch=1, grid=(N_CHIPS, R // tr),
            in_specs=[mine] * n + [blk] * n,
            out_specs=[blk] * n),
        out_shape=[jax.ShapeDtypeStruct((N_CHIPS, R, C), BF16)] * n,
        compiler_params=_params(("parallel", "parallel")),
    )(core, *gs, *recvs)


def chip_sum(chip, owns, recvs):
    n = len(owns)
    _, R, C = owns[0].shape
    tr = _row_tile(R, C)
    slot_of_relation = {2: 0, 1: 1, 3: 2}

    def body(chip_ref, *refs):
        me = chip_ref[0]
        for own_ref, r_ref, o_ref in zip(refs[:n], refs[n:2 * n], refs[2 * n:]):
            mine = own_ref[...].astype(F32)
            theirs = {rel: r_ref[k].astype(F32) for rel, k in slot_of_relation.items()}
            acc = None
            for s in range(N_CHIPS):
                rel = jnp.bitwise_xor(me, s)
                part = jnp.where(rel == 0, mine, jnp.where(rel == 2, theirs[2], jnp.where(rel == 1, theirs[1], theirs[3])))
                acc = part if acc is None else acc + part
            o_ref[...] = acc

    return pl.pallas_call(
        body, name="chip_sum",
        grid_spec=pltpu.PrefetchScalarGridSpec(
            num_scalar_prefetch=1, grid=(R // tr,),
            in_specs=[pl.BlockSpec((None, tr, C), lambda i, chip_ref: (chip_ref[0], i, 0))] * n
            + [pl.BlockSpec((N_CHIPS - 1, tr, C), lambda i, chip_ref: (0, i, 0))] * n,
            out_specs=[pl.BlockSpec((tr, C), lambda i, chip_ref: (i, 0))] * n),
        out_shape=[jax.ShapeDtypeStruct((R, C), F32)] * n,
        compiler_params=_params(("parallel",)),
    )(chip, *owns, *recvs)


def adamw(core, w, g_pairs, m, v):
    L, H, R, C = w.shape
    tr = _row_tile(R, C)

    def body(c_ref, w_ref, *rest):
        g_refs = rest[:2 * L]
        m_ref, v_ref, g_ref, d_ref, nm_ref, nv_ref = rest[2 * L:]
        mine = pl.program_id(1) == c_ref[0]
        g = None
        for l in range(L):
            g_l = jnp.where(mine, g_refs[2 * l][...], g_refs[2 * l + 1][...])
            g = g_l if g is None else jnp.where(pl.program_id(0) == l, g_l, g)
        m = ADAM_B1 * m_ref[...] + (1.0 - ADAM_B1) * g
        v = ADAM_B2 * v_ref[...] + (1.0 - ADAM_B2) * (g * g)
        m_hat = m / (1.0 - ADAM_B1 ** ADAM_STEP)
        v_hat = v / (1.0 - ADAM_B2 ** ADAM_STEP)
        g_ref[...] = g
        d_ref[...] = -ADAM_LR * (m_hat / (jnp.sqrt(v_hat) + ADAM_EPS) + ADAM_WD * w_ref[...])
        nm_ref[...] = m
        nv_ref[...] = v

    blk = pl.BlockSpec((None, None, tr, C), lambda l, h, i, c_ref: (l, h, i, 0))

    def half(layer):
        return pl.BlockSpec((tr, C), lambda l, h, i, c_ref: (jnp.where(l == layer, i, 0), 0))

    shape = jax.ShapeDtypeStruct((L, H, R, C), F32)
    return pl.pallas_call(
        body, name="adamw",
        grid_spec=pltpu.PrefetchScalarGridSpec(
            num_scalar_prefetch=1, grid=(L, H, R // tr),
            in_specs=[blk] + [half(l) for l in range(L) for _ in range(2)] + [blk, blk],
            out_specs=[blk] * 4),
        out_shape=[shape] * 4,
        compiler_params=_params(("parallel", "parallel", "parallel")),
    )(core, w, *[g for pair in g_pairs for g in pair], m, v)


def _place():
    return lax.axis_index("x"), lax.axis_index("y"), lax.axis_index("c")


def _other_chips(x, y):
    return [(2 * (1 - x) + y, (1 - x, y)), (2 * x + (1 - y), (x, 1 - y)), (2 * (1 - x) + (1 - y), (1 - x, 1 - y))]


def all_gather_shards(shards, placed):
    n = len(shards)

    def body(*refs):
        ins, outs = refs[:n], refs[2 * n:3 * n]
        send_sems, recv_sems = refs[3 * n:]
        x, y, c = _place()
        me = 2 * x + y
        sibling = (x, y, 1 - c)
        chips = _other_chips(x, y)

        def copy(a, k, src, dst, to):
            return pltpu.make_async_remote_copy(src_ref=src, dst_ref=dst, send_sem=send_sems.at[a, k], recv_sem=recv_sems.at[a, k],
                                                device_id=to, device_id_type=MESH)

        sends = []
        for a in range(n):
            for k, (_, (px, py)) in enumerate(chips):
                cp = copy(a, k, ins[a].at[c], outs[a].at[me, c], (px, py, c))
                cp.start()
                sends.append(cp)
        for a in range(n):
            for k, (chip, _) in enumerate(chips):
                landed = outs[a].at[chip, c]
                copy(a, k, landed, landed, sibling).wait_recv()
                cp = copy(a, 3 + k, landed, landed, sibling)
                cp.start()
                sends.append(cp)
        for a in range(n):
            for k, (chip, _) in enumerate(chips):
                other = outs[a].at[chip, 1 - c]
                copy(a, 3 + k, other, other, sibling).wait_recv()
        for cp in sends:
            cp.wait_send()

    return pl.pallas_call(
        body, name="all_gather_shards",
        in_specs=[ANY] * (2 * n),
        out_specs=[ANY] * n,
        out_shape=[jax.ShapeDtypeStruct(p.shape, p.dtype) for p in placed],
        scratch_shapes=[pltpu.SemaphoreType.DMA((n, 6)), pltpu.SemaphoreType.DMA((n, 6))],
        input_output_aliases={n + a: a for a in range(n)},
        compiler_params=pltpu.CompilerParams(has_side_effects=True),
    )(*shards, *placed)


def all_gather_shards_async(shards, placed, collective_id, name):
    n = len(shards)
    srcs = [jax.new_ref(s, memory_space=pltpu.MemorySpace.HBM) for s in shards]
    dsts = [jax.new_ref(p, memory_space=pltpu.MemorySpace.HBM) for p in placed]

    @pl.kernel(mesh=plsc.ScalarSubcoreMesh(axis_name="sequencer", num_cores=1), name=name,
               scratch_types=(pltpu.SemaphoreType.DMA((n, 6)), pltpu.SemaphoreType.DMA((n, 6))),
               compiler_params=pltpu.CompilerParams(collective_id=collective_id))
    def launch(send_sems, recv_sems):
        x, y, c = _place()
        me = 2 * x + y
        sibling = (x, y, 1 - c)
        chips = _other_chips(x, y)
        barrier = pltpu.get_barrier_semaphore()
        peers = [sibling] + [(px, py, c) for _, (px, py) in chips]
        for peer in peers:
            pl.semaphore_signal(barrier, inc=1, device_id=peer, device_id_type=MESH)
        pl.semaphore_wait(barrier, len(peers))

        def copy(a, k, src, dst, to):
            return pltpu.make_async_remote_copy(src_ref=src, dst_ref=dst, send_sem=send_sems.at[a, k], recv_sem=recv_sems.at[a, k],
                                                device_id=to, device_id_type=MESH)

        sends = []
        for a in range(n):
            for k, (_, (px, py)) in enumerate(chips):
                cp = copy(a, k, srcs[a].at[c], dsts[a].at[me, c], (px, py, c))
                cp.start()
                sends.append(cp)
        for a in range(n):
            for k, (chip, _) in enumerate(chips):
                landed = dsts[a].at[chip, c]
                copy(a, k, landed, landed, sibling).wait_recv()
                cp = copy(a, 3 + k, landed, landed, sibling)
                cp.start()
                sends.append(cp)
        for a in range(n):
            for k, (chip, _) in enumerate(chips):
                other = dsts[a].at[chip, 1 - c]
                copy(a, 3 + k, other, other, sibling).wait_recv()
        for cp in sends:
            cp.wait_send()

    launch()
    return [d[...] for d in dsts]


def _exchange(body, ins, out_shapes, sem_shapes, name, peers=None, collective_id=None):
    n_in, n_out = len(ins), len(out_shapes)
    sems = [pltpu.SemaphoreType.DMA(shape) for shape in sem_shapes]
    if collective_id is None:
        def tc_body(*refs):
            body(refs[:n_in], refs[n_in:n_in + n_out], *refs[n_in + n_out:])

        return pl.pallas_call(tc_body, name=name, in_specs=[ANY] * n_in, out_specs=[ANY] * n_out, out_shape=out_shapes,
                              scratch_shapes=sems, compiler_params=pltpu.CompilerParams(has_side_effects=True))(*ins)
    srcs = [jax.new_ref(a, memory_space=pltpu.MemorySpace.HBM) for a in ins]
    dsts = [jax.empty_ref(shape, memory_space=pltpu.MemorySpace.HBM) for shape in out_shapes]

    @pl.kernel(mesh=plsc.ScalarSubcoreMesh(axis_name="sequencer", num_cores=1), name=name, scratch_types=tuple(sems),
               compiler_params=pltpu.CompilerParams(collective_id=collective_id))
    def launch(*sem_refs):
        barrier = pltpu.get_barrier_semaphore()
        others = peers(*_place())
        for peer in others:
            pl.semaphore_signal(barrier, inc=1, device_id=peer, device_id_type=MESH)
        pl.semaphore_wait(barrier, len(others))
        body(srcs, dsts, *sem_refs)

    launch()
    return [d[...] for d in dsts]


def _sibling(x, y, c):
    return [(x, y, 1 - c)]


def _same_core_of_other_chips(x, y, c):
    return [(px, py, c) for _, (px, py) in _other_chips(x, y)]


def sibling_exchange_halves(grads, name="sibling_exchange_halves", collective_id=None):
    n = len(grads)

    def body(ins, outs, send_sems, recv_sems):
        x, y, c = _place()
        copies = [pltpu.make_async_remote_copy(src_ref=ins[a].at[:, 1 - c], dst_ref=outs[a], send_sem=send_sems.at[a],
                                               recv_sem=recv_sems.at[a], device_id=(x, y, 1 - c), device_id_type=MESH) for a in range(n)]
        for cp in copies:
            cp.start()
        for cp in copies:
            cp.wait()

    shapes = [jax.ShapeDtypeStruct((N_CHIPS,) + g.shape[2:], g.dtype) for g in grads]
    return _exchange(body, grads, shapes, [(n,), (n,)], name, _sibling, collective_id)


def chip_exchange(sums, name="chip_exchange", collective_id=None):
    n = len(sums)

    def body(ins, outs, send_sems, recv_sems):
        x, y, c = _place()
        copies = []
        for a in range(n):
            for k, (chip, (px, py)) in enumerate(_other_chips(x, y)):
                cp = pltpu.make_async_remote_copy(src_ref=ins[a].at[chip], dst_ref=outs[a].at[k], send_sem=send_sems.at[a, k],
                                                  recv_sem=recv_sems.at[a, k], device_id=(px, py, c), device_id_type=MESH)
                cp.start()
                copies.append(cp)
        for cp in copies:
            cp.wait()

    shapes = [jax.ShapeDtypeStruct((N_CHIPS - 1,) + s.shape[1:], s.dtype) for s in sums]
    return _exchange(body, sums, shapes, [(n, 3), (n, 3)], name, _same_core_of_other_chips, collective_id)


def sibling_share(halves, name="sibling_share", collective_id=None):
    n = len(halves)

    def body(ins, outs, send_sems, recv_sems):
        x, y, c = _place()
        copies = [pltpu.make_async_remote_copy(src_ref=ins[a], dst_ref=outs[a], send_sem=send_sems.at[a], recv_sem=recv_sems.at[a],
                                               device_id=(x, y, 1 - c), device_id_type=MESH) for a in range(n)]
        for cp in copies:
            cp.start()
        for cp in copies:
            cp.wait()

    shapes = [jax.ShapeDtypeStruct(h.shape, h.dtype) for h in halves]
    return _exchange(body, halves, shapes, [(n,), (n,)], name, _sibling, collective_id)


def all_reduce_small(v):
    R, C = v.shape
    n_dev = 8

    def body(v_ref, o_ref, buf, send_sems, recv_sems):
        x, y, c = _place()
        me = 4 * x + 2 * y + c
        buf[me] = v_ref[...]
        copies = []
        for rel in range(1, n_dev):
            fx, fy, fc = rel >> 2, (rel >> 1) & 1, rel & 1
            peer = (x ^ fx, y ^ fy, c ^ fc)
            cp = pltpu.make_async_remote_copy(src_ref=v_ref, dst_ref=buf.at[me], send_sem=send_sems.at[rel - 1],
                                              recv_sem=recv_sems.at[rel - 1], device_id=peer, device_id_type=MESH)
            cp.start()
            copies.append(cp)
        for cp in copies:
            cp.wait()
        acc = buf[0]
        for k in range(1, n_dev):
            acc = acc + buf[k]
        o_ref[...] = acc

    return pl.pallas_call(
        body, name="all_reduce_small",
        in_specs=[pl.BlockSpec(memory_space=pltpu.VMEM)],
        out_specs=pl.BlockSpec(memory_space=pltpu.VMEM),
        out_shape=jax.ShapeDtypeStruct((R, C), F32),
        scratch_shapes=[pltpu.VMEM((n_dev, R, C), F32), pltpu.SemaphoreType.DMA((n_dev - 1,)), pltpu.SemaphoreType.DMA((n_dev - 1,))],
        compiler_params=pltpu.CompilerParams(has_side_effects=True),
    )(v)


WEIGHT_NAMES = ("ffn1_w_gate", "ffn1_w_up", "ffn1_w_down", "ffn2_w_gate", "ffn2_w_up", "ffn2_w_down", "ln_gain", "ln_bias",
                "pool_w_in", "pool_w_group", "pool_scale", "pool_w_out", "attn_w_qkv", "attn_w_out")
MATRIX_NAMES = ("ffn1_w_gate", "ffn1_w_up", "ffn1_w_down", "ffn2_w_gate", "ffn2_w_up", "ffn2_w_down",
                "pool_w_in", "pool_w_group", "pool_w_out", "attn_w_qkv", "attn_w_out")


TRANSPOSED_NAMES = ("ffn1_w_gate", "ffn1_w_up", "ffn2_w_gate", "ffn2_w_up")


def _halves(name, w):
    if name in TRANSPOSED_NAMES:
        w = jnp.swapaxes(w, 1, 2)
    return w.reshape(2, -1, w.shape[-1])


def _unhalves(name, t, shape):
    if name in TRANSPOSED_NAMES:
        return jnp.swapaxes(t.reshape(shape[0], shape[2], shape[1]), 1, 2)
    return t.reshape(shape)


def kernel(x, ffn1_w_gate, ffn1_w_up, ffn1_w_down, ffn2_w_gate, ffn2_w_up, ffn2_w_down, ln_gain, ln_bias, pool_w_in, pool_w_group, pool_scale, pool_w_out, attn_w_qkv, attn_w_out, loss_target, m_ffn1_w_gate, m_ffn1_w_up, m_ffn1_w_down, m_ffn2_w_gate, m_ffn2_w_up, m_ffn2_w_down, m_ln_gain, m_ln_bias, m_pool_w_in, m_pool_w_group, m_pool_scale, m_pool_w_out, m_attn_w_qkv, m_attn_w_out, v_ffn1_w_gate, v_ffn1_w_up, v_ffn1_w_down, v_ffn2_w_gate, v_ffn2_w_up, v_ffn2_w_down, v_ln_gain, v_ln_bias, v_pool_w_in, v_pool_w_group, v_pool_scale, v_pool_w_out, v_attn_w_qkv, v_attn_w_out):
    weights = dict(zip(WEIGHT_NAMES, (ffn1_w_gate, ffn1_w_up, ffn1_w_down, ffn2_w_gate, ffn2_w_up, ffn2_w_down, ln_gain, ln_bias,
                                      pool_w_in, pool_w_group, pool_scale, pool_w_out, attn_w_qkv, attn_w_out)))
    moms = dict(zip(WEIGHT_NAMES, (m_ffn1_w_gate, m_ffn1_w_up, m_ffn1_w_down, m_ffn2_w_gate, m_ffn2_w_up, m_ffn2_w_down, m_ln_gain,
                                   m_ln_bias, m_pool_w_in, m_pool_w_group, m_pool_scale, m_pool_w_out, m_attn_w_qkv, m_attn_w_out)))
    vels = dict(zip(WEIGHT_NAMES, (v_ffn1_w_gate, v_ffn1_w_up, v_ffn1_w_down, v_ffn2_w_gate, v_ffn2_w_up, v_ffn2_w_down, v_ln_gain,
                                   v_ln_bias, v_pool_w_in, v_pool_w_group, v_pool_scale, v_pool_w_out, v_attn_w_qkv, v_attn_w_out)))
    S, D = x.shape[1], x.shape[2]
    FB = ffn1_w_gate.shape[2]
    QKV = attn_w_qkv.shape[2] * N_CHIPS
    G, CB = pool_w_group.shape[1], pool_w_group.shape[2]
    C = pool_w_group.shape[3]
    cx, cy, cc = _place()
    chip = 2 * cx + cy
    xs = x.reshape(S, D)
    target = loss_target.reshape(S, D)

    ln_rows = jnp.concatenate([ln_gain, ln_bias, jnp.zeros((DEPTH, 2, ln_gain.shape[2]), F32)], axis=1)
    shard = {n: _halves(n, weights[n]).astype(BF16) for n in MATRIX_NAMES}
    ffn_layer = lambda f, i: {f"{f}{s}@{i}": shard[f + s][i].reshape(2, FB // 2, D) for s in ("_w_gate", "_w_up", "_w_down")}
    pool = {n: shard[n] for n in ("pool_w_in", "pool_w_group", "pool_w_out")}
    attn = {n: shard[n] for n in ("attn_w_qkv", "attn_w_out")}
    groups = [dict(ffn_layer("ffn1", 0), ln=ln_rows), pool, ffn_layer("ffn2", 0), ffn_layer("ffn1", 1), attn, ffn_layer("ffn2", 1)]
    full = {}

    def launch(k, after=None):
        parts = list(groups[k].values())
        if after is not None:
            *parts, after = lax.optimization_barrier((*parts, after))
        placed = [lax.dynamic_update_slice(lax.empty((N_CHIPS,) + s.shape, s.dtype), s[None], (chip, 0, 0, 0)) for s in parts]
        full.update(zip(groups[k], all_gather_shards_async(parts, placed, GATHER_COLLECTIVE_ID, f"gather_weights_{k}")))
        return after

    for k in range(4):
        launch(k)
    ffn_w = lambda f, i: [full[f"{f}{s}@{i}"].reshape(N_CHIPS, FB, D) for s in ("_w_gate", "_w_up", "_w_down")]
    w_pool_in = full["pool_w_in"].reshape(D, D)
    w_pool_out = full["pool_w_out"].reshape(D, D)
    w_group = full["pool_w_group"].reshape(N_CHIPS, G, CB, C).transpose(1, 0, 2, 3).reshape(G, N_CHIPS * CB, C)
    ln_full = full["ln"].transpose(1, 2, 0, 3).reshape(DEPTH, 8, D)
    gain = lambda i, k: ln_full[i, k].reshape(1, D)
    bias = lambda i, k: ln_full[i, 3 + k].reshape(1, D)

    dils = [dil for _, dil in DIL_CONFIGS]
    moved_dils = [dil for dil in dils if dil > 1]

    saved = []
    y = xs
    for i in range(DEPTH):
        y_in = y
        y, xh, rs, a, u = ffn_fwd(y_in, *ffn_w("ffn1", i), gain(i, 0), bias(i, 0))
        f1 = (y_in, xh, rs, a, u)
        if i == 0:
            y = launch(4, after=y)
        y_mid = y
        if i % 2 == 0:
            pu = mm_nn(y_mid, w_pool_in, F32)[0]
            mixed, pv = pool_mix(pu, w_group, pool_scale)
            y, xh, rs = proj_ln(pv, w_pool_out, y_mid, gain(i, 1), bias(i, 1))
            mix = (y_mid, xh, rs, mixed, pv)
            if i == 0:
                y = launch(5, after=y)
        else:
            qkv_blocks, y_mid = lax.optimization_barrier((full["attn_w_qkv"], y_mid))
            w_qkv = chip_blocks_to_columns(qkv_blocks.reshape(N_CHIPS, D, QKV // N_CHIPS))
            w_attn_out = full["attn_w_out"].reshape(D, D)
            moved = dict(zip(moved_dils, dilate_rows(y_mid, moved_dils, BF16)))
            srcs = [moved.get(dil, y_mid) for dil in dils]
            qkvs = [mm_nn(src, w_qkv, BF16, first_block=3 * g, nb=3) for g, src in enumerate(srcs)]
            parts = [attn_fwd(qkv, g, dil) for g, (qkv, dil) in enumerate(zip(qkvs, dils))]
            ao, lse = attn_combine([p[0] for p in parts], [p[1] for p in parts], dils)
            y, xh, rs = proj_ln(ao, w_attn_out, y_mid, gain(i, 1), bias(i, 1))
            mix = (y_mid, xh, rs, srcs, qkvs, ao, lse, w_qkv, w_attn_out)
        y_in2 = y
        y, xh, rs, a, u = ffn_fwd(y_in2, *ffn_w("ffn2", i), gain(i, 2), bias(i, 2))
        f2 = (y_in2, xh, rs, a, u)
        saved.append((f1, mix, f2))

    dy, loss_part = loss_head(y, target)
    loss = lax.psum(loss_part[0, 0], ("x", "y", "c"))

    core = cc.reshape(1).astype(jnp.int32)
    chip_id = chip.reshape(1).astype(jnp.int32)
    dgain = [[None] * 3 for _ in range(DEPTH)]
    dbias = [[None] * 3 for _ in range(DEPTH)]
    dscale = None
    pieces = []
    own_half, other_half = {}, {}

    def tie(arrays, after):
        *arrays, after = lax.optimization_barrier((*arrays, after))
        return arrays, after

    def by_shape(fn, xs, ys):
        out = [None] * len(xs)
        for shape in dict.fromkeys(x.shape for x in xs):
            idx = [k for k, x in enumerate(xs) if x.shape == shape]
            for k, res in zip(idx, fn([xs[k] for k in idx], [ys[k] for k in idx])):
                out[k] = res
        return out

    def start_piece(keys, arrays):
        blocks = [g.reshape(N_CHIPS, 2, -1, g.shape[-1]) for g in arrays]
        k = len(pieces)
        pieces.append(dict(keys=keys, blocks=blocks, from_sibling=sibling_exchange_halves(
            blocks, name=f"reduce_halves_{k}", collective_id=SIBLING_COLLECTIVE_ID)))

    def pair_sums_and_chip_exchange(k, after):
        piece = pieces[k]
        received, after = tie(piece["from_sibling"], after)
        piece["pair_sums"] = by_shape(lambda gs, rs: pair_sum(core, gs, rs), piece["blocks"], received)
        piece["from_chips"] = chip_exchange(piece["pair_sums"], name=f"reduce_chips_{k}", collective_id=CHIPS_COLLECTIVE_ID)
        return after

    def chip_sums_and_share(k, after):
        piece = pieces[k]
        received, after = tie(piece["from_chips"], after)
        mine = by_shape(lambda ps, rs: chip_sum(chip_id, ps, rs), piece["pair_sums"], received)
        theirs = sibling_share(mine, name=f"reduce_share_{k}", collective_id=SIBLING_COLLECTIVE_ID)
        own_half.update(zip(piece["keys"], mine))
        other_half.update(zip(piece["keys"], theirs))
        return after

    def piece_done(keys, arrays, dy):
        start_piece(keys, arrays)
        k = len(pieces) - 1
        if k >= 1:
            dy = pair_sums_and_chip_exchange(k - 1, dy)
        if k >= 2:
            dy = chip_sums_and_share(k - 2, dy)
        return dy

    def ffn_backward(name, i, dy, state):
        y_in, xh, rs, a, u = state
        k = 0 if name == "ffn1" else 2
        dz, dgain[i][k], dbias[i][k] = ln_bwd(dy, xh, rs, gain(i, k))
        dx, h, da, du = ffn_bwd(dz, a, u, *ffn_w(name, i))
        outs = mm_tn([(da, y_in, 1.0), (du, y_in, 1.0), (h, dz, MACARON_WEIGHT)], nblk=N_CHIPS, out_shape=(N_CHIPS, FB, D),
                     out_block=(None, FB, D), out_index=lambda j: (j, 0, 0), name="ffn_wgrad")
        return piece_done([(name + s, i) for s in ("_w_gate", "_w_up", "_w_down")], outs, dx)

    def square_grad(a, b):
        return mm_tn([(a, b, 1.0)], nblk=1, out_shape=(D, D), out_block=(D, D), out_index=lambda j: (0, 0), name="square_wgrad")[0]

    for i in reversed(range(DEPTH)):
        f1, mix, f2 = saved[i]
        dy = ffn_backward("ffn2", i, dy, f2)
        dz, dgain[i][1], dbias[i][1] = ln_bwd(dy, mix[1], mix[2], gain(i, 1))
        if i % 2 == 0:
            y_mid, _, _, mixed, pv = mix
            g_out = square_grad(pv, dz)
            dv = mm_nt(dz, w_pool_out, None, a_blocked=False)
            du, dwg, dscale = pool_mix_bwd(dv, mixed, w_group, pool_scale)
            g_group = dwg.reshape(G, N_CHIPS, CB, C).transpose(1, 0, 2, 3)
            g_in = square_grad(y_mid, du)
            dy = mm_nt(du, w_pool_in, dz, a_blocked=False)
            dy = piece_done([("pool_w_out", 0), ("pool_w_group", 0), ("pool_w_in", 0)], [g_out, g_group, g_in], dy)
        else:
            y_mid, _, _, srcs, qkvs, ao, lse, w_qkv, w_attn_out = mix
            g_out = square_grad(ao, dz)
            dao = mm_nt(dz, w_attn_out, None, a_blocked=False)
            in_order = [dict(zip(moved_dils, dilate_rows(t, moved_dils, F32))) for t in (dao, ao, lse)]
            dqkvs = [attn_bwd(qkvs[g], *[m.get(dil, t) for m, t in zip(in_order, (dao, ao, lse))], g, dil)
                     for g, dil in enumerate(dils)]
            g_qkv = [mm_tn([(src, dqkv, 1.0)], nblk=3, out_shape=(D, 3 * D), out_block=(D, D), out_index=lambda j: (0, j),
                           name="qkv_wgrad")[0] for src, dqkv in zip(srcs, dqkvs)]
            g_qkv = jnp.concatenate(g_qkv, axis=1).reshape(D, N_CHIPS, QKV // N_CHIPS).transpose(1, 0, 2)
            dy = mm_nt_dilated(dqkvs, dils, w_qkv, dz)
            dy = piece_done([("attn_w_out", 0), ("attn_w_qkv", 0)], [g_out, g_qkv], dy)
        dy = ffn_backward("ffn1", i, dy, f1)
    grad_x = dy.reshape(x.shape)

    last = len(pieces) - 1
    small = jnp.concatenate([jnp.concatenate(dgain[i] + dbias[i], axis=0) for i in range(DEPTH)] + [dscale, jnp.zeros((3, D), F32)], axis=0)
    small = all_reduce_small(small)
    per_layer = small[:6 * DEPTH].reshape(DEPTH, 6, D)
    cols = D // N_CHIPS
    small_grads = {"ln_gain": lax.dynamic_slice_in_dim(per_layer[:, 0:3], chip * cols, cols, axis=2),
                   "ln_bias": lax.dynamic_slice_in_dim(per_layer[:, 3:6], chip * cols, cols, axis=2),
                   "pool_scale": small[6 * DEPTH:6 * DEPTH + 1]}

    grad_w, delta, new_m, new_v = {}, {}, {}, {}

    def update(n, after):
        shape = weights[n].shape
        if n in MATRIX_NAMES:
            layers = DEPTH if (n, 1) in own_half else 1
            as4 = lambda t: _halves(n, t).reshape(layers, 2, -1, shape[-1] if n not in TRANSPOSED_NAMES else shape[1])
            mine, after = tie([own_half[n, l] for l in range(layers)], after)
            outs = adamw(core, as4(weights[n]), [(mine[l], other_half[n, l]) for l in range(layers)], as4(moms[n]), as4(vels[n]))
            grad_w[n], delta[n], new_m[n], new_v[n] = [_unhalves(n, t, shape) for t in outs]
        else:
            as4 = lambda t: t.reshape(1, 1, -1, shape[-1])
            g2 = small_grads[n].reshape(-1, shape[-1])
            outs = adamw(core, as4(weights[n]), [(g2, g2)], as4(moms[n]), as4(vels[n]))
            grad_w[n], delta[n], new_m[n], new_v[n] = [t.reshape(shape) for t in outs]
        return outs[1]

    marker = small
    for n in ("ln_gain", "ln_bias", "pool_scale"):
        marker = update(n, marker)
    marker = pair_sums_and_chip_exchange(last, marker)
    marker = chip_sums_and_share(last - 1, marker)
    for n in MATRIX_NAMES:
        if not n.startswith("ffn1"):
            marker = update(n, marker)
    marker = chip_sums_and_share(last, marker)
    for n in MATRIX_NAMES:
        if n.startswith("ffn1"):
            marker = update(n, marker)

    return (loss, grad_x, *[grad_w[n] for n in WEIGHT_NAMES], *[delta[n] for n in WEIGHT_NAMES],
            *[new_m[n] for n in WEIGHT_NAMES], *[new_v[n] for n in WEIGHT_NAMES])
```
